```python
import functools
import jax, jax.numpy as jnp
from jax import lax
import numpy as np

D_MODEL = 2048
BATCH = 4
SEQ = 2048
DEPTH = 4
DEC_BATCH = 8
DEC_SEQ = 1
PAST_LEN = 16384
PAGE_SIZE = 128

HEAD_DIM = 64
ATT_W = D_MODEL // 4
RWKV_W = (3 * D_MODEL) // 8
CONV_W = D_MODEL - ATT_W - RWKV_W
ATT_HEADS = ATT_W // HEAD_DIM
RWKV_HEADS = RWKV_W // HEAD_DIM
W_LORA = 64
A_LORA = 64
G_LORA = 128
RWKV_PROJ = 3 * RWKV_W + W_LORA + A_LORA + G_LORA
MIX_PROJ = 3 * ATT_W + RWKV_PROJ + 3 * CONV_W
D_FF = ((8 * D_MODEL) // 3 + 255) // 256 * 256
CONV_K = 3
DILATION_GROUPS = ((128, 1), (512, 4), (2048, 16))
SWA_BUF = 2048
NORM_EPS = 1e-6
RWKV_GN_EPS = 64e-5

kernel_name = "hybrid_rwkv7_dilated_swa_shortconv_step"


def rms_norm(x, g):
    xf = x.astype(jnp.float32)
    xf = xf * lax.rsqrt(jnp.mean(xf * xf, axis=-1, keepdims=True) + NORM_EPS)
    return (xf * g.astype(jnp.float32)).astype(x.dtype)


def swiglu(h, w_gate, w_up, w_down):
    return (jax.nn.silu(h @ w_gate) * (h @ w_up)) @ w_down


def alibi_slopes():
    return 2.0 ** (-8.0 * jnp.arange(1, ATT_HEADS + 1, dtype=jnp.float32) / ATT_HEADS)


def dilated_group_prompt(q, k, v, window, dil, slopes):
    n, t, nh, e = q.shape
    span = window // dil
    sub = t // dil
    nb = -(-sub // span)
    sub_p = nb * span

    def to_blocks(a):
        a = a.reshape(n, sub, dil, nh, e).transpose(0, 2, 1, 3, 4)
        a = jnp.pad(a, ((0, 0), (0, 0), (0, sub_p - sub), (0, 0), (0, 0)))
        return a.reshape(n, dil, nb, span, nh, e)

    def with_prev_block(a):
        prev = jnp.pad(a, ((0, 0), (0, 0), (1, 0), (0, 0), (0, 0), (0, 0)))[:, :, :-1]
        return jnp.concatenate([prev, a], axis=3)

    qb = to_blocks(q)
    kb = with_prev_block(to_blocks(k))
    vb = with_prev_block(to_blocks(v))
    s = jnp.einsum('nrbqhe,nrbkhe->nrbhqk', qb, kb,
                   preferred_element_type=jnp.float32) * (HEAD_DIM ** -0.5)
    qi = jnp.arange(span)[:, None]
    ki = jnp.arange(2 * span)[None, :]
    steps = qi + span - ki
    key_sub = jnp.arange(nb)[:, None, None] * span + ki[None] - span
    valid = (steps >= 0) & (steps <= span) & (key_sub >= 0)
    s = s - slopes[:, None, None] * (dil * steps).astype(jnp.float32)
    s = jnp.where(valid[None, None, :, None], s, -jnp.inf)
    lse = jax.nn.logsumexp(s, axis=-1)
    p = jnp.exp(s - lse[..., None]).astype(v.dtype)
    o = jnp.einsum('nrbhqk,nrbkhe->nrbqhe', p, vb)
    o = o.reshape(n, dil, sub_p, nh, e)[:, :, :sub].transpose(0, 2, 1, 3, 4).reshape(n, t, nh, e)
    lse = lse.transpose(0, 1, 2, 4, 3).reshape(n, dil, sub_p, nh)[:, :, :sub]
    lse = lse.transpose(0, 2, 1, 3).reshape(n, t, nh)
    return o, lse


def dilated_group_sample(q, k_all, v_all, past, window, dil, slopes):
    tn = q.shape[1]
    span = window // dil
    j = jnp.arange(span + 1)
    idx = past + jnp.arange(tn)[:, None] - dil * j[None, :]
    valid = idx >= 0
    idx = jnp.maximum(idx, 0)
    kg = k_all[:, idx]
    vg = v_all[:, idx]
    s = jnp.einsum('nqhe,nqjhe->nqhj', q, kg,
                   preferred_element_type=jnp.float32) * (HEAD_DIM ** -0.5)
    s = s - slopes[:, None] * (dil * j).astype(jnp.float32)[None, :]
    s = jnp.where(valid[None, :, None, :], s, -jnp.inf)
    lse = jax.nn.logsumexp(s, axis=-1)
    p = jnp.exp(s - lse[..., None]).astype(v_all.dtype)
    o = jnp.einsum('nqhj,nqjhe->nqhe', p, vg)
    return o, lse


def merge_groups(outs, lses):
    wts = jax.nn.softmax(jnp.stack(lses, axis=-1), axis=-1)
    o = outs[0].astype(jnp.float32) * wts[..., 0:1]
    for g in range(1, len(outs)):
        o = o + outs[g].astype(jnp.float32) * wts[..., g:g + 1]
    return o.astype(outs[0].dtype)


def attend_prompt(q, k, v):
    slopes = alibi_slopes()
    outs, lses = [], []
    for window, dil in DILATION_GROUPS:
        o, l = dilated_group_prompt(q, k, v, window, dil, slopes)
        outs.append(o)
        lses.append(l)
    keep = min(SWA_BUF, q.shape[1])
    return merge_groups(outs, lses), k[:, -keep:], v[:, -keep:]


def attend_sample(q, k, v, buf_k, buf_v):
    slopes = alibi_slopes()
    past = buf_k.shape[1]
    k_all = jnp.concatenate([buf_k.astype(k.dtype), k], axis=1)
    v_all = jnp.concatenate([buf_v.astype(v.dtype), v], axis=1)
    outs, lses = [], []
    for window, dil in DILATION_GROUPS:
        o, l = dilated_group_sample(q, k_all, v_all, past, window, dil, slopes)
        outs.append(o)
        lses.append(l)
    return merge_groups(outs, lses), k, v


def wkv7_scan(s0, r, w, k, v, a, b):
    def step(s, inp):
        r_t, w_t, k_t, v_t, a_t, b_t = inp
        sa = jnp.einsum('nhvk,nhk->nhv', s, a_t)
        s = s * w_t[:, :, None, :] + sa[..., None] * b_t[:, :, None, :] + v_t[..., None] * k_t[:, :, None, :]
        return s, jnp.einsum('nhvk,nhk->nhv', s, r_t)
    xs = tuple(jnp.moveaxis(z, 1, 0) for z in (r, w, k, v, a, b))
    s, ys = lax.scan(step, s0, xs)
    return jnp.moveaxis(ys, 0, 1), s


def token_mixers(h, lp, shift_prev, wkv0, conv_prev, attend):
    n, t, _ = h.shape
    f32 = jnp.float32
    p = h @ lp['w_in']
    o1 = 3 * ATT_W
    o2 = o1 + RWKV_PROJ
    p_att, p_rwkv, p_conv = p[..., :o1], p[..., o1:o2], p[..., o2:]

    ah = (n, t, ATT_HEADS, HEAD_DIM)
    q = rms_norm(p_att[..., :ATT_W].reshape(ah), lp['q_norm'])
    k = rms_norm(p_att[..., ATT_W:2 * ATT_W].reshape(ah), lp['k_norm'])
    v = p_att[..., 2 * ATT_W:].reshape(ah)
    o_att, k_rows, v_rows = attend(q, k, v)

    prev = jnp.concatenate([shift_prev[:, None, :].astype(p_rwkv.dtype), p_rwkv[:, :-1]], axis=1)
    xm = p_rwkv + (prev - p_rwkv) * lp['rwkv_mu']
    c1, c2, c3 = RWKV_W, 2 * RWKV_W, 3 * RWKV_W
    r = xm[..., :c1]
    kr = xm[..., c1:c2]
    vr = xm[..., c2:c3]
    wl = xm[..., c3:c3 + W_LORA]
    al = xm[..., c3 + W_LORA:c3 + W_LORA + A_LORA]
    gl = xm[..., c3 + W_LORA + A_LORA:]
    w_log = -jax.nn.softplus(-(lp['rwkv_w0'] + jnp.tanh(wl) @ lp['rwkv_w2']).astype(f32)) - 0.5
    decay = jnp.exp(-jnp.exp(w_log))
    a = jax.nn.sigmoid((lp['rwkv_a0'] + al @ lp['rwkv_a2']).astype(f32))
    g = (jax.nn.sigmoid(gl) @ lp['rwkv_g2']).astype(f32)
    rh = (n, t, RWKV_HEADS, HEAD_DIM)
    krf = kr.astype(f32)
    kk = (krf * lp['rwkv_k_k']).reshape(rh)
    kk = kk / jnp.maximum(jnp.sqrt(jnp.sum(kk * kk, axis=-1, keepdims=True)), 1e-12)
    kmod = (krf * (1.0 + (a - 1.0) * lp['rwkv_k_a'])).reshape(rh)
    rf = r.astype(f32).reshape(rh)
    vf = vr.astype(f32).reshape(rh)
    y, wkv_new = wkv7_scan(wkv0.astype(f32), rf, decay.reshape(rh), kmod, vf, -kk, kk * a.reshape(rh))
    mu_y = jnp.mean(y, axis=-1, keepdims=True)
    var_y = jnp.mean(jnp.square(y - mu_y), axis=-1, keepdims=True)
    y = ((y - mu_y) * lax.rsqrt(var_y + RWKV_GN_EPS)).reshape(n, t, RWKV_W)
    y = y * lp['rwkv_ln_w'] + lp['rwkv_ln_b']
    bonus = jnp.sum(rf * kmod * lp['rwkv_r_k'], axis=-1, keepdims=True) * vf
    y_rwkv = ((y + bonus.reshape(n, t, RWKV_W)) * g).astype(h.dtype)
    shift_new = p_rwkv[:, -1]

    u = p_conv[..., :CONV_W]
    gate_b = p_conv[..., CONV_W:2 * CONV_W]
    gate_c = p_conv[..., 2 * CONV_W:]
    z = gate_c * u
    zc = jnp.concatenate([conv_prev.astype(z.dtype), z], axis=1)
    yc = lp['conv_b']
    for i in range(CONV_K):
        yc = yc + lp['conv_w'][i] * zc[:, i:i + t]
    y_conv = gate_b * yc
    conv_new = zc[:, -(CONV_K - 1):]

    mix = jnp.concatenate([o_att.reshape(n, t, ATT_W), y_rwkv, y_conv.astype(h.dtype)], axis=-1) @ lp['w_out']
    return mix, (k_rows, v_rows, wkv_new, shift_new, conv_new)


def trunk_layer(x, lp, shift_prev, wkv0, conv_prev, attend):
    x = x + 0.5 * swiglu(rms_norm(x, lp['ffn1_norm']), lp['ffn1_w_gate'], lp['ffn1_w_up'], lp['ffn1_w_down'])
    mix, new_state = token_mixers(rms_norm(x, lp['mix_norm']), lp, shift_prev, wkv0, conv_prev, attend)
    x = x + mix
    x = x + 0.5 * swiglu(rms_norm(x, lp['ffn2_norm']), lp['ffn2_w_gate'], lp['ffn2_w_up'], lp['ffn2_w_down'])
    return x, new_state


def setup_inputs(seed: int = 0) -> dict:
    key = jax.random.key(seed)
    keys = iter(jax.random.split(key, 40))
    f32 = jnp.float32

    def nrm(shape, scale=1.0):
        return scale * jax.random.normal(next(keys), shape, f32)

    def gain(shape):
        return 1.0 + 0.02 * jax.random.normal(next(keys), shape, f32)

    def unif(shape, lo, hi):
        return jax.random.uniform(next(keys), shape, f32, lo, hi)

    swa_len = min(SWA_BUF, PAST_LEN)
    return {
        'x_prompt': nrm((BATCH, SEQ, D_MODEL)),
        'x_sample': nrm((DEC_BATCH, DEC_SEQ, D_MODEL)),
        'cache_swa_k': nrm((DEPTH, DEC_BATCH, swa_len, ATT_HEADS, HEAD_DIM)),
        'cache_swa_v': nrm((DEPTH, DEC_BATCH, swa_len, ATT_HEADS, HEAD_DIM)),
        'state_wkv': nrm((DEPTH, DEC_BATCH, RWKV_HEADS, HEAD_DIM, HEAD_DIM), 0.3),
        'state_shift': nrm((DEPTH, DEC_BATCH, RWKV_PROJ)),
        'state_conv': nrm((DEPTH, DEC_BATCH, CONV_K - 1, CONV_W)),
        'ffn1_norm': gain((DEPTH, D_MODEL)),
        'ffn1_w_gate': nrm((DEPTH, D_MODEL, D_FF), D_MODEL ** -0.5),
        'ffn1_w_up': nrm((DEPTH, D_MODEL, D_FF), D_MODEL ** -0.5),
        'ffn1_w_down': nrm((DEPTH, D_FF, D_MODEL), D_FF ** -0.5),
        'mix_norm': gain((DEPTH, D_MODEL)),
        'w_in': nrm((DEPTH, D_MODEL, MIX_PROJ), D_MODEL ** -0.5),
        'q_norm': gain((DEPTH, HEAD_DIM)),
        'k_norm': gain((DEPTH, HEAD_DIM)),
        'rwkv_mu': unif((DEPTH, RWKV_PROJ), 0.0, 1.0),
        'rwkv_w0': unif((DEPTH, RWKV_W), -6.0, 1.0),
        'rwkv_w2': nrm((DEPTH, W_LORA, RWKV_W), 0.5 * W_LORA ** -0.5),
        'rwkv_a0': nrm((DEPTH, RWKV_W), 0.5),
        'rwkv_a2': nrm((DEPTH, A_LORA, RWKV_W), A_LORA ** -0.5),
        'rwkv_g2': nrm((DEPTH, G_LORA, RWKV_W), G_LORA ** -0.5),
        'rwkv_k_k': 0.85 + nrm((DEPTH, RWKV_W), 0.05),
        'rwkv_k_a': 1.0 + nrm((DEPTH, RWKV_W), 0.05),
        'rwkv_r_k': nrm((DEPTH, RWKV_HEADS, HEAD_DIM), 0.1),
        'rwkv_ln_w': gain((DEPTH, RWKV_W)),
        'rwkv_ln_b': nrm((DEPTH, RWKV_W), 0.02),
        'conv_w': nrm((DEPTH, CONV_K, CONV_W), CONV_K ** -0.5),
        'conv_b': nrm((DEPTH, CONV_W), 0.02),
        'w_out': nrm((DEPTH, D_MODEL, D_MODEL), D_MODEL ** -0.5),
        'ffn2_norm': gain((DEPTH, D_MODEL)),
        'ffn2_w_gate': nrm((DEPTH, D_MODEL, D_FF), D_MODEL ** -0.5),
        'ffn2_w_up': nrm((DEPTH, D_MODEL, D_FF), D_MODEL ** -0.5),
        'ffn2_w_down': nrm((DEPTH, D_FF, D_MODEL), D_FF ** -0.5),
    }


def reference(x_prompt, x_sample, cache_swa_k, cache_swa_v, state_wkv, state_shift, state_conv,
              ffn1_norm, ffn1_w_gate, ffn1_w_up, ffn1_w_down, mix_norm, w_in, q_norm, k_norm,
              rwkv_mu, rwkv_w0, rwkv_w2, rwkv_a0, rwkv_a2, rwkv_g2, rwkv_k_k, rwkv_k_a, rwkv_r_k,
              rwkv_ln_w, rwkv_ln_b, conv_w, conv_b, w_out,
              ffn2_norm, ffn2_w_gate, ffn2_w_up, ffn2_w_down):
    xp, xs = x_prompt, x_sample
    n_p = xp.shape[0]
    p_states, s_states = [], []
    for l in range(DEPTH):
        lp = {
            'ffn1_norm': ffn1_norm[l], 'ffn1_w_gate': ffn1_w_gate[l], 'ffn1_w_up': ffn1_w_up[l],
            'ffn1_w_down': ffn1_w_down[l], 'mix_norm': mix_norm[l], 'w_in': w_in[l],
            'q_norm': q_norm[l], 'k_norm': k_norm[l], 'rwkv_mu': rwkv_mu[l], 'rwkv_w0': rwkv_w0[l],
            'rwkv_w2': rwkv_w2[l], 'rwkv_a0': rwkv_a0[l], 'rwkv_a2': rwkv_a2[l], 'rwkv_g2': rwkv_g2[l],
            'rwkv_k_k': rwkv_k_k[l], 'rwkv_k_a': rwkv_k_a[l], 'rwkv_r_k': rwkv_r_k[l],
            'rwkv_ln_w': rwkv_ln_w[l], 'rwkv_ln_b': rwkv_ln_b[l], 'conv_w': conv_w[l],
            'conv_b': conv_b[l], 'w_out': w_out[l], 'ffn2_norm': ffn2_norm[l],
            'ffn2_w_gate': ffn2_w_gate[l], 'ffn2_w_up': ffn2_w_up[l], 'ffn2_w_down': ffn2_w_down[l],
        }
        xp, st_p = trunk_layer(xp, lp,
                               jnp.zeros((n_p, RWKV_PROJ), xp.dtype),
                               jnp.zeros((n_p, RWKV_HEADS, HEAD_DIM, HEAD_DIM), jnp.float32),
                               jnp.zeros((n_p, CONV_K - 1, CONV_W), xp.dtype),
                               attend_prompt)
        p_states.append(st_p)
        xs, st_s = trunk_layer(xs, lp, state_shift[l], state_wkv[l], state_conv[l],
                               functools.partial(attend_sample, buf_k=cache_swa_k[l], buf_v=cache_swa_v[l]))
        s_states.append(st_s)

    def stacked(states, i):
        return jnp.stack([st[i] for st in states], axis=0)

    prompt_swa_k = stacked(p_states, 0)
    prompt_swa_v = stacked(p_states, 1)
    prompt_wkv = stacked(p_states, 2)
    prompt_shift = stacked(p_states, 3)
    prompt_conv = stacked(p_states, 4)
    sample_swa_k = stacked(s_states, 0)
    sample_swa_v = stacked(s_states, 1)
    sample_wkv = stacked(s_states, 2)
    sample_shift = stacked(s_states, 3)
    sample_conv = stacked(s_states, 4)
    return (xp, xs, prompt_swa_k, prompt_swa_v, prompt_wkv, prompt_shift, prompt_conv,
            sample_swa_k, sample_swa_v, sample_wkv, sample_shift, sample_conv)
```

```python
import functools

import jax
import jax.numpy as jnp
from jax import lax
from jax.experimental import pallas as pl
from jax.experimental.pallas import tpu as pltpu

F32 = jnp.float32
BF16 = jnp.bfloat16

HEAD_DIM = 64
HEAD_SHIFT = 6
LANES = 128
NORM_EPS = 1e-6
RWKV_GN_EPS = 64e-5
DILATION_GROUPS = ((128, 1), (512, 4), (2048, 16))
ATT_SPAN = 128
W_LORA, A_LORA, G_LORA = 64, 64, 128
LORA_W = W_LORA + A_LORA + G_LORA
CONV_K = 3
RWKV_CHUNK = 64
VMEM_LIMIT = 56 * 1024 * 1024
NEG_BIG = -1e30

_NT = (((1,), (1,)), ((), ()))
_NN = (((1,), (0,)), ((), ()))


def _cparams(sem):
    return pltpu.CompilerParams(dimension_semantics=sem, vmem_limit_bytes=VMEM_LIMIT)


def _rms(x, g):
    return x * lax.rsqrt(jnp.mean(x * x, axis=-1, keepdims=True) + NORM_EPS) * g


def _split2(x):
    hi = x.astype(BF16)
    lo = (x - hi.astype(F32)).astype(BF16)
    return hi, lo


def _dot3(a, b, dims=_NN):
    ah, al = _split2(a)
    bh, bl = _split2(b)
    d = lambda x, y: lax.dot_general(x, y, dims, preferred_element_type=F32)
    return d(ah, bh) + (d(al, bh) + d(ah, bl))


def _dot_sel(x, sel):
    h1 = x.astype(BF16)
    r1 = x - h1.astype(F32)
    h2 = r1.astype(BF16)
    h3 = (r1 - h2.astype(F32)).astype(BF16)
    d = lambda y: jnp.dot(y, sel, preferred_element_type=F32)
    return d(h1) + d(h2) + d(h3)


def _head_blockdiag(n, dtype):
    r = lax.broadcasted_iota(jnp.int32, (n, n), 0) >> HEAD_SHIFT
    c = lax.broadcasted_iota(jnp.int32, (n, n), 1) >> HEAD_SHIFT
    return (r == c).astype(dtype)


def _segsum64(x):
    bd = _head_blockdiag(LANES, BF16)
    parts = [_dot_sel(x[:, i:i + LANES], bd) for i in range(0, x.shape[1], LANES)]
    return parts[0] if len(parts) == 1 else jnp.concatenate(parts, axis=1)


def _ffn_body(x_ref, g_ref, wg_ref, wu_ref, wd_ref, o_ref, xn_ref, acc_ref):
    j = pl.program_id(1)

    @pl.when(j == 0)
    def _():
        xn_ref[...] = _rms(x_ref[...], g_ref[...]).astype(BF16)
        acc_ref[...] = jnp.zeros_like(acc_ref)

    xn = xn_ref[...]
    h = jnp.dot(xn, wg_ref[...], preferred_element_type=F32)
    u = jnp.dot(xn, wu_ref[...], preferred_element_type=F32)
    act = (h * jax.nn.sigmoid(h) * u).astype(BF16)
    acc_ref[...] += jnp.dot(act, wd_ref[...], preferred_element_type=F32)

    @pl.when(j == pl.num_programs(1) - 1)
    def _():
        o_ref[...] = x_ref[...] + 0.5 * acc_ref[...]


def _ffn(x, gain, wg, wu, wd, layer, tm, tf):
    m, d = x.shape
    f = wg.shape[2]
    return pl.pallas_call(
        _ffn_body,
        grid=(m // tm, f // tf),
        in_specs=[
            pl.BlockSpec((tm, d), lambda i, j: (i, 0)),
            pl.BlockSpec((None, 1, d), lambda i, j: (layer, 0, 0)),
            pl.BlockSpec((None, d, tf), lambda i, j: (layer, 0, j)),
            pl.BlockSpec((None, d, tf), lambda i, j: (layer, 0, j)),
            pl.BlockSpec((None, tf, d), lambda i, j: (layer, j, 0)),
        ],
        out_specs=pl.BlockSpec((tm, d), lambda i, j: (i, 0)),
        out_shape=jax.ShapeDtypeStruct((m, d), F32),
        scratch_shapes=[pltpu.VMEM((tm, d), BF16), pltpu.VMEM((tm, d), F32)],
        compiler_params=_cparams(("parallel", "arbitrary")),
        name="ffn",
    )(x, gain, wg, wu, wd)


def _norm_matmul_body(x_ref, g_ref, w_ref, o_ref, xn_ref):
    @pl.when(pl.program_id(1) == 0)
    def _():
        xn_ref[...] = _rms(x_ref[...], g_ref[...]).astype(BF16)

    o_ref[...] = jnp.dot(xn_ref[...], w_ref[...], preferred_element_type=F32)


def _norm_matmul(x, gain, w, layer, tm, tn):
    m, d = x.shape
    n = w.shape[2]
    return pl.pallas_call(
        _norm_matmul_body,
        grid=(m // tm, n // tn),
        in_specs=[
            pl.BlockSpec((tm, d), lambda i, j: (i, 0)),
            pl.BlockSpec((None, 1, d), lambda i, j: (layer, 0, 0)),
            pl.BlockSpec((None, d, tn), lambda i, j: (layer, 0, j)),
        ],
        out_specs=pl.BlockSpec((tm, tn), lambda i, j: (i, j)),
        out_shape=jax.ShapeDtypeStruct((m, n), F32),
        scratch_shapes=[pltpu.VMEM((tm, d), BF16)],
        compiler_params=_cparams(("parallel", "arbitrary")),
        name="norm_matmul",
    )(x, gain, w)


def _out_proj_body(x_ref, a_ref, r_ref, c_ref, w_ref, o_ref):
    wa = a_ref.shape[1]
    wr = r_ref.shape[1]
    acc = jnp.dot(a_ref[...].astype(BF16), w_ref[0:wa, :], preferred_element_type=F32)
    acc += jnp.dot(r_ref[...].astype(BF16), w_ref[wa:wa + wr, :], preferred_element_type=F32)
    acc += jnp.dot(c_ref[...].astype(BF16), w_ref[wa + wr:, :], preferred_element_type=F32)
    o_ref[...] = x_ref[...] + acc


def _out_proj(x, o_att, y_rwkv, y_conv, w, layer, tm, tn):
    m, d = x.shape
    return pl.pallas_call(
        _out_proj_body,
        grid=(m // tm, d // tn),
        in_specs=[
            pl.BlockSpec((tm, tn), lambda i, j: (i, j)),
            pl.BlockSpec((tm, o_att.shape[1]), lambda i, j: (i, 0)),
            pl.BlockSpec((tm, y_rwkv.shape[1]), lambda i, j: (i, 0)),
            pl.BlockSpec((tm, y_conv.shape[1]), lambda i, j: (i, 0)),
            pl.BlockSpec((None, w.shape[1], tn), lambda i, j: (layer, 0, j)),
        ],
        out_specs=pl.BlockSpec((tm, tn), lambda i, j: (i, j)),
        out_shape=jax.ShapeDtypeStruct((m, d), F32),
        compiler_params=_cparams(("parallel", "arbitrary")),
        name="out_proj",
    )(x, o_att, y_rwkv, y_conv, w)


def _pair_headnorm(x, g, lo_half):
    x2 = x * x
    s0 = jnp.sum(jnp.where(lo_half, x2, 0.0), axis=-1, keepdims=True)
    s1 = jnp.sum(jnp.where(lo_half, 0.0, x2), axis=-1, keepdims=True)
    ms = jnp.where(lo_half, s0, s1) * (1.0 / HEAD_DIM)
    return x * lax.rsqrt(ms + NORM_EPS) * g


def _attn_body(slopes_ref, q_ref, k_ref, v_ref, qg_ref, kg_ref,
               o_ref, ko_ref, vo_ref, qs_ref, og_ref, lse_ref):
    hp = pl.program_id(1)
    t = q_ref.shape[0]
    span = ATT_SPAN
    lo_half = lax.broadcasted_iota(jnp.int32, (1, LANES), 1) < HEAD_DIM

    ko_ref[...] = _pair_headnorm(k_ref[...], kg_ref[...], lo_half)
    vo_ref[...] = v_ref[...]
    qs_ref[...] = _pair_headnorm(q_ref[...], qg_ref[...], lo_half) * (HEAD_DIM ** -0.5)

    def unit(g, dil, start, has_prev):
        rows = pl.ds(start, span, stride=dil)
        qb = qs_ref[rows, :]
        if has_prev:
            prow = pl.ds(start - span * dil, span, stride=dil)
            kb = jnp.concatenate([ko_ref[prow, :], ko_ref[rows, :]], axis=0)
            vb = jnp.concatenate([v_ref[prow, :], v_ref[rows, :]], axis=0)
        else:
            kb = ko_ref[rows, :]
            vb = v_ref[rows, :]
        nk = kb.shape[0]
        qi = lax.broadcasted_iota(jnp.int32, (span, nk), 0)
        ki = lax.broadcasted_iota(jnp.int32, (span, nk), 1)
        steps = qi + (nk - span) - ki
        valid = (steps >= 0) & (steps <= span)
        dist = (steps * dil).astype(F32)
        kb16 = kb.astype(BF16)
        vb16 = vb.astype(BF16)
        outs, lses = [], []
        for h in range(2):
            qh = jnp.where(lo_half if h == 0 else jnp.logical_not(lo_half), qb, 0.0).astype(BF16)
            s = lax.dot_general(qh, kb16, _NT, preferred_element_type=F32)
            s = s - slopes_ref[2 * hp + h] * dist
            s = jnp.where(valid, s, NEG_BIG)
            m = jnp.max(s, axis=-1, keepdims=True)
            p = jnp.exp(s - m)
            l = jnp.sum(p, axis=-1, keepdims=True)
            o = jnp.dot(p.astype(BF16), vb16, preferred_element_type=F32)
            outs.append(o / l)
            lses.append(m + jnp.log(l))
        og_ref[g, rows, :] = jnp.where(lo_half, outs[0], outs[1])
        lse_ref[g, rows, :] = jnp.where(lo_half, lses[0], lses[1])

    for g, (window, dil) in enumerate(DILATION_GROUPS):
        assert window // dil == span
        nb = t // (span * dil)

        def residue(r, carry, g=g, dil=dil, nb=nb):
            unit(g, dil, r, False)
            if nb > 1:
                def blk(b, c):
                    unit(g, dil, b * (span * dil) + r, True)
                    return c
                lax.fori_loop(1, nb, blk, 0)
            return carry

        if dil == 1:
            residue(0, 0)
        else:
            lax.fori_loop(0, dil, residue, 0)

    l0, l1, l2 = lse_ref[0], lse_ref[1], lse_ref[2]
    mx = jnp.maximum(jnp.maximum(l0, l1), l2)
    w0, w1, w2 = jnp.exp(l0 - mx), jnp.exp(l1 - mx), jnp.exp(l2 - mx)
    o_ref[...] = (og_ref[0] * w0 + og_ref[1] * w1 + og_ref[2] * w2) / (w0 + w1 + w2)


def _attention_prompt(p_att, qg, kg, slopes, layer, n_seq, t):
    m, w3 = p_att.shape
    att_w = w3 // 3
    pairs = att_w // LANES
    blk = lambda off: pl.BlockSpec((t, LANES), lambda n, hp: (n, off + hp))
    gain = pl.BlockSpec((None, 1, LANES), lambda n, hp: (layer, 0, 0))
    out = jax.ShapeDtypeStruct((m, att_w), F32)
    return pl.pallas_call(
        _attn_body,
        grid=(n_seq, pairs),
        in_specs=[pl.BlockSpec(memory_space=pltpu.SMEM),
                  blk(0), blk(pairs), blk(2 * pairs), gain, gain],
        out_specs=[blk(0), blk(0), blk(0)],
        out_shape=[out, out, out],
        scratch_shapes=[pltpu.VMEM((t, LANES), F32),
                        pltpu.VMEM((3, t, LANES), F32),
                        pltpu.VMEM((3, t, LANES), F32)],
        compiler_params=_cparams(("parallel", "arbitrary")),
        name="attn_prompt",
    )(slopes, p_att, p_att, p_att, qg, kg)


def _softplus(z):
    return jnp.maximum(z, 0.0) + jnp.log(1.0 + jnp.exp(-jnp.abs(z)))


def _rwkv_token_math(p_r, p_k, p_v, p_l, q_r, q_k, q_v, q_l, mu_r, mu_k, mu_v, mu_l,
                     w0, w2p, a0, a2p, g2p, k_k, k_a):
    xr = p_r + (q_r - p_r) * mu_r
    xk = p_k + (q_k - p_k) * mu_k
    xv = p_v + (q_v - p_v) * mu_v
    xl = p_l + (q_l - p_l) * mu_l
    w_log = -_softplus(-(w0 + _dot3(jnp.tanh(xl), w2p))) - 0.5
    log_decay = -jnp.exp(w_log)
    sig_l = jax.nn.sigmoid(xl)
    a_gate = jax.nn.sigmoid(a0 + _dot3(xl, a2p))
    g = _dot3(sig_l, g2p)
    kk = xk * k_k
    kk = kk / jnp.maximum(jnp.sqrt(_segsum64(kk * kk)), 1e-12)
    kmod = xk * (1.0 + (a_gate - 1.0) * k_a)
    return xr, log_decay, kmod, xv, -kk, kk * a_gate, g


def _rwkv_post(y, r, k, v, g, ln_w, ln_b, r_k):
    mu_y = _segsum64(y) * (1.0 / HEAD_DIM)
    yc = y - mu_y
    var_y = _segsum64(yc * yc) * (1.0 / HEAD_DIM)
    yn = yc * lax.rsqrt(var_y + RWKV_GN_EPS) * ln_w + ln_b
    bonus = _segsum64(r * k * r_k) * v
    return (yn + bonus) * g


def _mixprep_body(pr_ref, pk_ref, pv_ref, plo_ref, qr_ref, qk_ref, qv_ref, qlo_ref,
                  mur_ref, muk_ref, muv_ref, mul_ref, w0_ref, w2_ref, a0_ref, a2_ref, g2_ref,
                  kk_ref, ka_ref,
                  r_out, lw_out, k_out, v_out, a_out, b_out, g_out,
                  sr_ref, sk_ref, sv_ref, sl_ref, *, blocks_per_seq):
    tm = pr_ref.shape[0]
    first = (pl.program_id(0) % blocks_per_seq) == 0

    def shifted(cur_ref, prev8_ref, s_ref):
        s_ref[0:8, :] = jnp.where(first, 0.0, prev8_ref[...])
        s_ref[8:, :] = cur_ref[...]
        return s_ref[pl.ds(7, tm), :]

    outs = _rwkv_token_math(
        pr_ref[...], pk_ref[...], pv_ref[...], plo_ref[...],
        shifted(pr_ref, qr_ref, sr_ref), shifted(pk_ref, qk_ref, sk_ref),
        shifted(pv_ref, qv_ref, sv_ref), shifted(plo_ref, qlo_ref, sl_ref),
        mur_ref[...], muk_ref[...], muv_ref[...], mul_ref[...],
        w0_ref[...], w2_ref[...], a0_ref[...], a2_ref[...], g2_ref[...], kk_ref[...], ka_ref[...])
    for ref, val in zip((r_out, lw_out, k_out, v_out, a_out, b_out, g_out), outs):
        ref[...] = val


def _rwkv_prep_prompt(p_rwkv, lp, layer, t, tm):
    m = p_rwkv.shape[0]
    rw = lp["rwkv_w"]
    assert tm % 8 == 0 and t % tm == 0
    cur = lambda w, c: pl.BlockSpec((tm, w), lambda i: (i, c))
    prev = lambda w, c: pl.BlockSpec((8, w), lambda i: (jnp.maximum(i * (tm // 8) - 1, 0), c))
    lcol = (3 * rw) // LORA_W
    vec = lambda w, c: pl.BlockSpec((None, 1, w), lambda i: (layer, 0, c))
    mat = pl.BlockSpec((None, LORA_W, rw), lambda i: (layer, 0, 0))
    out = jax.ShapeDtypeStruct((m, rw), F32)
    return pl.pallas_call(
        functools.partial(_mixprep_body, blocks_per_seq=t // tm),
        grid=(m // tm,),
        in_specs=[cur(rw, 0), cur(rw, 1), cur(rw, 2), cur(LORA_W, lcol),
                  prev(rw, 0), prev(rw, 1), prev(rw, 2), prev(LORA_W, lcol),
                  vec(rw, 0), vec(rw, 1), vec(rw, 2), vec(LORA_W, lcol),
                  vec(rw, 0), mat, vec(rw, 0), mat, mat, vec(rw, 0), vec(rw, 0)],
        out_specs=[pl.BlockSpec((tm, rw), lambda i: (i, 0))] * 7,
        out_shape=[out] * 7,
        scratch_shapes=[pltpu.VMEM((tm + 8, rw), F32)] * 3 + [pltpu.VMEM((tm + 8, LORA_W), F32)],
        compiler_params=_cparams(("parallel",)),
        name="rwkv_prep",
    )(p_rwkv, p_rwkv, p_rwkv, p_rwkv, p_rwkv, p_rwkv, p_rwkv, p_rwkv,
      lp["mu"], lp["mu"], lp["mu"], lp["mu"],
      lp["w0"], lp["w2p"], lp["a0"], lp["a2p"], lp["g2p"], lp["k_k"], lp["k_a"])


def _conv_body(u_ref, gb_ref, gc_ref, pu_ref, pgc_ref, w_ref, b_ref, y_ref, zl_ref, zc_ref,
               *, blocks_per_seq):
    tm = u_ref.shape[0]
    first = (pl.program_id(0) % blocks_per_seq) == 0
    z = gc_ref[...] * u_ref[...]
    zc_ref[0:8, :] = jnp.where(first, 0.0, pgc_ref[...] * pu_ref[...])
    zc_ref[8:, :] = z
    yc = b_ref[...] + w_ref[0:1, :] * zc_ref[pl.ds(6, tm), :]
    yc = yc + w_ref[1:2, :] * zc_ref[pl.ds(7, tm), :]
    yc = yc + w_ref[2:3, :] * z
    y_ref[...] = gb_ref[...] * yc
    zl_ref[...] = zc_ref[pl.ds(tm, 8), :]


def _conv_prompt(p_conv, conv_w, conv_b, layer, t, tm):
    m, w3 = p_conv.shape
    cw = w3 // 3
    cur = lambda c: pl.BlockSpec((tm, cw), lambda i: (i, c))
    prev = lambda c: pl.BlockSpec((8, cw), lambda i: (jnp.maximum(i * (tm // 8) - 1, 0), c))
    return pl.pallas_call(
        functools.partial(_conv_body, blocks_per_seq=t // tm),
        grid=(m // tm,),
        in_specs=[cur(0), cur(1), cur(2), prev(0), prev(2),
                  pl.BlockSpec((None, CONV_K, cw), lambda i: (layer, 0, 0)),
                  pl.BlockSpec((None, 1, cw), lambda i: (layer, 0, 0))],
        out_specs=[pl.BlockSpec((tm, cw), lambda i: (i, 0)), pl.BlockSpec((8, cw), lambda i: (i, 0))],
        out_shape=[jax.ShapeDtypeStruct((m, cw), F32), jax.ShapeDtypeStruct((m // tm * 8, cw), F32)],
        scratch_shapes=[pltpu.VMEM((tm + 8, cw), F32)],
        compiler_params=_cparams(("parallel",)),
        name="conv_prompt",
    )(p_conv, p_conv, p_conv, p_conv, p_conv, conv_w, conv_b)


def _scan_pair(r, lw, k, v, a, b, s0):
    c = r.shape[0]
    c2 = 2 * c
    lane = lax.broadcasted_iota(jnp.int32, (1, LANES), 1)
    m0 = (lane < HEAD_DIM).astype(F32)
    m1 = 1.0 - m0
    stack = lambda x: jnp.concatenate([x * m0, x * m1], axis=0)

    ri = lax.broadcasted_iota(jnp.int32, (c, c), 0)
    ci = lax.broadcasted_iota(jnp.int32, (c, c), 1)
    tri_incl = (ri >= ci).astype(BF16)
    cum = _cumsum_rows(lw, tri_incl)
    cum_prev = cum - lw
    cum_last = cum[c - 1:c, :]
    e_pos = jnp.exp(cum)
    e_neg = jnp.exp(-cum)
    e_rem = jnp.exp(cum_last - cum)
    at = a * jnp.exp(cum_prev)
    rt = r * e_pos
    bt = b * e_neg
    kt = k * e_neg
    bh = b * e_rem
    kh = k * e_rem

    big = _dot3(jnp.concatenate([stack(at), stack(rt)], axis=0),
                jnp.concatenate([stack(bt), stack(kt)], axis=0), _NT)
    row = lax.broadcasted_iota(jnp.int32, (c2, c2), 0)
    col = lax.broadcasted_iota(jnp.int32, (c2, c2), 1)
    assert c == HEAD_DIM
    same_head = (row >> HEAD_SHIFT) == (col >> HEAD_SHIFT)
    tr, tc = row & (c - 1), col & (c - 1)
    strict = same_head & (tr > tc)
    incl = same_head & (tr >= tc)
    lmat = jnp.where(strict, big[0:c2, 0:c2], 0.0)
    ak = jnp.where(strict, big[0:c2, c2:], 0.0)
    rb = jnp.where(incl, big[c2:, 0:c2], 0.0)
    rk = jnp.where(incl, big[c2:, c2:], 0.0)

    eye = (row == col).astype(F32)
    x = eye + jnp.where((row >> 1) == (col >> 1), lmat, 0.0)
    s, log2s = 2, 1
    while s < c:
        joins = ((row >> (log2s + 1)) == (col >> (log2s + 1))) & ((row & s) != 0) & ((col & s) == 0)
        x = x + _dot3(_dot3(x, jnp.where(joins, lmat, 0.0)), x)
        s, log2s = 2 * s, log2s + 1

    v_st = stack(v)
    w = _dot3(jnp.concatenate([at, rt], axis=0), s0, _NT)
    u_st = _dot3(x, stack(w[0:c]) + _dot3(ak, v_st))
    y_st = _dot3(jnp.concatenate([rb, rk], axis=1), jnp.concatenate([u_st, v_st], axis=0))
    y = w[c:] + y_st[0:c] + y_st[c:]
    u = u_st[0:c] + u_st[c:]
    uv_t = jnp.transpose(jnp.concatenate([u, v], axis=0))
    upd = _dot3(uv_t, jnp.concatenate([bh, kh], axis=0))
    s_new = s0 * jnp.exp(cum_last) + _head_blockdiag(LANES, F32) * upd
    return y, s_new


def _cumsum_rows(x, tri_incl):
    h1 = x.astype(BF16)
    r1 = x - h1.astype(F32)
    h2 = r1.astype(BF16)
    h3 = (r1 - h2.astype(F32)).astype(BF16)
    d = lambda y: jnp.dot(tri_incl, y, preferred_element_type=F32)
    return d(h1) + d(h2) + d(h3)


def _scan_body(r_ref, lw_ref, k_ref, v_ref, a_ref, b_ref, g_ref, lnw_ref, lnb_ref, rk_ref,
               y_ref, s_out_ref, s_ref):
    ch = pl.program_id(1)

    @pl.when(ch == 0)
    def _():
        s_ref[...] = jnp.zeros_like(s_ref)

    for p in range(s_ref.shape[0]):
        sl = slice(p * LANES, (p + 1) * LANES)
        r, k, v = r_ref[:, sl], k_ref[:, sl], v_ref[:, sl]
        y, s_new = _scan_pair(r, lw_ref[:, sl], k, v, a_ref[:, sl], b_ref[:, sl], s_ref[p])
        s_ref[p] = s_new
        y_ref[:, sl] = _rwkv_post(y, r, k, v, g_ref[:, sl], lnw_ref[:, sl], lnb_ref[:, sl], rk_ref[:, sl])

    @pl.when(ch == pl.num_programs(1) - 1)
    def _():
        for p in range(s_ref.shape[0]):
            s_out_ref[2 * p] = s_ref[p, 0:HEAD_DIM, 0:HEAD_DIM]
            s_out_ref[2 * p + 1] = s_ref[p, HEAD_DIM:, HEAD_DIM:]


def _rwkv_scan_prompt(r, lw, k, v, a, b, g, lp, layer, n_seq, t):
    m, rw = r.shape
    c = RWKV_CHUNK
    heads = rw // HEAD_DIM
    tok = pl.BlockSpec((c, rw), lambda n, ch: (n * (t // c) + ch, 0))
    vec = pl.BlockSpec((None, 1, rw), lambda n, ch: (layer, 0, 0))
    return pl.pallas_call(
        _scan_body,
        grid=(n_seq, t // c),
        in_specs=[tok] * 7 + [vec] * 3,
        out_specs=[tok, pl.BlockSpec((None, heads, HEAD_DIM, HEAD_DIM), lambda n, ch: (n, 0, 0, 0))],
        out_shape=[jax.ShapeDtypeStruct((m, rw), F32),
                   jax.ShapeDtypeStruct((n_seq, heads, HEAD_DIM, HEAD_DIM), F32)],
        scratch_shapes=[pltpu.VMEM((rw // LANES, LANES, LANES), F32)],
        compiler_params=_cparams(("parallel", "arbitrary")),
        name="rwkv_scan",
    )(r, lw, k, v, a, b, g, lp["ln_w"], lp["ln_b"], lp["r_k"])


def _sample_body(slope_ref, pa_ref, pr_ref, pc_ref,
                 k1_ref, v1_ref, k4_ref, v4_ref, k16_ref, v16_ref,
                 wkv_ref, shift_ref, cst_ref, qg_ref, kg_ref,
                 mu_ref, w0_ref, w2_ref, a0_ref, a2_ref, g2_ref, kk_ref, ka_ref,
                 rk_ref, lnw_ref, lnb_ref, cw_ref, cb_ref,
                 oatt_ref, yr_ref, yc_ref, kn_ref, vn_ref, wkvn_ref, cstn_ref):
    aw = oatt_ref.shape[1]
    rw = yr_ref.shape[1]
    cw = yc_ref.shape[1]
    rows8 = lambda x: jnp.broadcast_to(x, (8, x.shape[1]))

    pa = pa_ref[...]
    q8 = rows8(pa[:, 0:aw])
    k8 = rows8(pa[:, aw:2 * aw])
    v_new = pa[:, 2 * aw:3 * aw]
    qn = q8 * lax.rsqrt(_segsum64(q8 * q8) * (1.0 / HEAD_DIM) + NORM_EPS) * qg_ref[...]
    kn = k8 * lax.rsqrt(_segsum64(k8 * k8) * (1.0 / HEAD_DIM) + NORM_EPS) * kg_ref[...]
    kn_ref[...] = kn[0:1]
    vn_ref[...] = v_new
    scale = HEAD_DIM ** -0.5
    s_new = (_segsum64(kn * qn) * scale)[0:1]
    q1 = qn[0:1]
    slope = slope_ref[...]
    span = k1_ref.shape[0]
    steps = (span - lax.broadcasted_iota(jnp.int32, (span, 1), 0)).astype(F32)
    outs, lses = [], []
    for (window, dil), kc_ref, vc_ref in zip(DILATION_GROUPS, (k1_ref, k4_ref, k16_ref),
                                             (v1_ref, v4_ref, v16_ref)):
        s = _segsum64(kc_ref[...] * q1) * scale - slope * (dil * steps)
        m = jnp.maximum(jnp.max(s, axis=0, keepdims=True), s_new)
        p = jnp.exp(s - m)
        p_new = jnp.exp(s_new - m)
        l = jnp.sum(p, axis=0, keepdims=True) + p_new
        o = (jnp.sum(p * vc_ref[...], axis=0, keepdims=True) + p_new * v_new) / l
        outs.append(o)
        lses.append(m + jnp.log(l))
    mx = jnp.maximum(jnp.maximum(lses[0], lses[1]), lses[2])
    ws = [jnp.exp(l - mx) for l in lses]
    oatt_ref[...] = (outs[0] * ws[0] + outs[1] * ws[1] + outs[2] * ws[2]) / (ws[0] + ws[1] + ws[2])

    pr = rows8(pr_ref[...])
    sh = rows8(shift_ref[...])
    mu = mu_ref[...]
    c3 = 3 * rw
    r, lw, k, v, a, b, g = _rwkv_token_math(
        pr[:, 0:rw], pr[:, rw:2 * rw], pr[:, 2 * rw:c3], pr[:, c3:],
        sh[:, 0:rw], sh[:, rw:2 * rw], sh[:, 2 * rw:c3], sh[:, c3:],
        mu[:, 0:rw], mu[:, rw:2 * rw], mu[:, 2 * rw:c3], mu[:, c3:],
        w0_ref[...], w2_ref[...], a0_ref[...], a2_ref[...], g2_ref[...], kk_ref[...], ka_ref[...])
    decay = jnp.exp(lw)
    er = lax.broadcasted_iota(jnp.int32, (HEAD_DIM, HEAD_DIM), 0)
    ec = lax.broadcasted_iota(jnp.int32, (HEAD_DIM, HEAD_DIM), 1)
    eye = er == ec
    place_r = lax.broadcasted_iota(jnp.int32, (HEAD_DIM, rw), 0)
    place_c = lax.broadcasted_iota(jnp.int32, (HEAD_DIM, rw), 1)
    y_full = jnp.zeros((1, rw), F32)
    for h in range(rw // HEAD_DIM):
        hs = slice(h * HEAD_DIM, (h + 1) * HEAD_DIM)
        s_old = wkv_ref[h]
        sa = jnp.sum(s_old * a[0:1, hs], axis=1, keepdims=True)
        v_col = jnp.sum(jnp.where(eye, v[0:1, hs], 0.0), axis=1, keepdims=True)
        s_h = s_old * decay[0:1, hs] + sa * b[0:1, hs] + v_col * k[0:1, hs]
        wkvn_ref[h] = s_h
        y_col = jnp.sum(s_h * r[0:1, hs], axis=1, keepdims=True)
        y_full = y_full + jnp.sum(jnp.where(place_c == place_r + h * HEAD_DIM, y_col, 0.0),
                                  axis=0, keepdims=True)
    y8 = rows8(y_full)
    yr_ref[...] = _rwkv_post(y8, r, k, v, g, lnw_ref[...], lnb_ref[...], rk_ref[...])[0:1]

    pc = pc_ref[...]
    z = pc[:, 2 * cw:] * pc[:, 0:cw]
    yc = cb_ref[...] + cw_ref[0:1, :] * cst_ref[0:1, :]
    yc = yc + cw_ref[1:2, :] * cst_ref[1:2, :]
    yc = yc + cw_ref[2:3, :] * z
    yc_ref[...] = pc[:, cw:2 * cw] * yc
    cstn_ref[0:1, :] = cst_ref[1:2, :]
    cstn_ref[1:2, :] = z


def _sample_mixers(p_att, p_rwkv, p_conv, cache_k, cache_v, state_wkv, state_shift, state_conv,
                   lp, slope_lanes, layer):
    nb = p_att.shape[0]
    aw = p_att.shape[1] // 3
    rw = lp["rwkv_w"]
    cw = p_conv.shape[1] // 3
    past = cache_k.shape[2]
    heads = rw // HEAD_DIM
    depth = cache_k.shape[0]
    span = ATT_SPAN

    row = lambda w: pl.BlockSpec((None, 1, w), lambda n: (n, 0, 0))
    lvec = lambda w: pl.BlockSpec((None, 1, w), lambda n: (layer, 0, 0))
    lmat = lambda r_, w: pl.BlockSpec((None, r_, w), lambda n: (layer, 0, 0))

    def cache_spec(dil):
        return pl.BlockSpec((None, None, span, aw), lambda n: (layer, n, past // (dil * span) - 1, 0))

    views = []
    for _, dil in DILATION_GROUPS:
        assert past % (dil * span) == 0
        shape = (depth, nb, past // dil, dil * aw)
        views += [cache_k.reshape(shape), cache_v.reshape(shape)]
    cache_specs = []
    for _, dil in DILATION_GROUPS:
        cache_specs += [cache_spec(dil), cache_spec(dil)]

    outs = pl.pallas_call(
        _sample_body,
        grid=(nb,),
        in_specs=[pl.BlockSpec((1, aw), lambda n: (0, 0)),
                  row(3 * aw), row(p_rwkv.shape[1]), row(3 * cw)] + cache_specs + [
                  pl.BlockSpec((None, None, heads, HEAD_DIM, HEAD_DIM), lambda n: (layer, n, 0, 0, 0)),
                  pl.BlockSpec((None, None, 1, p_rwkv.shape[1]), lambda n: (layer, n, 0, 0)),
                  pl.BlockSpec((None, None, CONV_K - 1, cw), lambda n: (layer, n, 0, 0)),
                  lvec(aw), lvec(aw),
                  lvec(p_rwkv.shape[1]), lvec(rw), lmat(LORA_W, rw), lvec(rw), lmat(LORA_W, rw),
                  lmat(LORA_W, rw), lvec(rw), lvec(rw), lvec(rw), lvec(rw), lvec(rw),
                  lmat(CONV_K, cw), lvec(cw)],
        out_specs=[row(aw), row(rw), row(cw), row(aw), row(aw),
                   pl.BlockSpec((None, heads, HEAD_DIM, HEAD_DIM), lambda n: (n, 0, 0, 0)),
                   pl.BlockSpec((None, CONV_K - 1, cw), lambda n: (n, 0, 0))],
        out_shape=[jax.ShapeDtypeStruct((nb, 1, aw), F32), jax.ShapeDtypeStruct((nb, 1, rw), F32),
                   jax.ShapeDtypeStruct((nb, 1, cw), F32), jax.ShapeDtypeStruct((nb, 1, aw), F32),
                   jax.ShapeDtypeStruct((nb, 1, aw), F32),
                   jax.ShapeDtypeStruct((nb, heads, HEAD_DIM, HEAD_DIM), F32),
                   jax.ShapeDtypeStruct((nb, CONV_K - 1, cw), F32)],
        compiler_params=_cparams(("parallel",)),
        name="sample_mixers",
    )(slope_lanes, p_att[:, None, :], p_rwkv[:, None, :], p_conv[:, None, :], *views,
      state_wkv, state_shift[:, :, None, :], state_conv,
      lp["q_gain_full"], lp["k_gain_full"],
      lp["mu"], lp["w0"], lp["w2p"], lp["a0"], lp["a2p"], lp["g2p"], lp["k_k"], lp["k_a"],
      lp["r_k"], lp["ln_w"], lp["ln_b"], lp["conv_w"], lp["conv_b"])
    o_att, y_rwkv, y_conv, k_new, v_new, wkv_new, conv_new = outs
    return o_att[:, 0], y_rwkv[:, 0], y_conv[:, 0], k_new[:, 0], v_new[:, 0], wkv_new, conv_new


def kernel(x_prompt, x_sample, cache_swa_k, cache_swa_v, state_wkv, state_shift, state_conv, ffn1_norm, ffn1_w_gate, ffn1_w_up, ffn1_w_down, mix_norm, w_in, q_norm, k_norm, rwkv_mu, rwkv_w0, rwkv_w2, rwkv_a0, rwkv_a2, rwkv_g2, rwkv_k_k, rwkv_k_a, rwkv_r_k, rwkv_ln_w, rwkv_ln_b, conv_w, conv_b, w_out, ffn2_norm, ffn2_w_gate, ffn2_w_up, ffn2_w_down):
    n_seq, t, d = x_prompt.shape
    nb = x_sample.shape[0]
    assert x_sample.shape[1] == 1
    depth = w_in.shape[0]
    att_heads, head_dim = cache_swa_k.shape[3], cache_swa_k.shape[4]
    assert head_dim == HEAD_DIM
    aw = att_heads * HEAD_DIM
    rw = rwkv_w0.shape[1]
    cw = conv_b.shape[1]
    rp = rwkv_mu.shape[1]
    assert rp == 3 * rw + LORA_W and w_in.shape[2] == 3 * aw + rp + 3 * cw
    heads = rw // HEAD_DIM

    row3 = lambda p: p.reshape(depth, 1, -1)
    b16 = lambda p: p.astype(BF16)
    wg1, wu1, wd1 = b16(ffn1_w_gate), b16(ffn1_w_up), b16(ffn1_w_down)
    wg2, wu2, wd2 = b16(ffn2_w_gate), b16(ffn2_w_up), b16(ffn2_w_down)
    w_in_att = b16(w_in[:, :, :3 * aw])
    w_in_rwkv = b16(w_in[:, :, 3 * aw:3 * aw + rp])
    w_in_conv = b16(w_in[:, :, 3 * aw + rp:])
    w_out16 = b16(w_out)
    zpad = lambda w, before: jnp.pad(w, ((0, 0), (before, LORA_W - before - w.shape[1]), (0, 0)))
    lp = {
        "rwkv_w": rw,
        "mu": row3(rwkv_mu), "w0": row3(rwkv_w0), "a0": row3(rwkv_a0),
        "w2p": zpad(rwkv_w2, 0), "a2p": zpad(rwkv_a2, W_LORA), "g2p": zpad(rwkv_g2, W_LORA + A_LORA),
        "k_k": row3(rwkv_k_k), "k_a": row3(rwkv_k_a), "r_k": row3(rwkv_r_k),
        "ln_w": row3(rwkv_ln_w), "ln_b": row3(rwkv_ln_b),
        "conv_w": conv_w, "conv_b": row3(conv_b),
        "q_gain_pair": row3(jnp.tile(q_norm, (1, LANES // HEAD_DIM))),
        "k_gain_pair": row3(jnp.tile(k_norm, (1, LANES // HEAD_DIM))),
        "q_gain_full": row3(jnp.tile(q_norm, (1, att_heads))),
        "k_gain_full": row3(jnp.tile(k_norm, (1, att_heads))),
    }
    g_ffn1, g_mix, g_ffn2 = row3(ffn1_norm), row3(mix_norm), row3(ffn2_norm)
    slopes = 2.0 ** (-8.0 * jnp.arange(1, att_heads + 1, dtype=F32) / att_heads)
    slope_lanes = jnp.repeat(slopes, HEAD_DIM)[None, :]

    m = n_seq * t
    xp = x_prompt.reshape(m, d)
    xs = x_sample.reshape(nb, d)
    tm_p, tf = 512, 512
    p_states, s_states = [], []
    for l in range(depth):
        xp = _ffn(xp, g_ffn1, wg1, wu1, wd1, l, tm_p, tf)
        pa = _norm_matmul(xp, g_mix, w_in_att, l, tm_p, 3 * aw // 2)
        pr = _norm_matmul(xp, g_mix, w_in_rwkv, l, tm_p, rp // 2)
        pc = _norm_matmul(xp, g_mix, w_in_conv, l, tm_p, 3 * cw // 2)
        o_att, k_rows, v_rows = _attention_prompt(pa, lp["q_gain_pair"], lp["k_gain_pair"], slopes, l, n_seq, t)
        r_, lw_, k_, v_, a_, b_, g_ = _rwkv_prep_prompt(pr, lp, l, t, 256)
        y_rwkv, wkv_p = _rwkv_scan_prompt(r_, lw_, k_, v_, a_, b_, g_, lp, l, n_seq, t)
        tc = 512
        y_conv, z_last = _conv_prompt(pc, lp["conv_w"], lp["conv_b"], l, t, tc)
        xp = _out_proj(xp, o_att, y_rwkv, y_conv, w_out16, l, tm_p, 512)
        xp = _ffn(xp, g_ffn2, wg2, wu2, wd2, l, tm_p, tf)
        p_states.append((
            k_rows.reshape(n_seq, t, att_heads, HEAD_DIM), v_rows.reshape(n_seq, t, att_heads, HEAD_DIM),
            wkv_p, pr.reshape(n_seq, t, rp)[:, -1],
            z_last.reshape(n_seq, t // tc, 8, cw)[:, -1, 8 - (CONV_K - 1):]))

        xs = _ffn(xs, g_ffn1, wg1, wu1, wd1, l, nb, tf)
        sa = _norm_matmul(xs, g_mix, w_in_att, l, nb, 3 * aw // 2)
        sr = _norm_matmul(xs, g_mix, w_in_rwkv, l, nb, rp // 2)
        sc = _norm_matmul(xs, g_mix, w_in_conv, l, nb, 3 * cw // 2)
        so, sy, syc, sk, sv, swkv, sconv = _sample_mixers(
            sa, sr, sc, cache_swa_k, cache_swa_v, state_wkv, state_shift, state_conv, lp, slope_lanes, l)
        xs = _out_proj(xs, so, sy, syc, w_out16, l, nb, 512)
        xs = _ffn(xs, g_ffn2, wg2, wu2, wd2, l, nb, tf)
        s_states.append((sk.reshape(nb, 1, att_heads, HEAD_DIM), sv.reshape(nb, 1, att_heads, HEAD_DIM),
                         swkv, sr, sconv))

    stacked = lambda states, i: jnp.stack([st[i] for st in states], axis=0)
    return (xp.reshape(n_seq, t, d), xs.reshape(nb, 1, d),
            stacked(p_states, 0), stacked(p_states, 1), stacked(p_states, 2), stacked(p_states, 3),
            stacked(p_states, 4),
            stacked(s_states, 0), stacked(s_states, 1), stacked(s_states, 2), stacked(s_states, 3),
            stacked(s_states, 4))
```

```python
import functools

import jax
import jax.numpy as jnp
from jax import lax
from jax.experimental import pallas as pl
from jax.experimental.pallas import tpu as pltpu

F32 = jnp.float32
BF16 = jnp.bfloat16

HEAD_DIM = 64
HEAD_SHIFT = 6
LANES = 128
NORM_EPS = 1e-6
RWKV_GN_EPS = 64e-5
DILATION_GROUPS = ((128, 1), (512, 4), (2048, 16))
ATT_SPAN = 128
W_LORA, A_LORA, G_LORA = 64, 64, 128
LORA_W = W_LORA + A_LORA + G_LORA
CONV_K = 3
CONV_TILE = 256
RWKV_CHUNK = 64
VMEM_LIMIT = 56 * 1024 * 1024
NEG_BIG = -1e30

_NT = (((1,), (1,)), ((), ()))
_NN = (((1,), (0,)), ((), ()))
_BNN = (((2,), (1,)), ((0,), (0,)))
_BNT = (((2,), (2,)), ((0,), (0,)))
_BTN = (((1,), (1,)), ((0,), (0,)))


def _cparams(sem):
    return pltpu.CompilerParams(dimension_semantics=sem, vmem_limit_bytes=VMEM_LIMIT)


def _rms(x, g):
    return x * lax.rsqrt(jnp.mean(x * x, axis=-1, keepdims=True) + NORM_EPS) * g


def _split2(x):
    hi = x.astype(BF16)
    lo = (x - hi.astype(F32)).astype(BF16)
    return hi, lo


def _dot3(a, b, dims=_NN):
    ah, al = _split2(a)
    bh, bl = _split2(b)
    d = lambda x, y: lax.dot_general(x, y, dims, preferred_element_type=F32)
    return d(ah, bh) + (d(al, bh) + d(ah, bl))


def _split3(x):
    h1 = x.astype(BF16)
    r1 = x - h1.astype(F32)
    h2 = r1.astype(BF16)
    h3 = (r1 - h2.astype(F32)).astype(BF16)
    return h1, h2, h3


def _dot_sel(x, sel):
    return sum(jnp.dot(h, sel, preferred_element_type=F32) for h in _split3(x))


def _cumsum_rows(x, tri_incl):
    return sum(jnp.dot(tri_incl, h, preferred_element_type=F32) for h in _split3(x))


def _head_blockdiag(n, dtype):
    r = lax.broadcasted_iota(jnp.int32, (n, n), 0) >> HEAD_SHIFT
    c = lax.broadcasted_iota(jnp.int32, (n, n), 1) >> HEAD_SHIFT
    return (r == c).astype(dtype)


def _segsum64(x):
    bd = _head_blockdiag(LANES, BF16)
    parts = [_dot_sel(x[:, i:i + LANES], bd) for i in range(0, x.shape[1], LANES)]
    return parts[0] if len(parts) == 1 else jnp.concatenate(parts, axis=1)


def _ffn_body(x_ref, g_ref, wg_ref, wu_ref, wd_ref, o_ref, xn_ref, acc_ref):
    j = pl.program_id(1)

    @pl.when(j == 0)
    def _():
        xn_ref[...] = _rms(x_ref[...], g_ref[...]).astype(BF16)
        acc_ref[...] = jnp.zeros_like(acc_ref)

    xn = xn_ref[...]
    h = jnp.dot(xn, wg_ref[...], preferred_element_type=F32)
    u = jnp.dot(xn, wu_ref[...], preferred_element_type=F32)
    act = (h * jax.nn.sigmoid(h) * u).astype(BF16)
    acc_ref[...] += jnp.dot(act, wd_ref[...], preferred_element_type=F32)

    @pl.when(j == pl.num_programs(1) - 1)
    def _():
        o_ref[...] = x_ref[...] + 0.5 * acc_ref[...]


def _ffn(x, gain, wg, wu, wd, layer, tm, tf):
    m, d = x.shape
    f = wg.shape[2]
    return pl.pallas_call(
        _ffn_body,
        grid=(m // tm, f // tf),
        in_specs=[
            pl.BlockSpec((tm, d), lambda i, j: (i, 0)),
            pl.BlockSpec((None, 1, d), lambda i, j: (layer, 0, 0)),
            pl.BlockSpec((None, d, tf), lambda i, j: (layer, 0, j)),
            pl.BlockSpec((None, d, tf), lambda i, j: (layer, 0, j)),
            pl.BlockSpec((None, tf, d), lambda i, j: (layer, j, 0)),
        ],
        out_specs=pl.BlockSpec((tm, d), lambda i, j: (i, 0)),
        out_shape=jax.ShapeDtypeStruct((m, d), F32),
        scratch_shapes=[pltpu.VMEM((tm, d), BF16), pltpu.VMEM((tm, d), F32)],
        compiler_params=_cparams(("parallel", "arbitrary")),
        name="ffn",
    )(x, gain, wg, wu, wd)


def _norm_matmul_body(x_ref, g_ref, w_ref, o_ref, xn_ref):
    @pl.when(pl.program_id(1) == 0)
    def _():
        xn_ref[...] = _rms(x_ref[...], g_ref[...]).astype(BF16)

    o_ref[...] = jnp.dot(xn_ref[...], w_ref[...], preferred_element_type=F32)


def _norm_matmul(x, gain, w, layer, tm, tn):
    m, d = x.shape
    n = w.shape[2]
    return pl.pallas_call(
        _norm_matmul_body,
        grid=(m // tm, n // tn),
        in_specs=[
            pl.BlockSpec((tm, d), lambda i, j: (i, 0)),
            pl.BlockSpec((None, 1, d), lambda i, j: (layer, 0, 0)),
            pl.BlockSpec((None, d, tn), lambda i, j: (layer, 0, j)),
        ],
        out_specs=pl.BlockSpec((tm, tn), lambda i, j: (i, j)),
        out_shape=jax.ShapeDtypeStruct((m, n), F32),
        scratch_shapes=[pltpu.VMEM((tm, d), BF16)],
        compiler_params=_cparams(("parallel", "arbitrary")),
        name="norm_matmul",
    )(x, gain, w)


def _out_proj_body(x_ref, a_ref, r_ref, c_ref, w_ref, o_ref):
    wa = a_ref.shape[1]
    wr = r_ref.shape[1]
    acc = jnp.dot(a_ref[...].astype(BF16), w_ref[0:wa, :], preferred_element_type=F32)
    acc += jnp.dot(r_ref[...].astype(BF16), w_ref[wa:wa + wr, :], preferred_element_type=F32)
    acc += jnp.dot(c_ref[...].astype(BF16), w_ref[wa + wr:, :], preferred_element_type=F32)
    o_ref[...] = x_ref[...] + acc


def _out_proj(x, o_att, y_rwkv, y_conv, w, layer, tm, tn):
    m, d = x.shape
    return pl.pallas_call(
        _out_proj_body,
        grid=(m // tm, d // tn),
        in_specs=[
            pl.BlockSpec((tm, tn), lambda i, j: (i, j)),
            pl.BlockSpec((tm, o_att.shape[1]), lambda i, j: (i, 0)),
            pl.BlockSpec((tm, y_rwkv.shape[1]), lambda i, j: (i, 0)),
            pl.BlockSpec((tm, y_conv.shape[1]), lambda i, j: (i, 0)),
            pl.BlockSpec((None, w.shape[1], tn), lambda i, j: (layer, 0, j)),
        ],
        out_specs=pl.BlockSpec((tm, tn), lambda i, j: (i, j)),
        out_shape=jax.ShapeDtypeStruct((m, d), F32),
        compiler_params=_cparams(("parallel", "arbitrary")),
        name="out_proj",
    )(x, o_att, y_rwkv, y_conv, w)


def _pair_headnorm(x, g, lo_half):
    x2 = x * x
    s0 = jnp.sum(jnp.where(lo_half, x2, 0.0), axis=-1, keepdims=True)
    s1 = jnp.sum(jnp.where(lo_half, 0.0, x2), axis=-1, keepdims=True)
    ms = jnp.where(lo_half, s0, s1) * (1.0 / HEAD_DIM)
    return x * lax.rsqrt(ms + NORM_EPS) * g


def _attn_body(slopes_ref, q_ref, k_ref, v_ref, qg_ref, kg_ref,
               o_ref, qs_ref, kn_ref, bias_ref, og_ref, lse_ref):
    hp = pl.program_id(1)
    t = q_ref.shape[0]
    span = ATT_SPAN
    lo_half = lax.broadcasted_iota(jnp.int32, (1, LANES), 1) < HEAD_DIM

    kn_ref[...] = _pair_headnorm(k_ref[...], kg_ref[...], lo_half)
    qs_ref[...] = _pair_headnorm(q_ref[...], qg_ref[...], lo_half) * (HEAD_DIM ** -0.5)

    qi = lax.broadcasted_iota(jnp.int32, (span, 2 * span), 0)
    ki = lax.broadcasted_iota(jnp.int32, (span, 2 * span), 1)
    steps = qi + span - ki
    valid = (steps >= 0) & (steps <= span)
    for g, (_, dil) in enumerate(DILATION_GROUPS):
        dist = (steps * dil).astype(F32)
        for h in range(2):
            bias_ref[g, h * span:(h + 1) * span, :] = jnp.where(valid, -(slopes_ref[2 * hp + h] * dist), NEG_BIG)

    def unit(g, dil, start, has_prev):
        rows = pl.ds(start, span, stride=dil)
        qb = qs_ref[rows, :]
        q2 = jnp.concatenate([jnp.where(lo_half, qb, 0.0), jnp.where(lo_half, 0.0, qb)], axis=0).astype(BF16)
        if has_prev:
            prow = pl.ds(start - span * dil, span, stride=dil)
            kb = jnp.concatenate([kn_ref[prow, :], kn_ref[rows, :]], axis=0)
            vb = jnp.concatenate([v_ref[prow, :], v_ref[rows, :]], axis=0)
            bias = bias_ref[g]
        else:
            kb = kn_ref[rows, :]
            vb = v_ref[rows, :]
            bias = bias_ref[g, :, span:]
        s = lax.dot_general(q2, kb.astype(BF16), _NT, preferred_element_type=F32) + bias
        m = jnp.max(s, axis=-1, keepdims=True)
        p = jnp.exp(s - m)
        l = jnp.sum(p, axis=-1, keepdims=True)
        o = jnp.dot(p.astype(BF16), vb.astype(BF16), preferred_element_type=F32) / l
        lse = m + jnp.log(l)
        og_ref[g, rows, :] = jnp.where(lo_half, o[0:span], o[span:])
        lse_ref[g, rows, :] = jnp.where(lo_half, lse[0:span], lse[span:])

    for g, (window, dil) in enumerate(DILATION_GROUPS):
        assert window // dil == span
        nb = t // (span * dil)

        def residue(r, carry, g=g, dil=dil, nb=nb):
            unit(g, dil, r, False)
            if nb > 1:
                def blk(b, c):
                    unit(g, dil, b * (span * dil) + r, True)
                    return c
                lax.fori_loop(1, nb, blk, 0, unroll=3 if nb <= 4 else 5)
            return carry

        if dil == 1:
            residue(0, 0)
        else:
            lax.fori_loop(0, dil, residue, 0, unroll=4 if nb == 1 else 1)

    l0, l1, l2 = lse_ref[0], lse_ref[1], lse_ref[2]
    mx = jnp.maximum(jnp.maximum(l0, l1), l2)
    w0, w1, w2 = jnp.exp(l0 - mx), jnp.exp(l1 - mx), jnp.exp(l2 - mx)
    o_ref[...] = (og_ref[0] * w0 + og_ref[1] * w1 + og_ref[2] * w2) / (w0 + w1 + w2)


def _attention_prompt(p, att_w, qg, kg, slopes, layer, n_seq, t):
    m = p.shape[0]
    pairs = att_w // LANES
    blk = lambda off: pl.BlockSpec((t, LANES), lambda n, hp: (n, off + hp))
    gain = pl.BlockSpec((None, 1, LANES), lambda n, hp: (layer, 0, 0))
    return pl.pallas_call(
        _attn_body,
        grid=(n_seq, pairs),
        in_specs=[pl.BlockSpec(memory_space=pltpu.SMEM),
                  blk(0), blk(pairs), blk(2 * pairs), gain, gain],
        out_specs=blk(0),
        out_shape=jax.ShapeDtypeStruct((m, att_w), F32),
        scratch_shapes=[pltpu.VMEM((t, LANES), F32),
                        pltpu.VMEM((t, LANES), F32),
                        pltpu.VMEM((3, 2 * ATT_SPAN, 2 * ATT_SPAN), F32),
                        pltpu.VMEM((3, t, LANES), F32),
                        pltpu.VMEM((3, t, LANES), F32)],
        compiler_params=_cparams(("parallel", "arbitrary")),
        name="attn_prompt",
    )(slopes, p, p, p, qg, kg)


def _kv_rows_body(k_ref, v_ref, kg_ref, ko_ref, vo_ref):
    k = k_ref[...]
    kn = k * lax.rsqrt(_segsum64(k * k) * (1.0 / HEAD_DIM) + NORM_EPS) * kg_ref[...]
    v = v_ref[...]
    for h in range(ko_ref.shape[1]):
        hs = slice(h * HEAD_DIM, (h + 1) * HEAD_DIM)
        ko_ref[:, h, :] = kn[:, hs]
        vo_ref[:, h, :] = v[:, hs]


def _kv_rows_prompt(p, att_w, kg_full, layer, tm):
    m = p.shape[0]
    heads = att_w // HEAD_DIM
    out = jax.ShapeDtypeStruct((m, heads, HEAD_DIM), F32)
    return pl.pallas_call(
        _kv_rows_body,
        grid=(m // tm,),
        in_specs=[pl.BlockSpec((tm, att_w), lambda i: (i, 1)),
                  pl.BlockSpec((tm, att_w), lambda i: (i, 2)),
                  pl.BlockSpec((None, 1, att_w), lambda i: (layer, 0, 0))],
        out_specs=[pl.BlockSpec((tm, heads, HEAD_DIM), lambda i: (i, 0, 0))] * 2,
        out_shape=[out, out],
        compiler_params=_cparams(("parallel",)),
        name="kv_rows",
    )(p, p, kg_full)


def _softplus(z):
    return jnp.maximum(z, 0.0) + jnp.log(1.0 + jnp.exp(-jnp.abs(z)))


def _rwkv_token_math(p_r, p_k, p_v, p_l, q_r, q_k, q_v, q_l, mu_r, mu_k, mu_v, mu_l,
                     w0, w2p, a0, a2p, g2p, k_k, k_a):
    xr = p_r + (q_r - p_r) * mu_r
    xk = p_k + (q_k - p_k) * mu_k
    xv = p_v + (q_v - p_v) * mu_v
    xl = p_l + (q_l - p_l) * mu_l
    w_log = -_softplus(-(w0 + _dot3(jnp.tanh(xl), w2p))) - 0.5
    log_decay = -jnp.exp(w_log)
    sig_l = jax.nn.sigmoid(xl)
    a_gate = jax.nn.sigmoid(a0 + _dot3(xl, a2p))
    g = _dot3(sig_l, g2p)
    kk = xk * k_k
    kk = kk / jnp.maximum(jnp.sqrt(_segsum64(kk * kk)), 1e-12)
    kmod = xk * (1.0 + (a_gate - 1.0) * k_a)
    return xr, log_decay, kmod, xv, -kk, kk * a_gate, g


def _rwkv_post(y, r, k, v, g, ln_w, ln_b, r_k):
    mu_y = _segsum64(y) * (1.0 / HEAD_DIM)
    yc = y - mu_y
    var_y = _segsum64(yc * yc) * (1.0 / HEAD_DIM)
    yn = yc * lax.rsqrt(var_y + RWKV_GN_EPS) * ln_w + ln_b
    bonus = _segsum64(r * k * r_k) * v
    return (yn + bonus) * g


def _mixprep_body(pr_ref, pk_ref, pv_ref, plo_ref, qr_ref, qk_ref, qv_ref, qlo_ref,
                  mur_ref, muk_ref, muv_ref, mul_ref, w0_ref, w2_ref, a0_ref, a2_ref, g2_ref,
                  kk_ref, ka_ref,
                  r_out, lw_out, k_out, v_out, a_out, b_out, g_out,
                  sr_ref, sk_ref, sv_ref, sl_ref, *, blocks_per_seq):
    tm = pr_ref.shape[0]
    first = (pl.program_id(0) % blocks_per_seq) == 0

    def shifted(cur_ref, prev8_ref, s_ref):
        s_ref[0:8, :] = jnp.where(first, 0.0, prev8_ref[...])
        s_ref[8:, :] = cur_ref[...]
        return s_ref[pl.ds(7, tm), :]

    outs = _rwkv_token_math(
        pr_ref[...], pk_ref[...], pv_ref[...], plo_ref[...],
        shifted(pr_ref, qr_ref, sr_ref), shifted(pk_ref, qk_ref, sk_ref),
        shifted(pv_ref, qv_ref, sv_ref), shifted(plo_ref, qlo_ref, sl_ref),
        mur_ref[...], muk_ref[...], muv_ref[...], mul_ref[...],
        w0_ref[...], w2_ref[...], a0_ref[...], a2_ref[...], g2_ref[...], kk_ref[...], ka_ref[...])
    for ref, val in zip((r_out, lw_out, k_out, v_out, a_out, b_out, g_out), outs):
        ref[...] = val


def _rwkv_prep_prompt(p, col0, lp, layer, t, tm):
    m = p.shape[0]
    rw = lp["rwkv_w"]
    assert tm % 8 == 0 and t % tm == 0 and col0 % rw == 0 and (col0 + 3 * rw) % LORA_W == 0
    c_slab = col0 // rw
    c_lora = (col0 + 3 * rw) // LORA_W
    cur = lambda w, c: pl.BlockSpec((tm, w), lambda i: (i, c))
    prev = lambda w, c: pl.BlockSpec((8, w), lambda i: (jnp.maximum(i * (tm // 8) - 1, 0), c))
    vec = lambda w, c: pl.BlockSpec((None, 1, w), lambda i: (layer, 0, c))
    mat = pl.BlockSpec((None, LORA_W, rw), lambda i: (layer, 0, 0))
    out = jax.ShapeDtypeStruct((m, rw), F32)
    return pl.pallas_call(
        functools.partial(_mixprep_body, blocks_per_seq=t // tm),
        grid=(m // tm,),
        in_specs=[cur(rw, c_slab), cur(rw, c_slab + 1), cur(rw, c_slab + 2), cur(LORA_W, c_lora),
                  prev(rw, c_slab), prev(rw, c_slab + 1), prev(rw, c_slab + 2), prev(LORA_W, c_lora),
                  vec(rw, 0), vec(rw, 1), vec(rw, 2), vec(LORA_W, (3 * rw) // LORA_W),
                  vec(rw, 0), mat, vec(rw, 0), mat, mat, vec(rw, 0), vec(rw, 0)],
        out_specs=[pl.BlockSpec((tm, rw), lambda i: (i, 0))] * 7,
        out_shape=[out] * 7,
        scratch_shapes=[pltpu.VMEM((tm + 8, rw), F32)] * 3 + [pltpu.VMEM((tm + 8, LORA_W), F32)],
        compiler_params=_cparams(("parallel",)),
        name="rwkv_prep",
    )(p, p, p, p, p, p, p, p,
      lp["mu"], lp["mu"], lp["mu"], lp["mu"],
      lp["w0"], lp["w2p"], lp["a0"], lp["a2p"], lp["g2p"], lp["k_k"], lp["k_a"])


def _conv_body(u_ref, gb_ref, gc_ref, pu_ref, pgc_ref, w_ref, b_ref, y_ref, zl_ref, zc_ref,
               *, blocks_per_seq):
    tm = u_ref.shape[0]
    first = (pl.program_id(0) % blocks_per_seq) == 0
    z = gc_ref[...] * u_ref[...]
    zc_ref[0:8, :] = jnp.where(first, 0.0, pgc_ref[...] * pu_ref[...])
    zc_ref[8:, :] = z
    yc = b_ref[...] + w_ref[0:1, :] * zc_ref[pl.ds(6, tm), :]
    yc = yc + w_ref[1:2, :] * zc_ref[pl.ds(7, tm), :]
    yc = yc + w_ref[2:3, :] * z
    y_ref[...] = gb_ref[...] * yc
    zl_ref[...] = zc_ref[pl.ds(tm, 8), :]


def _conv_prompt(p, col0, cw, conv_w, conv_b, layer, t, tm):
    m = p.shape[0]
    tc = CONV_TILE
    assert col0 % tc == 0 and cw % tc == 0
    c0, nc = col0 // tc, cw // tc
    cur = lambda s: pl.BlockSpec((tm, tc), lambda i, j: (i, c0 + s * nc + j))
    prev = lambda s: pl.BlockSpec((8, tc), lambda i, j: (jnp.maximum(i * (tm // 8) - 1, 0), c0 + s * nc + j))
    return pl.pallas_call(
        functools.partial(_conv_body, blocks_per_seq=t // tm),
        grid=(m // tm, nc),
        in_specs=[cur(0), cur(1), cur(2), prev(0), prev(2),
                  pl.BlockSpec((None, CONV_K, tc), lambda i, j: (layer, 0, j)),
                  pl.BlockSpec((None, 1, tc), lambda i, j: (layer, 0, j))],
        out_specs=[pl.BlockSpec((tm, tc), lambda i, j: (i, j)), pl.BlockSpec((8, tc), lambda i, j: (i, j))],
        out_shape=[jax.ShapeDtypeStruct((m, cw), F32), jax.ShapeDtypeStruct((m // tm * 8, cw), F32)],
        scratch_shapes=[pltpu.VMEM((tm + 8, tc), F32)],
        compiler_params=_cparams(("parallel", "parallel")),
        name="conv_prompt",
    )(p, p, p, p, p, conv_w, conv_b)


def _scan_chunk(r, lw, cum, k, v, a, b, s0):
    c = r.shape[1]
    c2 = 2 * c
    lane = lax.broadcasted_iota(jnp.int32, (1, 1, LANES), 2)
    m0 = (lane < HEAD_DIM).astype(F32)
    m1 = 1.0 - m0
    stack = lambda x: jnp.concatenate([x * m0, x * m1], axis=1)

    cum_prev = cum - lw
    cum_last = cum[:, c - 1:c, :]
    e_pos = jnp.exp(cum)
    e_neg = jnp.exp(-cum)
    e_rem = jnp.exp(cum_last - cum)
    at = a * jnp.exp(cum_prev)
    rt = r * e_pos
    bt = b * e_neg
    kt = k * e_neg
    bh = b * e_rem
    kh = k * e_rem

    big = _dot3(jnp.concatenate([stack(at), stack(rt)], axis=1),
                jnp.concatenate([stack(bt), stack(kt)], axis=1), _BNT)
    row = lax.broadcasted_iota(jnp.int32, (1, c2, c2), 1)
    col = lax.broadcasted_iota(jnp.int32, (1, c2, c2), 2)
    assert c == HEAD_DIM
    same_head = (row >> HEAD_SHIFT) == (col >> HEAD_SHIFT)
    tr, tc = row & (c - 1), col & (c - 1)
    strict = same_head & (tr > tc)
    incl = same_head & (tr >= tc)
    lmat = jnp.where(strict, big[:, 0:c2, 0:c2], 0.0)
    ak = jnp.where(strict, big[:, 0:c2, c2:], 0.0)
    rb = jnp.where(incl, big[:, c2:, 0:c2], 0.0)
    rk = jnp.where(incl, big[:, c2:, c2:], 0.0)

    eye = (row == col).astype(F32)
    x = eye + jnp.where((row >> 1) == (col >> 1), lmat, 0.0)
    s, log2s = 2, 1
    while s < c:
        joins = ((row >> (log2s + 1)) == (col >> (log2s + 1))) & ((row & s) != 0) & ((col & s) == 0)
        x = x + _dot3(_dot3(x, jnp.where(joins, lmat, 0.0), _BNN), x, _BNN)
        s, log2s = 2 * s, log2s + 1

    v_st = stack(v)
    w = _dot3(jnp.concatenate([at, rt], axis=1), s0, _BNT)
    u_st = _dot3(x, stack(w[:, 0:c]) + _dot3(ak, v_st, _BNN), _BNN)
    y_st = _dot3(jnp.concatenate([rb, rk], axis=2), jnp.concatenate([u_st, v_st], axis=1), _BNN)
    y = w[:, c:] + y_st[:, 0:c] + y_st[:, c:]
    u = u_st[:, 0:c] + u_st[:, c:]
    upd = _dot3(jnp.concatenate([u, v], axis=1), jnp.concatenate([bh, kh], axis=1), _BTN)
    s_new = s0 * jnp.exp(cum_last) + _head_blockdiag(LANES, F32)[None] * upd
    return y, s_new


def _scan_body(r_ref, lw_ref, k_ref, v_ref, a_ref, b_ref, g_ref, lnw_ref, lnb_ref, rk_ref,
               y_ref, s_out_ref, s_ref):
    ch = pl.program_id(1)

    @pl.when(ch == 0)
    def _():
        s_ref[...] = jnp.zeros_like(s_ref)

    npair = s_ref.shape[0]
    c = r_ref.shape[0]
    pairs = lambda x: jnp.stack([x[:, p * LANES:(p + 1) * LANES] for p in range(npair)], axis=0)
    tri_incl = (lax.broadcasted_iota(jnp.int32, (c, c), 0)
                >= lax.broadcasted_iota(jnp.int32, (c, c), 1)).astype(BF16)
    r, lw, k, v = r_ref[...], lw_ref[...], k_ref[...], v_ref[...]
    cum = _cumsum_rows(lw, tri_incl)
    y, s_new = _scan_chunk(pairs(r), pairs(lw), pairs(cum), pairs(k), pairs(v),
                           pairs(a_ref[...]), pairs(b_ref[...]), s_ref[...])
    s_ref[...] = s_new
    y = jnp.concatenate([y[p] for p in range(npair)], axis=1)
    y_ref[...] = _rwkv_post(y, r, k, v, g_ref[...], lnw_ref[...], lnb_ref[...], rk_ref[...])

    @pl.when(ch == pl.num_programs(1) - 1)
    def _():
        for p in range(npair):
            s_out_ref[2 * p] = s_ref[p, 0:HEAD_DIM, 0:HEAD_DIM]
            s_out_ref[2 * p + 1] = s_ref[p, HEAD_DIM:, HEAD_DIM:]


def _rwkv_scan_prompt(r, lw, k, v, a, b, g, lp, layer, n_seq, t):
    m, rw = r.shape
    c = RWKV_CHUNK
    heads = rw // HEAD_DIM
    tok = pl.BlockSpec((c, rw), lambda n, ch: (n * (t // c) + ch, 0))
    vec = pl.BlockSpec((None, 1, rw), lambda n, ch: (layer, 0, 0))
    return pl.pallas_call(
        _scan_body,
        grid=(n_seq, t // c),
        in_specs=[tok] * 7 + [vec] * 3,
        out_specs=[tok, pl.BlockSpec((None, heads, HEAD_DIM, HEAD_DIM), lambda n, ch: (n, 0, 0, 0))],
        out_shape=[jax.ShapeDtypeStruct((m, rw), F32),
                   jax.ShapeDtypeStruct((n_seq, heads, HEAD_DIM, HEAD_DIM), F32)],
        scratch_shapes=[pltpu.VMEM((rw // LANES, LANES, LANES), F32)],
        compiler_params=_cparams(("parallel", "arbitrary")),
        name="rwkv_scan",
    )(r, lw, k, v, a, b, g, lp["ln_w"], lp["ln_b"], lp["r_k"])


def _sample_body(slope_ref, p_ref,
                 k1_ref, v1_ref, k4_ref, v4_ref, k16_ref, v16_ref,
                 wkv_ref, shift_ref, cst_ref, qg_ref, kg_ref,
                 mu_ref, w0_ref, w2_ref, a0_ref, a2_ref, g2_ref, kk_ref, ka_ref,
                 rk_ref, lnw_ref, lnb_ref, cw_ref, cb_ref,
                 oatt_ref, yr_ref, yc_ref, kn_ref, vn_ref, wkvn_ref, cstn_ref):
    heads_a = oatt_ref.shape[0]
    aw = heads_a * HEAD_DIM
    rw = yr_ref.shape[1]
    cw = yc_ref.shape[1]
    rp = shift_ref.shape[1]
    rows8 = lambda x: jnp.broadcast_to(x, (8, x.shape[1]))
    p_all = p_ref[...]

    by_head = lambda row: jnp.concatenate(
        [row[:, h * HEAD_DIM:(h + 1) * HEAD_DIM] for h in range(heads_a)], axis=0)
    q, k, v_new = by_head(p_all[:, 0:aw]), by_head(p_all[:, aw:2 * aw]), by_head(p_all[:, 2 * aw:3 * aw])
    qn = _rms(q, qg_ref[...]) * (HEAD_DIM ** -0.5)
    kn = _rms(k, kg_ref[...])
    kn_ref[...] = kn
    vn_ref[...] = v_new
    s_new = jnp.sum(kn * qn, axis=-1, keepdims=True)
    slope = slope_ref[...]
    span = k1_ref.shape[0]
    steps = (span - lax.broadcasted_iota(jnp.int32, (span, 1, 1), 0)).astype(F32)
    outs, lses = [], []
    for (window, dil), kc_ref, vc_ref in zip(DILATION_GROUPS, (k1_ref, k4_ref, k16_ref),
                                             (v1_ref, v4_ref, v16_ref)):
        s = jnp.sum(kc_ref[...] * qn[None], axis=-1, keepdims=True) - slope[None] * (dil * steps)
        m = jnp.maximum(jnp.max(s, axis=0), s_new)
        p = jnp.exp(s - m[None])
        p_new = jnp.exp(s_new - m)
        l = jnp.sum(p, axis=0) + p_new
        outs.append((jnp.sum(p * vc_ref[...], axis=0) + p_new * v_new) / l)
        lses.append(m + jnp.log(l))
    mx = jnp.maximum(jnp.maximum(lses[0], lses[1]), lses[2])
    ws = [jnp.exp(l - mx) for l in lses]
    oatt_ref[...] = (outs[0] * ws[0] + outs[1] * ws[1] + outs[2] * ws[2]) / (ws[0] + ws[1] + ws[2])

    c0 = 3 * aw
    pr = rows8(p_all[:, c0:c0 + rp])
    sh = rows8(shift_ref[...])
    mu = mu_ref[...]
    c3 = 3 * rw
    r, lw, k, v, a, b, g = _rwkv_token_math(
        pr[:, 0:rw], pr[:, rw:2 * rw], pr[:, 2 * rw:c3], pr[:, c3:],
        sh[:, 0:rw], sh[:, rw:2 * rw], sh[:, 2 * rw:c3], sh[:, c3:],
        mu[:, 0:rw], mu[:, rw:2 * rw], mu[:, 2 * rw:c3], mu[:, c3:],
        w0_ref[...], w2_ref[...], a0_ref[...], a2_ref[...], g2_ref[...], kk_ref[...], ka_ref[...])
    decay = jnp.exp(lw)
    er = lax.broadcasted_iota(jnp.int32, (HEAD_DIM, HEAD_DIM), 0)
    ec = lax.broadcasted_iota(jnp.int32, (HEAD_DIM, HEAD_DIM), 1)
    eye = er == ec
    place_r = lax.broadcasted_iota(jnp.int32, (HEAD_DIM, rw), 0)
    place_c = lax.broadcasted_iota(jnp.int32, (HEAD_DIM, rw), 1)
    y_full = jnp.zeros((1, rw), F32)
    for h in range(rw // HEAD_DIM):
        hs = slice(h * HEAD_DIM, (h + 1) * HEAD_DIM)
        s_old = wkv_ref[h]
        sa = jnp.sum(s_old * a[0:1, hs], axis=1, keepdims=True)
        v_col = jnp.sum(jnp.where(eye, v[0:1, hs], 0.0), axis=1, keepdims=True)
        s_h = s_old * decay[0:1, hs] + sa * b[0:1, hs] + v_col * k[0:1, hs]
        wkvn_ref[h] = s_h
        y_col = jnp.sum(s_h * r[0:1, hs], axis=1, keepdims=True)
        y_full = y_full + jnp.sum(jnp.where(place_c == place_r + h * HEAD_DIM, y_col, 0.0),
                                  axis=0, keepdims=True)
    y8 = rows8(y_full)
    yr_ref[...] = _rwkv_post(y8, r, k, v, g, lnw_ref[...], lnb_ref[...], rk_ref[...])[0:1]

    c1 = c0 + rp
    z = p_all[:, c1 + 2 * cw:c1 + 3 * cw] * p_all[:, c1:c1 + cw]
    yc = cb_ref[...] + cw_ref[0:1, :] * cst_ref[0:1, :]
    yc = yc + cw_ref[1:2, :] * cst_ref[1:2, :]
    yc = yc + cw_ref[2:3, :] * z
    yc_ref[...] = p_all[:, c1 + cw:c1 + 2 * cw] * yc
    cstn_ref[0:1, :] = cst_ref[1:2, :]
    cstn_ref[1:2, :] = z


def _sample_mixers(p, cache_k, cache_v, state_wkv, state_shift, state_conv, lp, slope_col, layer):
    nb = p.shape[0]
    depth, _, past, heads_a, _ = cache_k.shape
    rw = lp["rwkv_w"]
    rp = state_shift.shape[2]
    cw = state_conv.shape[3]
    heads = rw // HEAD_DIM
    span = ATT_SPAN

    row = lambda w: pl.BlockSpec((None, 1, w), lambda n: (n, 0, 0))
    lvec = lambda w: pl.BlockSpec((None, 1, w), lambda n: (layer, 0, 0))
    lmat = lambda r_, w: pl.BlockSpec((None, r_, w), lambda n: (layer, 0, 0))
    tile = pl.BlockSpec((None, heads_a, HEAD_DIM), lambda n: (n, 0, 0))

    views, cache_specs = [], []
    for _, dil in DILATION_GROUPS:
        assert past % (dil * span) == 0
        shape = (depth, nb, past // dil, dil, heads_a, HEAD_DIM)
        spec = pl.BlockSpec((None, None, span, None, heads_a, HEAD_DIM),
                            lambda n, dil=dil: (layer, n, past // (dil * span) - 1, 0, 0, 0))
        views += [cache_k.reshape(shape), cache_v.reshape(shape)]
        cache_specs += [spec, spec]

    outs = pl.pallas_call(
        _sample_body,
        grid=(nb,),
        in_specs=[pl.BlockSpec((heads_a, 1), lambda n: (0, 0)), row(p.shape[1])] + cache_specs + [
                  pl.BlockSpec((None, None, heads, HEAD_DIM, HEAD_DIM), lambda n: (layer, n, 0, 0, 0)),
                  pl.BlockSpec((None, None, 1, rp), lambda n: (layer, n, 0, 0)),
                  pl.BlockSpec((None, None, CONV_K - 1, cw), lambda n: (layer, n, 0, 0)),
                  lvec(HEAD_DIM), lvec(HEAD_DIM),
                  lvec(rp), lvec(rw), lmat(LORA_W, rw), lvec(rw), lmat(LORA_W, rw),
                  lmat(LORA_W, rw), lvec(rw), lvec(rw), lvec(rw), lvec(rw), lvec(rw),
                  lmat(CONV_K, cw), lvec(cw)],
        out_specs=[tile, row(rw), row(cw), tile, tile,
                   pl.BlockSpec((None, heads, HEAD_DIM, HEAD_DIM), lambda n: (n, 0, 0, 0)),
                   pl.BlockSpec((None, CONV_K - 1, cw), lambda n: (n, 0, 0))],
        out_shape=[jax.ShapeDtypeStruct((nb, heads_a, HEAD_DIM), F32), jax.ShapeDtypeStruct((nb, 1, rw), F32),
                   jax.ShapeDtypeStruct((nb, 1, cw), F32),
                   jax.ShapeDtypeStruct((nb, heads_a, HEAD_DIM), F32),
                   jax.ShapeDtypeStruct((nb, heads_a, HEAD_DIM), F32),
                   jax.ShapeDtypeStruct((nb, heads, HEAD_DIM, HEAD_DIM), F32),
                   jax.ShapeDtypeStruct((nb, CONV_K - 1, cw), F32)],
        compiler_params=_cparams(("parallel",)),
        name="sample_mixers",
    )(slope_col, p[:, None, :], *views,
      state_wkv, state_shift[:, :, None, :], state_conv,
      lp["q_gain"], lp["k_gain"],
      lp["mu"], lp["w0"], lp["w2p"], lp["a0"], lp["a2p"], lp["g2p"], lp["k_k"], lp["k_a"],
      lp["r_k"], lp["ln_w"], lp["ln_b"], lp["conv_w"], lp["conv_b"])
    o_att, y_rwkv, y_conv, k_new, v_new, wkv_new, conv_new = outs
    return o_att.reshape(nb, heads_a * HEAD_DIM), y_rwkv[:, 0], y_conv[:, 0], k_new, v_new, wkv_new, conv_new


def kernel(x_prompt, x_sample, cache_swa_k, cache_swa_v, state_wkv, state_shift, state_conv, ffn1_norm, ffn1_w_gate, ffn1_w_up, ffn1_w_down, mix_norm, w_in, q_norm, k_norm, rwkv_mu, rwkv_w0, rwkv_w2, rwkv_a0, rwkv_a2, rwkv_g2, rwkv_k_k, rwkv_k_a, rwkv_r_k, rwkv_ln_w, rwkv_ln_b, conv_w, conv_b, w_out, ffn2_norm, ffn2_w_gate, ffn2_w_up, ffn2_w_down):
    n_seq, t, d = x_prompt.shape
    nb = x_sample.shape[0]
    assert x_sample.shape[1] == 1
    depth = w_in.shape[0]
    att_heads, head_dim = cache_swa_k.shape[3], cache_swa_k.shape[4]
    assert head_dim == HEAD_DIM
    aw = att_heads * HEAD_DIM
    rw = rwkv_w0.shape[1]
    cw = conv_b.shape[1]
    rp = rwkv_mu.shape[1]
    assert rp == 3 * rw + LORA_W and w_in.shape[2] == 3 * aw + rp + 3 * cw

    row3 = lambda p: p.reshape(depth, 1, -1)
    b16 = lambda p: p.astype(BF16)
    wg1, wu1, wd1 = b16(ffn1_w_gate), b16(ffn1_w_up), b16(ffn1_w_down)
    wg2, wu2, wd2 = b16(ffn2_w_gate), b16(ffn2_w_up), b16(ffn2_w_down)
    w_in16, w_out16 = b16(w_in), b16(w_out)
    zpad = lambda w, before: jnp.pad(w, ((0, 0), (before, LORA_W - before - w.shape[1]), (0, 0)))
    lp = {
        "rwkv_w": rw,
        "mu": row3(rwkv_mu), "w0": row3(rwkv_w0), "a0": row3(rwkv_a0),
        "w2p": zpad(rwkv_w2, 0), "a2p": zpad(rwkv_a2, W_LORA), "g2p": zpad(rwkv_g2, W_LORA + A_LORA),
        "k_k": row3(rwkv_k_k), "k_a": row3(rwkv_k_a), "r_k": row3(rwkv_r_k),
        "ln_w": row3(rwkv_ln_w), "ln_b": row3(rwkv_ln_b),
        "conv_w": conv_w, "conv_b": row3(conv_b),
        "q_gain": row3(q_norm), "k_gain": row3(k_norm),
        "q_gain_pair": row3(jnp.tile(q_norm, (1, LANES // HEAD_DIM))),
        "k_gain_pair": row3(jnp.tile(k_norm, (1, LANES // HEAD_DIM))),
        "k_gain_full": row3(jnp.tile(k_norm, (1, att_heads))),
    }
    g_ffn1, g_mix, g_ffn2 = row3(ffn1_norm), row3(mix_norm), row3(ffn2_norm)
    slopes = 2.0 ** (-8.0 * jnp.arange(1, att_heads + 1, dtype=F32) / att_heads)
    slope_col = slopes[:, None]

    m = n_seq * t
    xp = x_prompt.reshape(m, d)
    xs = x_sample.reshape(nb, d)
    tm_p, tf = 512, 512
    tn_in = 1280
    tc = 512
    col_rwkv, col_conv = 3 * aw, 3 * aw + rp
    p_states, s_states = [], []
    for l in range(depth):
        xp = _ffn(xp, g_ffn1, wg1, wu1, wd1, l, tm_p, tf)
        pp = _norm_matmul(xp, g_mix, w_in16, l, 1024, tn_in)
        o_att = _attention_prompt(pp, aw, lp["q_gain_pair"], lp["k_gain_pair"], slopes, l, n_seq, t)
        k_rows, v_rows = _kv_rows_prompt(pp, aw, lp["k_gain_full"], l, 512)
        r_, lw_, k_, v_, a_, b_, g_ = _rwkv_prep_prompt(pp, col_rwkv, lp, l, t, 256)
        y_rwkv, wkv_p = _rwkv_scan_prompt(r_, lw_, k_, v_, a_, b_, g_, lp, l, n_seq, t)
        y_conv, z_last = _conv_prompt(pp, col_conv, cw, lp["conv_w"], lp["conv_b"], l, t, tc)
        xp = _out_proj(xp, o_att, y_rwkv, y_conv, w_out16, l, tm_p, 512)
        xp = _ffn(xp, g_ffn2, wg2, wu2, wd2, l, tm_p, tf)
        p_states.append((
            k_rows.reshape(n_seq, t, att_heads, HEAD_DIM), v_rows.reshape(n_seq, t, att_heads, HEAD_DIM),
            wkv_p, pp.reshape(n_seq, t, -1)[:, -1, col_rwkv:col_conv],
            z_last.reshape(n_seq, t // tc, 8, cw)[:, -1, 8 - (CONV_K - 1):]))

        xs = _ffn(xs, g_ffn1, wg1, wu1, wd1, l, nb, tf)
        ps = _norm_matmul(xs, g_mix, w_in16, l, nb, tn_in)
        so, sy, syc, sk, sv, swkv, sconv = _sample_mixers(
            ps, cache_swa_k, cache_swa_v, state_wkv, state_shift, state_conv, lp, slope_col, l)
        xs = _out_proj(xs, so, sy, syc, w_out16, l, nb, 512)
        xs = _ffn(xs, g_ffn2, wg2, wu2, wd2, l, nb, tf)
        s_states.append((sk[:, None], sv[:, None], swkv, ps[:, col_rwkv:col_conv], sconv))

    stacked = lambda states, i: jnp.stack([st[i] for st in states], axis=0)
    return (xp.reshape(n_seq, t, d), xs.reshape(nb, 1, d),
            stacked(p_states, 0), stacked(p_states, 1), stacked(p_states, 2), stacked(p_states, 3),
            stacked(p_states, 4),
            stacked(s_states, 0), stacked(s_states, 1), stacked(s_states, 2), stacked(s_states, 3),
            stacked(s_states, 4))
```

```python
import functools

import jax
import jax.numpy as jnp
from jax import lax
from jax.experimental import pallas as pl
from jax.experimental.pallas import tpu as pltpu

F32 = jnp.float32
BF16 = jnp.bfloat16

HEAD_DIM = 64
HEAD_SHIFT = 6
LANES = 128
NORM_EPS = 1e-6
RWKV_GN_EPS = 64e-5
DILATION_GROUPS = ((128, 1), (512, 4), (2048, 16))
ATT_SPAN = 128
W_LORA, A_LORA, G_LORA = 64, 64, 128
LORA_W = W_LORA + A_LORA + G_LORA
CONV_K = 3
CONV_TILE = 256
RWKV_CHUNK = 64
VMEM_LIMIT = 56 * 1024 * 1024
NEG_BIG = -1e30

_NT = (((1,), (1,)), ((), ()))
_NN = (((1,), (0,)), ((), ()))
_BNN = (((2,), (1,)), ((0,), (0,)))
_BNT = (((2,), (2,)), ((0,), (0,)))
_BTN = (((1,), (1,)), ((0,), (0,)))


def _cparams(sem):
    return pltpu.CompilerParams(dimension_semantics=sem, vmem_limit_bytes=VMEM_LIMIT)


def _rms(x, g):
    return x * lax.rsqrt(jnp.mean(x * x, axis=-1, keepdims=True) + NORM_EPS) * g


def _split2(x):
    hi = x.astype(BF16)
    lo = (x - hi.astype(F32)).astype(BF16)
    return hi, lo


def _dot3(a, b, dims=_NN):
    ah, al = _split2(a)
    bh, bl = _split2(b)
    d = lambda x, y: lax.dot_general(x, y, dims, preferred_element_type=F32)
    return d(ah, bh) + (d(al, bh) + d(ah, bl))


def _dot1(a, b, dims=_NN):
    return lax.dot_general(a.astype(BF16), b.astype(BF16), dims, preferred_element_type=F32)


def _split3(x):
    h1 = x.astype(BF16)
    r1 = x - h1.astype(F32)
    h2 = r1.astype(BF16)
    h3 = (r1 - h2.astype(F32)).astype(BF16)
    return h1, h2, h3


def _dot_sel(x, sel):
    return sum(jnp.dot(h, sel, preferred_element_type=F32) for h in _split3(x))


def _cumsum_rows(x, tri_incl):
    return sum(jnp.dot(tri_incl, h, preferred_element_type=F32) for h in _split3(x))


def _head_blockdiag(n, dtype):
    r = lax.broadcasted_iota(jnp.int32, (n, n), 0) >> HEAD_SHIFT
    c = lax.broadcasted_iota(jnp.int32, (n, n), 1) >> HEAD_SHIFT
    return (r == c).astype(dtype)


def _segsum64(x):
    bd = _head_blockdiag(LANES, BF16)
    parts = [_dot_sel(x[:, i:i + LANES], bd) for i in range(0, x.shape[1], LANES)]
    return parts[0] if len(parts) == 1 else jnp.concatenate(parts, axis=1)


def _ffn_body(x_ref, g_ref, wg_ref, wu_ref, wd_ref, o_ref, xn_ref, acc_ref):
    j = pl.program_id(1)

    @pl.when(j == 0)
    def _():
        xn_ref[...] = _rms(x_ref[...], g_ref[...]).astype(BF16)
        acc_ref[...] = jnp.zeros_like(acc_ref)

    xn = xn_ref[...]
    h = jnp.dot(xn, wg_ref[...], preferred_element_type=F32)
    u = jnp.dot(xn, wu_ref[...], preferred_element_type=F32)
    act = (h * jax.nn.sigmoid(h) * u).astype(BF16)
    acc_ref[...] += jnp.dot(act, wd_ref[...], preferred_element_type=F32)

    @pl.when(j == pl.num_programs(1) - 1)
    def _():
        o_ref[...] = x_ref[...] + 0.5 * acc_ref[...]


def _ffn(x, gain, wg, wu, wd, layer, tm, tf):
    m, d = x.shape
    f = wg.shape[2]
    return pl.pallas_call(
        _ffn_body,
        grid=(m // tm, f // tf),
        in_specs=[
            pl.BlockSpec((tm, d), lambda i, j: (i, 0)),
            pl.BlockSpec((None, 1, d), lambda i, j: (layer, 0, 0)),
            pl.BlockSpec((None, d, tf), lambda i, j: (layer, 0, j)),
            pl.BlockSpec((None, d, tf), lambda i, j: (layer, 0, j)),
            pl.BlockSpec((None, tf, d), lambda i, j: (layer, j, 0)),
        ],
        out_specs=pl.BlockSpec((tm, d), lambda i, j: (i, 0)),
        out_shape=jax.ShapeDtypeStruct((m, d), F32),
        scratch_shapes=[pltpu.VMEM((tm, d), BF16), pltpu.VMEM((tm, d), F32)],
        compiler_params=_cparams(("parallel", "arbitrary")),
        name="ffn",
    )(x, gain, wg, wu, wd)


def _norm_matmul_body(x_ref, g_ref, w_ref, o_ref, xn_ref):
    @pl.when(pl.program_id(1) == 0)
    def _():
        xn_ref[...] = _rms(x_ref[...], g_ref[...]).astype(BF16)

    o_ref[...] = jnp.dot(xn_ref[...], w_ref[...], preferred_element_type=F32)


def _norm_matmul(x, gain, w, layer, tm, tn):
    m, d = x.shape
    n = w.shape[2]
    return pl.pallas_call(
        _norm_matmul_body,
        grid=(m // tm, n // tn),
        in_specs=[
            pl.BlockSpec((tm, d), lambda i, j: (i, 0)),
            pl.BlockSpec((None, 1, d), lambda i, j: (layer, 0, 0)),
            pl.BlockSpec((None, d, tn), lambda i, j: (layer, 0, j)),
        ],
        out_specs=pl.BlockSpec((tm, tn), lambda i, j: (i, j)),
        out_shape=jax.ShapeDtypeStruct((m, n), F32),
        scratch_shapes=[pltpu.VMEM((tm, d), BF16)],
        compiler_params=_cparams(("parallel", "arbitrary")),
        name="norm_matmul",
    )(x, gain, w)


def _out_proj_body(x_ref, a_ref, r_ref, c_ref, w_ref, o_ref):
    wa = a_ref.shape[1]
    wr = r_ref.shape[1]
    acc = jnp.dot(a_ref[...].astype(BF16), w_ref[0:wa, :], preferred_element_type=F32)
    acc += jnp.dot(r_ref[...].astype(BF16), w_ref[wa:wa + wr, :], preferred_element_type=F32)
    acc += jnp.dot(c_ref[...].astype(BF16), w_ref[wa + wr:, :], preferred_element_type=F32)
    o_ref[...] = x_ref[...] + acc


def _out_proj(x, o_att, y_rwkv, y_conv, w, layer, tm, tn):
    m, d = x.shape
    return pl.pallas_call(
        _out_proj_body,
        grid=(m // tm, d // tn),
        in_specs=[
            pl.BlockSpec((tm, tn), lambda i, j: (i, j)),
            pl.BlockSpec((tm, o_att.shape[1]), lambda i, j: (i, 0)),
            pl.BlockSpec((tm, y_rwkv.shape[1]), lambda i, j: (i, 0)),
            pl.BlockSpec((tm, y_conv.shape[1]), lambda i, j: (i, 0)),
            pl.BlockSpec((None, w.shape[1], tn), lambda i, j: (layer, 0, j)),
        ],
        out_specs=pl.BlockSpec((tm, tn), lambda i, j: (i, j)),
        out_shape=jax.ShapeDtypeStruct((m, d), F32),
        compiler_params=_cparams(("parallel", "arbitrary")),
        name="out_proj",
    )(x, o_att, y_rwkv, y_conv, w)


def _pair_headnorm(x, g, lo_half):
    x2 = x * x
    s0 = jnp.sum(jnp.where(lo_half, x2, 0.0), axis=-1, keepdims=True)
    s1 = jnp.sum(jnp.where(lo_half, 0.0, x2), axis=-1, keepdims=True)
    ms = jnp.where(lo_half, s0, s1) * (1.0 / HEAD_DIM)
    return x * lax.rsqrt(ms + NORM_EPS) * g


def _attn_body(slopes_ref, q_ref, k_ref, v_ref, qg_ref, kg_ref,
               o_ref, kt_ref, vt_ref, qs_ref, kn_ref, bias_ref, og_ref, lse_ref):
    hp = pl.program_id(1)
    t = q_ref.shape[0]
    span = ATT_SPAN
    lo_half = lax.broadcasted_iota(jnp.int32, (1, LANES), 1) < HEAD_DIM

    kn = _pair_headnorm(k_ref[...], kg_ref[...], lo_half)
    kn_ref[...] = kn
    kt_ref[...] = kn.T
    vt_ref[...] = v_ref[...].T
    qs_ref[...] = _pair_headnorm(q_ref[...], qg_ref[...], lo_half) * (HEAD_DIM ** -0.5)

    qi = lax.broadcasted_iota(jnp.int32, (span, 2 * span), 0)
    ki = lax.broadcasted_iota(jnp.int32, (span, 2 * span), 1)
    steps = qi + span - ki
    valid = (steps >= 0) & (steps <= span)
    for g, (_, dil) in enumerate(DILATION_GROUPS):
        dist = (steps * dil).astype(F32)
        for h in range(2):
            bias_ref[g, h * span:(h + 1) * span, :] = jnp.where(valid, -(slopes_ref[2 * hp + h] * dist), NEG_BIG)

    def unit(g, dil, start, has_prev):
        rows = pl.ds(start, span, stride=dil)
        qb = qs_ref[rows, :]
        q2 = jnp.concatenate([jnp.where(lo_half, qb, 0.0), jnp.where(lo_half, 0.0, qb)], axis=0).astype(BF16)
        if has_prev:
            prow = pl.ds(start - span * dil, span, stride=dil)
            kb = jnp.concatenate([kn_ref[prow, :], kn_ref[rows, :]], axis=0)
            vb = jnp.concatenate([v_ref[prow, :], v_ref[rows, :]], axis=0)
            bias = bias_ref[g]
        else:
            kb = kn_ref[rows, :]
            vb = v_ref[rows, :]
            bias = bias_ref[g, :, span:]
        s = lax.dot_general(q2, kb.astype(BF16), _NT, preferred_element_type=F32) + bias
        m = jnp.max(s, axis=-1, keepdims=True)
        p = jnp.exp(s - m)
        l = jnp.sum(p, axis=-1, keepdims=True)
        o = jnp.dot(p.astype(BF16), vb.astype(BF16), preferred_element_type=F32) / l
        lse = m + jnp.log(l)
        og_ref[g, rows, :] = jnp.where(lo_half, o[0:span], o[span:])
        lse_ref[g, rows, :] = jnp.where(lo_half, lse[0:span], lse[span:])

    for g, (window, dil) in enumerate(DILATION_GROUPS):
        assert window // dil == span
        nb = t // (span * dil)

        def residue(r, carry, g=g, dil=dil, nb=nb):
            unit(g, dil, r, False)
            if nb > 1:
                def blk(b, c):
                    unit(g, dil, b * (span * dil) + r, True)
                    return c
                lax.fori_loop(1, nb, blk, 0, unroll=3 if nb <= 4 else 5)
            return carry

        if dil == 1:
            residue(0, 0)
        else:
            lax.fori_loop(0, dil, residue, 0, unroll=4 if nb == 1 else 1)

    l0, l1, l2 = lse_ref[0], lse_ref[1], lse_ref[2]
    mx = jnp.maximum(jnp.maximum(l0, l1), l2)
    w0, w1, w2 = jnp.exp(l0 - mx), jnp.exp(l1 - mx), jnp.exp(l2 - mx)
    o_ref[...] = (og_ref[0] * w0 + og_ref[1] * w1 + og_ref[2] * w2) / (w0 + w1 + w2)


def _attention_prompt(p, att_w, qg, kg, slopes, layer, n_seq, t):
    m = p.shape[0]
    pairs = att_w // LANES
    blk = lambda off: pl.BlockSpec((t, LANES), lambda n, hp: (n, off + hp))
    gain = pl.BlockSpec((None, 1, LANES), lambda n, hp: (layer, 0, 0))
    tr_spec = pl.BlockSpec((None, LANES, t), lambda n, hp: (n, hp, 0))
    tr_shape = jax.ShapeDtypeStruct((n_seq, att_w, t), F32)
    return pl.pallas_call(
        _attn_body,
        grid=(n_seq, pairs),
        in_specs=[pl.BlockSpec(memory_space=pltpu.SMEM),
                  blk(0), blk(pairs), blk(2 * pairs), gain, gain],
        out_specs=[blk(0), tr_spec, tr_spec],
        out_shape=[jax.ShapeDtypeStruct((m, att_w), F32), tr_shape, tr_shape],
        scratch_shapes=[pltpu.VMEM((t, LANES), F32),
                        pltpu.VMEM((t, LANES), F32),
                        pltpu.VMEM((3, 2 * ATT_SPAN, 2 * ATT_SPAN), F32),
                        pltpu.VMEM((3, t, LANES), F32),
                        pltpu.VMEM((3, t, LANES), F32)],
        compiler_params=_cparams(("parallel", "arbitrary")),
        name="attn_prompt",
    )(slopes, p, p, p, qg, kg)


def _softplus(z):
    return jnp.maximum(z, 0.0) + jnp.log(1.0 + jnp.exp(-jnp.abs(z)))


def _rwkv_token_math(p_r, p_k, p_v, p_l, q_r, q_k, q_v, q_l, mu_r, mu_k, mu_v, mu_l,
                     w0, w2p, a0, a2p, g2p, k_k, k_a):
    xr = p_r + (q_r - p_r) * mu_r
    xk = p_k + (q_k - p_k) * mu_k
    xv = p_v + (q_v - p_v) * mu_v
    xl = p_l + (q_l - p_l) * mu_l
    w_log = -_softplus(-(w0 + _dot3(jnp.tanh(xl), w2p))) - 0.5
    log_decay = -jnp.exp(w_log)
    sig_l = jax.nn.sigmoid(xl)
    a_gate = jax.nn.sigmoid(a0 + _dot3(xl, a2p))
    g = _dot3(sig_l, g2p)
    kk = xk * k_k
    kk = kk / jnp.maximum(jnp.sqrt(_segsum64(kk * kk)), 1e-12)
    kmod = xk * (1.0 + (a_gate - 1.0) * k_a)
    return xr, log_decay, kmod, xv, -kk, kk * a_gate, g


def _rwkv_post(y, r, k, v, g, ln_w, ln_b, r_k):
    mu_y = _segsum64(y) * (1.0 / HEAD_DIM)
    yc = y - mu_y
    var_y = _segsum64(yc * yc) * (1.0 / HEAD_DIM)
    yn = yc * lax.rsqrt(var_y + RWKV_GN_EPS) * ln_w + ln_b
    bonus = _segsum64(r * k * r_k) * v
    return (yn + bonus) * g


def _mixprep_body(pr_ref, pk_ref, pv_ref, plo_ref, qr_ref, qk_ref, qv_ref, qlo_ref,
                  mur_ref, muk_ref, muv_ref, mul_ref, w0_ref, w2_ref, a0_ref, a2_ref, g2_ref,
                  kk_ref, ka_ref,
                  r_out, lw_out, k_out, v_out, a_out, b_out, g_out,
                  sr_ref, sk_ref, sv_ref, sl_ref, *, blocks_per_seq):
    tm = pr_ref.shape[0]
    first = (pl.program_id(0) % blocks_per_seq) == 0

    def shifted(cur_ref, prev8_ref, s_ref):
        s_ref[0:8, :] = jnp.where(first, 0.0, prev8_ref[...])
        s_ref[8:, :] = cur_ref[...]
        return s_ref[pl.ds(7, tm), :]

    outs = _rwkv_token_math(
        pr_ref[...], pk_ref[...], pv_ref[...], plo_ref[...],
        shifted(pr_ref, qr_ref, sr_ref), shifted(pk_ref, qk_ref, sk_ref),
        shifted(pv_ref, qv_ref, sv_ref), shifted(plo_ref, qlo_ref, sl_ref),
        mur_ref[...], muk_ref[...], muv_ref[...], mul_ref[...],
        w0_ref[...], w2_ref[...], a0_ref[...], a2_ref[...], g2_ref[...], kk_ref[...], ka_ref[...])
    for ref, val in zip((r_out, lw_out, k_out, v_out, a_out, b_out, g_out), outs):
        ref[...] = val


def _rwkv_prep_prompt(p, col0, lp, layer, t, tm):
    m = p.shape[0]
    rw = lp["rwkv_w"]
    assert tm % 8 == 0 and t % tm == 0 and col0 % rw == 0 and (col0 + 3 * rw) % LORA_W == 0
    c_slab = col0 // rw
    c_lora = (col0 + 3 * rw) // LORA_W
    cur = lambda w, c: pl.BlockSpec((tm, w), lambda i: (i, c))
    prev = lambda w, c: pl.BlockSpec((8, w), lambda i: (jnp.maximum(i * (tm // 8) - 1, 0), c))
    vec = lambda w, c: pl.BlockSpec((None, 1, w), lambda i: (layer, 0, c))
    mat = pl.BlockSpec((None, LORA_W, rw), lambda i: (layer, 0, 0))
    out = jax.ShapeDtypeStruct((m, rw), F32)
    return pl.pallas_call(
        functools.partial(_mixprep_body, blocks_per_seq=t // tm),
        grid=(m // tm,),
        in_specs=[cur(rw, c_slab), cur(rw, c_slab + 1), cur(rw, c_slab + 2), cur(LORA_W, c_lora),
                  prev(rw, c_slab), prev(rw, c_slab + 1), prev(rw, c_slab + 2), prev(LORA_W, c_lora),
                  vec(rw, 0), vec(rw, 1), vec(rw, 2), vec(LORA_W, (3 * rw) // LORA_W),
                  vec(rw, 0), mat, vec(rw, 0), mat, mat, vec(rw, 0), vec(rw, 0)],
        out_specs=[pl.BlockSpec((tm, rw), lambda i: (i, 0))] * 7,
        out_shape=[out] * 7,
        scratch_shapes=[pltpu.VMEM((tm + 8, rw), F32)] * 3 + [pltpu.VMEM((tm + 8, LORA_W), F32)],
        compiler_params=_cparams(("parallel",)),
        name="rwkv_prep",
    )(p, p, p, p, p, p, p, p,
      lp["mu"], lp["mu"], lp["mu"], lp["mu"],
      lp["w0"], lp["w2p"], lp["a0"], lp["a2p"], lp["g2p"], lp["k_k"], lp["k_a"])


def _conv_body(u_ref, gb_ref, gc_ref, pu_ref, pgc_ref, w_ref, b_ref, y_ref, zl_ref, zc_ref,
               *, blocks_per_seq):
    tm = u_ref.shape[0]
    first = (pl.program_id(0) % blocks_per_seq) == 0
    z = gc_ref[...] * u_ref[...]
    zc_ref[0:8, :] = jnp.where(first, 0.0, pgc_ref[...] * pu_ref[...])
    zc_ref[8:, :] = z
    yc = b_ref[...] + w_ref[0:1, :] * zc_ref[pl.ds(6, tm), :]
    yc = yc + w_ref[1:2, :] * zc_ref[pl.ds(7, tm), :]
    yc = yc + w_ref[2:3, :] * z
    y_ref[...] = gb_ref[...] * yc
    zl_ref[...] = zc_ref[pl.ds(tm, 8), :]


def _conv_prompt(p, col0, cw, conv_w, conv_b, layer, t, tm):
    m = p.shape[0]
    tc = CONV_TILE
    assert col0 % tc == 0 and cw % tc == 0
    c0, nc = col0 // tc, cw // tc
    cur = lambda s: pl.BlockSpec((tm, tc), lambda i, j: (i, c0 + s * nc + j))
    prev = lambda s: pl.BlockSpec((8, tc), lambda i, j: (jnp.maximum(i * (tm // 8) - 1, 0), c0 + s * nc + j))
    return pl.pallas_call(
        functools.partial(_conv_body, blocks_per_seq=t // tm),
        grid=(m // tm, nc),
        in_specs=[cur(0), cur(1), cur(2), prev(0), prev(2),
                  pl.BlockSpec((None, CONV_K, tc), lambda i, j: (layer, 0, j)),
                  pl.BlockSpec((None, 1, tc), lambda i, j: (layer, 0, j))],
        out_specs=[pl.BlockSpec((tm, tc), lambda i, j: (i, j)), pl.BlockSpec((8, tc), lambda i, j: (i, j))],
        out_shape=[jax.ShapeDtypeStruct((m, cw), F32), jax.ShapeDtypeStruct((m // tm * 8, cw), F32)],
        scratch_shapes=[pltpu.VMEM((tm + 8, tc), F32)],
        compiler_params=_cparams(("parallel", "parallel")),
        name="conv_prompt",
    )(p, p, p, p, p, conv_w, conv_b)


def _scan_chunk(r, lw, cum, k, v, a, b, s0):
    c = r.shape[1]
    c2 = 2 * c
    lane = lax.broadcasted_iota(jnp.int32, (1, 1, LANES), 2)
    m0 = (lane < HEAD_DIM).astype(F32)
    m1 = 1.0 - m0
    stack = lambda x: jnp.concatenate([x * m0, x * m1], axis=1)

    cum_prev = cum - lw
    cum_last = cum[:, c - 1:c, :]
    e_pos = jnp.exp(cum)
    e_neg = jnp.exp(-cum)
    e_rem = jnp.exp(cum_last - cum)
    at = a * jnp.exp(cum_prev)
    rt = r * e_pos
    bt = b * e_neg
    kt = k * e_neg
    bh = b * e_rem
    kh = k * e_rem

    big = _dot1(jnp.concatenate([stack(at), stack(rt)], axis=1),
                jnp.concatenate([stack(bt), stack(kt)], axis=1), _BNT)
    row = lax.broadcasted_iota(jnp.int32, (1, c2, c2), 1)
    col = lax.broadcasted_iota(jnp.int32, (1, c2, c2), 2)
    assert c == HEAD_DIM
    same_head = (row >> HEAD_SHIFT) == (col >> HEAD_SHIFT)
    tr, tc = row & (c - 1), col & (c - 1)
    strict = same_head & (tr > tc)
    incl = same_head & (tr >= tc)
    lmat = jnp.where(strict, big[:, 0:c2, 0:c2], 0.0)
    ak = jnp.where(strict, big[:, 0:c2, c2:], 0.0)
    rb = jnp.where(incl, big[:, c2:, 0:c2], 0.0)
    rk = jnp.where(incl, big[:, c2:, c2:], 0.0)

    eye = (row == col).astype(F32)
    x = eye + jnp.where((row >> 1) == (col >> 1), lmat, 0.0)
    s, log2s = 2, 1
    while s < c:
        joins = ((row >> (log2s + 1)) == (col >> (log2s + 1))) & ((row & s) != 0) & ((col & s) == 0)
        x = x + _dot1(_dot1(x, jnp.where(joins, lmat, 0.0), _BNN), x, _BNN)
        s, log2s = 2 * s, log2s + 1

    v_st = stack(v)
    w = _dot1(jnp.concatenate([at, rt], axis=1), s0, _BNT)
    u_st = _dot1(x, stack(w[:, 0:c]) + _dot1(ak, v_st, _BNN), _BNN)
    y_st = _dot1(jnp.concatenate([rb, rk], axis=2), jnp.concatenate([u_st, v_st], axis=1), _BNN)
    y = w[:, c:] + y_st[:, 0:c] + y_st[:, c:]
    u = u_st[:, 0:c] + u_st[:, c:]
    upd = _dot1(jnp.concatenate([u, v], axis=1), jnp.concatenate([bh, kh], axis=1), _BTN)
    s_new = s0 * jnp.exp(cum_last) + _head_blockdiag(LANES, F32)[None] * upd
    return y, s_new


def _scan_body(r_ref, lw_ref, k_ref, v_ref, a_ref, b_ref, g_ref, lnw_ref, lnb_ref, rk_ref,
               y_ref, s_out_ref, s_ref):
    ch = pl.program_id(1)

    @pl.when(ch == 0)
    def _():
        s_ref[...] = jnp.zeros_like(s_ref)

    npair = s_ref.shape[0]
    c = r_ref.shape[0]
    pairs = lambda x: jnp.stack([x[:, p * LANES:(p + 1) * LANES] for p in range(npair)], axis=0)
    tri_incl = (lax.broadcasted_iota(jnp.int32, (c, c), 0)
                >= lax.broadcasted_iota(jnp.int32, (c, c), 1)).astype(BF16)
    r, lw, k, v = r_ref[...], lw_ref[...], k_ref[...], v_ref[...]
    cum = _cumsum_rows(lw, tri_incl)
    y, s_new = _scan_chunk(pairs(r), pairs(lw), pairs(cum), pairs(k), pairs(v),
                           pairs(a_ref[...]), pairs(b_ref[...]), s_ref[...])
    s_ref[...] = s_new
    y = jnp.concatenate([y[p] for p in range(npair)], axis=1)
    y_ref[...] = _rwkv_post(y, r, k, v, g_ref[...], lnw_ref[...], lnb_ref[...], rk_ref[...])

    @pl.when(ch == pl.num_programs(1) - 1)
    def _():
        for p in range(npair):
            s_out_ref[2 * p] = s_ref[p, 0:HEAD_DIM, 0:HEAD_DIM]
            s_out_ref[2 * p + 1] = s_ref[p, HEAD_DIM:, HEAD_DIM:]


def _rwkv_scan_prompt(r, lw, k, v, a, b, g, lp, layer, n_seq, t):
    m, rw = r.shape
    c = RWKV_CHUNK
    heads = rw // HEAD_DIM
    tok = pl.BlockSpec((c, rw), lambda n, ch: (n * (t // c) + ch, 0))
    vec = pl.BlockSpec((None, 1, rw), lambda n, ch: (layer, 0, 0))
    return pl.pallas_call(
        _scan_body,
        grid=(n_seq, t // c),
        in_specs=[tok] * 7 + [vec] * 3,
        out_specs=[tok, pl.BlockSpec((None, heads, HEAD_DIM, HEAD_DIM), lambda n, ch: (n, 0, 0, 0))],
        out_shape=[jax.ShapeDtypeStruct((m, rw), F32),
                   jax.ShapeDtypeStruct((n_seq, heads, HEAD_DIM, HEAD_DIM), F32)],
        scratch_shapes=[pltpu.VMEM((rw // LANES, LANES, LANES), F32)],
        compiler_params=_cparams(("parallel", "arbitrary")),
        name="rwkv_scan",
    )(r, lw, k, v, a, b, g, lp["ln_w"], lp["ln_b"], lp["r_k"])


def _sample_body(slope_ref, p_ref, kc_ref, vc_ref,
                 wkv_ref, shift_ref, cst_ref, qg_ref, kg_ref,
                 mu_ref, w0_ref, w2_ref, a0_ref, a2_ref, g2_ref, kk_ref, ka_ref,
                 rk_ref, lnw_ref, lnb_ref, cw_ref, cb_ref,
                 oatt_ref, yr_ref, yc_ref, kn_ref, vn_ref, wkvn_ref, cstn_ref):
    heads_a = oatt_ref.shape[0]
    aw = heads_a * HEAD_DIM
    rw = yr_ref.shape[1]
    cw = yc_ref.shape[1]
    rp = shift_ref.shape[1]
    rows8 = lambda x: jnp.broadcast_to(x, (8, x.shape[1]))
    p_all = p_ref[...]

    by_head = lambda row: jnp.concatenate(
        [row[:, h * HEAD_DIM:(h + 1) * HEAD_DIM] for h in range(heads_a)], axis=0)
    q, k, v_new = by_head(p_all[:, 0:aw]), by_head(p_all[:, aw:2 * aw]), by_head(p_all[:, 2 * aw:3 * aw])
    qn = _rms(q, qg_ref[...]) * (HEAD_DIM ** -0.5)
    kn = _rms(k, kg_ref[...])
    kn_ref[...] = kn
    vn_ref[...] = v_new
    s_new = jnp.sum(kn * qn, axis=-1, keepdims=True)
    slope = slope_ref[...]
    past = kc_ref.shape[2]
    er = lax.broadcasted_iota(jnp.int32, (HEAD_DIM, HEAD_DIM), 0)
    ec = lax.broadcasted_iota(jnp.int32, (HEAD_DIM, HEAD_DIM), 1)
    eye = er == ec
    q_col = jnp.stack([jnp.sum(jnp.where(eye, qn[h:h + 1, :], 0.0), axis=1, keepdims=True)
                       for h in range(heads_a)], axis=0)
    dist = past - lax.broadcasted_iota(jnp.int32, (1, past), 1)
    biased = jnp.sum(kc_ref[...] * q_col, axis=1) - slope * dist.astype(F32)
    parts = []
    for window, dil in DILATION_GROUPS:
        assert dil & (dil - 1) == 0
        valid = ((dist & (dil - 1)) == 0) & (dist <= window)
        s = jnp.where(valid, biased, NEG_BIG)
        m = jnp.maximum(jnp.max(s, axis=1, keepdims=True), s_new)
        p = jnp.exp(s - m)
        p_new = jnp.exp(s_new - m)
        l = jnp.sum(p, axis=1, keepdims=True) + p_new
        parts.append((p, p_new, l, m + jnp.log(l)))
    mx = jnp.maximum(jnp.maximum(parts[0][3], parts[1][3]), parts[2][3])
    ws = [jnp.exp(lse - mx) for _, _, _, lse in parts]
    wsum = ws[0] + ws[1] + ws[2]
    scale_g = [w / (l * wsum) for w, (_, _, l, _) in zip(ws, parts)]
    coef = parts[0][0] * scale_g[0] + parts[1][0] * scale_g[1] + parts[2][0] * scale_g[2]
    coef_new = parts[0][1] * scale_g[0] + parts[1][1] * scale_g[1] + parts[2][1] * scale_g[2]
    o_col = jnp.sum(vc_ref[...] * coef[:, None, :], axis=2, keepdims=True)
    o_rows = jnp.concatenate([jnp.sum(jnp.where(eye, o_col[h], 0.0), axis=0, keepdims=True)
                              for h in range(heads_a)], axis=0)
    oatt_ref[...] = o_rows + coef_new * v_new

    c0 = 3 * aw
    pr = rows8(p_all[:, c0:c0 + rp])
    sh = rows8(shift_ref[...])
    mu = mu_ref[...]
    c3 = 3 * rw
    r, lw, k, v, a, b, g = _rwkv_token_math(
        pr[:, 0:rw], pr[:, rw:2 * rw], pr[:, 2 * rw:c3], pr[:, c3:],
        sh[:, 0:rw], sh[:, rw:2 * rw], sh[:, 2 * rw:c3], sh[:, c3:],
        mu[:, 0:rw], mu[:, rw:2 * rw], mu[:, 2 * rw:c3], mu[:, c3:],
        w0_ref[...], w2_ref[...], a0_ref[...], a2_ref[...], g2_ref[...], kk_ref[...], ka_ref[...])
    decay = jnp.exp(lw)
    place_r = lax.broadcasted_iota(jnp.int32, (HEAD_DIM, rw), 0)
    place_c = lax.broadcasted_iota(jnp.int32, (HEAD_DIM, rw), 1)
    y_full = jnp.zeros((1, rw), F32)
    for h in range(rw // HEAD_DIM):
        hs = slice(h * HEAD_DIM, (h + 1) * HEAD_DIM)
        s_old = wkv_ref[h]
        sa = jnp.sum(s_old * a[0:1, hs], axis=1, keepdims=True)
        v_col = jnp.sum(jnp.where(eye, v[0:1, hs], 0.0), axis=1, keepdims=True)
        s_h = s_old * decay[0:1, hs] + sa * b[0:1, hs] + v_col * k[0:1, hs]
        wkvn_ref[h] = s_h
        y_col = jnp.sum(s_h * r[0:1, hs], axis=1, keepdims=True)
        y_full = y_full + jnp.sum(jnp.where(place_c == place_r + h * HEAD_DIM, y_col, 0.0),
                                  axis=0, keepdims=True)
    y8 = rows8(y_full)
    yr_ref[...] = _rwkv_post(y8, r, k, v, g, lnw_ref[...], lnb_ref[...], rk_ref[...])[0:1]

    c1 = c0 + rp
    z = p_all[:, c1 + 2 * cw:c1 + 3 * cw] * p_all[:, c1:c1 + cw]
    yc = cb_ref[...] + cw_ref[0:1, :] * cst_ref[0:1, :]
    yc = yc + cw_ref[1:2, :] * cst_ref[1:2, :]
    yc = yc + cw_ref[2:3, :] * z
    yc_ref[...] = p_all[:, c1 + cw:c1 + 2 * cw] * yc
    cstn_ref[0:1, :] = cst_ref[1:2, :]
    cstn_ref[1:2, :] = z


def _sample_mixers(p, cache_k, cache_v, state_wkv, state_shift, state_conv, lp, slope_col, layer):
    nb = p.shape[0]
    depth, _, past, heads_a, _ = cache_k.shape
    rw = lp["rwkv_w"]
    rp = state_shift.shape[2]
    cw = state_conv.shape[3]
    heads = rw // HEAD_DIM
    span = ATT_SPAN

    row = lambda w: pl.BlockSpec((None, 1, w), lambda n: (n, 0, 0))
    lvec = lambda w: pl.BlockSpec((None, 1, w), lambda n: (layer, 0, 0))
    lmat = lambda r_, w: pl.BlockSpec((None, r_, w), lambda n: (layer, 0, 0))
    tile = pl.BlockSpec((None, heads_a, HEAD_DIM), lambda n: (n, 0, 0))

    assert past >= DILATION_GROUPS[-1][0]
    views = [jnp.transpose(c, (0, 1, 3, 4, 2)) for c in (cache_k, cache_v)]
    cache_specs = [pl.BlockSpec((None, None, heads_a, HEAD_DIM, past), lambda n: (layer, n, 0, 0, 0))] * 2

    outs = pl.pallas_call(
        _sample_body,
        grid=(nb,),
        in_specs=[pl.BlockSpec((heads_a, 1), lambda n: (0, 0)), row(p.shape[1])] + cache_specs + [
                  pl.BlockSpec((None, None, heads, HEAD_DIM, HEAD_DIM), lambda n: (layer, n, 0, 0, 0)),
                  pl.BlockSpec((None, None, 1, rp), lambda n: (layer, n, 0, 0)),
                  pl.BlockSpec((None, None, CONV_K - 1, cw), lambda n: (layer, n, 0, 0)),
                  lvec(HEAD_DIM), lvec(HEAD_DIM),
                  lvec(rp), lvec(rw), lmat(LORA_W, rw), lvec(rw), lmat(LORA_W, rw),
                  lmat(LORA_W, rw), lvec(rw), lvec(rw), lvec(rw), lvec(rw), lvec(rw),
                  lmat(CONV_K, cw), lvec(cw)],
        out_specs=[tile, row(rw), row(cw), tile, tile,
                   pl.BlockSpec((None, heads, HEAD_DIM, HEAD_DIM), lambda n: (n, 0, 0, 0)),
                   pl.BlockSpec((None, CONV_K - 1, cw), lambda n: (n, 0, 0))],
        out_shape=[jax.ShapeDtypeStruct((nb, heads_a, HEAD_DIM), F32), jax.ShapeDtypeStruct((nb, 1, rw), F32),
                   jax.ShapeDtypeStruct((nb, 1, cw), F32),
                   jax.ShapeDtypeStruct((nb, heads_a, HEAD_DIM), F32),
                   jax.ShapeDtypeStruct((nb, heads_a, HEAD_DIM), F32),
                   jax.ShapeDtypeStruct((nb, heads, HEAD_DIM, HEAD_DIM), F32),
                   jax.ShapeDtypeStruct((nb, CONV_K - 1, cw), F32)],
        compiler_params=_cparams(("parallel",)),
        name="sample_mixers",
    )(slope_col, p[:, None, :], *views,
      state_wkv, state_shift[:, :, None, :], state_conv,
      lp["q_gain"], lp["k_gain"],
      lp["mu"], lp["w0"], lp["w2p"], lp["a0"], lp["a2p"], lp["g2p"], lp["k_k"], lp["k_a"],
      lp["r_k"], lp["ln_w"], lp["ln_b"], lp["conv_w"], lp["conv_b"])
    o_att, y_rwkv, y_conv, k_new, v_new, wkv_new, conv_new = outs
    return o_att.reshape(nb, heads_a * HEAD_DIM), y_rwkv[:, 0], y_conv[:, 0], k_new, v_new, wkv_new, conv_new


def kernel(x_prompt, x_sample, cache_swa_k, cache_swa_v, state_wkv, state_shift, state_conv, ffn1_norm, ffn1_w_gate, ffn1_w_up, ffn1_w_down, mix_norm, w_in, q_norm, k_norm, rwkv_mu, rwkv_w0, rwkv_w2, rwkv_a0, rwkv_a2, rwkv_g2, rwkv_k_k, rwkv_k_a, rwkv_r_k, rwkv_ln_w, rwkv_ln_b, conv_w, conv_b, w_out, ffn2_norm, ffn2_w_gate, ffn2_w_up, ffn2_w_down):
    n_seq, t, d = x_prompt.shape
    nb = x_sample.shape[0]
    assert x_sample.shape[1] == 1
    depth = w_in.shape[0]
    att_heads, head_dim = cache_swa_k.shape[3], cache_swa_k.shape[4]
    assert head_dim == HEAD_DIM
    aw = att_heads * HEAD_DIM
    rw = rwkv_w0.shape[1]
    cw = conv_b.shape[1]
    rp = rwkv_mu.shape[1]
    assert rp == 3 * rw + LORA_W and w_in.shape[2] == 3 * aw + rp + 3 * cw

    row3 = lambda p: p.reshape(depth, 1, -1)
    b16 = lambda p: p.astype(BF16)
    wg1, wu1, wd1 = b16(ffn1_w_gate), b16(ffn1_w_up), b16(ffn1_w_down)
    wg2, wu2, wd2 = b16(ffn2_w_gate), b16(ffn2_w_up), b16(ffn2_w_down)
    w_in16, w_out16 = b16(w_in), b16(w_out)
    zpad = lambda w, before: jnp.pad(w, ((0, 0), (before, LORA_W - before - w.shape[1]), (0, 0)))
    lp = {
        "rwkv_w": rw,
        "mu": row3(rwkv_mu), "w0": row3(rwkv_w0), "a0": row3(rwkv_a0),
        "w2p": zpad(rwkv_w2, 0), "a2p": zpad(rwkv_a2, W_LORA), "g2p": zpad(rwkv_g2, W_LORA + A_LORA),
        "k_k": row3(rwkv_k_k), "k_a": row3(rwkv_k_a), "r_k": row3(rwkv_r_k),
        "ln_w": row3(rwkv_ln_w), "ln_b": row3(rwkv_ln_b),
        "conv_w": conv_w, "conv_b": row3(conv_b),
        "q_gain": row3(q_norm), "k_gain": row3(k_norm),
        "q_gain_pair": row3(jnp.tile(q_norm, (1, LANES // HEAD_DIM))),
        "k_gain_pair": row3(jnp.tile(k_norm, (1, LANES // HEAD_DIM))),
    }
    g_ffn1, g_mix, g_ffn2 = row3(ffn1_norm), row3(mix_norm), row3(ffn2_norm)
    slopes = 2.0 ** (-8.0 * jnp.arange(1, att_heads + 1, dtype=F32) / att_heads)
    slope_col = slopes[:, None]

    m = n_seq * t
    xp = x_prompt.reshape(m, d)
    xs = x_sample.reshape(nb, d)
    tm_p, tf = 512, 512
    tn_in = 1280
    tc = 512
    col_rwkv, col_conv = 3 * aw, 3 * aw + rp
    p_states, s_states = [], []
    for l in range(depth):
        xp = _ffn(xp, g_ffn1, wg1, wu1, wd1, l, tm_p, tf)
        pp = _norm_matmul(xp, g_mix, w_in16, l, 1024, tn_in)
        o_att, k_t, v_t = _attention_prompt(pp, aw, lp["q_gain_pair"], lp["k_gain_pair"], slopes, l, n_seq, t)
        r_, lw_, k_, v_, a_, b_, g_ = _rwkv_prep_prompt(pp, col_rwkv, lp, l, t, 256)
        y_rwkv, wkv_p = _rwkv_scan_prompt(r_, lw_, k_, v_, a_, b_, g_, lp, l, n_seq, t)
        y_conv, z_last = _conv_prompt(pp, col_conv, cw, lp["conv_w"], lp["conv_b"], l, t, tc)
        xp = _out_proj(xp, o_att, y_rwkv, y_conv, w_out16, l, tm_p, 512)
        xp = _ffn(xp, g_ffn2, wg2, wu2, wd2, l, tm_p, tf)
        rows = lambda x_t: jnp.transpose(x_t.reshape(n_seq, att_heads, HEAD_DIM, t), (0, 3, 1, 2))
        p_states.append((
            rows(k_t), rows(v_t),
            wkv_p, pp.reshape(n_seq, t, -1)[:, -1, col_rwkv:col_conv],
            z_last.reshape(n_seq, t // tc, 8, cw)[:, -1, 8 - (CONV_K - 1):]))

        xs = _ffn(xs, g_ffn1, wg1, wu1, wd1, l, nb, tf)
        ps = _norm_matmul(xs, g_mix, w_in16, l, nb, tn_in)
        so, sy, syc, sk, sv, swkv, sconv = _sample_mixers(
            ps, cache_swa_k, cache_swa_v, state_wkv, state_shift, state_conv, lp, slope_col, l)
        xs = _out_proj(xs, so, sy, syc, w_out16, l, nb, 512)
        xs = _ffn(xs, g_ffn2, wg2, wu2, wd2, l, nb, tf)
        s_states.append((sk[:, None], sv[:, None], swkv, ps[:, col_rwkv:col_conv], sconv))

    stacked = lambda states, i: jnp.stack([st[i] for st in states], axis=0)
    return (xp.reshape(n_seq, t, d), xs.reshape(nb, 1, d),
            stacked(p_states, 0), stacked(p_states, 1), stacked(p_states, 2), stacked(p_states, 3),
            stacked(p_states, 4),
            stacked(s_states, 0), stacked(s_states, 1), stacked(s_states, 2), stacked(s_states, 3),
            stacked(s_states, 4))
```

```python
import functools

import jax
import jax.numpy as jnp
from jax import lax
from jax.experimental import pallas as pl
from jax.experimental.pallas import tpu as pltpu

F32 = jnp.float32
BF16 = jnp.bfloat16

HEAD_DIM = 64
HEAD_SHIFT = 6
LANES = 128
NORM_EPS = 1e-6
RWKV_GN_EPS = 64e-5
DILATION_GROUPS = ((128, 1), (512, 4), (2048, 16))
ATT_SPAN = 128
W_LORA, A_LORA, G_LORA = 64, 64, 128
LORA_W = W_LORA + A_LORA + G_LORA
CONV_K = 3
CONV_TILE = 256
RWKV_CHUNK = 64
VMEM_LIMIT = 56 * 1024 * 1024
NEG_BIG = -1e30

_NT = (((1,), (1,)), ((), ()))
_NN = (((1,), (0,)), ((), ()))
_BNN = (((2,), (1,)), ((0,), (0,)))
_BNT = (((2,), (2,)), ((0,), (0,)))
_BTN = (((1,), (1,)), ((0,), (0,)))


def _cparams(sem):
    return pltpu.CompilerParams(dimension_semantics=sem, vmem_limit_bytes=VMEM_LIMIT)


def _rms(x, g):
    return x * lax.rsqrt(jnp.mean(x * x, axis=-1, keepdims=True) + NORM_EPS) * g


def _split2(x):
    hi = x.astype(BF16)
    lo = (x - hi.astype(F32)).astype(BF16)
    return hi, lo


def _dot3(a, b, dims=_NN):
    ah, al = _split2(a)
    bh, bl = _split2(b)
    d = lambda x, y: lax.dot_general(x, y, dims, preferred_element_type=F32)
    return d(ah, bh) + (d(al, bh) + d(ah, bl))


def _dot1(a, b, dims=_NN):
    return lax.dot_general(a.astype(BF16), b.astype(BF16), dims, preferred_element_type=F32)


def _split3(x):
    h1 = x.astype(BF16)
    r1 = x - h1.astype(F32)
    h2 = r1.astype(BF16)
    h3 = (r1 - h2.astype(F32)).astype(BF16)
    return h1, h2, h3


def _dot_sel(x, sel):
    return sum(jnp.dot(h, sel, preferred_element_type=F32) for h in _split3(x))


def _cumsum_rows(x, tri_incl):
    return sum(jnp.dot(tri_incl, h, preferred_element_type=F32) for h in _split3(x))


def _head_blockdiag(n, dtype):
    r = lax.broadcasted_iota(jnp.int32, (n, n), 0) >> HEAD_SHIFT
    c = lax.broadcasted_iota(jnp.int32, (n, n), 1) >> HEAD_SHIFT
    return (r == c).astype(dtype)


def _segsum64(x):
    bd = _head_blockdiag(LANES, BF16)
    parts = [_dot_sel(x[:, i:i + LANES], bd) for i in range(0, x.shape[1], LANES)]
    return parts[0] if len(parts) == 1 else jnp.concatenate(parts, axis=1)


def _ffn_body(x_ref, xs_ref, g_ref, wg_ref, wu_ref, wd_ref, o_ref, os_ref, xn_ref):
    j = pl.program_id(1)
    tm, d = x_ref.shape

    @pl.when(j == 0)
    def _():
        xn_ref[0:tm, :] = _rms(x_ref[...], g_ref[...]).astype(BF16)
        xn_ref[tm:, :] = _rms(xs_ref[...], g_ref[...]).astype(BF16)
        o_ref[...] = jnp.zeros_like(o_ref)
        os_ref[...] = jnp.zeros_like(os_ref)

    xn = xn_ref[...]
    h = jnp.dot(xn, wg_ref[...].astype(BF16), preferred_element_type=F32)
    u = jnp.dot(xn, wu_ref[...].astype(BF16), preferred_element_type=F32)
    act = (h * jax.nn.sigmoid(h) * u).astype(BF16)
    half = d // 2
    for c in range(2):
        cols = slice(c * half, (c + 1) * half)
        part = jnp.dot(act, wd_ref[:, cols].astype(BF16), preferred_element_type=F32)
        o_ref[:, cols] += part[0:tm]
        os_ref[:, cols] += part[tm:]

    @pl.when(j == pl.num_programs(1) - 1)
    def _():
        o_ref[...] = x_ref[...] + 0.5 * o_ref[...]
        os_ref[...] = xs_ref[...] + 0.5 * os_ref[...]


def _ffn(x, xs, gain, wg, wu, wd, layer, tm, tf):
    m, d = x.shape
    ms = xs.shape[0]
    f = wg.shape[2]
    return pl.pallas_call(
        _ffn_body,
        grid=(m // tm, f // tf),
        in_specs=[
            pl.BlockSpec((tm, d), lambda i, j: (i, 0)),
            pl.BlockSpec((ms, d), lambda i, j: (0, 0)),
            pl.BlockSpec((None, 1, d), lambda i, j: (layer, 0, 0)),
            pl.BlockSpec((None, d, tf), lambda i, j: (layer, 0, j)),
            pl.BlockSpec((None, d, tf), lambda i, j: (layer, 0, j)),
            pl.BlockSpec((None, tf, d), lambda i, j: (layer, j, 0)),
        ],
        out_specs=[pl.BlockSpec((tm, d), lambda i, j: (i, 0)),
                   pl.BlockSpec((ms, d), lambda i, j: (0, 0))],
        out_shape=[jax.ShapeDtypeStruct((m, d), F32), jax.ShapeDtypeStruct((ms, d), F32)],
        scratch_shapes=[pltpu.VMEM((tm + ms, d), BF16)],
        compiler_params=_cparams(("arbitrary", "arbitrary")),
        name="ffn",
    )(x, xs, gain, wg, wu, wd)


def _norm_matmul_body(x_ref, g_ref, w_ref, o_ref, xn_ref):
    @pl.when(pl.program_id(1) == 0)
    def _():
        xn_ref[...] = _rms(x_ref[...], g_ref[...]).astype(BF16)

    o_ref[...] = jnp.dot(xn_ref[...], w_ref[...], preferred_element_type=F32)


def _norm_matmul(x, gain, w, layer, tm, tn):
    m, d = x.shape
    n = w.shape[2]
    return pl.pallas_call(
        _norm_matmul_body,
        grid=(m // tm, n // tn),
        in_specs=[
            pl.BlockSpec((tm, d), lambda i, j: (i, 0)),
            pl.BlockSpec((None, 1, d), lambda i, j: (layer, 0, 0)),
            pl.BlockSpec((None, d, tn), lambda i, j: (layer, 0, j)),
        ],
        out_specs=pl.BlockSpec((tm, tn), lambda i, j: (i, j)),
        out_shape=jax.ShapeDtypeStruct((m, n), F32),
        scratch_shapes=[pltpu.VMEM((tm, d), BF16)],
        compiler_params=_cparams(("parallel", "arbitrary")),
        name="norm_matmul",
    )(x, gain, w)


def _out_proj_body(x_ref, a_ref, r_ref, c_ref, w_ref, o_ref):
    wa = a_ref.shape[1]
    wr = r_ref.shape[1]
    acc = jnp.dot(a_ref[...].astype(BF16), w_ref[0:wa, :], preferred_element_type=F32)
    acc += jnp.dot(r_ref[...].astype(BF16), w_ref[wa:wa + wr, :], preferred_element_type=F32)
    acc += jnp.dot(c_ref[...].astype(BF16), w_ref[wa + wr:, :], preferred_element_type=F32)
    o_ref[...] = x_ref[...] + acc


def _out_proj(x, o_att, y_rwkv, y_conv, w, layer, tm):
    m, d = x.shape
    return pl.pallas_call(
        _out_proj_body,
        grid=(m // tm,),
        in_specs=[
            pl.BlockSpec((tm, d), lambda i: (i, 0)),
            pl.BlockSpec((tm, o_att.shape[1]), lambda i: (i, 0)),
            pl.BlockSpec((tm, y_rwkv.shape[1]), lambda i: (i, 0)),
            pl.BlockSpec((tm, y_conv.shape[1]), lambda i: (i, 0)),
            pl.BlockSpec((None, w.shape[1], d), lambda i: (layer, 0, 0), pipeline_mode=pl.Buffered(1)),
        ],
        out_specs=pl.BlockSpec((tm, d), lambda i: (i, 0)),
        out_shape=jax.ShapeDtypeStruct((m, d), F32),
        compiler_params=_cparams(("parallel",)),
        name="out_proj",
    )(x, o_att, y_rwkv, y_conv, w)


def _pair_headnorm(x, g, lo_half):
    x2 = x * x
    s0 = jnp.sum(jnp.where(lo_half, x2, 0.0), axis=-1, keepdims=True)
    s1 = jnp.sum(jnp.where(lo_half, 0.0, x2), axis=-1, keepdims=True)
    ms = jnp.where(lo_half, s0, s1) * (1.0 / HEAD_DIM)
    return x * lax.rsqrt(ms + NORM_EPS) * g


def _attn_body(slopes_ref, q_ref, k_ref, v_ref, qg_ref, kg_ref,
               o_ref, kt_ref, vt_ref, qs_ref, kn_ref, bias_ref, og_ref, lse_ref):
    hp = pl.program_id(1)
    t = q_ref.shape[0]
    span = ATT_SPAN
    lo_half = lax.broadcasted_iota(jnp.int32, (1, LANES), 1) < HEAD_DIM

    kn = _pair_headnorm(k_ref[...], kg_ref[...], lo_half)
    kn_ref[...] = kn
    kt_ref[...] = kn.T
    vt_ref[...] = v_ref[...].T
    qs_ref[...] = _pair_headnorm(q_ref[...], qg_ref[...], lo_half) * (HEAD_DIM ** -0.5)

    qi = lax.broadcasted_iota(jnp.int32, (span, 2 * span), 0)
    ki = lax.broadcasted_iota(jnp.int32, (span, 2 * span), 1)
    steps = qi + span - ki
    valid = (steps >= 0) & (steps <= span)
    for g, (_, dil) in enumerate(DILATION_GROUPS):
        dist = (steps * dil).astype(F32)
        for h in range(2):
            bias_ref[g, h * span:(h + 1) * span, :] = jnp.where(valid, -(slopes_ref[2 * hp + h] * dist), NEG_BIG)

    def unit(g, dil, start, has_prev):
        rows = pl.ds(start, span, stride=dil)
        qb = qs_ref[rows, :]
        q2 = jnp.concatenate([jnp.where(lo_half, qb, 0.0), jnp.where(lo_half, 0.0, qb)], axis=0).astype(BF16)
        if has_prev:
            prow = pl.ds(start - span * dil, span, stride=dil)
            kb = jnp.concatenate([kn_ref[prow, :], kn_ref[rows, :]], axis=0)
            vb = jnp.concatenate([v_ref[prow, :], v_ref[rows, :]], axis=0)
            bias = bias_ref[g]
        else:
            kb = kn_ref[rows, :]
            vb = v_ref[rows, :]
            bias = bias_ref[g, :, span:]
        s = lax.dot_general(q2, kb.astype(BF16), _NT, preferred_element_type=F32) + bias
        m = jnp.max(s, axis=-1, keepdims=True)
        p = jnp.exp(s - m)
        l = jnp.sum(p, axis=-1, keepdims=True)
        o = jnp.dot(p.astype(BF16), vb.astype(BF16), preferred_element_type=F32) / l
        lse = m + jnp.log(l)
        og_ref[g, rows, :] = jnp.where(lo_half, o[0:span], o[span:])
        lse_ref[g, rows, :] = jnp.where(lo_half, lse[0:span], lse[span:])

    for g, (window, dil) in enumerate(DILATION_GROUPS):
        assert window // dil == span
        nb = t // (span * dil)

        def residue(r, carry, g=g, dil=dil, nb=nb):
            unit(g, dil, r, False)
            if nb > 1:
                def blk(b, c):
                    unit(g, dil, b * (span * dil) + r, True)
                    return c
                lax.fori_loop(1, nb, blk, 0, unroll=3 if nb <= 4 else 5)
            return carry

        if dil == 1:
            residue(0, 0)
        else:
            lax.fori_loop(0, dil, residue, 0, unroll=4 if nb == 1 else 1)

    l0, l1, l2 = lse_ref[0], lse_ref[1], lse_ref[2]
    mx = jnp.maximum(jnp.maximum(l0, l1), l2)
    w0, w1, w2 = jnp.exp(l0 - mx), jnp.exp(l1 - mx), jnp.exp(l2 - mx)
    o_ref[...] = (og_ref[0] * w0 + og_ref[1] * w1 + og_ref[2] * w2) / (w0 + w1 + w2)


def _attention_prompt(p, att_w, qg, kg, slopes, layer, n_seq, t):
    m = p.shape[0]
    pairs = att_w // LANES
    blk = lambda off: pl.BlockSpec((t, LANES), lambda n, hp: (n, off + hp))
    gain = pl.BlockSpec((None, 1, LANES), lambda n, hp: (layer, 0, 0))
    tr_spec = pl.BlockSpec((None, LANES, t), lambda n, hp: (n, hp, 0))
    tr_shape = jax.ShapeDtypeStruct((n_seq, att_w, t), F32)
    return pl.pallas_call(
        _attn_body,
        grid=(n_seq, pairs),
        in_specs=[pl.BlockSpec(memory_space=pltpu.SMEM),
                  blk(0), blk(pairs), blk(2 * pairs), gain, gain],
        out_specs=[blk(0), tr_spec, tr_spec],
        out_shape=[jax.ShapeDtypeStruct((m, att_w), F32), tr_shape, tr_shape],
        scratch_shapes=[pltpu.VMEM((t, LANES), F32),
                        pltpu.VMEM((t, LANES), F32),
                        pltpu.VMEM((3, 2 * ATT_SPAN, 2 * ATT_SPAN), F32),
                        pltpu.VMEM((3, t, LANES), F32),
                        pltpu.VMEM((3, t, LANES), F32)],
        compiler_params=_cparams(("parallel", "arbitrary")),
        name="attn_prompt",
    )(slopes, p, p, p, qg, kg)


def _softplus(z):
    return jnp.maximum(z, 0.0) + jnp.log(1.0 + jnp.exp(-jnp.abs(z)))


def _rwkv_token_math(p_r, p_k, p_v, p_l, q_r, q_k, q_v, q_l, mu_r, mu_k, mu_v, mu_l,
                     w0, w2p, a0, a2p, g2p, k_k, k_a):
    xr = p_r + (q_r - p_r) * mu_r
    xk = p_k + (q_k - p_k) * mu_k
    xv = p_v + (q_v - p_v) * mu_v
    xl = p_l + (q_l - p_l) * mu_l
    w_log = -_softplus(-(w0 + _dot3(jnp.tanh(xl), w2p))) - 0.5
    log_decay = -jnp.exp(w_log)
    sig_l = jax.nn.sigmoid(xl)
    a_gate = jax.nn.sigmoid(a0 + _dot3(xl, a2p))
    g = _dot3(sig_l, g2p)
    kk = xk * k_k
    kk = kk / jnp.maximum(jnp.sqrt(_segsum64(kk * kk)), 1e-12)
    kmod = xk * (1.0 + (a_gate - 1.0) * k_a)
    return xr, log_decay, kmod, xv, -kk, kk * a_gate, g


def _rwkv_post(y, r, k, v, g, ln_w, ln_b, r_k):
    mu_y = _segsum64(y) * (1.0 / HEAD_DIM)
    yc = y - mu_y
    var_y = _segsum64(yc * yc) * (1.0 / HEAD_DIM)
    yn = yc * lax.rsqrt(var_y + RWKV_GN_EPS) * ln_w + ln_b
    bonus = _segsum64(r * k * r_k) * v
    return (yn + bonus) * g


def _mixprep_body(pr_ref, pk_ref, pv_ref, plo_ref, qr_ref, qk_ref, qv_ref, qlo_ref,
                  mur_ref, muk_ref, muv_ref, mul_ref, w0_ref, w2_ref, a0_ref, a2_ref, g2_ref,
                  kk_ref, ka_ref,
                  r_out, lw_out, k_out, v_out, a_out, b_out, g_out,
                  sr_ref, sk_ref, sv_ref, sl_ref, *, blocks_per_seq):
    tm = pr_ref.shape[0]
    first = (pl.program_id(0) % blocks_per_seq) == 0

    def shifted(cur_ref, prev8_ref, s_ref):
        s_ref[0:8, :] = jnp.where(first, 0.0, prev8_ref[...])
        s_ref[8:, :] = cur_ref[...]
        return s_ref[pl.ds(7, tm), :]

    outs = _rwkv_token_math(
        pr_ref[...], pk_ref[...], pv_ref[...], plo_ref[...],
        shifted(pr_ref, qr_ref, sr_ref), shifted(pk_ref, qk_ref, sk_ref),
        shifted(pv_ref, qv_ref, sv_ref), shifted(plo_ref, qlo_ref, sl_ref),
        mur_ref[...], muk_ref[...], muv_ref[...], mul_ref[...],
        w0_ref[...], w2_ref[...], a0_ref[...], a2_ref[...], g2_ref[...], kk_ref[...], ka_ref[...])
    for ref, val in zip((r_out, lw_out, k_out, v_out, a_out, b_out, g_out), outs):
        ref[...] = val


def _rwkv_prep_prompt(p, col0, lp, layer, t, tm):
    m = p.shape[0]
    rw = lp["rwkv_w"]
    assert tm % 8 == 0 and t % tm == 0 and col0 % rw == 0 and (col0 + 3 * rw) % LORA_W == 0
    c_slab = col0 // rw
    c_lora = (col0 + 3 * rw) // LORA_W
    cur = lambda w, c: pl.BlockSpec((tm, w), lambda i: (i, c))
    prev = lambda w, c: pl.BlockSpec((8, w), lambda i: (jnp.maximum(i * (tm // 8) - 1, 0), c))
    vec = lambda w, c: pl.BlockSpec((None, 1, w), lambda i: (layer, 0, c))
    mat = pl.BlockSpec((None, LORA_W, rw), lambda i: (layer, 0, 0))
    out = jax.ShapeDtypeStruct((m, rw), F32)
    return pl.pallas_call(
        functools.partial(_mixprep_body, blocks_per_seq=t // tm),
        grid=(m // tm,),
        in_specs=[cur(rw, c_slab), cur(rw, c_slab + 1), cur(rw, c_slab + 2), cur(LORA_W, c_lora),
                  prev(rw, c_slab), prev(rw, c_slab + 1), prev(rw, c_slab + 2), prev(LORA_W, c_lora),
                  vec(rw, 0), vec(rw, 1), vec(rw, 2), vec(LORA_W, (3 * rw) // LORA_W),
                  vec(rw, 0), mat, vec(rw, 0), mat, mat, vec(rw, 0), vec(rw, 0)],
        out_specs=[pl.BlockSpec((tm, rw), lambda i: (i, 0))] * 7,
        out_shape=[out] * 7,
        scratch_shapes=[pltpu.VMEM((tm + 8, rw), F32)] * 3 + [pltpu.VMEM((tm + 8, LORA_W), F32)],
        compiler_params=_cparams(("parallel",)),
        name="rwkv_prep",
    )(p, p, p, p, p, p, p, p,
      lp["mu"], lp["mu"], lp["mu"], lp["mu"],
      lp["w0"], lp["w2p"], lp["a0"], lp["a2p"], lp["g2p"], lp["k_k"], lp["k_a"])


def _conv_body(u_ref, gb_ref, gc_ref, pu_ref, pgc_ref, w_ref, b_ref, y_ref, zl_ref, zc_ref,
               *, blocks_per_seq):
    tm = u_ref.shape[0]
    first = (pl.program_id(0) % blocks_per_seq) == 0
    z = gc_ref[...] * u_ref[...]
    zc_ref[0:8, :] = jnp.where(first, 0.0, pgc_ref[...] * pu_ref[...])
    zc_ref[8:, :] = z
    yc = b_ref[...] + w_ref[0:1, :] * zc_ref[pl.ds(6, tm), :]
    yc = yc + w_ref[1:2, :] * zc_ref[pl.ds(7, tm), :]
    yc = yc + w_ref[2:3, :] * z
    y_ref[...] = gb_ref[...] * yc
    zl_ref[...] = zc_ref[pl.ds(tm, 8), :]


def _conv_prompt(p, col0, cw, conv_w, conv_b, layer, t, tm):
    m = p.shape[0]
    tc = CONV_TILE
    assert col0 % tc == 0 and cw % tc == 0
    c0, nc = col0 // tc, cw // tc
    cur = lambda s: pl.BlockSpec((tm, tc), lambda i, j: (i, c0 + s * nc + j))
    prev = lambda s: pl.BlockSpec((8, tc), lambda i, j: (jnp.maximum(i * (tm // 8) - 1, 0), c0 + s * nc + j))
    return pl.pallas_call(
        functools.partial(_conv_body, blocks_per_seq=t // tm),
        grid=(m // tm, nc),
        in_specs=[cur(0), cur(1), cur(2), prev(0), prev(2),
                  pl.BlockSpec((None, CONV_K, tc), lambda i, j: (layer, 0, j)),
                  pl.BlockSpec((None, 1, tc), lambda i, j: (layer, 0, j))],
        out_specs=[pl.BlockSpec((tm, tc), lambda i, j: (i, j)), pl.BlockSpec((8, tc), lambda i, j: (i, j))],
        out_shape=[jax.ShapeDtypeStruct((m, cw), F32), jax.ShapeDtypeStruct((m // tm * 8, cw), F32)],
        scratch_shapes=[pltpu.VMEM((tm + 8, tc), F32)],
        compiler_params=_cparams(("parallel", "parallel")),
        name="conv_prompt",
    )(p, p, p, p, p, conv_w, conv_b)


def _scan_chunk(r, lw, cum, k, v, a, b, s0):
    c = r.shape[1]
    c2 = 2 * c
    lane = lax.broadcasted_iota(jnp.int32, (1, 1, LANES), 2)
    m0 = (lane < HEAD_DIM).astype(F32)
    m1 = 1.0 - m0
    stack = lambda x: jnp.concatenate([x * m0, x * m1], axis=1)

    cum_prev = cum - lw
    cum_last = cum[:, c - 1:c, :]
    e_pos = jnp.exp(cum)
    e_neg = jnp.exp(-cum)
    e_rem = jnp.exp(cum_last - cum)
    at = a * jnp.exp(cum_prev)
    rt = r * e_pos
    bt = b * e_neg
    kt = k * e_neg
    bh = b * e_rem
    kh = k * e_rem

    big = _dot1(jnp.concatenate([stack(at), stack(rt)], axis=1),
                jnp.concatenate([stack(bt), stack(kt)], axis=1), _BNT)
    row = lax.broadcasted_iota(jnp.int32, (1, c2, c2), 1)
    col = lax.broadcasted_iota(jnp.int32, (1, c2, c2), 2)
    assert c == HEAD_DIM
    same_head = (row >> HEAD_SHIFT) == (col >> HEAD_SHIFT)
    tr, tc = row & (c - 1), col & (c - 1)
    strict = same_head & (tr > tc)
    incl = same_head & (tr >= tc)
    lmat = jnp.where(strict, big[:, 0:c2, 0:c2], 0.0)
    ak = jnp.where(strict, big[:, 0:c2, c2:], 0.0)
    rb = jnp.where(incl, big[:, c2:, 0:c2], 0.0)
    rk = jnp.where(incl, big[:, c2:, c2:], 0.0)

    eye = (row == col).astype(F32)
    x = eye + jnp.where((row >> 1) == (col >> 1), lmat, 0.0)
    s, log2s = 2, 1
    while s < c:
        joins = ((row >> (log2s + 1)) == (col >> (log2s + 1))) & ((row & s) != 0) & ((col & s) == 0)
        x = x + _dot1(_dot1(x, jnp.where(joins, lmat, 0.0), _BNN), x, _BNN)
        s, log2s = 2 * s, log2s + 1

    v_st = stack(v)
    w = _dot1(jnp.concatenate([at, rt], axis=1), s0, _BNT)
    u_st = _dot1(x, stack(w[:, 0:c]) + _dot1(ak, v_st, _BNN), _BNN)
    y_st = _dot1(jnp.concatenate([rb, rk], axis=2), jnp.concatenate([u_st, v_st], axis=1), _BNN)
    y = w[:, c:] + y_st[:, 0:c] + y_st[:, c:]
    u = u_st[:, 0:c] + u_st[:, c:]
    upd = _dot1(jnp.concatenate([u, v], axis=1), jnp.concatenate([bh, kh], axis=1), _BTN)
    s_new = s0 * jnp.exp(cum_last) + _head_blockdiag(LANES, F32)[None] * upd
    return y, s_new


def _scan_body(r_ref, lw_ref, k_ref, v_ref, a_ref, b_ref, g_ref, lnw_ref, lnb_ref, rk_ref,
               y_ref, s_out_ref, s_ref):
    ch = pl.program_id(1)

    @pl.when(ch == 0)
    def _():
        s_ref[...] = jnp.zeros_like(s_ref)

    npair = s_ref.shape[0]
    c = r_ref.shape[0]
    pairs = lambda x: jnp.stack([x[:, p * LANES:(p + 1) * LANES] for p in range(npair)], axis=0)
    tri_incl = (lax.broadcasted_iota(jnp.int32, (c, c), 0)
                >= lax.broadcasted_iota(jnp.int32, (c, c), 1)).astype(BF16)
    r, lw, k, v = r_ref[...], lw_ref[...], k_ref[...], v_ref[...]
    cum = _cumsum_rows(lw, tri_incl)
    y, s_new = _scan_chunk(pairs(r), pairs(lw), pairs(cum), pairs(k), pairs(v),
                           pairs(a_ref[...]), pairs(b_ref[...]), s_ref[...])
    s_ref[...] = s_new
    y = jnp.concatenate([y[p] for p in range(npair)], axis=1)
    y_ref[...] = _rwkv_post(y, r, k, v, g_ref[...], lnw_ref[...], lnb_ref[...], rk_ref[...])

    @pl.when(ch == pl.num_programs(1) - 1)
    def _():
        for p in range(npair):
            s_out_ref[2 * p] = s_ref[p, 0:HEAD_DIM, 0:HEAD_DIM]
            s_out_ref[2 * p + 1] = s_ref[p, HEAD_DIM:, HEAD_DIM:]


def _rwkv_scan_prompt(r, lw, k, v, a, b, g, lp, layer, n_seq, t):
    m, rw = r.shape
    c = RWKV_CHUNK
    heads = rw // HEAD_DIM
    tok = pl.BlockSpec((c, rw), lambda n, ch: (n * (t // c) + ch, 0))
    vec = pl.BlockSpec((None, 1, rw), lambda n, ch: (layer, 0, 0))
    return pl.pallas_call(
        _scan_body,
        grid=(n_seq, t // c),
        in_specs=[tok] * 7 + [vec] * 3,
        out_specs=[tok, pl.BlockSpec((None, heads, HEAD_DIM, HEAD_DIM), lambda n, ch: (n, 0, 0, 0))],
        out_shape=[jax.ShapeDtypeStruct((m, rw), F32),
                   jax.ShapeDtypeStruct((n_seq, heads, HEAD_DIM, HEAD_DIM), F32)],
        scratch_shapes=[pltpu.VMEM((rw // LANES, LANES, LANES), F32)],
        compiler_params=_cparams(("parallel", "arbitrary")),
        name="rwkv_scan",
    )(r, lw, k, v, a, b, g, lp["ln_w"], lp["ln_b"], lp["r_k"])


def _sample_body(slope_ref, p_ref, kc_ref, vc_ref,
                 wkv_ref, shift_ref, cst_ref, qg_ref, kg_ref,
                 mu_ref, w0_ref, w2_ref, a0_ref, a2_ref, g2_ref, kk_ref, ka_ref,
                 rk_ref, lnw_ref, lnb_ref, cw_ref, cb_ref,
                 oatt_ref, yr_ref, yc_ref, kn_ref, vn_ref, wkvn_ref, cstn_ref):
    heads_a = oatt_ref.shape[0]
    aw = heads_a * HEAD_DIM
    rw = yr_ref.shape[1]
    cw = yc_ref.shape[1]
    rp = shift_ref.shape[1]
    rows8 = lambda x: jnp.broadcast_to(x, (8, x.shape[1]))
    p_all = p_ref[...]

    by_head = lambda row: jnp.concatenate(
        [row[:, h * HEAD_DIM:(h + 1) * HEAD_DIM] for h in range(heads_a)], axis=0)
    q, k, v_new = by_head(p_all[:, 0:aw]), by_head(p_all[:, aw:2 * aw]), by_head(p_all[:, 2 * aw:3 * aw])
    qn = _rms(q, qg_ref[...]) * (HEAD_DIM ** -0.5)
    kn = _rms(k, kg_ref[...])
    kn_ref[...] = kn
    vn_ref[...] = v_new
    s_new = jnp.sum(kn * qn, axis=-1, keepdims=True)
    slope = slope_ref[...]
    past = kc_ref.shape[2]
    er = lax.broadcasted_iota(jnp.int32, (HEAD_DIM, HEAD_DIM), 0)
    ec = lax.broadcasted_iota(jnp.int32, (HEAD_DIM, HEAD_DIM), 1)
    eye = er == ec
    q_col = jnp.stack([jnp.sum(jnp.where(eye, qn[h:h + 1, :], 0.0), axis=1, keepdims=True)
                       for h in range(heads_a)], axis=0)
    dist = past - lax.broadcasted_iota(jnp.int32, (1, past), 1)
    biased = jnp.sum(kc_ref[...] * q_col, axis=1) - slope * dist.astype(F32)
    parts = []
    for window, dil in DILATION_GROUPS:
        assert dil & (dil - 1) == 0
        valid = ((dist & (dil - 1)) == 0) & (dist <= window)
        s = jnp.where(valid, biased, NEG_BIG)
        m = jnp.maximum(jnp.max(s, axis=1, keepdims=True), s_new)
        p = jnp.exp(s - m)
        p_new = jnp.exp(s_new - m)
        l = jnp.sum(p, axis=1, keepdims=True) + p_new
        parts.append((p, p_new, l, m + jnp.log(l)))
    mx = jnp.maximum(jnp.maximum(parts[0][3], parts[1][3]), parts[2][3])
    ws = [jnp.exp(lse - mx) for _, _, _, lse in parts]
    wsum = ws[0] + ws[1] + ws[2]
    scale_g = [w / (l * wsum) for w, (_, _, l, _) in zip(ws, parts)]
    coef = parts[0][0] * scale_g[0] + parts[1][0] * scale_g[1] + parts[2][0] * scale_g[2]
    coef_new = parts[0][1] * scale_g[0] + parts[1][1] * scale_g[1] + parts[2][1] * scale_g[2]
    o_col = jnp.sum(vc_ref[...] * coef[:, None, :], axis=2, keepdims=True)
    o_rows = jnp.concatenate([jnp.sum(jnp.where(eye, o_col[h], 0.0), axis=0, keepdims=True)
                              for h in range(heads_a)], axis=0)
    oatt_ref[...] = o_rows + coef_new * v_new

    c0 = 3 * aw
    pr = rows8(p_all[:, c0:c0 + rp])
    sh = rows8(shift_ref[...])
    mu = mu_ref[...]
    c3 = 3 * rw
    r, lw, k, v, a, b, g = _rwkv_token_math(
        pr[:, 0:rw], pr[:, rw:2 * rw], pr[:, 2 * rw:c3], pr[:, c3:],
        sh[:, 0:rw], sh[:, rw:2 * rw], sh[:, 2 * rw:c3], sh[:, c3:],
        mu[:, 0:rw], mu[:, rw:2 * rw], mu[:, 2 * rw:c3], mu[:, c3:],
        w0_ref[...], w2_ref[...], a0_ref[...], a2_ref[...], g2_ref[...], kk_ref[...], ka_ref[...])
    decay = jnp.exp(lw)
    place_r = lax.broadcasted_iota(jnp.int32, (HEAD_DIM, rw), 0)
    place_c = lax.broadcasted_iota(jnp.int32, (HEAD_DIM, rw), 1)
    y_full = jnp.zeros((1, rw), F32)
    for h in range(rw // HEAD_DIM):
        hs = slice(h * HEAD_DIM, (h + 1) * HEAD_DIM)
        s_old = wkv_ref[h]
        sa = jnp.sum(s_old * a[0:1, hs], axis=1, keepdims=True)
        v_col = jnp.sum(jnp.where(eye, v[0:1, hs], 0.0), axis=1, keepdims=True)
        s_h = s_old * decay[0:1, hs] + sa * b[0:1, hs] + v_col * k[0:1, hs]
        wkvn_ref[h] = s_h
        y_col = jnp.sum(s_h * r[0:1, hs], axis=1, keepdims=True)
        y_full = y_full + jnp.sum(jnp.where(place_c == place_r + h * HEAD_DIM, y_col, 0.0),
                                  axis=0, keepdims=True)
    y8 = rows8(y_full)
    yr_ref[...] = _rwkv_post(y8, r, k, v, g, lnw_ref[...], lnb_ref[...], rk_ref[...])[0:1]

    c1 = c0 + rp
    z = p_all[:, c1 + 2 * cw:c1 + 3 * cw] * p_all[:, c1:c1 + cw]
    yc = cb_ref[...] + cw_ref[0:1, :] * cst_ref[0:1, :]
    yc = yc + cw_ref[1:2, :] * cst_ref[1:2, :]
    yc = yc + cw_ref[2:3, :] * z
    yc_ref[...] = p_all[:, c1 + cw:c1 + 2 * cw] * yc
    cstn_ref[0:1, :] = cst_ref[1:2, :]
    cstn_ref[1:2, :] = z


def _sample_mixers(p, cache_k, cache_v, state_wkv, state_shift, state_conv, lp, slope_col, layer):
    nb = p.shape[0]
    depth, _, past, heads_a, _ = cache_k.shape
    rw = lp["rwkv_w"]
    rp = state_shift.shape[2]
    cw = state_conv.shape[3]
    heads = rw // HEAD_DIM
    span = ATT_SPAN

    row = lambda w: pl.BlockSpec((None, 1, w), lambda n: (n, 0, 0))
    lvec = lambda w: pl.BlockSpec((None, 1, w), lambda n: (layer, 0, 0))
    lmat = lambda r_, w: pl.BlockSpec((None, r_, w), lambda n: (layer, 0, 0))
    tile = pl.BlockSpec((None, heads_a, HEAD_DIM), lambda n: (n, 0, 0))

    assert past >= DILATION_GROUPS[-1][0]
    views = [jnp.transpose(c, (0, 1, 3, 4, 2)) for c in (cache_k, cache_v)]
    cache_specs = [pl.BlockSpec((None, None, heads_a, HEAD_DIM, past), lambda n: (layer, n, 0, 0, 0))] * 2

    outs = pl.pallas_call(
        _sample_body,
        grid=(nb,),
        in_specs=[pl.BlockSpec((heads_a, 1), lambda n: (0, 0)), row(p.shape[1])] + cache_specs + [
                  pl.BlockSpec((None, None, heads, HEAD_DIM, HEAD_DIM), lambda n: (layer, n, 0, 0, 0)),
                  pl.BlockSpec((None, None, 1, rp), lambda n: (layer, n, 0, 0)),
                  pl.BlockSpec((None, None, CONV_K - 1, cw), lambda n: (layer, n, 0, 0)),
                  lvec(HEAD_DIM), lvec(HEAD_DIM),
                  lvec(rp), lvec(rw), lmat(LORA_W, rw), lvec(rw), lmat(LORA_W, rw),
                  lmat(LORA_W, rw), lvec(rw), lvec(rw), lvec(rw), lvec(rw), lvec(rw),
                  lmat(CONV_K, cw), lvec(cw)],
        out_specs=[tile, row(rw), row(cw), tile, tile,
                   pl.BlockSpec((None, heads, HEAD_DIM, HEAD_DIM), lambda n: (n, 0, 0, 0)),
                   pl.BlockSpec((None, CONV_K - 1, cw), lambda n: (n, 0, 0))],
        out_shape=[jax.ShapeDtypeStruct((nb, heads_a, HEAD_DIM), F32), jax.ShapeDtypeStruct((nb, 1, rw), F32),
                   jax.ShapeDtypeStruct((nb, 1, cw), F32),
                   jax.ShapeDtypeStruct((nb, heads_a, HEAD_DIM), F32),
                   jax.ShapeDtypeStruct((nb, heads_a, HEAD_DIM), F32),
                   jax.ShapeDtypeStruct((nb, heads, HEAD_DIM, HEAD_DIM), F32),
                   jax.ShapeDtypeStruct((nb, CONV_K - 1, cw), F32)],
        compiler_params=_cparams(("parallel",)),
        name="sample_mixers",
    )(slope_col, p[:, None, :], *views,
      state_wkv, state_shift[:, :, None, :], state_conv,
      lp["q_gain"], lp["k_gain"],
      lp["mu"], lp["w0"], lp["w2p"], lp["a0"], lp["a2p"], lp["g2p"], lp["k_k"], lp["k_a"],
      lp["r_k"], lp["ln_w"], lp["ln_b"], lp["conv_w"], lp["conv_b"])
    o_att, y_rwkv, y_conv, k_new, v_new, wkv_new, conv_new = outs
    return o_att.reshape(nb, heads_a * HEAD_DIM), y_rwkv[:, 0], y_conv[:, 0], k_new, v_new, wkv_new, conv_new


def kernel(x_prompt, x_sample, cache_swa_k, cache_swa_v, state_wkv, state_shift, state_conv, ffn1_norm, ffn1_w_gate, ffn1_w_up, ffn1_w_down, mix_norm, w_in, q_norm, k_norm, rwkv_mu, rwkv_w0, rwkv_w2, rwkv_a0, rwkv_a2, rwkv_g2, rwkv_k_k, rwkv_k_a, rwkv_r_k, rwkv_ln_w, rwkv_ln_b, conv_w, conv_b, w_out, ffn2_norm, ffn2_w_gate, ffn2_w_up, ffn2_w_down):
    n_seq, t, d = x_prompt.shape
    nb = x_sample.shape[0]
    assert x_sample.shape[1] == 1
    depth = w_in.shape[0]
    att_heads, head_dim = cache_swa_k.shape[3], cache_swa_k.shape[4]
    assert head_dim == HEAD_DIM
    aw = att_heads * HEAD_DIM
    rw = rwkv_w0.shape[1]
    cw = conv_b.shape[1]
    rp = rwkv_mu.shape[1]
    assert rp == 3 * rw + LORA_W and w_in.shape[2] == 3 * aw + rp + 3 * cw

    row3 = lambda p: p.reshape(depth, 1, -1)
    w_in16, w_out16 = w_in.astype(BF16), w_out.astype(BF16)
    zpad = lambda w, before: jnp.pad(w, ((0, 0), (before, LORA_W - before - w.shape[1]), (0, 0)))
    lp = {
        "rwkv_w": rw,
        "mu": row3(rwkv_mu), "w0": row3(rwkv_w0), "a0": row3(rwkv_a0),
        "w2p": zpad(rwkv_w2, 0), "a2p": zpad(rwkv_a2, W_LORA), "g2p": zpad(rwkv_g2, W_LORA + A_LORA),
        "k_k": row3(rwkv_k_k), "k_a": row3(rwkv_k_a), "r_k": row3(rwkv_r_k),
        "ln_w": row3(rwkv_ln_w), "ln_b": row3(rwkv_ln_b),
        "conv_w": conv_w, "conv_b": row3(conv_b),
        "q_gain": row3(q_norm), "k_gain": row3(k_norm),
        "q_gain_pair": row3(jnp.tile(q_norm, (1, LANES // HEAD_DIM))),
        "k_gain_pair": row3(jnp.tile(k_norm, (1, LANES // HEAD_DIM))),
    }
    g_ffn1, g_mix, g_ffn2 = row3(ffn1_norm), row3(mix_norm), row3(ffn2_norm)
    slopes = 2.0 ** (-8.0 * jnp.arange(1, att_heads + 1, dtype=F32) / att_heads)
    slope_col = slopes[:, None]

    m = n_seq * t
    xp = x_prompt.reshape(m, d)
    xs = x_sample.reshape(nb, d)
    tm_ffn, tf = 1024, 256
    tm_in, tn_in = 1024, 1280
    tm_out = 512
    tc = 512
    col_rwkv, col_conv = 3 * aw, 3 * aw + rp
    p_states, s_states = [], []
    for l in range(depth):
        xp, xs = _ffn(xp, xs, g_ffn1, ffn1_w_gate, ffn1_w_up, ffn1_w_down, l, tm_ffn, tf)

        pp = _norm_matmul(xp, g_mix, w_in16, l, tm_in, tn_in)
        o_att, k_t, v_t = _attention_prompt(pp, aw, lp["q_gain_pair"], lp["k_gain_pair"], slopes, l, n_seq, t)
        r_, lw_, k_, v_, a_, b_, g_ = _rwkv_prep_prompt(pp, col_rwkv, lp, l, t, 256)
        y_rwkv, wkv_p = _rwkv_scan_prompt(r_, lw_, k_, v_, a_, b_, g_, lp, l, n_seq, t)
        y_conv, z_last = _conv_prompt(pp, col_conv, cw, lp["conv_w"], lp["conv_b"], l, t, tc)
        xp = _out_proj(xp, o_att, y_rwkv, y_conv, w_out16, l, tm_out)
        rows = lambda x_t: jnp.transpose(x_t.reshape(n_seq, att_heads, HEAD_DIM, t), (0, 3, 1, 2))
        p_states.append((
            rows(k_t), rows(v_t),
            wkv_p, pp.reshape(n_seq, t, -1)[:, -1, col_rwkv:col_conv],
            z_last.reshape(n_seq, t // tc, 8, cw)[:, -1, 8 - (CONV_K - 1):]))

        ps = _norm_matmul(xs, g_mix, w_in16, l, nb, tn_in)
        so, sy, syc, sk, sv, swkv, sconv = _sample_mixers(
            ps, cache_swa_k, cache_swa_v, state_wkv, state_shift, state_conv, lp, slope_col, l)
        xs = _out_proj(xs, so, sy, syc, w_out16, l, nb)
        s_states.append((sk[:, None], sv[:, None], swkv, ps[:, col_rwkv:col_conv], sconv))

        xp, xs = _ffn(xp, xs, g_ffn2, ffn2_w_gate, ffn2_w_up, ffn2_w_down, l, tm_ffn, tf)

    stacked = lambda states, i: jnp.stack([st[i] for st in states], axis=0)
    return (xp.reshape(n_seq, t, d), xs.reshape(nb, 1, d),
            stacked(p_states, 0), stacked(p_states, 1), stacked(p_states, 2), stacked(p_states, 3),
            stacked(p_states, 4),
            stacked(s_states, 0), stacked(s_states, 1), stacked(s_states, 2), stacked(s_states, 3),
            stacked(s_states, 4))
```

```python
import functools

import jax
import jax.numpy as jnp
from jax import lax
from jax.experimental import pallas as pl
from jax.experimental.pallas import tpu as pltpu

F32 = jnp.float32
BF16 = jnp.bfloat16

HEAD_DIM = 64
HEAD_SHIFT = 6
LANES = 128
NORM_EPS = 1e-6
RWKV_GN_EPS = 64e-5
DILATION_GROUPS = ((128, 1), (512, 4), (2048, 16))
ATT_SPAN = 128
W_LORA, A_LORA, G_LORA = 64, 64, 128
LORA_W = W_LORA + A_LORA + G_LORA
CONV_K = 3
RWKV_CHUNK = 64
VMEM_LIMIT = 56 * 1024 * 1024
NEG_BIG = -1e30

_NT = (((1,), (1,)), ((), ()))
_NN = (((1,), (0,)), ((), ()))
_BNN = (((2,), (1,)), ((0,), (0,)))
_BNT = (((2,), (2,)), ((0,), (0,)))
_BTN = (((1,), (1,)), ((0,), (0,)))


def _cparams(sem):
    return pltpu.CompilerParams(dimension_semantics=sem, vmem_limit_bytes=VMEM_LIMIT)


def _rms(x, g):
    return x * lax.rsqrt(jnp.mean(x * x, axis=-1, keepdims=True) + NORM_EPS) * g


def _split2(x):
    hi = x.astype(BF16)
    lo = (x - hi.astype(F32)).astype(BF16)
    return hi, lo


def _dot3(a, b, dims=_NN):
    ah, al = _split2(a)
    bh, bl = _split2(b)
    d = lambda x, y: lax.dot_general(x, y, dims, preferred_element_type=F32)
    return d(ah, bh) + (d(al, bh) + d(ah, bl))


def _dot1(a, b, dims=_NN):
    return lax.dot_general(a.astype(BF16), b.astype(BF16), dims, preferred_element_type=F32)


def _split3(x):
    h1 = x.astype(BF16)
    r1 = x - h1.astype(F32)
    h2 = r1.astype(BF16)
    h3 = (r1 - h2.astype(F32)).astype(BF16)
    return h1, h2, h3


def _dot_sel(x, sel):
    return sum(jnp.dot(h, sel, preferred_element_type=F32) for h in _split3(x))


def _cumsum_rows(x, tri_incl):
    return sum(jnp.dot(tri_incl, h, preferred_element_type=F32) for h in _split3(x))


def _head_blockdiag(n, dtype):
    r = lax.broadcasted_iota(jnp.int32, (n, n), 0) >> HEAD_SHIFT
    c = lax.broadcasted_iota(jnp.int32, (n, n), 1) >> HEAD_SHIFT
    return (r == c).astype(dtype)


def _segsum64(x):
    bd = _head_blockdiag(LANES, BF16)
    parts = [_dot_sel(x[:, i:i + LANES], bd) for i in range(0, x.shape[1], LANES)]
    return parts[0] if len(parts) == 1 else jnp.concatenate(parts, axis=1)


def _ffn_body(x_ref, xs_ref, g_ref, wg_ref, wu_ref, wd_ref, o_ref, os_ref, xn_ref):
    j = pl.program_id(1)
    tm, d = x_ref.shape

    @pl.when(j == 0)
    def _():
        xn_ref[0:tm, :] = _rms(x_ref[...], g_ref[...]).astype(BF16)
        xn_ref[tm:, :] = _rms(xs_ref[...], g_ref[...]).astype(BF16)
        o_ref[...] = jnp.zeros_like(o_ref)
        os_ref[...] = jnp.zeros_like(os_ref)

    xn = xn_ref[...]
    h = jnp.dot(xn, wg_ref[...].astype(BF16), preferred_element_type=F32)
    u = jnp.dot(xn, wu_ref[...].astype(BF16), preferred_element_type=F32)
    act = (h * jax.nn.sigmoid(h) * u).astype(BF16)
    half = d // 2
    for c in range(2):
        cols = slice(c * half, (c + 1) * half)
        part = jnp.dot(act, wd_ref[:, cols].astype(BF16), preferred_element_type=F32)
        o_ref[:, cols] += part[0:tm]
        os_ref[:, cols] += part[tm:]

    @pl.when(j == pl.num_programs(1) - 1)
    def _():
        o_ref[...] = x_ref[...] + 0.5 * o_ref[...]
        os_ref[...] = xs_ref[...] + 0.5 * os_ref[...]


def _ffn(x, xs, gain, wg, wu, wd, layer, tm, tf):
    m, d = x.shape
    ms = xs.shape[0]
    f = wg.shape[2]
    return pl.pallas_call(
        _ffn_body,
        grid=(m // tm, f // tf),
        in_specs=[
            pl.BlockSpec((tm, d), lambda i, j: (i, 0)),
            pl.BlockSpec((ms, d), lambda i, j: (0, 0)),
            pl.BlockSpec((None, 1, d), lambda i, j: (layer, 0, 0)),
            pl.BlockSpec((None, d, tf), lambda i, j: (layer, 0, j)),
            pl.BlockSpec((None, d, tf), lambda i, j: (layer, 0, j)),
            pl.BlockSpec((None, tf, d), lambda i, j: (layer, j, 0)),
        ],
        out_specs=[pl.BlockSpec((tm, d), lambda i, j: (i, 0)),
                   pl.BlockSpec((ms, d), lambda i, j: (0, 0))],
        out_shape=[jax.ShapeDtypeStruct((m, d), F32), jax.ShapeDtypeStruct((ms, d), F32)],
        scratch_shapes=[pltpu.VMEM((tm + ms, d), BF16)],
        compiler_params=_cparams(("arbitrary", "arbitrary")),
        name="ffn",
    )(x, xs, gain, wg, wu, wd)


def _norm_matmul_body(x_ref, g_ref, w_ref, o_ref, xn_ref):
    @pl.when(pl.program_id(1) == 0)
    def _():
        xn_ref[...] = _rms(x_ref[...], g_ref[...]).astype(BF16)

    o_ref[...] = jnp.dot(xn_ref[...], w_ref[...], preferred_element_type=F32)


def _norm_matmul(x, gain, w, layer, tm, tn):
    m, d = x.shape
    n = w.shape[2]
    return pl.pallas_call(
        _norm_matmul_body,
        grid=(m // tm, n // tn),
        in_specs=[
            pl.BlockSpec((tm, d), lambda i, j: (i, 0)),
            pl.BlockSpec((None, 1, d), lambda i, j: (layer, 0, 0)),
            pl.BlockSpec((None, d, tn), lambda i, j: (layer, 0, j)),
        ],
        out_specs=pl.BlockSpec((tm, tn), lambda i, j: (i, j)),
        out_shape=jax.ShapeDtypeStruct((m, n), F32),
        scratch_shapes=[pltpu.VMEM((tm, d), BF16)],
        compiler_params=_cparams(("parallel", "arbitrary")),
        name="norm_matmul",
    )(x, gain, w)


def _mix_out(x, o_att, y_rwkv, y_conv, w_ref):
    wa, wr = o_att.shape[1], y_rwkv.shape[1]
    acc = jnp.dot(o_att.astype(BF16), w_ref[0:wa, :], preferred_element_type=F32)
    acc += jnp.dot(y_rwkv.astype(BF16), w_ref[wa:wa + wr, :], preferred_element_type=F32)
    acc += jnp.dot(y_conv.astype(BF16), w_ref[wa + wr:, :], preferred_element_type=F32)
    return x + acc


def _out_proj_body(x_ref, a_ref, r_ref, c_ref, w_ref, o_ref):
    o_ref[...] = _mix_out(x_ref[...], a_ref[...], r_ref[...], c_ref[...], w_ref)


def _out_proj_conv_body(x_ref, a_ref, r_ref, u_ref, gb_ref, gc_ref, pu_ref, pgc_ref, cw_ref, cb_ref, w_ref,
                        o_ref, zl_ref, zc_ref, *, blocks_per_seq):
    tm = u_ref.shape[0]
    first = (pl.program_id(0) % blocks_per_seq) == 0
    z = gc_ref[...] * u_ref[...]
    zc_ref[0:8, :] = jnp.where(first, 0.0, pgc_ref[...] * pu_ref[...])
    zc_ref[8:, :] = z
    yc = cb_ref[...] + cw_ref[0:1, :] * zc_ref[pl.ds(6, tm), :]
    yc = yc + cw_ref[1:2, :] * zc_ref[pl.ds(7, tm), :]
    yc = yc + cw_ref[2:3, :] * z
    zl_ref[...] = zc_ref[pl.ds(tm, 8), :]
    o_ref[...] = _mix_out(x_ref[...], a_ref[...], r_ref[...], gb_ref[...] * yc, w_ref)


def _out_proj_conv(x, o_att, y_rwkv, p, col0, cw, conv_w, conv_b, w, layer, t, tm):
    m, d = x.shape
    assert col0 % cw == 0 and t % tm == 0
    c0 = col0 // cw
    cur = lambda s: pl.BlockSpec((tm, cw), lambda i: (i, c0 + s))
    prev = lambda s: pl.BlockSpec((8, cw), lambda i: (jnp.maximum(i * (tm // 8) - 1, 0), c0 + s))
    return pl.pallas_call(
        functools.partial(_out_proj_conv_body, blocks_per_seq=t // tm),
        grid=(m // tm,),
        in_specs=[
            pl.BlockSpec((tm, d), lambda i: (i, 0)),
            pl.BlockSpec((tm, o_att.shape[1]), lambda i: (i, 0)),
            pl.BlockSpec((tm, y_rwkv.shape[1]), lambda i: (i, 0)),
            cur(0), cur(1), cur(2), prev(0), prev(2),
            pl.BlockSpec((None, CONV_K, cw), lambda i: (layer, 0, 0)),
            pl.BlockSpec((None, 1, cw), lambda i: (layer, 0, 0)),
            pl.BlockSpec((None, w.shape[1], d), lambda i: (layer, 0, 0), pipeline_mode=pl.Buffered(1)),
        ],
        out_specs=[pl.BlockSpec((tm, d), lambda i: (i, 0)), pl.BlockSpec((8, cw), lambda i: (i, 0))],
        out_shape=[jax.ShapeDtypeStruct((m, d), F32), jax.ShapeDtypeStruct((m // tm * 8, cw), F32)],
        scratch_shapes=[pltpu.VMEM((tm + 8, cw), F32)],
        compiler_params=_cparams(("parallel",)),
        name="out_proj_conv",
    )(x, o_att, y_rwkv, p, p, p, p, p, conv_w, conv_b, w)


def _out_proj(x, o_att, y_rwkv, y_conv, w, layer, tm):
    m, d = x.shape
    return pl.pallas_call(
        _out_proj_body,
        grid=(m // tm,),
        in_specs=[
            pl.BlockSpec((tm, d), lambda i: (i, 0)),
            pl.BlockSpec((tm, o_att.shape[1]), lambda i: (i, 0)),
            pl.BlockSpec((tm, y_rwkv.shape[1]), lambda i: (i, 0)),
            pl.BlockSpec((tm, y_conv.shape[1]), lambda i: (i, 0)),
            pl.BlockSpec((None, w.shape[1], d), lambda i: (layer, 0, 0), pipeline_mode=pl.Buffered(1)),
        ],
        out_specs=pl.BlockSpec((tm, d), lambda i: (i, 0)),
        out_shape=jax.ShapeDtypeStruct((m, d), F32),
        compiler_params=_cparams(("parallel",)),
        name="out_proj",
    )(x, o_att, y_rwkv, y_conv, w)


def _pair_headnorm(x, g, lo_half):
    x2 = x * x
    s0 = jnp.sum(jnp.where(lo_half, x2, 0.0), axis=-1, keepdims=True)
    s1 = jnp.sum(jnp.where(lo_half, 0.0, x2), axis=-1, keepdims=True)
    ms = jnp.where(lo_half, s0, s1) * (1.0 / HEAD_DIM)
    return x * lax.rsqrt(ms + NORM_EPS) * g


def _attn_body(slopes_ref, q_ref, k_ref, v_ref, qg_ref, kg_ref,
               o_ref, kt_ref, vt_ref, qs_ref, kn_ref, bias_ref, og_ref, lse_ref):
    hp = pl.program_id(1)
    t = q_ref.shape[0]
    span = ATT_SPAN
    lo_half = lax.broadcasted_iota(jnp.int32, (1, LANES), 1) < HEAD_DIM

    kn = _pair_headnorm(k_ref[...], kg_ref[...], lo_half)
    kn_ref[...] = kn
    kt_ref[...] = kn.T
    vt_ref[...] = v_ref[...].T
    qs_ref[...] = _pair_headnorm(q_ref[...], qg_ref[...], lo_half) * (HEAD_DIM ** -0.5)

    qi = lax.broadcasted_iota(jnp.int32, (span, 2 * span), 0)
    ki = lax.broadcasted_iota(jnp.int32, (span, 2 * span), 1)
    steps = qi + span - ki
    valid = (steps >= 0) & (steps <= span)
    for g, (_, dil) in enumerate(DILATION_GROUPS):
        dist = (steps * dil).astype(F32)
        for h in range(2):
            bias_ref[g, h * span:(h + 1) * span, :] = jnp.where(valid, -(slopes_ref[2 * hp + h] * dist), NEG_BIG)

    def unit(g, dil, start, has_prev):
        rows = pl.ds(start, span, stride=dil)
        qb = qs_ref[rows, :]
        q2 = jnp.concatenate([jnp.where(lo_half, qb, 0.0), jnp.where(lo_half, 0.0, qb)], axis=0).astype(BF16)
        if has_prev:
            prow = pl.ds(start - span * dil, span, stride=dil)
            kb = jnp.concatenate([kn_ref[prow, :], kn_ref[rows, :]], axis=0)
            vb = jnp.concatenate([v_ref[prow, :], v_ref[rows, :]], axis=0)
            bias = bias_ref[g]
        else:
            kb = kn_ref[rows, :]
            vb = v_ref[rows, :]
            bias = bias_ref[g, :, span:]
        s = lax.dot_general(q2, kb.astype(BF16), _NT, preferred_element_type=F32) + bias
        m = jnp.max(s, axis=-1, keepdims=True)
        p = jnp.exp(s - m)
        l = jnp.sum(p, axis=-1, keepdims=True)
        o = jnp.dot(p.astype(BF16), vb.astype(BF16), preferred_element_type=F32) / l
        lse = m + jnp.log(l)
        og_ref[g, rows, :] = jnp.where(lo_half, o[0:span], o[span:])
        lse_ref[g, rows, :] = jnp.where(lo_half, lse[0:span], lse[span:])

    for g, (window, dil) in enumerate(DILATION_GROUPS):
        assert window // dil == span
        nb = t // (span * dil)

        def residue(r, carry, g=g, dil=dil, nb=nb):
            unit(g, dil, r, False)
            if nb > 1:
                def blk(b, c):
                    unit(g, dil, b * (span * dil) + r, True)
                    return c
                lax.fori_loop(1, nb, blk, 0, unroll=3 if nb <= 4 else 5)
            return carry

        if dil == 1:
            residue(0, 0)
        else:
            lax.fori_loop(0, dil, residue, 0, unroll=4 if nb == 1 else 1)

    l0, l1, l2 = lse_ref[0], lse_ref[1], lse_ref[2]
    mx = jnp.maximum(jnp.maximum(l0, l1), l2)
    w0, w1, w2 = jnp.exp(l0 - mx), jnp.exp(l1 - mx), jnp.exp(l2 - mx)
    o_ref[...] = (og_ref[0] * w0 + og_ref[1] * w1 + og_ref[2] * w2) / (w0 + w1 + w2)


def _attention_prompt(p, col0, att_w, qg, kg, slopes, layer, n_seq, t):
    m = p.shape[0]
    pairs = att_w // LANES
    assert col0 % LANES == 0
    blk = lambda off: pl.BlockSpec((t, LANES), lambda n, hp: (n, off + hp))
    c0 = col0 // LANES
    gain = pl.BlockSpec((None, 1, LANES), lambda n, hp: (layer, 0, 0))
    tr_spec = pl.BlockSpec((None, LANES, t), lambda n, hp: (n, hp, 0))
    tr_shape = jax.ShapeDtypeStruct((n_seq, att_w, t), F32)
    return pl.pallas_call(
        _attn_body,
        grid=(n_seq, pairs),
        in_specs=[pl.BlockSpec(memory_space=pltpu.SMEM),
                  blk(c0), blk(c0 + pairs), blk(c0 + 2 * pairs), gain, gain],
        out_specs=[blk(0), tr_spec, tr_spec],
        out_shape=[jax.ShapeDtypeStruct((m, att_w), F32), tr_shape, tr_shape],
        scratch_shapes=[pltpu.VMEM((t, LANES), F32),
                        pltpu.VMEM((t, LANES), F32),
                        pltpu.VMEM((3, 2 * ATT_SPAN, 2 * ATT_SPAN), F32),
                        pltpu.VMEM((3, t, LANES), F32),
                        pltpu.VMEM((3, t, LANES), F32)],
        compiler_params=_cparams(("parallel", "arbitrary")),
        name="attn_prompt",
    )(slopes, p, p, p, qg, kg)


def _softplus(z):
    return jnp.maximum(z, 0.0) + jnp.log(1.0 + jnp.exp(-jnp.abs(z)))


def _rwkv_token_math(p_r, p_k, p_v, p_l, q_r, q_k, q_v, q_l, mu_r, mu_k, mu_v, mu_l,
                     w0, w2p, a0, a2p, g2p, k_k, k_a):
    xr = p_r + (q_r - p_r) * mu_r
    xk = p_k + (q_k - p_k) * mu_k
    xv = p_v + (q_v - p_v) * mu_v
    xl = p_l + (q_l - p_l) * mu_l
    w_log = -_softplus(-(w0 + _dot3(jnp.tanh(xl), w2p))) - 0.5
    log_decay = -jnp.exp(w_log)
    sig_l = jax.nn.sigmoid(xl)
    a_gate = jax.nn.sigmoid(a0 + _dot3(xl, a2p))
    g = _dot3(sig_l, g2p)
    kk = xk * k_k
    kk = kk / jnp.maximum(jnp.sqrt(_segsum64(kk * kk)), 1e-12)
    kmod = xk * (1.0 + (a_gate - 1.0) * k_a)
    return xr, log_decay, kmod, xv, -kk, kk * a_gate, g


def _segsum64_many(xs):
    rows, width = xs[0].shape
    if rows % 16 != 0:
        return [_segsum64(x) for x in xs]
    nsl = width // LANES
    pieces = [part[:, i * LANES:(i + 1) * LANES]
              for x in xs for part in _split3(x) for i in range(nsl)]
    res = jnp.dot(jnp.concatenate(pieces, axis=0), _head_blockdiag(LANES, BF16), preferred_element_type=F32)
    outs = []
    for n in range(len(xs)):
        parts = [jnp.concatenate([res[((3 * n + j) * nsl + i) * rows:((3 * n + j) * nsl + i + 1) * rows]
                                  for i in range(nsl)], axis=1) for j in range(3)]
        outs.append(parts[0] + parts[1] + parts[2])
    return outs


def _rwkv_post(y, r, k, v, g, ln_w, ln_b, r_k):
    sum_y, sum_rk = _segsum64_many([y, r * k * r_k])
    yc = y - sum_y * (1.0 / HEAD_DIM)
    var_y = _segsum64_many([yc * yc])[0] * (1.0 / HEAD_DIM)
    yn = yc * lax.rsqrt(var_y + RWKV_GN_EPS) * ln_w + ln_b
    return (yn + sum_rk * v) * g


def _mixprep_body(pr_ref, pk_ref, pv_ref, plo_ref, qr_ref, qk_ref, qv_ref, qlo_ref,
                  mur_ref, muk_ref, muv_ref, mul_ref, w0_ref, w2_ref, a0_ref, a2_ref, g2_ref,
                  kk_ref, ka_ref,
                  r_out, lw_out, k_out, v_out, a_out, b_out, g_out,
                  sr_ref, sk_ref, sv_ref, sl_ref, *, blocks_per_seq):
    tm = pr_ref.shape[0]
    first = (pl.program_id(0) % blocks_per_seq) == 0

    def shifted(cur_ref, prev8_ref, s_ref):
        s_ref[0:8, :] = jnp.where(first, 0.0, prev8_ref[...])
        s_ref[8:, :] = cur_ref[...]
        return s_ref[pl.ds(7, tm), :]

    outs = _rwkv_token_math(
        pr_ref[...], pk_ref[...], pv_ref[...], plo_ref[...],
        shifted(pr_ref, qr_ref, sr_ref), shifted(pk_ref, qk_ref, sk_ref),
        shifted(pv_ref, qv_ref, sv_ref), shifted(plo_ref, qlo_ref, sl_ref),
        mur_ref[...], muk_ref[...], muv_ref[...], mul_ref[...],
        w0_ref[...], w2_ref[...], a0_ref[...], a2_ref[...], g2_ref[...], kk_ref[...], ka_ref[...])
    for ref, val in zip((r_out, lw_out, k_out, v_out, a_out, b_out, g_out), outs):
        ref[...] = val


def _rwkv_prep_prompt(p, col0, lp, layer, t, tm):
    m = p.shape[0]
    rw = lp["rwkv_w"]
    assert tm % 8 == 0 and t % tm == 0 and col0 % rw == 0 and (col0 + 3 * rw) % LORA_W == 0
    c_slab = col0 // rw
    c_lora = (col0 + 3 * rw) // LORA_W
    cur = lambda w, c: pl.BlockSpec((tm, w), lambda i: (i, c))
    prev = lambda w, c: pl.BlockSpec((8, w), lambda i: (jnp.maximum(i * (tm // 8) - 1, 0), c))
    vec = lambda w, c: pl.BlockSpec((None, 1, w), lambda i: (layer, 0, c))
    mat = pl.BlockSpec((None, LORA_W, rw), lambda i: (layer, 0, 0))
    out = jax.ShapeDtypeStruct((m, rw), F32)
    return pl.pallas_call(
        functools.partial(_mixprep_body, blocks_per_seq=t // tm),
        grid=(m // tm,),
        in_specs=[cur(rw, c_slab), cur(rw, c_slab + 1), cur(rw, c_slab + 2), cur(LORA_W, c_lora),
                  prev(rw, c_slab), prev(rw, c_slab + 1), prev(rw, c_slab + 2), prev(LORA_W, c_lora),
                  vec(rw, 0), vec(rw, 1), vec(rw, 2), vec(LORA_W, (3 * rw) // LORA_W),
                  vec(rw, 0), mat, vec(rw, 0), mat, mat, vec(rw, 0), vec(rw, 0)],
        out_specs=[pl.BlockSpec((tm, rw), lambda i: (i, 0))] * 7,
        out_shape=[out] * 7,
        scratch_shapes=[pltpu.VMEM((tm + 8, rw), F32)] * 3 + [pltpu.VMEM((tm + 8, LORA_W), F32)],
        compiler_params=_cparams(("parallel",)),
        name="rwkv_prep",
    )(p, p, p, p, p, p, p, p,
      lp["mu"], lp["mu"], lp["mu"], lp["mu"],
      lp["w0"], lp["w2p"], lp["a0"], lp["a2p"], lp["g2p"], lp["k_k"], lp["k_a"])


def _scan_chunk(r, lw, cum, k, v, a, b, s0):
    c = r.shape[1]
    c2 = 2 * c
    lane = lax.broadcasted_iota(jnp.int32, (1, 1, LANES), 2)
    m0 = (lane < HEAD_DIM).astype(F32)
    m1 = 1.0 - m0
    stack = lambda x: jnp.concatenate([x * m0, x * m1], axis=1)

    cum_prev = cum - lw
    cum_last = cum[:, c - 1:c, :]
    e_pos = jnp.exp(cum)
    e_neg = jnp.exp(-cum)
    e_rem = jnp.exp(cum_last - cum)
    at = a * jnp.exp(cum_prev)
    rt = r * e_pos
    bt = b * e_neg
    kt = k * e_neg
    bh = b * e_rem
    kh = k * e_rem

    big = _dot1(jnp.concatenate([stack(at), stack(rt)], axis=1),
                jnp.concatenate([stack(bt), stack(kt)], axis=1), _BNT)
    row = lax.broadcasted_iota(jnp.int32, (1, c2, c2), 1)
    col = lax.broadcasted_iota(jnp.int32, (1, c2, c2), 2)
    assert c == HEAD_DIM
    same_head = (row >> HEAD_SHIFT) == (col >> HEAD_SHIFT)
    tr, tc = row & (c - 1), col & (c - 1)
    strict = same_head & (tr > tc)
    incl = same_head & (tr >= tc)
    lmat = jnp.where(strict, big[:, 0:c2, 0:c2], 0.0)
    ak = jnp.where(strict, big[:, 0:c2, c2:], 0.0)
    rb = jnp.where(incl, big[:, c2:, 0:c2], 0.0)
    rk = jnp.where(incl, big[:, c2:, c2:], 0.0)

    eye = (row == col).astype(F32)
    x = eye + jnp.where((row >> 1) == (col >> 1), lmat, 0.0)
    s, log2s = 2, 1
    while s < c:
        joins = ((row >> (log2s + 1)) == (col >> (log2s + 1))) & ((row & s) != 0) & ((col & s) == 0)
        x = x + _dot1(_dot1(x, jnp.where(joins, lmat, 0.0), _BNN), x, _BNN)
        s, log2s = 2 * s, log2s + 1

    v_st = stack(v)
    w = _dot1(jnp.concatenate([at, rt], axis=1), s0, _BNT)
    u_st = _dot1(x, stack(w[:, 0:c]) + _dot1(ak, v_st, _BNN), _BNN)
    y_st = _dot1(jnp.concatenate([rb, rk], axis=2), jnp.concatenate([u_st, v_st], axis=1), _BNN)
    y = w[:, c:] + y_st[:, 0:c] + y_st[:, c:]
    u = u_st[:, 0:c] + u_st[:, c:]
    upd = _dot1(jnp.concatenate([u, v], axis=1), jnp.concatenate([bh, kh], axis=1), _BTN)
    s_new = s0 * jnp.exp(cum_last) + _head_blockdiag(LANES, F32)[None] * upd
    return y, s_new


def _scan_body(r_ref, lw_ref, k_ref, v_ref, a_ref, b_ref, g_ref, lnw_ref, lnb_ref, rk_ref,
               y_ref, s_out_ref, s_ref):
    ch = pl.program_id(1)

    @pl.when(ch == 0)
    def _():
        s_ref[...] = jnp.zeros_like(s_ref)

    nseq, c, rw = r_ref.shape
    npair = rw // LANES
    pairs = lambda x: jnp.stack([x[s, :, p * LANES:(p + 1) * LANES]
                                 for s in range(nseq) for p in range(npair)], axis=0)
    flat = lambda ref: ref[...].reshape(nseq * c, rw)
    tri_incl = (lax.broadcasted_iota(jnp.int32, (c, c), 0)
                >= lax.broadcasted_iota(jnp.int32, (c, c), 1)).astype(BF16)
    r, lw, k, v = r_ref[...], lw_ref[...], k_ref[...], v_ref[...]
    cum = jnp.stack([_cumsum_rows(lw[s], tri_incl) for s in range(nseq)], axis=0)
    y, s_new = _scan_chunk(pairs(r), pairs(lw), pairs(cum), pairs(k), pairs(v),
                           pairs(a_ref[...]), pairs(b_ref[...]), s_ref[...])
    s_ref[...] = s_new
    y = jnp.concatenate([jnp.concatenate([y[s * npair + p] for p in range(npair)], axis=1)
                         for s in range(nseq)], axis=0)
    out = _rwkv_post(y, flat(r_ref), flat(k_ref), flat(v_ref), flat(g_ref),
                     lnw_ref[...], lnb_ref[...], rk_ref[...])
    y_ref[...] = out.reshape(nseq, c, rw)

    @pl.when(ch == pl.num_programs(1) - 1)
    def _():
        for s in range(nseq):
            for p in range(npair):
                s_out_ref[s, 2 * p] = s_ref[s * npair + p, 0:HEAD_DIM, 0:HEAD_DIM]
                s_out_ref[s, 2 * p + 1] = s_ref[s * npair + p, HEAD_DIM:, HEAD_DIM:]


def _rwkv_scan_prompt(r, lw, k, v, a, b, g, lp, layer, n_seq, t, seq_per_step):
    m, rw = r.shape
    c = RWKV_CHUNK
    heads = rw // HEAD_DIM
    assert n_seq % seq_per_step == 0
    tok = pl.BlockSpec((seq_per_step, c, rw), lambda n, ch: (n, ch, 0))
    vec = pl.BlockSpec((None, 1, rw), lambda n, ch: (layer, 0, 0))
    by_seq = lambda x: x.reshape(n_seq, t, rw)
    y, wkv = pl.pallas_call(
        _scan_body,
        grid=(n_seq // seq_per_step, t // c),
        in_specs=[tok] * 7 + [vec] * 3,
        out_specs=[tok, pl.BlockSpec((seq_per_step, heads, HEAD_DIM, HEAD_DIM), lambda n, ch: (n, 0, 0, 0))],
        out_shape=[jax.ShapeDtypeStruct((n_seq, t, rw), F32),
                   jax.ShapeDtypeStruct((n_seq, heads, HEAD_DIM, HEAD_DIM), F32)],
        scratch_shapes=[pltpu.VMEM((seq_per_step * (rw // LANES), LANES, LANES), F32)],
        compiler_params=_cparams(("parallel", "arbitrary")),
        name="rwkv_scan",
    )(by_seq(r), by_seq(lw), by_seq(k), by_seq(v), by_seq(a), by_seq(b), by_seq(g),
      lp["ln_w"], lp["ln_b"], lp["r_k"])
    return y.reshape(m, rw), wkv


def _sample_body(slope_ref, p_ref, kc_ref, vc_ref,
                 wkv_ref, shift_ref, cst_ref, qg_ref, kg_ref,
                 mu_ref, w0_ref, w2_ref, a0_ref, a2_ref, g2_ref, kk_ref, ka_ref,
                 rk_ref, lnw_ref, lnb_ref, cw_ref, cb_ref,
                 oatt_ref, yr_ref, yc_ref, kn_ref, vn_ref, wkvn_ref, cstn_ref,
                 *, col_att, col_rwkv, col_conv):
    heads_a = oatt_ref.shape[0]
    aw = heads_a * HEAD_DIM
    rw = yr_ref.shape[1]
    cw = yc_ref.shape[1]
    rp = shift_ref.shape[1]
    rows8 = lambda x: jnp.broadcast_to(x, (8, x.shape[1]))
    p_all = p_ref[...]

    by_head = lambda row: jnp.concatenate(
        [row[:, h * HEAD_DIM:(h + 1) * HEAD_DIM] for h in range(heads_a)], axis=0)
    ca = col_att
    q, k, v_new = (by_head(p_all[:, ca:ca + aw]), by_head(p_all[:, ca + aw:ca + 2 * aw]),
                   by_head(p_all[:, ca + 2 * aw:ca + 3 * aw]))
    qn = _rms(q, qg_ref[...]) * (HEAD_DIM ** -0.5)
    kn = _rms(k, kg_ref[...])
    kn_ref[...] = kn
    vn_ref[...] = v_new
    s_new = jnp.sum(kn * qn, axis=-1, keepdims=True)
    slope = slope_ref[...]
    past = kc_ref.shape[2]
    er = lax.broadcasted_iota(jnp.int32, (HEAD_DIM, HEAD_DIM), 0)
    ec = lax.broadcasted_iota(jnp.int32, (HEAD_DIM, HEAD_DIM), 1)
    eye = er == ec
    q_col = jnp.stack([jnp.sum(jnp.where(eye, qn[h:h + 1, :], 0.0), axis=1, keepdims=True)
                       for h in range(heads_a)], axis=0)
    dist = past - lax.broadcasted_iota(jnp.int32, (1, past), 1)
    biased = jnp.sum(kc_ref[...] * q_col, axis=1) - slope * dist.astype(F32)
    parts = []
    for window, dil in DILATION_GROUPS:
        assert dil & (dil - 1) == 0
        valid = ((dist & (dil - 1)) == 0) & (dist <= window)
        s = jnp.where(valid, biased, NEG_BIG)
        m = jnp.maximum(jnp.max(s, axis=1, keepdims=True), s_new)
        p = jnp.exp(s - m)
        p_new = jnp.exp(s_new - m)
        l = jnp.sum(p, axis=1, keepdims=True) + p_new
        parts.append((p, p_new, l, m + jnp.log(l)))
    mx = jnp.maximum(jnp.maximum(parts[0][3], parts[1][3]), parts[2][3])
    ws = [jnp.exp(lse - mx) for _, _, _, lse in parts]
    wsum = ws[0] + ws[1] + ws[2]
    scale_g = [w / (l * wsum) for w, (_, _, l, _) in zip(ws, parts)]
    coef = parts[0][0] * scale_g[0] + parts[1][0] * scale_g[1] + parts[2][0] * scale_g[2]
    coef_new = parts[0][1] * scale_g[0] + parts[1][1] * scale_g[1] + parts[2][1] * scale_g[2]
    o_col = jnp.sum(vc_ref[...] * coef[:, None, :], axis=2, keepdims=True)
    o_rows = jnp.concatenate([jnp.sum(jnp.where(eye, o_col[h], 0.0), axis=0, keepdims=True)
                              for h in range(heads_a)], axis=0)
    oatt_ref[...] = o_rows + coef_new * v_new

    pr = rows8(p_all[:, col_rwkv:col_rwkv + rp])
    sh = rows8(shift_ref[...])
    mu = mu_ref[...]
    c3 = 3 * rw
    r, lw, k, v, a, b, g = _rwkv_token_math(
        pr[:, 0:rw], pr[:, rw:2 * rw], pr[:, 2 * rw:c3], pr[:, c3:],
        sh[:, 0:rw], sh[:, rw:2 * rw], sh[:, 2 * rw:c3], sh[:, c3:],
        mu[:, 0:rw], mu[:, rw:2 * rw], mu[:, 2 * rw:c3], mu[:, c3:],
        w0_ref[...], w2_ref[...], a0_ref[...], a2_ref[...], g2_ref[...], kk_ref[...], ka_ref[...])
    decay = jnp.exp(lw)
    place_r = lax.broadcasted_iota(jnp.int32, (HEAD_DIM, rw), 0)
    place_c = lax.broadcasted_iota(jnp.int32, (HEAD_DIM, rw), 1)
    y_full = jnp.zeros((1, rw), F32)
    for h in range(rw // HEAD_DIM):
        hs = slice(h * HEAD_DIM, (h + 1) * HEAD_DIM)
        s_old = wkv_ref[h]
        sa = jnp.sum(s_old * a[0:1, hs], axis=1, keepdims=True)
        v_col = jnp.sum(jnp.where(eye, v[0:1, hs], 0.0), axis=1, keepdims=True)
        s_h = s_old * decay[0:1, hs] + sa * b[0:1, hs] + v_col * k[0:1, hs]
        wkvn_ref[h] = s_h
        y_col = jnp.sum(s_h * r[0:1, hs], axis=1, keepdims=True)
        y_full = y_full + jnp.sum(jnp.where(place_c == place_r + h * HEAD_DIM, y_col, 0.0),
                                  axis=0, keepdims=True)
    y8 = rows8(y_full)
    yr_ref[...] = _rwkv_post(y8, r, k, v, g, lnw_ref[...], lnb_ref[...], rk_ref[...])[0:1]

    c1 = col_conv
    z = p_all[:, c1 + 2 * cw:c1 + 3 * cw] * p_all[:, c1:c1 + cw]
    yc = cb_ref[...] + cw_ref[0:1, :] * cst_ref[0:1, :]
    yc = yc + cw_ref[1:2, :] * cst_ref[1:2, :]
    yc = yc + cw_ref[2:3, :] * z
    yc_ref[...] = p_all[:, c1 + cw:c1 + 2 * cw] * yc
    cstn_ref[0:1, :] = cst_ref[1:2, :]
    cstn_ref[1:2, :] = z


def _sample_mixers(p, cols, cache_k, cache_v, state_wkv, state_shift, state_conv, lp, slope_col, layer):
    nb = p.shape[0]
    depth, _, past, heads_a, _ = cache_k.shape
    rw = lp["rwkv_w"]
    rp = state_shift.shape[2]
    cw = state_conv.shape[3]
    heads = rw // HEAD_DIM
    span = ATT_SPAN

    row = lambda w: pl.BlockSpec((None, 1, w), lambda n: (n, 0, 0))
    lvec = lambda w: pl.BlockSpec((None, 1, w), lambda n: (layer, 0, 0))
    lmat = lambda r_, w: pl.BlockSpec((None, r_, w), lambda n: (layer, 0, 0))
    tile = pl.BlockSpec((None, heads_a, HEAD_DIM), lambda n: (n, 0, 0))

    assert past >= DILATION_GROUPS[-1][0]
    views = [jnp.transpose(c, (0, 1, 3, 4, 2)) for c in (cache_k, cache_v)]
    cache_specs = [pl.BlockSpec((None, None, heads_a, HEAD_DIM, past), lambda n: (layer, n, 0, 0, 0))] * 2

    outs = pl.pallas_call(
        functools.partial(_sample_body, col_att=cols[0], col_rwkv=cols[1], col_conv=cols[2]),
        grid=(nb,),
        in_specs=[pl.BlockSpec((heads_a, 1), lambda n: (0, 0)), row(p.shape[1])] + cache_specs + [
                  pl.BlockSpec((None, None, heads, HEAD_DIM, HEAD_DIM), lambda n: (layer, n, 0, 0, 0)),
                  pl.BlockSpec((None, None, 1, rp), lambda n: (layer, n, 0, 0)),
                  pl.BlockSpec((None, None, CONV_K - 1, cw), lambda n: (layer, n, 0, 0)),
                  lvec(HEAD_DIM), lvec(HEAD_DIM),
                  lvec(rp), lvec(rw), lmat(LORA_W, rw), lvec(rw), lmat(LORA_W, rw),
                  lmat(LORA_W, rw), lvec(rw), lvec(rw), lvec(rw), lvec(rw), lvec(rw),
                  lmat(CONV_K, cw), lvec(cw)],
        out_specs=[tile, row(rw), row(cw), tile, tile,
                   pl.BlockSpec((None, heads, HEAD_DIM, HEAD_DIM), lambda n: (n, 0, 0, 0)),
                   pl.BlockSpec((None, CONV_K - 1, cw), lambda n: (n, 0, 0))],
        out_shape=[jax.ShapeDtypeStruct((nb, heads_a, HEAD_DIM), F32), jax.ShapeDtypeStruct((nb, 1, rw), F32),
                   jax.ShapeDtypeStruct((nb, 1, cw), F32),
                   jax.ShapeDtypeStruct((nb, heads_a, HEAD_DIM), F32),
                   jax.ShapeDtypeStruct((nb, heads_a, HEAD_DIM), F32),
                   jax.ShapeDtypeStruct((nb, heads, HEAD_DIM, HEAD_DIM), F32),
                   jax.ShapeDtypeStruct((nb, CONV_K - 1, cw), F32)],
        compiler_params=_cparams(("parallel",)),
        name="sample_mixers",
    )(slope_col, p[:, None, :], *views,
      state_wkv, state_shift[:, :, None, :], state_conv,
      lp["q_gain"], lp["k_gain"],
      lp["mu"], lp["w0"], lp["w2p"], lp["a0"], lp["a2p"], lp["g2p"], lp["k_k"], lp["k_a"],
      lp["r_k"], lp["ln_w"], lp["ln_b"], lp["conv_w"], lp["conv_b"])
    o_att, y_rwkv, y_conv, k_new, v_new, wkv_new, conv_new = outs
    return o_att.reshape(nb, heads_a * HEAD_DIM), y_rwkv[:, 0], y_conv[:, 0], k_new, v_new, wkv_new, conv_new


def kernel(x_prompt, x_sample, cache_swa_k, cache_swa_v, state_wkv, state_shift, state_conv, ffn1_norm, ffn1_w_gate, ffn1_w_up, ffn1_w_down, mix_norm, w_in, q_norm, k_norm, rwkv_mu, rwkv_w0, rwkv_w2, rwkv_a0, rwkv_a2, rwkv_g2, rwkv_k_k, rwkv_k_a, rwkv_r_k, rwkv_ln_w, rwkv_ln_b, conv_w, conv_b, w_out, ffn2_norm, ffn2_w_gate, ffn2_w_up, ffn2_w_down):
    n_seq, t, d = x_prompt.shape
    nb = x_sample.shape[0]
    assert x_sample.shape[1] == 1
    depth = w_in.shape[0]
    att_heads, head_dim = cache_swa_k.shape[3], cache_swa_k.shape[4]
    assert head_dim == HEAD_DIM
    aw = att_heads * HEAD_DIM
    rw = rwkv_w0.shape[1]
    cw = conv_b.shape[1]
    rp = rwkv_mu.shape[1]
    assert rp == 3 * rw + LORA_W and w_in.shape[2] == 3 * aw + rp + 3 * cw

    row3 = lambda p: p.reshape(depth, 1, -1)
    col_conv, col_att, col_rwkv = 0, 3 * cw, 3 * cw + 3 * aw
    w_in16 = jnp.concatenate([w_in[:, :, 3 * aw + rp:], w_in[:, :, :3 * aw + rp]], axis=2).astype(BF16)
    w_out16 = w_out.astype(BF16)
    zpad = lambda w, before: jnp.pad(w, ((0, 0), (before, LORA_W - before - w.shape[1]), (0, 0)))
    lp = {
        "rwkv_w": rw,
        "mu": row3(rwkv_mu), "w0": row3(rwkv_w0), "a0": row3(rwkv_a0),
        "w2p": zpad(rwkv_w2, 0), "a2p": zpad(rwkv_a2, W_LORA), "g2p": zpad(rwkv_g2, W_LORA + A_LORA),
        "k_k": row3(rwkv_k_k), "k_a": row3(rwkv_k_a), "r_k": row3(rwkv_r_k),
        "ln_w": row3(rwkv_ln_w), "ln_b": row3(rwkv_ln_b),
        "conv_w": conv_w, "conv_b": row3(conv_b),
        "q_gain": row3(q_norm), "k_gain": row3(k_norm),
        "q_gain_pair": row3(jnp.tile(q_norm, (1, LANES // HEAD_DIM))),
        "k_gain_pair": row3(jnp.tile(k_norm, (1, LANES // HEAD_DIM))),
    }
    g_ffn1, g_mix, g_ffn2 = row3(ffn1_norm), row3(mix_norm), row3(ffn2_norm)
    slopes = 2.0 ** (-8.0 * jnp.arange(1, att_heads + 1, dtype=F32) / att_heads)
    slope_col = slopes[:, None]

    m = n_seq * t
    xp = x_prompt.reshape(m, d)
    xs = x_sample.reshape(nb, d)
    tm_ffn, tf = 1024, 256
    tm_in, tn_in = 1024, 1280
    tm_out = 512
    p_states, s_states = [], []
    for l in range(depth):
        xp, xs = _ffn(xp, xs, g_ffn1, ffn1_w_gate, ffn1_w_up, ffn1_w_down, l, tm_ffn, tf)

        pp = _norm_matmul(xp, g_mix, w_in16, l, tm_in, tn_in)
        o_att, k_t, v_t = _attention_prompt(pp, col_att, aw, lp["q_gain_pair"], lp["k_gain_pair"], slopes,
                                            l, n_seq, t)
        r_, lw_, k_, v_, a_, b_, g_ = _rwkv_prep_prompt(pp, col_rwkv, lp, l, t, 256)
        y_rwkv, wkv_p = _rwkv_scan_prompt(r_, lw_, k_, v_, a_, b_, g_, lp, l, n_seq, t, 4 if n_seq % 4 == 0 else 1)
        xp, z_last = _out_proj_conv(xp, o_att, y_rwkv, pp, col_conv, cw, lp["conv_w"], lp["conv_b"], w_out16,
                                    l, t, tm_out)
        rows = lambda x_t: jnp.transpose(x_t.reshape(n_seq, att_heads, HEAD_DIM, t), (0, 3, 1, 2))
        p_states.append((
            rows(k_t), rows(v_t),
            wkv_p, pp.reshape(n_seq, t, -1)[:, -1, col_rwkv:col_rwkv + rp],
            z_last.reshape(n_seq, t // tm_out, 8, cw)[:, -1, 8 - (CONV_K - 1):]))

        ps = _norm_matmul(xs, g_mix, w_in16, l, nb, tn_in)
        so, sy, syc, sk, sv, swkv, sconv = _sample_mixers(
            ps, (col_att, col_rwkv, col_conv), cache_swa_k, cache_swa_v, state_wkv, state_shift, state_conv,
            lp, slope_col, l)
        xs = _out_proj(xs, so, sy, syc, w_out16, l, nb)
        s_states.append((sk[:, None], sv[:, None], swkv, ps[:, col_rwkv:col_rwkv + rp], sconv))

        xp, xs = _ffn(xp, xs, g_ffn2, ffn2_w_gate, ffn2_w_up, ffn2_w_down, l, tm_ffn, tf)

    stacked = lambda states, i: jnp.stack([st[i] for st in states], axis=0)
    return (xp.reshape(n_seq, t, d), xs.reshape(nb, 1, d),
            stacked(p_states, 0), stacked(p_states, 1), stacked(p_states, 2), stacked(p_states, 3),
            stacked(p_states, 4),
            stacked(s_states, 0), stacked(s_states, 1), stacked(s_states, 2), stacked(s_states, 3),
            stacked(s_states, 4))
```

```python
import functools

import jax
import jax.numpy as jnp
from jax import lax
from jax.experimental import pallas as pl
from jax.experimental.pallas import tpu as pltpu

F32 = jnp.float32
BF16 = jnp.bfloat16

HEAD_DIM = 64
HEAD_SHIFT = 6
LANES = 128
NORM_EPS = 1e-6
RWKV_GN_EPS = 64e-5
DILATION_GROUPS = ((128, 1), (512, 4), (2048, 16))
ATT_SPAN = 128
W_LORA, A_LORA, G_LORA = 64, 64, 128
LORA_W = W_LORA + A_LORA + G_LORA
CONV_K = 3
RWKV_CHUNK = 64
VMEM_LIMIT = 56 * 1024 * 1024
NEG_BIG = -1e30

_NT = (((1,), (1,)), ((), ()))
_NN = (((1,), (0,)), ((), ()))
_BNN = (((2,), (1,)), ((0,), (0,)))
_BNT = (((2,), (2,)), ((0,), (0,)))
_BTN = (((1,), (1,)), ((0,), (0,)))


def _cparams(sem):
    return pltpu.CompilerParams(dimension_semantics=sem, vmem_limit_bytes=VMEM_LIMIT)


def _rms(x, g):
    return x * lax.rsqrt(jnp.mean(x * x, axis=-1, keepdims=True) + NORM_EPS) * g


def _split2(x):
    hi = x.astype(BF16)
    lo = (x - hi.astype(F32)).astype(BF16)
    return hi, lo


def _dot3(a, b, dims=_NN):
    ah, al = _split2(a)
    bh, bl = _split2(b)
    d = lambda x, y: lax.dot_general(x, y, dims, preferred_element_type=F32)
    return d(ah, bh) + (d(al, bh) + d(ah, bl))


def _dot1(a, b, dims=_NN):
    return lax.dot_general(a.astype(BF16), b.astype(BF16), dims, preferred_element_type=F32)


def _split3(x):
    h1 = x.astype(BF16)
    r1 = x - h1.astype(F32)
    h2 = r1.astype(BF16)
    h3 = (r1 - h2.astype(F32)).astype(BF16)
    return h1, h2, h3


def _dot_sel(x, sel):
    return sum(jnp.dot(h, sel, preferred_element_type=F32) for h in _split3(x))


def _cumsum_rows(x, tri_incl):
    return sum(jnp.dot(tri_incl, h, preferred_element_type=F32) for h in _split3(x))


def _head_blockdiag(n, dtype):
    r = lax.broadcasted_iota(jnp.int32, (n, n), 0) >> HEAD_SHIFT
    c = lax.broadcasted_iota(jnp.int32, (n, n), 1) >> HEAD_SHIFT
    return (r == c).astype(dtype)


def _segsum64(x):
    bd = _head_blockdiag(LANES, BF16)
    parts = [_dot_sel(x[:, i:i + LANES], bd) for i in range(0, x.shape[1], LANES)]
    return parts[0] if len(parts) == 1 else jnp.concatenate(parts, axis=1)


def _ffn_body(x_ref, xs_ref, g_ref, wg_ref, wu_ref, wd_ref, o_ref, os_ref, xn_ref):
    j = pl.program_id(1)
    tm, d = x_ref.shape

    @pl.when(j == 0)
    def _():
        xn_ref[0:tm, :] = _rms(x_ref[...], g_ref[...]).astype(BF16)
        xn_ref[tm:, :] = _rms(xs_ref[...], g_ref[...]).astype(BF16)
        o_ref[...] = jnp.zeros_like(o_ref)
        os_ref[...] = jnp.zeros_like(os_ref)

    xn = xn_ref[...]
    h = jnp.dot(xn, wg_ref[...].astype(BF16), preferred_element_type=F32)
    u = jnp.dot(xn, wu_ref[...].astype(BF16), preferred_element_type=F32)
    act = (h * jax.nn.sigmoid(h) * u).astype(BF16)
    half = d // 2
    for c in range(2):
        cols = slice(c * half, (c + 1) * half)
        part = jnp.dot(act, wd_ref[:, cols].astype(BF16), preferred_element_type=F32)
        o_ref[:, cols] += part[0:tm]
        os_ref[:, cols] += part[tm:]

    @pl.when(j == pl.num_programs(1) - 1)
    def _():
        o_ref[...] = x_ref[...] + 0.5 * o_ref[...]
        os_ref[...] = xs_ref[...] + 0.5 * os_ref[...]


def _ffn(x, xs, gain, wg, wu, wd, layer, tm, tf):
    m, d = x.shape
    ms = xs.shape[0]
    f = wg.shape[2]
    return pl.pallas_call(
        _ffn_body,
        grid=(m // tm, f // tf),
        in_specs=[
            pl.BlockSpec((tm, d), lambda i, j: (i, 0)),
            pl.BlockSpec((ms, d), lambda i, j: (0, 0)),
            pl.BlockSpec((None, 1, d), lambda i, j: (layer, 0, 0)),
            pl.BlockSpec((None, d, tf), lambda i, j: (layer, 0, j)),
            pl.BlockSpec((None, d, tf), lambda i, j: (layer, 0, j)),
            pl.BlockSpec((None, tf, d), lambda i, j: (layer, j, 0)),
        ],
        out_specs=[pl.BlockSpec((tm, d), lambda i, j: (i, 0)),
                   pl.BlockSpec((ms, d), lambda i, j: (0, 0))],
        out_shape=[jax.ShapeDtypeStruct((m, d), F32), jax.ShapeDtypeStruct((ms, d), F32)],
        scratch_shapes=[pltpu.VMEM((tm + ms, d), BF16)],
        compiler_params=_cparams(("arbitrary", "arbitrary")),
        name="ffn",
    )(x, xs, gain, wg, wu, wd)


def _norm_matmul_body(x_ref, xs_ref, g_ref, w_ref, o_ref, os_ref, xn_ref):
    tm = x_ref.shape[0]

    @pl.when(pl.program_id(1) == 0)
    def _():
        xn_ref[0:tm, :] = _rms(x_ref[...], g_ref[...]).astype(BF16)
        xn_ref[tm:, :] = _rms(xs_ref[...], g_ref[...]).astype(BF16)

    out = jnp.dot(xn_ref[...], w_ref[...], preferred_element_type=F32)
    o_ref[...] = out[0:tm]
    os_ref[...] = out[tm:]


def _norm_matmul(x, xs, gain, w, layer, tm, tn):
    m, d = x.shape
    ms = xs.shape[0]
    n = w.shape[2]
    return pl.pallas_call(
        _norm_matmul_body,
        grid=(m // tm, n // tn),
        in_specs=[
            pl.BlockSpec((tm, d), lambda i, j: (i, 0)),
            pl.BlockSpec((ms, d), lambda i, j: (0, 0)),
            pl.BlockSpec((None, 1, d), lambda i, j: (layer, 0, 0)),
            pl.BlockSpec((None, d, tn), lambda i, j: (layer, 0, j)),
        ],
        out_specs=[pl.BlockSpec((tm, tn), lambda i, j: (i, j)),
                   pl.BlockSpec((ms, tn), lambda i, j: (0, j))],
        out_shape=[jax.ShapeDtypeStruct((m, n), F32), jax.ShapeDtypeStruct((ms, n), F32)],
        scratch_shapes=[pltpu.VMEM((tm + ms, d), BF16)],
        compiler_params=_cparams(("arbitrary", "arbitrary")),
        name="norm_matmul",
    )(x, xs, gain, w)


def _mix_out(x, o_att, y_rwkv, y_conv, w_ref):
    wa, wr = o_att.shape[1], y_rwkv.shape[1]
    acc = jnp.dot(o_att.astype(BF16), w_ref[0:wa, :], preferred_element_type=F32)
    acc += jnp.dot(y_rwkv.astype(BF16), w_ref[wa:wa + wr, :], preferred_element_type=F32)
    acc += jnp.dot(y_conv.astype(BF16), w_ref[wa + wr:, :], preferred_element_type=F32)
    return x + acc


def _out_proj_body(x_ref, a_ref, r_ref, c_ref, w_ref, o_ref):
    o_ref[...] = _mix_out(x_ref[...], a_ref[...], r_ref[...], c_ref[...], w_ref)


def _out_proj_conv_body(x_ref, a_ref, r_ref, u_ref, gb_ref, gc_ref, pu_ref, pgc_ref, cw_ref, cb_ref, w_ref,
                        o_ref, zl_ref, zc_ref, *, blocks_per_seq):
    tm = u_ref.shape[0]
    first = (pl.program_id(0) % blocks_per_seq) == 0
    z = gc_ref[...] * u_ref[...]
    zc_ref[0:8, :] = jnp.where(first, 0.0, pgc_ref[...] * pu_ref[...])
    zc_ref[8:, :] = z
    yc = cb_ref[...] + cw_ref[0:1, :] * zc_ref[pl.ds(6, tm), :]
    yc = yc + cw_ref[1:2, :] * zc_ref[pl.ds(7, tm), :]
    yc = yc + cw_ref[2:3, :] * z
    zl_ref[...] = zc_ref[pl.ds(tm, 8), :]
    o_ref[...] = _mix_out(x_ref[...], a_ref[...], r_ref[...], gb_ref[...] * yc, w_ref)


def _out_proj_conv(x, o_att, y_rwkv, p, col0, cw, conv_w, conv_b, w, layer, t, tm):
    m, d = x.shape
    assert col0 % LANES == 0 and t % tm == 0
    cur = lambda s: pl.BlockSpec((pl.Element(tm), pl.Element(cw)),
                                 lambda i: (pl.multiple_of(i * tm, tm), col0 + s * cw))
    prev = lambda s: pl.BlockSpec((pl.Element(8), pl.Element(cw)),
                                  lambda i: (pl.multiple_of(jnp.maximum(i * tm - 8, 0), 8), col0 + s * cw))
    return pl.pallas_call(
        functools.partial(_out_proj_conv_body, blocks_per_seq=t // tm),
        grid=(m // tm,),
        in_specs=[
            pl.BlockSpec((tm, d), lambda i: (i, 0)),
            pl.BlockSpec((tm, o_att.shape[1]), lambda i: (i, 0)),
            pl.BlockSpec((tm, y_rwkv.shape[1]), lambda i: (i, 0)),
            cur(0), cur(1), cur(2), prev(0), prev(2),
            pl.BlockSpec((None, CONV_K, cw), lambda i: (layer, 0, 0)),
            pl.BlockSpec((None, 1, cw), lambda i: (layer, 0, 0)),
            pl.BlockSpec((None, w.shape[1], d), lambda i: (layer, 0, 0), pipeline_mode=pl.Buffered(1)),
        ],
        out_specs=[pl.BlockSpec((tm, d), lambda i: (i, 0)), pl.BlockSpec((8, cw), lambda i: (i, 0))],
        out_shape=[jax.ShapeDtypeStruct((m, d), F32), jax.ShapeDtypeStruct((m // tm * 8, cw), F32)],
        scratch_shapes=[pltpu.VMEM((tm + 8, cw), F32)],
        compiler_params=_cparams(("parallel",)),
        name="out_proj_conv",
    )(x, o_att, y_rwkv, p, p, p, p, p, conv_w, conv_b, w)


def _out_proj(x, o_att, y_rwkv, y_conv, w, layer, tm):
    m, d = x.shape
    return pl.pallas_call(
        _out_proj_body,
        grid=(m // tm,),
        in_specs=[
            pl.BlockSpec((tm, d), lambda i: (i, 0)),
            pl.BlockSpec((tm, o_att.shape[1]), lambda i: (i, 0)),
            pl.BlockSpec((tm, y_rwkv.shape[1]), lambda i: (i, 0)),
            pl.BlockSpec((tm, y_conv.shape[1]), lambda i: (i, 0)),
            pl.BlockSpec((None, w.shape[1], d), lambda i: (layer, 0, 0), pipeline_mode=pl.Buffered(1)),
        ],
        out_specs=pl.BlockSpec((tm, d), lambda i: (i, 0)),
        out_shape=jax.ShapeDtypeStruct((m, d), F32),
        compiler_params=_cparams(("parallel",)),
        name="out_proj",
    )(x, o_att, y_rwkv, y_conv, w)


def _pair_headnorm(x, g, lo_half):
    x2 = x * x
    s0 = jnp.sum(jnp.where(lo_half, x2, 0.0), axis=-1, keepdims=True)
    s1 = jnp.sum(jnp.where(lo_half, 0.0, x2), axis=-1, keepdims=True)
    ms = jnp.where(lo_half, s0, s1) * (1.0 / HEAD_DIM)
    return x * lax.rsqrt(ms + NORM_EPS) * g


def _attn_body(slopes_ref, q_ref, k_ref, v_ref, qg_ref, kg_ref, *refs, n_carried):
    o_ref, kt_ref, vt_ref, qs_ref, kn_ref, bias_ref, og_ref, lse_ref = refs[n_carried:]
    hp = pl.program_id(1)
    t = q_ref.shape[0]
    span = ATT_SPAN
    lo_half = lax.broadcasted_iota(jnp.int32, (1, LANES), 1) < HEAD_DIM

    kn = _pair_headnorm(k_ref[...], kg_ref[...], lo_half)
    kn_ref[...] = kn
    kt_ref[...] = kn.T
    vt_ref[...] = v_ref[...].T
    qs_ref[...] = _pair_headnorm(q_ref[...], qg_ref[...], lo_half) * (HEAD_DIM ** -0.5)

    qi = lax.broadcasted_iota(jnp.int32, (span, 2 * span), 0)
    ki = lax.broadcasted_iota(jnp.int32, (span, 2 * span), 1)
    steps = qi + span - ki
    valid = (steps >= 0) & (steps <= span)
    for g, (_, dil) in enumerate(DILATION_GROUPS):
        dist = (steps * dil).astype(F32)
        for h in range(2):
            bias_ref[g, h * span:(h + 1) * span, :] = jnp.where(valid, -(slopes_ref[2 * hp + h] * dist), NEG_BIG)

    def unit(g, dil, start, has_prev):
        rows = pl.ds(start, span, stride=dil)
        qb = qs_ref[rows, :]
        q2 = jnp.concatenate([jnp.where(lo_half, qb, 0.0), jnp.where(lo_half, 0.0, qb)], axis=0).astype(BF16)
        if has_prev:
            prow = pl.ds(start - span * dil, span, stride=dil)
            kb = jnp.concatenate([kn_ref[prow, :], kn_ref[rows, :]], axis=0)
            vb = jnp.concatenate([v_ref[prow, :], v_ref[rows, :]], axis=0)
            bias = bias_ref[g]
        else:
            kb = kn_ref[rows, :]
            vb = v_ref[rows, :]
            bias = bias_ref[g, :, span:]
        s = lax.dot_general(q2, kb.astype(BF16), _NT, preferred_element_type=F32) + bias
        m = jnp.max(s, axis=-1, keepdims=True)
        p = jnp.exp(s - m)
        l = jnp.sum(p, axis=-1, keepdims=True)
        o = jnp.dot(p.astype(BF16), vb.astype(BF16), preferred_element_type=F32) / l
        lse = m + jnp.log(l)
        og_ref[g, rows, :] = jnp.where(lo_half, o[0:span], o[span:])
        lse_ref[g, rows, :] = jnp.where(lo_half, lse[0:span], lse[span:])

    for g, (window, dil) in enumerate(DILATION_GROUPS):
        assert window // dil == span
        nb = t // (span * dil)

        def residue(r, carry, g=g, dil=dil, nb=nb):
            unit(g, dil, r, False)
            if nb > 1:
                def blk(b, c):
                    unit(g, dil, b * (span * dil) + r, True)
                    return c
                lax.fori_loop(1, nb, blk, 0, unroll=3 if nb <= 4 else 5)
            return carry

        if dil == 1:
            residue(0, 0)
        else:
            lax.fori_loop(0, dil, residue, 0, unroll=4 if nb == 1 else 1)

    l0, l1, l2 = lse_ref[0], lse_ref[1], lse_ref[2]
    mx = jnp.maximum(jnp.maximum(l0, l1), l2)
    w0, w1, w2 = jnp.exp(l0 - mx), jnp.exp(l1 - mx), jnp.exp(l2 - mx)
    o_ref[...] = (og_ref[0] * w0 + og_ref[1] * w1 + og_ref[2] * w2) / (w0 + w1 + w2)


def _attention_prompt(p, col0, att_w, qg, kg, slopes, layer, depth, n_seq, t, kv_stack):
    m = p.shape[0]
    pairs = att_w // LANES
    assert col0 % LANES == 0
    blk = lambda off: pl.BlockSpec((t, LANES), lambda n, hp: (n, off + hp))
    c0 = col0 // LANES
    gain = pl.BlockSpec((None, 1, LANES), lambda n, hp: (layer, 0, 0))
    tr_spec = pl.BlockSpec((None, None, LANES, t), lambda n, hp: (layer, n, hp, 0))
    tr_shape = jax.ShapeDtypeStruct((depth, n_seq, att_w, t), F32)
    carried = () if kv_stack is None else tuple(kv_stack)
    n_in = 6
    return pl.pallas_call(
        functools.partial(_attn_body, n_carried=len(carried)),
        grid=(n_seq, pairs),
        in_specs=[pl.BlockSpec(memory_space=pltpu.SMEM),
                  blk(c0), blk(c0 + pairs), blk(c0 + 2 * pairs), gain, gain]
                 + [pl.BlockSpec(memory_space=pl.ANY)] * len(carried),
        out_specs=[blk(0), tr_spec, tr_spec],
        out_shape=[jax.ShapeDtypeStruct((m, att_w), F32), tr_shape, tr_shape],
        input_output_aliases={n_in + i: 1 + i for i in range(len(carried))},
        scratch_shapes=[pltpu.VMEM((t, LANES), F32),
                        pltpu.VMEM((t, LANES), F32),
                        pltpu.VMEM((3, 2 * ATT_SPAN, 2 * ATT_SPAN), F32),
                        pltpu.VMEM((3, t, LANES), F32),
                        pltpu.VMEM((3, t, LANES), F32)],
        compiler_params=_cparams(("parallel", "arbitrary")),
        name="attn_prompt",
    )(slopes, p, p, p, qg, kg, *carried)


def _softplus(z):
    return jnp.maximum(z, 0.0) + jnp.log(1.0 + jnp.exp(-jnp.abs(z)))


def _rwkv_token_math(p_r, p_k, p_v, p_l, q_r, q_k, q_v, q_l, mu_r, mu_k, mu_v, mu_l,
                     w0, w2p, a0, a2p, g2p, k_k, k_a):
    xr = p_r + (q_r - p_r) * mu_r
    xk = p_k + (q_k - p_k) * mu_k
    xv = p_v + (q_v - p_v) * mu_v
    xl = p_l + (q_l - p_l) * mu_l
    w_log = -_softplus(-(w0 + _dot3(jnp.tanh(xl), w2p))) - 0.5
    log_decay = -jnp.exp(w_log)
    sig_l = jax.nn.sigmoid(xl)
    a_gate = jax.nn.sigmoid(a0 + _dot3(xl, a2p))
    g = _dot3(sig_l, g2p)
    kk = xk * k_k
    kk = kk / jnp.maximum(jnp.sqrt(_segsum64(kk * kk)), 1e-12)
    kmod = xk * (1.0 + (a_gate - 1.0) * k_a)
    return xr, log_decay, kmod, xv, -kk, kk * a_gate, g


def _segsum64_many(xs):
    rows, width = xs[0].shape
    if rows % 16 != 0:
        return [_segsum64(x) for x in xs]
    nsl = width // LANES
    pieces = [part[:, i * LANES:(i + 1) * LANES]
              for x in xs for part in _split3(x) for i in range(nsl)]
    res = jnp.dot(jnp.concatenate(pieces, axis=0), _head_blockdiag(LANES, BF16), preferred_element_type=F32)
    outs = []
    for n in range(len(xs)):
        parts = [jnp.concatenate([res[((3 * n + j) * nsl + i) * rows:((3 * n + j) * nsl + i + 1) * rows]
                                  for i in range(nsl)], axis=1) for j in range(3)]
        outs.append(parts[0] + parts[1] + parts[2])
    return outs


def _rwkv_post(y, r, k, v, g, ln_w, ln_b, r_k):
    sum_y, sum_rk = _segsum64_many([y, r * k * r_k])
    yc = y - sum_y * (1.0 / HEAD_DIM)
    var_y = _segsum64_many([yc * yc])[0] * (1.0 / HEAD_DIM)
    yn = yc * lax.rsqrt(var_y + RWKV_GN_EPS) * ln_w + ln_b
    return (yn + sum_rk * v) * g


def _mixprep_body(pr_ref, pk_ref, pv_ref, plo_ref, qr_ref, qk_ref, qv_ref, qlo_ref,
                  mur_ref, muk_ref, muv_ref, mul_ref, w0_ref, w2_ref, a0_ref, a2_ref, g2_ref,
                  kk_ref, ka_ref,
                  r_out, lw_out, k_out, v_out, a_out, b_out, g_out,
                  sr_ref, sk_ref, sv_ref, sl_ref, *, blocks_per_seq):
    tm = pr_ref.shape[0]
    first = (pl.program_id(0) % blocks_per_seq) == 0

    def shifted(cur_ref, prev8_ref, s_ref):
        s_ref[0:8, :] = jnp.where(first, 0.0, prev8_ref[...])
        s_ref[8:, :] = cur_ref[...]
        return s_ref[pl.ds(7, tm), :]

    outs = _rwkv_token_math(
        pr_ref[...], pk_ref[...], pv_ref[...], plo_ref[...],
        shifted(pr_ref, qr_ref, sr_ref), shifted(pk_ref, qk_ref, sk_ref),
        shifted(pv_ref, qv_ref, sv_ref), shifted(plo_ref, qlo_ref, sl_ref),
        mur_ref[...], muk_ref[...], muv_ref[...], mul_ref[...],
        w0_ref[...], w2_ref[...], a0_ref[...], a2_ref[...], g2_ref[...], kk_ref[...], ka_ref[...])
    for ref, val in zip((r_out, lw_out, k_out, v_out, a_out, b_out, g_out), outs):
        ref[...] = val


def _rwkv_prep_prompt(p, col0, lp, layer, t, tm):
    m = p.shape[0]
    rw = lp["rwkv_w"]
    assert tm % 8 == 0 and t % tm == 0 and col0 % rw == 0 and (col0 + 3 * rw) % LORA_W == 0
    c_slab = col0 // rw
    c_lora = (col0 + 3 * rw) // LORA_W
    cur = lambda w, c: pl.BlockSpec((tm, w), lambda i: (i, c))
    prev = lambda w, c: pl.BlockSpec((8, w), lambda i: (jnp.maximum(i * (tm // 8) - 1, 0), c))
    vec = lambda w, c: pl.BlockSpec((None, 1, w), lambda i: (layer, 0, c))
    mat = pl.BlockSpec((None, LORA_W, rw), lambda i: (layer, 0, 0))
    out = jax.ShapeDtypeStruct((m, rw), F32)
    return pl.pallas_call(
        functools.partial(_mixprep_body, blocks_per_seq=t // tm),
        grid=(m // tm,),
        in_specs=[cur(rw, c_slab), cur(rw, c_slab + 1), cur(rw, c_slab + 2), cur(LORA_W, c_lora),
                  prev(rw, c_slab), prev(rw, c_slab + 1), prev(rw, c_slab + 2), prev(LORA_W, c_lora),
                  vec(rw, 0), vec(rw, 1), vec(rw, 2), vec(LORA_W, (3 * rw) // LORA_W),
                  vec(rw, 0), mat, vec(rw, 0), mat, mat, vec(rw, 0), vec(rw, 0)],
        out_specs=[pl.BlockSpec((tm, rw), lambda i: (i, 0))] * 7,
        out_shape=[out] * 7,
        scratch_shapes=[pltpu.VMEM((tm + 8, rw), F32)] * 3 + [pltpu.VMEM((tm + 8, LORA_W), F32)],
        compiler_params=_cparams(("parallel",)),
        name="rwkv_prep",
    )(p, p, p, p, p, p, p, p,
      lp["mu"], lp["mu"], lp["mu"], lp["mu"],
      lp["w0"], lp["w2p"], lp["a0"], lp["a2p"], lp["g2p"], lp["k_k"], lp["k_a"])


def _scan_chunk(r, lw, cum, k, v, a, b, s0):
    c = r.shape[1]
    c2 = 2 * c
    lane = lax.broadcasted_iota(jnp.int32, (1, 1, LANES), 2)
    m0 = (lane < HEAD_DIM).astype(F32)
    m1 = 1.0 - m0
    stack = lambda x: jnp.concatenate([x * m0, x * m1], axis=1)

    cum_prev = cum - lw
    cum_last = cum[:, c - 1:c, :]
    e_pos = jnp.exp(cum)
    e_neg = jnp.exp(-cum)
    e_rem = jnp.exp(cum_last - cum)
    at = a * jnp.exp(cum_prev)
    rt = r * e_pos
    bt = b * e_neg
    kt = k * e_neg
    bh = b * e_rem
    kh = k * e_rem

    big = _dot1(jnp.concatenate([stack(at), stack(rt)], axis=1),
                jnp.concatenate([stack(bt), stack(kt)], axis=1), _BNT)
    row = lax.broadcasted_iota(jnp.int32, (1, c2, c2), 1)
    col = lax.broadcasted_iota(jnp.int32, (1, c2, c2), 2)
    assert c == HEAD_DIM
    same_head = (row >> HEAD_SHIFT) == (col >> HEAD_SHIFT)
    tr, tc = row & (c - 1), col & (c - 1)
    strict = same_head & (tr > tc)
    incl = same_head & (tr >= tc)
    lmat = jnp.where(strict, big[:, 0:c2, 0:c2], 0.0)
    ak = jnp.where(strict, big[:, 0:c2, c2:], 0.0)
    rb = jnp.where(incl, big[:, c2:, 0:c2], 0.0)
    rk = jnp.where(incl, big[:, c2:, c2:], 0.0)

    eye = (row == col).astype(F32)
    x = eye + jnp.where((row >> 1) == (col >> 1), lmat, 0.0)
    s, log2s = 2, 1
    while s < c:
        joins = ((row >> (log2s + 1)) == (col >> (log2s + 1))) & ((row & s) != 0) & ((col & s) == 0)
        x = x + _dot1(_dot1(x, jnp.where(joins, lmat, 0.0), _BNN), x, _BNN)
        s, log2s = 2 * s, log2s + 1

    v_st = stack(v)
    w = _dot1(jnp.concatenate([at, rt], axis=1), s0, _BNT)
    u_st = _dot1(x, stack(w[:, 0:c]) + _dot1(ak, v_st, _BNN), _BNN)
    y_st = _dot1(jnp.concatenate([rb, rk], axis=2), jnp.concatenate([u_st, v_st], axis=1), _BNN)
    y = w[:, c:] + y_st[:, 0:c] + y_st[:, c:]
    u = u_st[:, 0:c] + u_st[:, c:]
    upd = _dot1(jnp.concatenate([u, v], axis=1), jnp.concatenate([bh, kh], axis=1), _BTN)
    s_new = s0 * jnp.exp(cum_last) + _head_blockdiag(LANES, F32)[None] * upd
    return y, s_new


def _scan_body(r_ref, lw_ref, k_ref, v_ref, a_ref, b_ref, g_ref, lnw_ref, lnb_ref, rk_ref,
               y_ref, s_out_ref, s_ref):
    ch = pl.program_id(1)

    @pl.when(ch == 0)
    def _():
        s_ref[...] = jnp.zeros_like(s_ref)

    nseq, c, rw = r_ref.shape
    npair = rw // LANES
    pairs = lambda x: jnp.stack([x[s, :, p * LANES:(p + 1) * LANES]
                                 for s in range(nseq) for p in range(npair)], axis=0)
    flat = lambda ref: ref[...].reshape(nseq * c, rw)
    tri_incl = (lax.broadcasted_iota(jnp.int32, (c, c), 0)
                >= lax.broadcasted_iota(jnp.int32, (c, c), 1)).astype(BF16)
    r, lw, k, v = r_ref[...], lw_ref[...], k_ref[...], v_ref[...]
    cum = jnp.stack([_cumsum_rows(lw[s], tri_incl) for s in range(nseq)], axis=0)
    y, s_new = _scan_chunk(pairs(r), pairs(lw), pairs(cum), pairs(k), pairs(v),
                           pairs(a_ref[...]), pairs(b_ref[...]), s_ref[...])
    s_ref[...] = s_new
    y = jnp.concatenate([jnp.concatenate([y[s * npair + p] for p in range(npair)], axis=1)
                         for s in range(nseq)], axis=0)
    out = _rwkv_post(y, flat(r_ref), flat(k_ref), flat(v_ref), flat(g_ref),
                     lnw_ref[...], lnb_ref[...], rk_ref[...])
    y_ref[...] = out.reshape(nseq, c, rw)

    @pl.when(ch == pl.num_programs(1) - 1)
    def _():
        for s in range(nseq):
            for p in range(npair):
                s_out_ref[s, 2 * p] = s_ref[s * npair + p, 0:HEAD_DIM, 0:HEAD_DIM]
                s_out_ref[s, 2 * p + 1] = s_ref[s * npair + p, HEAD_DIM:, HEAD_DIM:]


def _rwkv_scan_prompt(r, lw, k, v, a, b, g, lp, layer, n_seq, t, seq_per_step):
    m, rw = r.shape
    c = RWKV_CHUNK
    heads = rw // HEAD_DIM
    assert n_seq % seq_per_step == 0
    tok = pl.BlockSpec((seq_per_step, c, rw), lambda n, ch: (n, ch, 0))
    vec = pl.BlockSpec((None, 1, rw), lambda n, ch: (layer, 0, 0))
    by_seq = lambda x: x.reshape(n_seq, t, rw)
    y, wkv = pl.pallas_call(
        _scan_body,
        grid=(n_seq // seq_per_step, t // c),
        in_specs=[tok] * 7 + [vec] * 3,
        out_specs=[tok, pl.BlockSpec((seq_per_step, heads, HEAD_DIM, HEAD_DIM), lambda n, ch: (n, 0, 0, 0))],
        out_shape=[jax.ShapeDtypeStruct((n_seq, t, rw), F32),
                   jax.ShapeDtypeStruct((n_seq, heads, HEAD_DIM, HEAD_DIM), F32)],
        scratch_shapes=[pltpu.VMEM((seq_per_step * (rw // LANES), LANES, LANES), F32)],
        compiler_params=_cparams(("parallel", "arbitrary")),
        name="rwkv_scan",
    )(by_seq(r), by_seq(lw), by_seq(k), by_seq(v), by_seq(a), by_seq(b), by_seq(g),
      lp["ln_w"], lp["ln_b"], lp["r_k"])
    return y.reshape(m, rw), wkv


def _sample_body(slope_ref, p_ref, kc_ref, vc_ref,
                 wkv_ref, shift_ref, cst_ref, qg_ref, kg_ref,
                 mu_ref, w0_ref, w2_ref, a0_ref, a2_ref, g2_ref, kk_ref, ka_ref,
                 rk_ref, lnw_ref, lnb_ref, cw_ref, cb_ref,
                 oatt_ref, yr_ref, yc_ref, kn_ref, vn_ref, wkvn_ref, cstn_ref,
                 *, col_att, col_rwkv, col_conv):
    heads_a = oatt_ref.shape[0]
    aw = heads_a * HEAD_DIM
    rw = yr_ref.shape[1]
    cw = yc_ref.shape[1]
    rp = shift_ref.shape[1]
    rows8 = lambda x: jnp.broadcast_to(x, (8, x.shape[1]))
    p_all = p_ref[...]

    by_head = lambda row: jnp.concatenate(
        [row[:, h * HEAD_DIM:(h + 1) * HEAD_DIM] for h in range(heads_a)], axis=0)
    ca = col_att
    q, k, v_new = (by_head(p_all[:, ca:ca + aw]), by_head(p_all[:, ca + aw:ca + 2 * aw]),
                   by_head(p_all[:, ca + 2 * aw:ca + 3 * aw]))
    qn = _rms(q, qg_ref[...]) * (HEAD_DIM ** -0.5)
    kn = _rms(k, kg_ref[...])
    kn_ref[...] = kn
    vn_ref[...] = v_new
    s_new = jnp.sum(kn * qn, axis=-1, keepdims=True)
    slope = slope_ref[...]
    past = kc_ref.shape[2]
    er = lax.broadcasted_iota(jnp.int32, (HEAD_DIM, HEAD_DIM), 0)
    ec = lax.broadcasted_iota(jnp.int32, (HEAD_DIM, HEAD_DIM), 1)
    eye = er == ec
    q_col = jnp.stack([jnp.sum(jnp.where(eye, qn[h:h + 1, :], 0.0), axis=1, keepdims=True)
                       for h in range(heads_a)], axis=0)
    dist = past - lax.broadcasted_iota(jnp.int32, (1, past), 1)
    biased = jnp.sum(kc_ref[...] * q_col, axis=1) - slope * dist.astype(F32)
    parts = []
    for window, dil in DILATION_GROUPS:
        assert dil & (dil - 1) == 0
        valid = ((dist & (dil - 1)) == 0) & (dist <= window)
        s = jnp.where(valid, biased, NEG_BIG)
        m = jnp.maximum(jnp.max(s, axis=1, keepdims=True), s_new)
        p = jnp.exp(s - m)
        p_new = jnp.exp(s_new - m)
        l = jnp.sum(p, axis=1, keepdims=True) + p_new
        parts.append((p, p_new, l, m + jnp.log(l)))
    mx = jnp.maximum(jnp.maximum(parts[0][3], parts[1][3]), parts[2][3])
    ws = [jnp.exp(lse - mx) for _, _, _, lse in parts]
    wsum = ws[0] + ws[1] + ws[2]
    scale_g = [w / (l * wsum) for w, (_, _, l, _) in zip(ws, parts)]
    coef = parts[0][0] * scale_g[0] + parts[1][0] * scale_g[1] + parts[2][0] * scale_g[2]
    coef_new = parts[0][1] * scale_g[0] + parts[1][1] * scale_g[1] + parts[2][1] * scale_g[2]
    o_col = jnp.sum(vc_ref[...] * coef[:, None, :], axis=2, keepdims=True)
    o_rows = jnp.concatenate([jnp.sum(jnp.where(eye, o_col[h], 0.0), axis=0, keepdims=True)
                              for h in range(heads_a)], axis=0)
    oatt_ref[...] = o_rows + coef_new * v_new

    pr = rows8(p_all[:, col_rwkv:col_rwkv + rp])
    sh = rows8(shift_ref[...])
    mu = mu_ref[...]
    c3 = 3 * rw
    r, lw, k, v, a, b, g = _rwkv_token_math(
        pr[:, 0:rw], pr[:, rw:2 * rw], pr[:, 2 * rw:c3], pr[:, c3:],
        sh[:, 0:rw], sh[:, rw:2 * rw], sh[:, 2 * rw:c3], sh[:, c3:],
        mu[:, 0:rw], mu[:, rw:2 * rw], mu[:, 2 * rw:c3], mu[:, c3:],
        w0_ref[...], w2_ref[...], a0_ref[...], a2_ref[...], g2_ref[...], kk_ref[...], ka_ref[...])
    decay = jnp.exp(lw)
    place_r = lax.broadcasted_iota(jnp.int32, (HEAD_DIM, rw), 0)
    place_c = lax.broadcasted_iota(jnp.int32, (HEAD_DIM, rw), 1)
    y_full = jnp.zeros((1, rw), F32)
    for h in range(rw // HEAD_DIM):
        hs = slice(h * HEAD_DIM, (h + 1) * HEAD_DIM)
        s_old = wkv_ref[h]
        sa = jnp.sum(s_old * a[0:1, hs], axis=1, keepdims=True)
        v_col = jnp.sum(jnp.where(eye, v[0:1, hs], 0.0), axis=1, keepdims=True)
        s_h = s_old * decay[0:1, hs] + sa * b[0:1, hs] + v_col * k[0:1, hs]
        wkvn_ref[h] = s_h
        y_col = jnp.sum(s_h * r[0:1, hs], axis=1, keepdims=True)
        y_full = y_full + jnp.sum(jnp.where(place_c == place_r + h * HEAD_DIM, y_col, 0.0),
                                  axis=0, keepdims=True)
    y8 = rows8(y_full)
    yr_ref[...] = _rwkv_post(y8, r, k, v, g, lnw_ref[...], lnb_ref[...], rk_ref[...])[0:1]

    c1 = col_conv
    z = p_all[:, c1 + 2 * cw:c1 + 3 * cw] * p_all[:, c1:c1 + cw]
    yc = cb_ref[...] + cw_ref[0:1, :] * cst_ref[0:1, :]
    yc = yc + cw_ref[1:2, :] * cst_ref[1:2, :]
    yc = yc + cw_ref[2:3, :] * z
    yc_ref[...] = p_all[:, c1 + cw:c1 + 2 * cw] * yc
    cstn_ref[0:1, :] = cst_ref[1:2, :]
    cstn_ref[1:2, :] = z


def _sample_mixers(p, cols, cache_k, cache_v, state_wkv, state_shift, state_conv, lp, slope_col, layer):
    nb = p.shape[0]
    depth, _, past, heads_a, _ = cache_k.shape
    rw = lp["rwkv_w"]
    rp = state_shift.shape[2]
    cw = state_conv.shape[3]
    heads = rw // HEAD_DIM
    span = ATT_SPAN

    row = lambda w: pl.BlockSpec((None, 1, w), lambda n: (n, 0, 0))
    lvec = lambda w: pl.BlockSpec((None, 1, w), lambda n: (layer, 0, 0))
    lmat = lambda r_, w: pl.BlockSpec((None, r_, w), lambda n: (layer, 0, 0))
    tile = pl.BlockSpec((None, heads_a, HEAD_DIM), lambda n: (n, 0, 0))

    assert past >= DILATION_GROUPS[-1][0]
    views = [jnp.transpose(c, (0, 1, 3, 4, 2)) for c in (cache_k, cache_v)]
    cache_specs = [pl.BlockSpec((None, None, heads_a, HEAD_DIM, past), lambda n: (layer, n, 0, 0, 0))] * 2

    outs = pl.pallas_call(
        functools.partial(_sample_body, col_att=cols[0], col_rwkv=cols[1], col_conv=cols[2]),
        grid=(nb,),
        in_specs=[pl.BlockSpec((heads_a, 1), lambda n: (0, 0)), row(p.shape[1])] + cache_specs + [
                  pl.BlockSpec((None, None, heads, HEAD_DIM, HEAD_DIM), lambda n: (layer, n, 0, 0, 0)),
                  pl.BlockSpec((None, None, 1, rp), lambda n: (layer, n, 0, 0)),
                  pl.BlockSpec((None, None, CONV_K - 1, cw), lambda n: (layer, n, 0, 0)),
                  lvec(HEAD_DIM), lvec(HEAD_DIM),
                  lvec(rp), lvec(rw), lmat(LORA_W, rw), lvec(rw), lmat(LORA_W, rw),
                  lmat(LORA_W, rw), lvec(rw), lvec(rw), lvec(rw), lvec(rw), lvec(rw),
                  lmat(CONV_K, cw), lvec(cw)],
        out_specs=[tile, row(rw), row(cw), tile, tile,
                   pl.BlockSpec((None, heads, HEAD_DIM, HEAD_DIM), lambda n: (n, 0, 0, 0)),
                   pl.BlockSpec((None, CONV_K - 1, cw), lambda n: (n, 0, 0))],
        out_shape=[jax.ShapeDtypeStruct((nb, heads_a, HEAD_DIM), F32), jax.ShapeDtypeStruct((nb, 1, rw), F32),
                   jax.ShapeDtypeStruct((nb, 1, cw), F32),
                   jax.ShapeDtypeStruct((nb, heads_a, HEAD_DIM), F32),
                   jax.ShapeDtypeStruct((nb, heads_a, HEAD_DIM), F32),
                   jax.ShapeDtypeStruct((nb, heads, HEAD_DIM, HEAD_DIM), F32),
                   jax.ShapeDtypeStruct((nb, CONV_K - 1, cw), F32)],
        compiler_params=_cparams(("parallel",)),
        name="sample_mixers",
    )(slope_col, p[:, None, :], *views,
      state_wkv, state_shift[:, :, None, :], state_conv,
      lp["q_gain"], lp["k_gain"],
      lp["mu"], lp["w0"], lp["w2p"], lp["a0"], lp["a2p"], lp["g2p"], lp["k_k"], lp["k_a"],
      lp["r_k"], lp["ln_w"], lp["ln_b"], lp["conv_w"], lp["conv_b"])
    o_att, y_rwkv, y_conv, k_new, v_new, wkv_new, conv_new = outs
    return o_att.reshape(nb, heads_a * HEAD_DIM), y_rwkv[:, 0], y_conv[:, 0], k_new, v_new, wkv_new, conv_new


def kernel(x_prompt, x_sample, cache_swa_k, cache_swa_v, state_wkv, state_shift, state_conv, ffn1_norm, ffn1_w_gate, ffn1_w_up, ffn1_w_down, mix_norm, w_in, q_norm, k_norm, rwkv_mu, rwkv_w0, rwkv_w2, rwkv_a0, rwkv_a2, rwkv_g2, rwkv_k_k, rwkv_k_a, rwkv_r_k, rwkv_ln_w, rwkv_ln_b, conv_w, conv_b, w_out, ffn2_norm, ffn2_w_gate, ffn2_w_up, ffn2_w_down):
    n_seq, t, d = x_prompt.shape
    nb = x_sample.shape[0]
    assert x_sample.shape[1] == 1
    depth = w_in.shape[0]
    att_heads, head_dim = cache_swa_k.shape[3], cache_swa_k.shape[4]
    assert head_dim == HEAD_DIM
    aw = att_heads * HEAD_DIM
    rw = rwkv_w0.shape[1]
    cw = conv_b.shape[1]
    rp = rwkv_mu.shape[1]
    assert rp == 3 * rw + LORA_W and w_in.shape[2] == 3 * aw + rp + 3 * cw

    row3 = lambda p: p.reshape(depth, 1, -1)
    col_att, col_rwkv, col_conv = 0, 3 * aw, 3 * aw + rp
    w_in16, w_out16 = w_in.astype(BF16), w_out.astype(BF16)
    zpad = lambda w, before: jnp.pad(w, ((0, 0), (before, LORA_W - before - w.shape[1]), (0, 0)))
    lp = {
        "rwkv_w": rw,
        "mu": row3(rwkv_mu), "w0": row3(rwkv_w0), "a0": row3(rwkv_a0),
        "w2p": zpad(rwkv_w2, 0), "a2p": zpad(rwkv_a2, W_LORA), "g2p": zpad(rwkv_g2, W_LORA + A_LORA),
        "k_k": row3(rwkv_k_k), "k_a": row3(rwkv_k_a), "r_k": row3(rwkv_r_k),
        "ln_w": row3(rwkv_ln_w), "ln_b": row3(rwkv_ln_b),
        "conv_w": conv_w, "conv_b": row3(conv_b),
        "q_gain": row3(q_norm), "k_gain": row3(k_norm),
        "q_gain_pair": row3(jnp.tile(q_norm, (1, LANES // HEAD_DIM))),
        "k_gain_pair": row3(jnp.tile(k_norm, (1, LANES // HEAD_DIM))),
    }
    g_ffn1, g_mix, g_ffn2 = row3(ffn1_norm), row3(mix_norm), row3(ffn2_norm)
    slopes = 2.0 ** (-8.0 * jnp.arange(1, att_heads + 1, dtype=F32) / att_heads)
    slope_col = slopes[:, None]

    m = n_seq * t
    xp = x_prompt.reshape(m, d)
    xs = x_sample.reshape(nb, d)
    tm_ffn, tf = 1024, 256
    tm_in, tn_in = 1024, 1280
    tm_out = 512
    p_states, s_states = [], []
    kv_stack = None
    for l in range(depth):
        xp, xs = _ffn(xp, xs, g_ffn1, ffn1_w_gate, ffn1_w_up, ffn1_w_down, l, tm_ffn, tf)
        pp, ps = _norm_matmul(xp, xs, g_mix, w_in16, l, tm_in, tn_in)

        o_att, *kv_stack = _attention_prompt(pp, col_att, aw, lp["q_gain_pair"], lp["k_gain_pair"], slopes,
                                             l, depth, n_seq, t, kv_stack)
        r_, lw_, k_, v_, a_, b_, g_ = _rwkv_prep_prompt(pp, col_rwkv, lp, l, t, 256)
        y_rwkv, wkv_p = _rwkv_scan_prompt(r_, lw_, k_, v_, a_, b_, g_, lp, l, n_seq, t, 4 if n_seq % 4 == 0 else 1)
        xp, z_last = _out_proj_conv(xp, o_att, y_rwkv, pp, col_conv, cw, lp["conv_w"], lp["conv_b"], w_out16,
                                    l, t, tm_out)
        p_states.append((
            wkv_p, pp.reshape(n_seq, t, -1)[:, -1, col_rwkv:col_rwkv + rp],
            z_last.reshape(n_seq, t // tm_out, 8, cw)[:, -1, 8 - (CONV_K - 1):]))

        so, sy, syc, sk, sv, swkv, sconv = _sample_mixers(
            ps, (col_att, col_rwkv, col_conv), cache_swa_k, cache_swa_v, state_wkv, state_shift, state_conv,
            lp, slope_col, l)
        xs = _out_proj(xs, so, sy, syc, w_out16, l, nb)
        s_states.append((sk[:, None], sv[:, None], swkv, ps[:, col_rwkv:col_rwkv + rp], sconv))

        xp, xs = _ffn(xp, xs, g_ffn2, ffn2_w_gate, ffn2_w_up, ffn2_w_down, l, tm_ffn, tf)

    stacked = lambda states, i: jnp.stack([st[i] for st in states], axis=0)
    rows = lambda x_t: jnp.transpose(x_t.reshape(depth, n_seq, att_heads, HEAD_DIM, t), (0, 1, 4, 2, 3))
    return (xp.reshape(n_seq, t, d), xs.reshape(nb, 1, d),
            rows(kv_stack[0]), rows(kv_stack[1]),
            stacked(p_states, 0), stacked(p_states, 1), stacked(p_states, 2),
            stacked(s_states, 0), stacked(s_states, 1), stacked(s_states, 2), stacked(s_states, 3),
            stacked(s_states, 4))
```

```python
import functools

import jax
import jax.numpy as jnp
from jax import lax
from jax.experimental import pallas as pl
from jax.experimental.pallas import tpu as pltpu

F32 = jnp.float32
BF16 = jnp.bfloat16

HEAD_DIM = 64
HEAD_SHIFT = 6
LANES = 128
NORM_EPS = 1e-6
RWKV_GN_EPS = 64e-5
DILATION_GROUPS = ((128, 1), (512, 4), (2048, 16))
ATT_SPAN = 128
W_LORA, A_LORA, G_LORA = 64, 64, 128
LORA_W = W_LORA + A_LORA + G_LORA
CONV_K = 3
RWKV_CHUNK = 64
VMEM_LIMIT = 56 * 1024 * 1024
NEG_BIG = -1e30

_NT = (((1,), (1,)), ((), ()))
_NN = (((1,), (0,)), ((), ()))
_BNN = (((2,), (1,)), ((0,), (0,)))
_BNT = (((2,), (2,)), ((0,), (0,)))
_BTN = (((1,), (1,)), ((0,), (0,)))


def _cparams(sem):
    return pltpu.CompilerParams(dimension_semantics=sem, vmem_limit_bytes=VMEM_LIMIT)


def _rms(x, g):
    return x * lax.rsqrt(jnp.mean(x * x, axis=-1, keepdims=True) + NORM_EPS) * g


def _split2(x):
    hi = x.astype(BF16)
    lo = (x - hi.astype(F32)).astype(BF16)
    return hi, lo


def _dot3(a, b, dims=_NN):
    ah, al = _split2(a)
    bh, bl = _split2(b)
    d = lambda x, y: lax.dot_general(x, y, dims, preferred_element_type=F32)
    return d(ah, bh) + (d(al, bh) + d(ah, bl))


def _dot1(a, b, dims=_NN):
    return lax.dot_general(a.astype(BF16), b.astype(BF16), dims, preferred_element_type=F32)


def _split3(x):
    h1 = x.astype(BF16)
    r1 = x - h1.astype(F32)
    h2 = r1.astype(BF16)
    h3 = (r1 - h2.astype(F32)).astype(BF16)
    return h1, h2, h3


def _dot_sel(x, sel):
    return sum(jnp.dot(h, sel, preferred_element_type=F32) for h in _split3(x))


def _cumsum_rows(x, tri_incl):
    return sum(jnp.dot(tri_incl, h, preferred_element_type=F32) for h in _split3(x))


def _head_blockdiag(n, dtype):
    r = lax.broadcasted_iota(jnp.int32, (n, n), 0) >> HEAD_SHIFT
    c = lax.broadcasted_iota(jnp.int32, (n, n), 1) >> HEAD_SHIFT
    return (r == c).astype(dtype)


def _segsum64(x):
    bd = _head_blockdiag(LANES, BF16)
    parts = [_dot_sel(x[:, i:i + LANES], bd) for i in range(0, x.shape[1], LANES)]
    return parts[0] if len(parts) == 1 else jnp.concatenate(parts, axis=1)


def _ffn_body(x_ref, xs_ref, g_ref, wg_ref, wu_ref, wd_ref, o_ref, os_ref, xn_ref):
    j = pl.program_id(1)
    tm, d = x_ref.shape

    @pl.when(j == 0)
    def _():
        xn_ref[0:tm, :] = _rms(x_ref[...], g_ref[...]).astype(BF16)
        xn_ref[tm:, :] = _rms(xs_ref[...], g_ref[...]).astype(BF16)
        o_ref[...] = jnp.zeros_like(o_ref)
        os_ref[...] = jnp.zeros_like(os_ref)

    xn = xn_ref[...]
    h = jnp.dot(xn, wg_ref[...].astype(BF16), preferred_element_type=F32)
    u = jnp.dot(xn, wu_ref[...].astype(BF16), preferred_element_type=F32)
    act = (h * jax.nn.sigmoid(h) * u).astype(BF16)
    half = d // 2
    for c in range(2):
        cols = slice(c * half, (c + 1) * half)
        part = jnp.dot(act, wd_ref[:, cols].astype(BF16), preferred_element_type=F32)
        o_ref[:, cols] += part[0:tm]
        os_ref[:, cols] += part[tm:]

    @pl.when(j == pl.num_programs(1) - 1)
    def _():
        o_ref[...] = x_ref[...] + 0.5 * o_ref[...]
        os_ref[...] = xs_ref[...] + 0.5 * os_ref[...]


def _ffn(x, xs, gain, wg, wu, wd, layer, tm, tf):
    m, d = x.shape
    ms = xs.shape[0]
    f = wg.shape[2]
    return pl.pallas_call(
        _ffn_body,
        grid=(m // tm, f // tf),
        in_specs=[
            pl.BlockSpec((tm, d), lambda i, j: (i, 0)),
            pl.BlockSpec((ms, d), lambda i, j: (0, 0)),
            pl.BlockSpec((None, 1, d), lambda i, j: (layer, 0, 0)),
            pl.BlockSpec((None, d, tf), lambda i, j: (layer, 0, j)),
            pl.BlockSpec((None, d, tf), lambda i, j: (layer, 0, j)),
            pl.BlockSpec((None, tf, d), lambda i, j: (layer, j, 0)),
        ],
        out_specs=[pl.BlockSpec((tm, d), lambda i, j: (i, 0)),
                   pl.BlockSpec((ms, d), lambda i, j: (0, 0))],
        out_shape=[jax.ShapeDtypeStruct((m, d), F32), jax.ShapeDtypeStruct((ms, d), F32)],
        scratch_shapes=[pltpu.VMEM((tm + ms, d), BF16)],
        compiler_params=_cparams(("arbitrary", "arbitrary")),
        name="ffn",
    )(x, xs, gain, wg, wu, wd)


def _norm_matmul_body(x_ref, xs_ref, g_ref, w_ref, o_ref, os_ref, xn_ref):
    tm = x_ref.shape[0]

    @pl.when(pl.program_id(1) == 0)
    def _():
        xn_ref[0:tm, :] = _rms(x_ref[...], g_ref[...]).astype(BF16)
        xn_ref[tm:, :] = _rms(xs_ref[...], g_ref[...]).astype(BF16)

    out = jnp.dot(xn_ref[...], w_ref[...], preferred_element_type=F32)
    o_ref[...] = out[0:tm]
    os_ref[...] = out[tm:]


def _norm_matmul(x, xs, gain, w, layer, tm, tn):
    m, d = x.shape
    ms = xs.shape[0]
    n = w.shape[2]
    return pl.pallas_call(
        _norm_matmul_body,
        grid=(m // tm, n // tn),
        in_specs=[
            pl.BlockSpec((tm, d), lambda i, j: (i, 0)),
            pl.BlockSpec((ms, d), lambda i, j: (0, 0)),
            pl.BlockSpec((None, 1, d), lambda i, j: (layer, 0, 0)),
            pl.BlockSpec((None, d, tn), lambda i, j: (layer, 0, j)),
        ],
        out_specs=[pl.BlockSpec((tm, tn), lambda i, j: (i, j)),
                   pl.BlockSpec((ms, tn), lambda i, j: (0, j))],
        out_shape=[jax.ShapeDtypeStruct((m, n), F32), jax.ShapeDtypeStruct((ms, n), F32)],
        scratch_shapes=[pltpu.VMEM((tm + ms, d), BF16)],
        compiler_params=_cparams(("arbitrary", "arbitrary")),
        name="norm_matmul",
    )(x, xs, gain, w)


def _mix_out(x, o_att, y_rwkv, y_conv, w_ref):
    wa, wr = o_att.shape[1], y_rwkv.shape[1]
    acc = jnp.dot(o_att.astype(BF16), w_ref[0:wa, :], preferred_element_type=F32)
    acc += jnp.dot(y_rwkv.astype(BF16), w_ref[wa:wa + wr, :], preferred_element_type=F32)
    acc += jnp.dot(y_conv.astype(BF16), w_ref[wa + wr:, :], preferred_element_type=F32)
    return x + acc


def _out_proj_body(x_ref, a_ref, r_ref, c_ref, w_ref, o_ref):
    o_ref[...] = _mix_out(x_ref[...], a_ref[...], r_ref[...], c_ref[...], w_ref)


def _out_proj_conv_body(x_ref, a_ref, r_ref, u_ref, gb_ref, gc_ref, pu_ref, pgc_ref, cw_ref, cb_ref, w_ref,
                        o_ref, zl_ref, zc_ref, *, blocks_per_seq):
    tm = u_ref.shape[0]
    first = (pl.program_id(0) % blocks_per_seq) == 0
    z = gc_ref[...] * u_ref[...]
    zc_ref[0:8, :] = jnp.where(first, 0.0, pgc_ref[...] * pu_ref[...])
    zc_ref[8:, :] = z
    yc = cb_ref[...] + cw_ref[0:1, :] * zc_ref[pl.ds(6, tm), :]
    yc = yc + cw_ref[1:2, :] * zc_ref[pl.ds(7, tm), :]
    yc = yc + cw_ref[2:3, :] * z
    zl_ref[...] = zc_ref[pl.ds(tm, 8), :]
    o_ref[...] = _mix_out(x_ref[...], a_ref[...], r_ref[...], gb_ref[...] * yc, w_ref)


def _out_proj_conv(x, o_att, y_rwkv, p, col0, cw, conv_w, conv_b, w, layer, t, tm):
    m, d = x.shape
    assert col0 % LANES == 0 and t % tm == 0
    cur = lambda s: pl.BlockSpec((pl.Element(tm), pl.Element(cw)),
                                 lambda i: (pl.multiple_of(i * tm, tm), col0 + s * cw))
    prev = lambda s: pl.BlockSpec((pl.Element(8), pl.Element(cw)),
                                  lambda i: (pl.multiple_of(jnp.maximum(i * tm - 8, 0), 8), col0 + s * cw))
    return pl.pallas_call(
        functools.partial(_out_proj_conv_body, blocks_per_seq=t // tm),
        grid=(m // tm,),
        in_specs=[
            pl.BlockSpec((tm, d), lambda i: (i, 0)),
            pl.BlockSpec((tm, o_att.shape[1]), lambda i: (i, 0)),
            pl.BlockSpec((tm, y_rwkv.shape[1]), lambda i: (i, 0)),
            cur(0), cur(1), cur(2), prev(0), prev(2),
            pl.BlockSpec((None, CONV_K, cw), lambda i: (layer, 0, 0)),
            pl.BlockSpec((None, 1, cw), lambda i: (layer, 0, 0)),
            pl.BlockSpec((None, w.shape[1], d), lambda i: (layer, 0, 0), pipeline_mode=pl.Buffered(1)),
        ],
        out_specs=[pl.BlockSpec((tm, d), lambda i: (i, 0)), pl.BlockSpec((8, cw), lambda i: (i, 0))],
        out_shape=[jax.ShapeDtypeStruct((m, d), F32), jax.ShapeDtypeStruct((m // tm * 8, cw), F32)],
        scratch_shapes=[pltpu.VMEM((tm + 8, cw), F32)],
        compiler_params=_cparams(("parallel",)),
        name="out_proj_conv",
    )(x, o_att, y_rwkv, p, p, p, p, p, conv_w, conv_b, w)


def _out_proj(x, o_att, y_rwkv, y_conv, w, layer, tm):
    m, d = x.shape
    return pl.pallas_call(
        _out_proj_body,
        grid=(m // tm,),
        in_specs=[
            pl.BlockSpec((tm, d), lambda i: (i, 0)),
            pl.BlockSpec((tm, o_att.shape[1]), lambda i: (i, 0)),
            pl.BlockSpec((tm, y_rwkv.shape[1]), lambda i: (i, 0)),
            pl.BlockSpec((tm, y_conv.shape[1]), lambda i: (i, 0)),
            pl.BlockSpec((None, w.shape[1], d), lambda i: (layer, 0, 0), pipeline_mode=pl.Buffered(1)),
        ],
        out_specs=pl.BlockSpec((tm, d), lambda i: (i, 0)),
        out_shape=jax.ShapeDtypeStruct((m, d), F32),
        compiler_params=_cparams(("parallel",)),
        name="out_proj",
    )(x, o_att, y_rwkv, y_conv, w)


def _pair_headnorm(x, g, lo_half):
    x2 = x * x
    s0 = jnp.sum(jnp.where(lo_half, x2, 0.0), axis=-1, keepdims=True)
    s1 = jnp.sum(jnp.where(lo_half, 0.0, x2), axis=-1, keepdims=True)
    ms = jnp.where(lo_half, s0, s1) * (1.0 / HEAD_DIM)
    return x * lax.rsqrt(ms + NORM_EPS) * g


def _attn_body(slopes_ref, q_ref, k_ref, v_ref, qg_ref, kg_ref, *refs, n_carried):
    o_ref, kt_ref, vt_ref, qs_ref, kn_ref, bias_ref, og_ref, lse_ref = refs[n_carried:]
    hp = pl.program_id(1)
    t = q_ref.shape[0]
    span = ATT_SPAN
    lo_half = lax.broadcasted_iota(jnp.int32, (1, LANES), 1) < HEAD_DIM

    kn = _pair_headnorm(k_ref[...], kg_ref[...], lo_half)
    kn_ref[...] = kn
    kt_ref[...] = kn.T
    vt_ref[...] = v_ref[...].T
    qs_ref[...] = _pair_headnorm(q_ref[...], qg_ref[...], lo_half) * (HEAD_DIM ** -0.5)

    qi = lax.broadcasted_iota(jnp.int32, (span, 2 * span), 0)
    ki = lax.broadcasted_iota(jnp.int32, (span, 2 * span), 1)
    steps = qi + span - ki
    valid = (steps >= 0) & (steps <= span)
    for g, (_, dil) in enumerate(DILATION_GROUPS):
        dist = (steps * dil).astype(F32)
        for h in range(2):
            bias_ref[g, h * span:(h + 1) * span, :] = jnp.where(valid, -(slopes_ref[2 * hp + h] * dist), NEG_BIG)

    def unit(g, dil, start, has_prev):
        rows = pl.ds(start, span, stride=dil)
        qb = qs_ref[rows, :]
        q2 = jnp.concatenate([jnp.where(lo_half, qb, 0.0), jnp.where(lo_half, 0.0, qb)], axis=0).astype(BF16)
        if has_prev:
            prow = pl.ds(start - span * dil, span, stride=dil)
            kb = jnp.concatenate([kn_ref[prow, :], kn_ref[rows, :]], axis=0)
            vb = jnp.concatenate([v_ref[prow, :], v_ref[rows, :]], axis=0)
            bias = bias_ref[g]
        else:
            kb = kn_ref[rows, :]
            vb = v_ref[rows, :]
            bias = bias_ref[g, :, span:]
        s = lax.dot_general(q2, kb.astype(BF16), _NT, preferred_element_type=F32) + bias
        m = jnp.max(s, axis=-1, keepdims=True)
        p = jnp.exp(s - m)
        l = jnp.sum(p, axis=-1, keepdims=True)
        o = jnp.dot(p.astype(BF16), vb.astype(BF16), preferred_element_type=F32) / l
        lse = m + jnp.log(l)
        og_ref[g, rows, :] = jnp.where(lo_half, o[0:span], o[span:])
        lse_ref[g, rows, :] = jnp.where(lo_half, lse[0:span], lse[span:])

    for g, (window, dil) in enumerate(DILATION_GROUPS):
        assert window // dil == span
        nb = t // (span * dil)

        def residue(r, carry, g=g, dil=dil, nb=nb):
            unit(g, dil, r, False)
            if nb > 1:
                def blk(b, c):
                    unit(g, dil, b * (span * dil) + r, True)
                    return c
                lax.fori_loop(1, nb, blk, 0, unroll=3 if nb <= 4 else 5)
            return carry

        if dil == 1:
            residue(0, 0)
        else:
            lax.fori_loop(0, dil, residue, 0, unroll=4 if nb == 1 else 1)

    l0, l1, l2 = lse_ref[0], lse_ref[1], lse_ref[2]
    mx = jnp.maximum(jnp.maximum(l0, l1), l2)
    w0, w1, w2 = jnp.exp(l0 - mx), jnp.exp(l1 - mx), jnp.exp(l2 - mx)
    o_ref[...] = (og_ref[0] * w0 + og_ref[1] * w1 + og_ref[2] * w2) / (w0 + w1 + w2)


def _attention_prompt(p, col0, att_w, qg, kg, slopes, layer, depth, n_seq, t, kv_stack):
    m = p.shape[0]
    pairs = att_w // LANES
    assert col0 % LANES == 0
    blk = lambda off: pl.BlockSpec((t, LANES), lambda n, hp: (n, off + hp))
    c0 = col0 // LANES
    gain = pl.BlockSpec((None, 1, LANES), lambda n, hp: (layer, 0, 0))
    tr_spec = pl.BlockSpec((None, None, LANES, t), lambda n, hp: (layer, n, hp, 0))
    tr_shape = jax.ShapeDtypeStruct((depth, n_seq, att_w, t), F32)
    carried = () if kv_stack is None else tuple(kv_stack)
    n_in = 6
    return pl.pallas_call(
        functools.partial(_attn_body, n_carried=len(carried)),
        grid=(n_seq, pairs),
        in_specs=[pl.BlockSpec(memory_space=pltpu.SMEM),
                  blk(c0), blk(c0 + pairs), blk(c0 + 2 * pairs), gain, gain]
                 + [pl.BlockSpec(memory_space=pl.ANY)] * len(carried),
        out_specs=[blk(0), tr_spec, tr_spec],
        out_shape=[jax.ShapeDtypeStruct((m, att_w), F32), tr_shape, tr_shape],
        input_output_aliases={n_in + i: 1 + i for i in range(len(carried))},
        scratch_shapes=[pltpu.VMEM((t, LANES), F32),
                        pltpu.VMEM((t, LANES), F32),
                        pltpu.VMEM((3, 2 * ATT_SPAN, 2 * ATT_SPAN), F32),
                        pltpu.VMEM((3, t, LANES), F32),
                        pltpu.VMEM((3, t, LANES), F32)],
        compiler_params=_cparams(("parallel", "arbitrary")),
        name="attn_prompt",
    )(slopes, p, p, p, qg, kg, *carried)


def _softplus(z):
    return jnp.maximum(z, 0.0) + jnp.log(1.0 + jnp.exp(-jnp.abs(z)))


def _rwkv_token_math(p_r, p_k, p_v, p_l, q_r, q_k, q_v, q_l, mu_r, mu_k, mu_v, mu_l,
                     w0, w2p, a0, a2p, g2p, k_k, k_a, gate_dot=_dot3):
    xr = p_r + (q_r - p_r) * mu_r
    xk = p_k + (q_k - p_k) * mu_k
    xv = p_v + (q_v - p_v) * mu_v
    xl = p_l + (q_l - p_l) * mu_l
    w_log = -_softplus(-(w0 + _dot3(jnp.tanh(xl), w2p))) - 0.5
    log_decay = -jnp.exp(w_log)
    sig_l = jax.nn.sigmoid(xl)
    a_gate = jax.nn.sigmoid(a0 + gate_dot(xl, a2p))
    g = gate_dot(sig_l, g2p)
    kk = xk * k_k
    kk = kk * lax.rsqrt(jnp.maximum(_segsum64_many([kk * kk])[0], 1e-24))
    kmod = xk * (1.0 + (a_gate - 1.0) * k_a)
    return xr, log_decay, kmod, xv, -kk, kk * a_gate, g


def _segsum64_many(xs):
    rows, width = xs[0].shape
    if rows % 16 != 0:
        return [_segsum64(x) for x in xs]
    nsl = width // LANES
    pieces = [part[:, i * LANES:(i + 1) * LANES]
              for x in xs for part in _split3(x) for i in range(nsl)]
    res = jnp.dot(jnp.concatenate(pieces, axis=0), _head_blockdiag(LANES, BF16), preferred_element_type=F32)
    outs = []
    for n in range(len(xs)):
        parts = [jnp.concatenate([res[((3 * n + j) * nsl + i) * rows:((3 * n + j) * nsl + i + 1) * rows]
                                  for i in range(nsl)], axis=1) for j in range(3)]
        outs.append(parts[0] + parts[1] + parts[2])
    return outs


def _rwkv_post(y, r, k, v, g, ln_w, ln_b, r_k):
    sum_y, sum_rk = _segsum64_many([y, r * k * r_k])
    yc = y - sum_y * (1.0 / HEAD_DIM)
    var_y = _segsum64_many([yc * yc])[0] * (1.0 / HEAD_DIM)
    yn = yc * lax.rsqrt(var_y + RWKV_GN_EPS) * ln_w + ln_b
    return (yn + sum_rk * v) * g


def _scan_chunk(r, lw, cum, k, v, a, b, s0):
    c = r.shape[1]
    c2 = 2 * c
    lane = lax.broadcasted_iota(jnp.int32, (1, 1, LANES), 2)
    m0 = (lane < HEAD_DIM).astype(F32)
    m1 = 1.0 - m0
    stack = lambda x: jnp.concatenate([x * m0, x * m1], axis=1)

    cum_prev = cum - lw
    cum_last = cum[:, c - 1:c, :]
    e_pos = jnp.exp(cum)
    e_neg = jnp.exp(-cum)
    e_rem = jnp.exp(cum_last - cum)
    at = a * jnp.exp(cum_prev)
    rt = r * e_pos
    bt = b * e_neg
    kt = k * e_neg
    bh = b * e_rem
    kh = k * e_rem

    big = _dot1(jnp.concatenate([stack(at), stack(rt)], axis=1),
                jnp.concatenate([stack(bt), stack(kt)], axis=1), _BNT)
    row = lax.broadcasted_iota(jnp.int32, (1, c2, c2), 1)
    col = lax.broadcasted_iota(jnp.int32, (1, c2, c2), 2)
    assert c == HEAD_DIM
    same_head = (row >> HEAD_SHIFT) == (col >> HEAD_SHIFT)
    tr, tc = row & (c - 1), col & (c - 1)
    strict = same_head & (tr > tc)
    incl = same_head & (tr >= tc)
    lmat = jnp.where(strict, big[:, 0:c2, 0:c2], 0.0)
    ak = jnp.where(strict, big[:, 0:c2, c2:], 0.0)
    rb = jnp.where(incl, big[:, c2:, 0:c2], 0.0)
    rk = jnp.where(incl, big[:, c2:, c2:], 0.0)

    eye = (row == col).astype(F32)
    x = eye + jnp.where((row >> 1) == (col >> 1), lmat, 0.0)
    s, log2s = 2, 1
    while s < c:
        joins = ((row >> (log2s + 1)) == (col >> (log2s + 1))) & ((row & s) != 0) & ((col & s) == 0)
        x = x + _dot1(_dot1(x, jnp.where(joins, lmat, 0.0), _BNN), x, _BNN)
        s, log2s = 2 * s, log2s + 1

    v_st = stack(v)
    w = _dot1(jnp.concatenate([at, rt], axis=1), s0, _BNT)
    u_st = _dot1(x, stack(w[:, 0:c]) + _dot1(ak, v_st, _BNN), _BNN)
    y_st = _dot1(jnp.concatenate([rb, rk], axis=2), jnp.concatenate([u_st, v_st], axis=1), _BNN)
    y = w[:, c:] + y_st[:, 0:c] + y_st[:, c:]
    u = u_st[:, 0:c] + u_st[:, c:]
    upd = _dot1(jnp.concatenate([u, v], axis=1), jnp.concatenate([bh, kh], axis=1), _BTN)
    s_new = s0 * jnp.exp(cum_last) + _head_blockdiag(LANES, F32)[None] * upd
    return y, s_new


def _rwkv_mix_body(pr_ref, pk_ref, pv_ref, plo_ref,
                   mur_ref, muk_ref, muv_ref, mul_ref, w0_ref, w2_ref, a0_ref, a2_ref, g2_ref,
                   kk_ref, ka_ref, lnw_ref, lnb_ref, rk_ref,
                   y_ref, s_out_ref, s_ref, sr_ref, sk_ref, sv_ref, sl_ref):
    ch = pl.program_id(1)
    nseq, c, rw = pr_ref.shape
    npair = rw // LANES
    shift_refs = (sr_ref, sk_ref, sv_ref, sl_ref)

    @pl.when(ch == 0)
    def _():
        s_ref[...] = jnp.zeros_like(s_ref)
        for ref in shift_refs:
            ref[:, 0:8, :] = jnp.zeros((nseq, 8, ref.shape[2]), F32)

    def cur_and_prev(cur_ref, sh_ref):
        cur = cur_ref[...]
        sh_ref[:, 8:, :] = cur
        prev = sh_ref[:, pl.ds(7, c), :]
        sh_ref[:, 7:8, :] = cur[:, c - 1:c, :]
        flat = lambda x: x.reshape(nseq * c, x.shape[2])
        return flat(cur), flat(prev)

    (p_r, q_r), (p_k, q_k), (p_v, q_v), (p_l, q_l) = [
        cur_and_prev(cur, sh) for cur, sh in zip((pr_ref, pk_ref, pv_ref, plo_ref), shift_refs)]
    r, lw, k, v, a, b, g = _rwkv_token_math(
        p_r, p_k, p_v, p_l, q_r, q_k, q_v, q_l,
        mur_ref[...], muk_ref[...], muv_ref[...], mul_ref[...],
        w0_ref[...], w2_ref[...], a0_ref[...], a2_ref[...], g2_ref[...], kk_ref[...], ka_ref[...],
        gate_dot=_dot1)

    pairs = lambda x: jnp.stack([x[s * c:(s + 1) * c, p * LANES:(p + 1) * LANES]
                                 for s in range(nseq) for p in range(npair)], axis=0)
    tri_incl = (lax.broadcasted_iota(jnp.int32, (c, c), 0)
                >= lax.broadcasted_iota(jnp.int32, (c, c), 1)).astype(BF16)
    cum = jnp.concatenate([_cumsum_rows(lw[s * c:(s + 1) * c], tri_incl) for s in range(nseq)], axis=0)
    y, s_new = _scan_chunk(pairs(r), pairs(lw), pairs(cum), pairs(k), pairs(v), pairs(a), pairs(b), s_ref[...])
    s_ref[...] = s_new
    y = jnp.concatenate([jnp.concatenate([y[s * npair + p] for p in range(npair)], axis=1)
                         for s in range(nseq)], axis=0)
    out = _rwkv_post(y, r, k, v, g, lnw_ref[...], lnb_ref[...], rk_ref[...])
    y_ref[...] = out.reshape(nseq, c, rw)

    @pl.when(ch == pl.num_programs(1) - 1)
    def _():
        for s in range(nseq):
            for p in range(npair):
                s_out_ref[s, 2 * p] = s_ref[s * npair + p, 0:HEAD_DIM, 0:HEAD_DIM]
                s_out_ref[s, 2 * p + 1] = s_ref[s * npair + p, HEAD_DIM:, HEAD_DIM:]


def _rwkv_mix_prompt(p, col0, lp, layer, n_seq, t, seq_per_step):
    m = p.shape[0]
    rw = lp["rwkv_w"]
    c = RWKV_CHUNK
    heads = rw // HEAD_DIM
    assert n_seq % seq_per_step == 0 and col0 % rw == 0 and (col0 + 3 * rw) % LORA_W == 0
    c_slab = col0 // rw
    c_lora = (col0 + 3 * rw) // LORA_W
    tok = lambda w, col: pl.BlockSpec((seq_per_step, c, w), lambda n, ch: (n, ch, col))
    vec = lambda w, col: pl.BlockSpec((None, 1, w), lambda n, ch: (layer, 0, col))
    mat = pl.BlockSpec((None, LORA_W, rw), lambda n, ch: (layer, 0, 0))
    p3 = p.reshape(n_seq, t, p.shape[1])
    y, wkv = pl.pallas_call(
        _rwkv_mix_body,
        grid=(n_seq // seq_per_step, t // c),
        in_specs=[tok(rw, c_slab), tok(rw, c_slab + 1), tok(rw, c_slab + 2), tok(LORA_W, c_lora),
                  vec(rw, 0), vec(rw, 1), vec(rw, 2), vec(LORA_W, (3 * rw) // LORA_W),
                  vec(rw, 0), mat, vec(rw, 0), mat, mat, vec(rw, 0), vec(rw, 0),
                  vec(rw, 0), vec(rw, 0), vec(rw, 0)],
        out_specs=[tok(rw, 0),
                   pl.BlockSpec((seq_per_step, heads, HEAD_DIM, HEAD_DIM), lambda n, ch: (n, 0, 0, 0))],
        out_shape=[jax.ShapeDtypeStruct((n_seq, t, rw), F32),
                   jax.ShapeDtypeStruct((n_seq, heads, HEAD_DIM, HEAD_DIM), F32)],
        scratch_shapes=[pltpu.VMEM((seq_per_step * (rw // LANES), LANES, LANES), F32)]
                       + [pltpu.VMEM((seq_per_step, c + 8, rw), F32)] * 3
                       + [pltpu.VMEM((seq_per_step, c + 8, LORA_W), F32)],
        compiler_params=_cparams(("parallel", "arbitrary")),
        name="rwkv_mix",
    )(p3, p3, p3, p3, lp["mu"], lp["mu"], lp["mu"], lp["mu"],
      lp["w0"], lp["w2p"], lp["a0"], lp["a2p"], lp["g2p"], lp["k_k"], lp["k_a"],
      lp["ln_w"], lp["ln_b"], lp["r_k"])
    return y.reshape(m, rw), wkv


def _sample_body(slope_ref, p_ref, kc_ref, vc_ref,
                 wkv_ref, shift_ref, cst_ref, qg_ref, kg_ref,
                 mu_ref, w0_ref, w2_ref, a0_ref, a2_ref, g2_ref, kk_ref, ka_ref,
                 rk_ref, lnw_ref, lnb_ref, cw_ref, cb_ref,
                 oatt_ref, yr_ref, yc_ref, kn_ref, vn_ref, wkvn_ref, cstn_ref,
                 *, col_att, col_rwkv, col_conv):
    heads_a = oatt_ref.shape[0]
    aw = heads_a * HEAD_DIM
    rw = yr_ref.shape[1]
    cw = yc_ref.shape[1]
    rp = shift_ref.shape[1]
    rows8 = lambda x: jnp.broadcast_to(x, (8, x.shape[1]))
    p_all = p_ref[...]

    by_head = lambda row: jnp.concatenate(
        [row[:, h * HEAD_DIM:(h + 1) * HEAD_DIM] for h in range(heads_a)], axis=0)
    ca = col_att
    q, k, v_new = (by_head(p_all[:, ca:ca + aw]), by_head(p_all[:, ca + aw:ca + 2 * aw]),
                   by_head(p_all[:, ca + 2 * aw:ca + 3 * aw]))
    qn = _rms(q, qg_ref[...]) * (HEAD_DIM ** -0.5)
    kn = _rms(k, kg_ref[...])
    kn_ref[...] = kn
    vn_ref[...] = v_new
    s_new = jnp.sum(kn * qn, axis=-1, keepdims=True)
    slope = slope_ref[...]
    past = kc_ref.shape[2]
    er = lax.broadcasted_iota(jnp.int32, (HEAD_DIM, HEAD_DIM), 0)
    ec = lax.broadcasted_iota(jnp.int32, (HEAD_DIM, HEAD_DIM), 1)
    eye = er == ec
    q_col = jnp.stack([jnp.sum(jnp.where(eye, qn[h:h + 1, :], 0.0), axis=1, keepdims=True)
                       for h in range(heads_a)], axis=0)
    dist = past - lax.broadcasted_iota(jnp.int32, (1, past), 1)
    biased = jnp.sum(kc_ref[...] * q_col, axis=1) - slope * dist.astype(F32)
    parts = []
    for window, dil in DILATION_GROUPS:
        assert dil & (dil - 1) == 0
        valid = ((dist & (dil - 1)) == 0) & (dist <= window)
        s = jnp.where(valid, biased, NEG_BIG)
        m = jnp.maximum(jnp.max(s, axis=1, keepdims=True), s_new)
        p = jnp.exp(s - m)
        p_new = jnp.exp(s_new - m)
        l = jnp.sum(p, axis=1, keepdims=True) + p_new
        parts.append((p, p_new, l, m + jnp.log(l)))
    mx = jnp.maximum(jnp.maximum(parts[0][3], parts[1][3]), parts[2][3])
    ws = [jnp.exp(lse - mx) for _, _, _, lse in parts]
    wsum = ws[0] + ws[1] + ws[2]
    scale_g = [w / (l * wsum) for w, (_, _, l, _) in zip(ws, parts)]
    coef = parts[0][0] * scale_g[0] + parts[1][0] * scale_g[1] + parts[2][0] * scale_g[2]
    coef_new = parts[0][1] * scale_g[0] + parts[1][1] * scale_g[1] + parts[2][1] * scale_g[2]
    o_col = jnp.sum(vc_ref[...] * coef[:, None, :], axis=2, keepdims=True)
    o_rows = jnp.concatenate([jnp.sum(jnp.where(eye, o_col[h], 0.0), axis=0, keepdims=True)
                              for h in range(heads_a)], axis=0)
    oatt_ref[...] = o_rows + coef_new * v_new

    pr = rows8(p_all[:, col_rwkv:col_rwkv + rp])
    sh = rows8(shift_ref[...])
    mu = mu_ref[...]
    c3 = 3 * rw
    r, lw, k, v, a, b, g = _rwkv_token_math(
        pr[:, 0:rw], pr[:, rw:2 * rw], pr[:, 2 * rw:c3], pr[:, c3:],
        sh[:, 0:rw], sh[:, rw:2 * rw], sh[:, 2 * rw:c3], sh[:, c3:],
        mu[:, 0:rw], mu[:, rw:2 * rw], mu[:, 2 * rw:c3], mu[:, c3:],
        w0_ref[...], w2_ref[...], a0_ref[...], a2_ref[...], g2_ref[...], kk_ref[...], ka_ref[...])
    decay = jnp.exp(lw)
    place_r = lax.broadcasted_iota(jnp.int32, (HEAD_DIM, rw), 0)
    place_c = lax.broadcasted_iota(jnp.int32, (HEAD_DIM, rw), 1)
    y_full = jnp.zeros((1, rw), F32)
    for h in range(rw // HEAD_DIM):
        hs = slice(h * HEAD_DIM, (h + 1) * HEAD_DIM)
        s_old = wkv_ref[h]
        sa = jnp.sum(s_old * a[0:1, hs], axis=1, keepdims=True)
        v_col = jnp.sum(jnp.where(eye, v[0:1, hs], 0.0), axis=1, keepdims=True)
        s_h = s_old * decay[0:1, hs] + sa * b[0:1, hs] + v_col * k[0:1, hs]
        wkvn_ref[h] = s_h
        y_col = jnp.sum(s_h * r[0:1, hs], axis=1, keepdims=True)
        y_full = y_full + jnp.sum(jnp.where(place_c == place_r + h * HEAD_DIM, y_col, 0.0),
                                  axis=0, keepdims=True)
    y8 = rows8(y_full)
    yr_ref[...] = _rwkv_post(y8, r, k, v, g, lnw_ref[...], lnb_ref[...], rk_ref[...])[0:1]

    c1 = col_conv
    z = p_all[:, c1 + 2 * cw:c1 + 3 * cw] * p_all[:, c1:c1 + cw]
    yc = cb_ref[...] + cw_ref[0:1, :] * cst_ref[0:1, :]
    yc = yc + cw_ref[1:2, :] * cst_ref[1:2, :]
    yc = yc + cw_ref[2:3, :] * z
    yc_ref[...] = p_all[:, c1 + cw:c1 + 2 * cw] * yc
    cstn_ref[0:1, :] = cst_ref[1:2, :]
    cstn_ref[1:2, :] = z


def _sample_mixers(p, cols, cache_k, cache_v, state_wkv, state_shift, state_conv, lp, slope_col, layer):
    nb = p.shape[0]
    depth, _, past, heads_a, _ = cache_k.shape
    rw = lp["rwkv_w"]
    rp = state_shift.shape[2]
    cw = state_conv.shape[3]
    heads = rw // HEAD_DIM
    span = ATT_SPAN

    row = lambda w: pl.BlockSpec((None, 1, w), lambda n: (n, 0, 0))
    lvec = lambda w: pl.BlockSpec((None, 1, w), lambda n: (layer, 0, 0))
    lmat = lambda r_, w: pl.BlockSpec((None, r_, w), lambda n: (layer, 0, 0))
    tile = pl.BlockSpec((None, heads_a, HEAD_DIM), lambda n: (n, 0, 0))

    assert past >= DILATION_GROUPS[-1][0]
    views = [jnp.transpose(c, (0, 1, 3, 4, 2)) for c in (cache_k, cache_v)]
    cache_specs = [pl.BlockSpec((None, None, heads_a, HEAD_DIM, past), lambda n: (layer, n, 0, 0, 0))] * 2

    outs = pl.pallas_call(
        functools.partial(_sample_body, col_att=cols[0], col_rwkv=cols[1], col_conv=cols[2]),
        grid=(nb,),
        in_specs=[pl.BlockSpec((heads_a, 1), lambda n: (0, 0)), row(p.shape[1])] + cache_specs + [
                  pl.BlockSpec((None, None, heads, HEAD_DIM, HEAD_DIM), lambda n: (layer, n, 0, 0, 0)),
                  pl.BlockSpec((None, None, 1, rp), lambda n: (layer, n, 0, 0)),
                  pl.BlockSpec((None, None, CONV_K - 1, cw), lambda n: (layer, n, 0, 0)),
                  lvec(HEAD_DIM), lvec(HEAD_DIM),
                  lvec(rp), lvec(rw), lmat(LORA_W, rw), lvec(rw), lmat(LORA_W, rw),
                  lmat(LORA_W, rw), lvec(rw), lvec(rw), lvec(rw), lvec(rw), lvec(rw),
                  lmat(CONV_K, cw), lvec(cw)],
        out_specs=[tile, row(rw), row(cw), tile, tile,
                   pl.BlockSpec((None, heads, HEAD_DIM, HEAD_DIM), lambda n: (n, 0, 0, 0)),
                   pl.BlockSpec((None, CONV_K - 1, cw), lambda n: (n, 0, 0))],
        out_shape=[jax.ShapeDtypeStruct((nb, heads_a, HEAD_DIM), F32), jax.ShapeDtypeStruct((nb, 1, rw), F32),
                   jax.ShapeDtypeStruct((nb, 1, cw), F32),
                   jax.ShapeDtypeStruct((nb, heads_a, HEAD_DIM), F32),
                   jax.ShapeDtypeStruct((nb, heads_a, HEAD_DIM), F32),
                   jax.ShapeDtypeStruct((nb, heads, HEAD_DIM, HEAD_DIM), F32),
                   jax.ShapeDtypeStruct((nb, CONV_K - 1, cw), F32)],
        compiler_params=_cparams(("parallel",)),
        name="sample_mixers",
    )(slope_col, p[:, None, :], *views,
      state_wkv, state_shift[:, :, None, :], state_conv,
      lp["q_gain"], lp["k_gain"],
      lp["mu"], lp["w0"], lp["w2p"], lp["a0"], lp["a2p"], lp["g2p"], lp["k_k"], lp["k_a"],
      lp["r_k"], lp["ln_w"], lp["ln_b"], lp["conv_w"], lp["conv_b"])
    o_att, y_rwkv, y_conv, k_new, v_new, wkv_new, conv_new = outs
    return o_att.reshape(nb, heads_a * HEAD_DIM), y_rwkv[:, 0], y_conv[:, 0], k_new, v_new, wkv_new, conv_new


def kernel(x_prompt, x_sample, cache_swa_k, cache_swa_v, state_wkv, state_shift, state_conv, ffn1_norm, ffn1_w_gate, ffn1_w_up, ffn1_w_down, mix_norm, w_in, q_norm, k_norm, rwkv_mu, rwkv_w0, rwkv_w2, rwkv_a0, rwkv_a2, rwkv_g2, rwkv_k_k, rwkv_k_a, rwkv_r_k, rwkv_ln_w, rwkv_ln_b, conv_w, conv_b, w_out, ffn2_norm, ffn2_w_gate, ffn2_w_up, ffn2_w_down):
    n_seq, t, d = x_prompt.shape
    nb = x_sample.shape[0]
    assert x_sample.shape[1] == 1
    depth = w_in.shape[0]
    att_heads, head_dim = cache_swa_k.shape[3], cache_swa_k.shape[4]
    assert head_dim == HEAD_DIM
    aw = att_heads * HEAD_DIM
    rw = rwkv_w0.shape[1]
    cw = conv_b.shape[1]
    rp = rwkv_mu.shape[1]
    assert rp == 3 * rw + LORA_W and w_in.shape[2] == 3 * aw + rp + 3 * cw

    row3 = lambda p: p.reshape(depth, 1, -1)
    col_att, col_rwkv, col_conv = 0, 3 * aw, 3 * aw + rp
    w_in16, w_out16 = w_in.astype(BF16), w_out.astype(BF16)
    zpad = lambda w, before: jnp.pad(w, ((0, 0), (before, LORA_W - before - w.shape[1]), (0, 0)))
    lp = {
        "rwkv_w": rw,
        "mu": row3(rwkv_mu), "w0": row3(rwkv_w0), "a0": row3(rwkv_a0),
        "w2p": zpad(rwkv_w2, 0), "a2p": zpad(rwkv_a2, W_LORA), "g2p": zpad(rwkv_g2, W_LORA + A_LORA),
        "k_k": row3(rwkv_k_k), "k_a": row3(rwkv_k_a), "r_k": row3(rwkv_r_k),
        "ln_w": row3(rwkv_ln_w), "ln_b": row3(rwkv_ln_b),
        "conv_w": conv_w, "conv_b": row3(conv_b),
        "q_gain": row3(q_norm), "k_gain": row3(k_norm),
        "q_gain_pair": row3(jnp.tile(q_norm, (1, LANES // HEAD_DIM))),
        "k_gain_pair": row3(jnp.tile(k_norm, (1, LANES // HEAD_DIM))),
    }
    g_ffn1, g_mix, g_ffn2 = row3(ffn1_norm), row3(mix_norm), row3(ffn2_norm)
    slopes = 2.0 ** (-8.0 * jnp.arange(1, att_heads + 1, dtype=F32) / att_heads)
    slope_col = slopes[:, None]

    m = n_seq * t
    xp = x_prompt.reshape(m, d)
    xs = x_sample.reshape(nb, d)
    tm_ffn, tf = 1024, 256
    tm_in, tn_in = 1024, 1280
    tm_out = 512
    p_states, s_states = [], []
    kv_stack = None
    for l in range(depth):
        xp, xs = _ffn(xp, xs, g_ffn1, ffn1_w_gate, ffn1_w_up, ffn1_w_down, l, tm_ffn, tf)
        pp, ps = _norm_matmul(xp, xs, g_mix, w_in16, l, tm_in, tn_in)

        o_att, *kv_stack = _attention_prompt(pp, col_att, aw, lp["q_gain_pair"], lp["k_gain_pair"], slopes,
                                             l, depth, n_seq, t, kv_stack)
        y_rwkv, wkv_p = _rwkv_mix_prompt(pp, col_rwkv, lp, l, n_seq, t, 4 if n_seq % 4 == 0 else 1)
        xp, z_last = _out_proj_conv(xp, o_att, y_rwkv, pp, col_conv, cw, lp["conv_w"], lp["conv_b"], w_out16,
                                    l, t, tm_out)
        p_states.append((
            wkv_p, pp.reshape(n_seq, t, -1)[:, -1, col_rwkv:col_rwkv + rp],
            z_last.reshape(n_seq, t // tm_out, 8, cw)[:, -1, 8 - (CONV_K - 1):]))

        so, sy, syc, sk, sv, swkv, sconv = _sample_mixers(
            ps, (col_att, col_rwkv, col_conv), cache_swa_k, cache_swa_v, state_wkv, state_shift, state_conv,
            lp, slope_col, l)
        xs = _out_proj(xs, so, sy, syc, w_out16, l, nb)
        s_states.append((sk[:, None], sv[:, None], swkv, ps[:, col_rwkv:col_rwkv + rp], sconv))

        xp, xs = _ffn(xp, xs, g_ffn2, ffn2_w_gate, ffn2_w_up, ffn2_w_down, l, tm_ffn, tf)

    stacked = lambda states, i: jnp.stack([st[i] for st in states], axis=0)
    rows = lambda x_t: jnp.transpose(x_t.reshape(depth, n_seq, att_heads, HEAD_DIM, t), (0, 1, 4, 2, 3))
    return (xp.reshape(n_seq, t, d), xs.reshape(nb, 1, d),
            rows(kv_stack[0]), rows(kv_stack[1]),
            stacked(p_states, 0), stacked(p_states, 1), stacked(p_states, 2),
            stacked(s_states, 0), stacked(s_states, 1), stacked(s_states, 2), stacked(s_states, 3),
            stacked(s_states, 4))
```

```python
import functools

import jax
import jax.numpy as jnp
from jax import lax
from jax.experimental import pallas as pl
from jax.experimental.pallas import tpu as pltpu

F32 = jnp.float32
BF16 = jnp.bfloat16

HEAD_DIM = 64
HEAD_SHIFT = 6
LANES = 128
NORM_EPS = 1e-6
RWKV_GN_EPS = 64e-5
DILATION_GROUPS = ((128, 1), (512, 4), (2048, 16))
ATT_SPAN = 128
W_LORA, A_LORA, G_LORA = 64, 64, 128
LORA_W = W_LORA + A_LORA + G_LORA
CONV_K = 3
RWKV_CHUNK = 64
VMEM_LIMIT = 56 * 1024 * 1024
NEG_BIG = -1e30

_NT = (((1,), (1,)), ((), ()))
_NN = (((1,), (0,)), ((), ()))
_BNN = (((2,), (1,)), ((0,), (0,)))
_BNT = (((2,), (2,)), ((0,), (0,)))
_BTN = (((1,), (1,)), ((0,), (0,)))


def _cparams(sem):
    return pltpu.CompilerParams(dimension_semantics=sem, vmem_limit_bytes=VMEM_LIMIT)


def _rms(x, g):
    return x * lax.rsqrt(jnp.mean(x * x, axis=-1, keepdims=True) + NORM_EPS) * g


def _split2(x):
    hi = x.astype(BF16)
    lo = (x - hi.astype(F32)).astype(BF16)
    return hi, lo


def _dot3(a, b, dims=_NN):
    ah, al = _split2(a)
    bh, bl = _split2(b)
    d = lambda x, y: lax.dot_general(x, y, dims, preferred_element_type=F32)
    return d(ah, bh) + (d(al, bh) + d(ah, bl))


def _dot1(a, b, dims=_NN):
    return lax.dot_general(a.astype(BF16), b.astype(BF16), dims, preferred_element_type=F32)


def _split3(x):
    h1 = x.astype(BF16)
    r1 = x - h1.astype(F32)
    h2 = r1.astype(BF16)
    h3 = (r1 - h2.astype(F32)).astype(BF16)
    return h1, h2, h3


def _dot_sel(x, sel):
    return sum(jnp.dot(h, sel, preferred_element_type=F32) for h in _split3(x))


def _cumsum_rows(x, tri_incl):
    return sum(jnp.dot(tri_incl, h, preferred_element_type=F32) for h in _split2(x))


def _head_blockdiag(n, dtype):
    r = lax.broadcasted_iota(jnp.int32, (n, n), 0) >> HEAD_SHIFT
    c = lax.broadcasted_iota(jnp.int32, (n, n), 1) >> HEAD_SHIFT
    return (r == c).astype(dtype)


def _segsum64(x):
    bd = _head_blockdiag(LANES, BF16)
    parts = [_dot_sel(x[:, i:i + LANES], bd) for i in range(0, x.shape[1], LANES)]
    return parts[0] if len(parts) == 1 else jnp.concatenate(parts, axis=1)


def _ffn_body(x_ref, xs_ref, g_ref, wg_ref, wu_ref, wd_ref, o_ref, os_ref, xn_ref):
    j = pl.program_id(1)
    tm, d = x_ref.shape

    @pl.when(j == 0)
    def _():
        xn_ref[0:tm, :] = _rms(x_ref[...], g_ref[...]).astype(BF16)
        xn_ref[tm:, :] = _rms(xs_ref[...], g_ref[...]).astype(BF16)
        o_ref[...] = jnp.zeros_like(o_ref)
        os_ref[...] = jnp.zeros_like(os_ref)

    xn = xn_ref[...]
    h = jnp.dot(xn, wg_ref[...].astype(BF16), preferred_element_type=F32)
    u = jnp.dot(xn, wu_ref[...].astype(BF16), preferred_element_type=F32)
    act = (h * jax.nn.sigmoid(h) * u).astype(BF16)
    half = d // 2
    for c in range(2):
        cols = slice(c * half, (c + 1) * half)
        part = jnp.dot(act, wd_ref[:, cols].astype(BF16), preferred_element_type=F32)
        o_ref[:, cols] += part[0:tm]
        os_ref[:, cols] += part[tm:]

    @pl.when(j == pl.num_programs(1) - 1)
    def _():
        o_ref[...] = x_ref[...] + 0.5 * o_ref[...]
        os_ref[...] = xs_ref[...] + 0.5 * os_ref[...]


def _ffn(x, xs, gain, wg, wu, wd, layer, tm, tf):
    m, d = x.shape
    ms = xs.shape[0]
    f = wg.shape[2]
    return pl.pallas_call(
        _ffn_body,
        grid=(m // tm, f // tf),
        in_specs=[
            pl.BlockSpec((tm, d), lambda i, j: (i, 0)),
            pl.BlockSpec((ms, d), lambda i, j: (0, 0)),
            pl.BlockSpec((None, 1, d), lambda i, j: (layer, 0, 0)),
            pl.BlockSpec((None, d, tf), lambda i, j: (layer, 0, j)),
            pl.BlockSpec((None, d, tf), lambda i, j: (layer, 0, j)),
            pl.BlockSpec((None, tf, d), lambda i, j: (layer, j, 0)),
        ],
        out_specs=[pl.BlockSpec((tm, d), lambda i, j: (i, 0)),
                   pl.BlockSpec((ms, d), lambda i, j: (0, 0))],
        out_shape=[jax.ShapeDtypeStruct((m, d), F32), jax.ShapeDtypeStruct((ms, d), F32)],
        scratch_shapes=[pltpu.VMEM((tm + ms, d), BF16)],
        compiler_params=_cparams(("arbitrary", "arbitrary")),
        name="ffn",
    )(x, xs, gain, wg, wu, wd)


def _norm_matmul_body(x_ref, xs_ref, g_ref, w_ref, o_ref, os_ref, xn_ref):
    tm = x_ref.shape[0]

    @pl.when(pl.program_id(1) == 0)
    def _():
        xn_ref[0:tm, :] = _rms(x_ref[...], g_ref[...]).astype(BF16)
        xn_ref[tm:, :] = _rms(xs_ref[...], g_ref[...]).astype(BF16)

    out = jnp.dot(xn_ref[...], w_ref[...], preferred_element_type=F32)
    o_ref[...] = out[0:tm]
    os_ref[...] = out[tm:]


def _norm_matmul(x, xs, gain, w, layer, tm, tn):
    m, d = x.shape
    ms = xs.shape[0]
    n = w.shape[2]
    return pl.pallas_call(
        _norm_matmul_body,
        grid=(m // tm, n // tn),
        in_specs=[
            pl.BlockSpec((tm, d), lambda i, j: (i, 0)),
            pl.BlockSpec((ms, d), lambda i, j: (0, 0)),
            pl.BlockSpec((None, 1, d), lambda i, j: (layer, 0, 0)),
            pl.BlockSpec((None, d, tn), lambda i, j: (layer, 0, j)),
        ],
        out_specs=[pl.BlockSpec((tm, tn), lambda i, j: (i, j)),
                   pl.BlockSpec((ms, tn), lambda i, j: (0, j))],
        out_shape=[jax.ShapeDtypeStruct((m, n), F32), jax.ShapeDtypeStruct((ms, n), F32)],
        scratch_shapes=[pltpu.VMEM((tm + ms, d), BF16)],
        compiler_params=_cparams(("arbitrary", "arbitrary")),
        name="norm_matmul",
    )(x, xs, gain, w)


def _mix_out(x, o_att, y_rwkv, y_conv, w_ref):
    wa, wr = o_att.shape[1], y_rwkv.shape[1]
    acc = jnp.dot(o_att.astype(BF16), w_ref[0:wa, :], preferred_element_type=F32)
    acc += jnp.dot(y_rwkv.astype(BF16), w_ref[wa:wa + wr, :], preferred_element_type=F32)
    acc += jnp.dot(y_conv.astype(BF16), w_ref[wa + wr:, :], preferred_element_type=F32)
    return x + acc


def _out_proj_body(x_ref, a_ref, r_ref, c_ref, w_ref, o_ref):
    o_ref[...] = _mix_out(x_ref[...], a_ref[...], r_ref[...], c_ref[...], w_ref)


def _out_proj_conv_body(x_ref, a_ref, r_ref, u_ref, gb_ref, gc_ref, pu_ref, pgc_ref, cw_ref, cb_ref, w_ref,
                        o_ref, zl_ref, zc_ref, *, blocks_per_seq):
    tm = u_ref.shape[0]
    first = (pl.program_id(0) % blocks_per_seq) == 0
    z = gc_ref[...] * u_ref[...]
    zc_ref[0:8, :] = jnp.where(first, 0.0, pgc_ref[...] * pu_ref[...])
    zc_ref[8:, :] = z
    yc = cb_ref[...] + cw_ref[0:1, :] * zc_ref[pl.ds(6, tm), :]
    yc = yc + cw_ref[1:2, :] * zc_ref[pl.ds(7, tm), :]
    yc = yc + cw_ref[2:3, :] * z
    zl_ref[...] = zc_ref[pl.ds(tm, 8), :]
    o_ref[...] = _mix_out(x_ref[...], a_ref[...], r_ref[...], gb_ref[...] * yc, w_ref)


def _out_proj_conv(x, o_att, y_rwkv, p, col0, cw, conv_w, conv_b, w, layer, t, tm):
    m, d = x.shape
    assert col0 % LANES == 0 and t % tm == 0
    cur = lambda s: pl.BlockSpec((pl.Element(tm), pl.Element(cw)),
                                 lambda i: (pl.multiple_of(i * tm, tm), col0 + s * cw))
    prev = lambda s: pl.BlockSpec((pl.Element(8), pl.Element(cw)),
                                  lambda i: (pl.multiple_of(jnp.maximum(i * tm - 8, 0), 8), col0 + s * cw))
    return pl.pallas_call(
        functools.partial(_out_proj_conv_body, blocks_per_seq=t // tm),
        grid=(m // tm,),
        in_specs=[
            pl.BlockSpec((tm, d), lambda i: (i, 0)),
            pl.BlockSpec((tm, o_att.shape[1]), lambda i: (i, 0)),
            pl.BlockSpec((tm, y_rwkv.shape[1]), lambda i: (i, 0)),
            cur(0), cur(1), cur(2), prev(0), prev(2),
            pl.BlockSpec((None, CONV_K, cw), lambda i: (layer, 0, 0)),
            pl.BlockSpec((None, 1, cw), lambda i: (layer, 0, 0)),
            pl.BlockSpec((None, w.shape[1], d), lambda i: (layer, 0, 0), pipeline_mode=pl.Buffered(1)),
        ],
        out_specs=[pl.BlockSpec((tm, d), lambda i: (i, 0)), pl.BlockSpec((8, cw), lambda i: (i, 0))],
        out_shape=[jax.ShapeDtypeStruct((m, d), F32), jax.ShapeDtypeStruct((m // tm * 8, cw), F32)],
        scratch_shapes=[pltpu.VMEM((tm + 8, cw), F32)],
        compiler_params=_cparams(("parallel",)),
        name="out_proj_conv",
    )(x, o_att, y_rwkv, p, p, p, p, p, conv_w, conv_b, w)


def _out_proj(x, o_att, y_rwkv, y_conv, w, layer, tm):
    m, d = x.shape
    return pl.pallas_call(
        _out_proj_body,
        grid=(m // tm,),
        in_specs=[
            pl.BlockSpec((tm, d), lambda i: (i, 0)),
            pl.BlockSpec((tm, o_att.shape[1]), lambda i: (i, 0)),
            pl.BlockSpec((tm, y_rwkv.shape[1]), lambda i: (i, 0)),
            pl.BlockSpec((tm, y_conv.shape[1]), lambda i: (i, 0)),
            pl.BlockSpec((None, w.shape[1], d), lambda i: (layer, 0, 0), pipeline_mode=pl.Buffered(1)),
        ],
        out_specs=pl.BlockSpec((tm, d), lambda i: (i, 0)),
        out_shape=jax.ShapeDtypeStruct((m, d), F32),
        compiler_params=_cparams(("parallel",)),
        name="out_proj",
    )(x, o_att, y_rwkv, y_conv, w)


def _pair_headnorm(x, g, lo_half):
    x2 = x * x
    s0 = jnp.sum(jnp.where(lo_half, x2, 0.0), axis=-1, keepdims=True)
    s1 = jnp.sum(jnp.where(lo_half, 0.0, x2), axis=-1, keepdims=True)
    ms = jnp.where(lo_half, s0, s1) * (1.0 / HEAD_DIM)
    return x * lax.rsqrt(ms + NORM_EPS) * g


def _attn_body(slopes_ref, q_ref, k_ref, v_ref, qg_ref, kg_ref, *refs, n_carried):
    o_ref, kt_ref, vt_ref, qs_ref, kn_ref, bias_ref, og_ref, lse_ref = refs[n_carried:]
    hp = pl.program_id(1)
    t = q_ref.shape[0]
    span = ATT_SPAN
    lo_half = lax.broadcasted_iota(jnp.int32, (1, LANES), 1) < HEAD_DIM

    kn = _pair_headnorm(k_ref[...], kg_ref[...], lo_half)
    kn_ref[...] = kn
    kt_ref[...] = kn.T
    vt_ref[...] = v_ref[...].T
    qs_ref[...] = _pair_headnorm(q_ref[...], qg_ref[...], lo_half) * (HEAD_DIM ** -0.5)

    qi = lax.broadcasted_iota(jnp.int32, (span, 2 * span), 0)
    ki = lax.broadcasted_iota(jnp.int32, (span, 2 * span), 1)
    steps = qi + span - ki
    valid = (steps >= 0) & (steps <= span)
    for g, (_, dil) in enumerate(DILATION_GROUPS):
        dist = (steps * dil).astype(F32)
        for h in range(2):
            bias_ref[g, h * span:(h + 1) * span, :] = jnp.where(valid, -(slopes_ref[2 * hp + h] * dist), NEG_BIG)

    def unit(g, dil, start, has_prev):
        rows = pl.ds(start, span, stride=dil)
        qb = qs_ref[rows, :]
        q2 = jnp.concatenate([jnp.where(lo_half, qb, 0.0), jnp.where(lo_half, 0.0, qb)], axis=0).astype(BF16)
        if has_prev:
            prow = pl.ds(start - span * dil, span, stride=dil)
            kb = jnp.concatenate([kn_ref[prow, :], kn_ref[rows, :]], axis=0)
            vb = jnp.concatenate([v_ref[prow, :], v_ref[rows, :]], axis=0)
            bias = bias_ref[g]
        else:
            kb = kn_ref[rows, :]
            vb = v_ref[rows, :]
            bias = bias_ref[g, :, span:]
        s = lax.dot_general(q2, kb.astype(BF16), _NT, preferred_element_type=F32) + bias
        m = jnp.max(s, axis=-1, keepdims=True)
        p = jnp.exp(s - m)
        l = jnp.sum(p, axis=-1, keepdims=True)
        o = jnp.dot(p.astype(BF16), vb.astype(BF16), preferred_element_type=F32) / l
        lse = m + jnp.log(l)
        og_ref[g, rows, :] = jnp.where(lo_half, o[0:span], o[span:])
        lse_ref[g, rows, :] = jnp.where(lo_half, lse[0:span], lse[span:])

    for g, (window, dil) in enumerate(DILATION_GROUPS):
        assert window // dil == span
        nb = t // (span * dil)

        def residue(r, carry, g=g, dil=dil, nb=nb):
            unit(g, dil, r, False)
            if nb > 1:
                def blk(b, c):
                    unit(g, dil, b * (span * dil) + r, True)
                    return c
                lax.fori_loop(1, nb, blk, 0, unroll=3 if nb <= 4 else 5)
            return carry

        if dil == 1:
            residue(0, 0)
        else:
            lax.fori_loop(0, dil, residue, 0, unroll=4 if nb == 1 else 1)

    l0, l1, l2 = lse_ref[0], lse_ref[1], lse_ref[2]
    mx = jnp.maximum(jnp.maximum(l0, l1), l2)
    w0, w1, w2 = jnp.exp(l0 - mx), jnp.exp(l1 - mx), jnp.exp(l2 - mx)
    o_ref[...] = (og_ref[0] * w0 + og_ref[1] * w1 + og_ref[2] * w2) / (w0 + w1 + w2)


def _attention_prompt(p, col0, att_w, qg, kg, slopes, layer, depth, n_seq, t, kv_stack):
    m = p.shape[0]
    pairs = att_w // LANES
    assert col0 % LANES == 0
    blk = lambda off: pl.BlockSpec((t, LANES), lambda n, hp: (n, off + hp))
    c0 = col0 // LANES
    gain = pl.BlockSpec((None, 1, LANES), lambda n, hp: (layer, 0, 0))
    tr_spec = pl.BlockSpec((None, None, LANES, t), lambda n, hp: (layer, n, hp, 0))
    tr_shape = jax.ShapeDtypeStruct((depth, n_seq, att_w, t), F32)
    carried = () if kv_stack is None else tuple(kv_stack)
    n_in = 6
    return pl.pallas_call(
        functools.partial(_attn_body, n_carried=len(carried)),
        grid=(n_seq, pairs),
        in_specs=[pl.BlockSpec(memory_space=pltpu.SMEM),
                  blk(c0), blk(c0 + pairs), blk(c0 + 2 * pairs), gain, gain]
                 + [pl.BlockSpec(memory_space=pl.ANY)] * len(carried),
        out_specs=[blk(0), tr_spec, tr_spec],
        out_shape=[jax.ShapeDtypeStruct((m, att_w), F32), tr_shape, tr_shape],
        input_output_aliases={n_in + i: 1 + i for i in range(len(carried))},
        scratch_shapes=[pltpu.VMEM((t, LANES), F32),
                        pltpu.VMEM((t, LANES), F32),
                        pltpu.VMEM((3, 2 * ATT_SPAN, 2 * ATT_SPAN), F32),
                        pltpu.VMEM((3, t, LANES), F32),
                        pltpu.VMEM((3, t, LANES), F32)],
        compiler_params=_cparams(("parallel", "arbitrary")),
        name="attn_prompt",
    )(slopes, p, p, p, qg, kg, *carried)


def _softplus(z):
    return jnp.maximum(z, 0.0) + jnp.log(1.0 + jnp.exp(-jnp.abs(z)))


def _rwkv_token_math(p_r, p_k, p_v, p_l, q_r, q_k, q_v, q_l, mu_r, mu_k, mu_v, mu_l,
                     w0, w2p, a0, a2p, g2p, k_k, k_a, gate_dot=_dot3):
    xr = p_r + (q_r - p_r) * mu_r
    xk = p_k + (q_k - p_k) * mu_k
    xv = p_v + (q_v - p_v) * mu_v
    xl = p_l + (q_l - p_l) * mu_l
    w_log = -_softplus(-(w0 + _dot3(jnp.tanh(xl), w2p))) - 0.5
    log_decay = -jnp.exp(w_log)
    sig_l = jax.nn.sigmoid(xl)
    a_gate = jax.nn.sigmoid(a0 + gate_dot(xl, a2p))
    g = gate_dot(sig_l, g2p)
    kk = xk * k_k
    kk = kk * lax.rsqrt(jnp.maximum(_segsum64_many([kk * kk])[0], 1e-24))
    kmod = xk * (1.0 + (a_gate - 1.0) * k_a)
    return xr, log_decay, kmod, xv, -kk, kk * a_gate, g


def _segsum64_many(xs):
    rows, width = xs[0].shape
    if rows % 16 != 0:
        return [_segsum64(x) for x in xs]
    nsl = width // LANES
    pieces = [part[:, i * LANES:(i + 1) * LANES]
              for x in xs for part in _split2(x) for i in range(nsl)]
    res = jnp.dot(jnp.concatenate(pieces, axis=0), _head_blockdiag(LANES, BF16), preferred_element_type=F32)
    outs = []
    for n in range(len(xs)):
        hi, lo = [jnp.concatenate([res[((2 * n + j) * nsl + i) * rows:((2 * n + j) * nsl + i + 1) * rows]
                                   for i in range(nsl)], axis=1) for j in range(2)]
        outs.append(hi + lo)
    return outs


def _rwkv_post(y, r, k, v, g, ln_w, ln_b, r_k):
    sum_y, sum_rk = _segsum64_many([y, r * k * r_k])
    yc = y - sum_y * (1.0 / HEAD_DIM)
    var_y = _segsum64_many([yc * yc])[0] * (1.0 / HEAD_DIM)
    yn = yc * lax.rsqrt(var_y + RWKV_GN_EPS) * ln_w + ln_b
    return (yn + sum_rk * v) * g


def _scan_chunk(r, lw, cum, k, v, a, b, s0):
    c = r.shape[1]
    c2 = 2 * c
    lane = lax.broadcasted_iota(jnp.int32, (1, 1, LANES), 2)
    head0 = lane < HEAD_DIM
    m0 = head0.astype(F32)
    m1 = 1.0 - m0
    stack = lambda x: jnp.concatenate([x * m0, x * m1], axis=1)
    twice = lambda x: jnp.concatenate([x, x], axis=1)
    pick = lambda x_st: jnp.where(head0, x_st[:, 0:c], x_st[:, c:])

    cum_prev = cum - lw
    cum_last = cum[:, c - 1:c, :]
    e_pos = jnp.exp(cum)
    e_neg = jnp.exp(-cum)
    e_rem = jnp.exp(cum_last - cum)
    at = a * jnp.exp(cum_prev)
    rt = r * e_pos
    bt = b * e_neg
    kt = k * e_neg
    bh = b * e_rem
    kh = k * e_rem

    big = _dot1(jnp.concatenate([stack(at), stack(rt)], axis=1),
                jnp.concatenate([twice(bt), twice(kt)], axis=1), _BNT)
    row = lax.broadcasted_iota(jnp.int32, (1, c2, c2), 1)
    col = lax.broadcasted_iota(jnp.int32, (1, c2, c2), 2)
    assert c == HEAD_DIM
    same_head = (row >> HEAD_SHIFT) == (col >> HEAD_SHIFT)
    tr, tc = row & (c - 1), col & (c - 1)
    strict = same_head & (tr > tc)
    incl = same_head & (tr >= tc)
    lmat = jnp.where(strict, big[:, 0:c2, 0:c2], 0.0)
    ak = jnp.where(strict, big[:, 0:c2, c2:], 0.0)
    rb = jnp.where(incl, big[:, c2:, 0:c2], 0.0)
    rk = jnp.where(incl, big[:, c2:, c2:], 0.0)

    eye = (row == col).astype(F32)
    x = eye + jnp.where((row >> 1) == (col >> 1), lmat, 0.0)
    s, log2s = 2, 1
    while s < c:
        joins = ((row >> (log2s + 1)) == (col >> (log2s + 1))) & ((row & s) != 0) & ((col & s) == 0)
        x = x + _dot1(_dot1(x, jnp.where(joins, lmat, 0.0), _BNN), x, _BNN)
        s, log2s = 2 * s, log2s + 1

    v_st = twice(v)
    w = _dot1(jnp.concatenate([at, rt], axis=1), s0, _BNT)
    u_st = _dot1(x, twice(w[:, 0:c]) + _dot1(ak, v_st, _BNN), _BNN)
    y_st = _dot1(jnp.concatenate([rb, rk], axis=2), jnp.concatenate([u_st, v_st], axis=1), _BNN)
    y = w[:, c:] + pick(y_st)
    u = pick(u_st)
    upd = _dot1(jnp.concatenate([u, v], axis=1), jnp.concatenate([bh, kh], axis=1), _BTN)
    s_new = s0 * jnp.exp(cum_last) + _head_blockdiag(LANES, F32)[None] * upd
    return y, s_new


def _rwkv_mix_body(pr_ref, pk_ref, pv_ref, plo_ref,
                   mur_ref, muk_ref, muv_ref, mul_ref, w0_ref, w2_ref, a0_ref, a2_ref, g2_ref,
                   kk_ref, ka_ref, lnw_ref, lnb_ref, rk_ref,
                   y_ref, s_out_ref, s_ref, sr_ref, sk_ref, sv_ref, sl_ref):
    ch = pl.program_id(1)
    nseq, c, rw = pr_ref.shape
    npair = rw // LANES
    shift_refs = (sr_ref, sk_ref, sv_ref, sl_ref)

    @pl.when(ch == 0)
    def _():
        s_ref[...] = jnp.zeros_like(s_ref)
        for ref in shift_refs:
            ref[:, 0:8, :] = jnp.zeros((nseq, 8, ref.shape[2]), F32)

    def cur_and_prev(cur_ref, sh_ref):
        cur = cur_ref[...]
        sh_ref[:, 8:, :] = cur
        prev = sh_ref[:, pl.ds(7, c), :]
        sh_ref[:, 7:8, :] = cur[:, c - 1:c, :]
        flat = lambda x: x.reshape(nseq * c, x.shape[2])
        return flat(cur), flat(prev)

    (p_r, q_r), (p_k, q_k), (p_v, q_v), (p_l, q_l) = [
        cur_and_prev(cur, sh) for cur, sh in zip((pr_ref, pk_ref, pv_ref, plo_ref), shift_refs)]
    r, lw, k, v, a, b, g = _rwkv_token_math(
        p_r, p_k, p_v, p_l, q_r, q_k, q_v, q_l,
        mur_ref[...], muk_ref[...], muv_ref[...], mul_ref[...],
        w0_ref[...], w2_ref[...], a0_ref[...], a2_ref[...], g2_ref[...], kk_ref[...], ka_ref[...],
        gate_dot=_dot1)

    pairs = lambda x: jnp.stack([x[s * c:(s + 1) * c, p * LANES:(p + 1) * LANES]
                                 for s in range(nseq) for p in range(npair)], axis=0)
    tri_incl = (lax.broadcasted_iota(jnp.int32, (c, c), 0)
                >= lax.broadcasted_iota(jnp.int32, (c, c), 1)).astype(BF16)
    cum = jnp.concatenate([_cumsum_rows(lw[s * c:(s + 1) * c], tri_incl) for s in range(nseq)], axis=0)
    y, s_new = _scan_chunk(pairs(r), pairs(lw), pairs(cum), pairs(k), pairs(v), pairs(a), pairs(b), s_ref[...])
    s_ref[...] = s_new
    y = jnp.concatenate([jnp.concatenate([y[s * npair + p] for p in range(npair)], axis=1)
                         for s in range(nseq)], axis=0)
    out = _rwkv_post(y, r, k, v, g, lnw_ref[...], lnb_ref[...], rk_ref[...])
    y_ref[...] = out.reshape(nseq, c, rw)

    @pl.when(ch == pl.num_programs(1) - 1)
    def _():
        for s in range(nseq):
            for p in range(npair):
                s_out_ref[s, 2 * p] = s_ref[s * npair + p, 0:HEAD_DIM, 0:HEAD_DIM]
                s_out_ref[s, 2 * p + 1] = s_ref[s * npair + p, HEAD_DIM:, HEAD_DIM:]


def _rwkv_mix_prompt(p, col0, lp, layer, n_seq, t, seq_per_step):
    m = p.shape[0]
    rw = lp["rwkv_w"]
    c = RWKV_CHUNK
    heads = rw // HEAD_DIM
    assert n_seq % seq_per_step == 0 and col0 % rw == 0 and (col0 + 3 * rw) % LORA_W == 0
    c_slab = col0 // rw
    c_lora = (col0 + 3 * rw) // LORA_W
    tok = lambda w, col: pl.BlockSpec((seq_per_step, c, w), lambda n, ch: (n, ch, col))
    vec = lambda w, col: pl.BlockSpec((None, 1, w), lambda n, ch: (layer, 0, col))
    mat = pl.BlockSpec((None, LORA_W, rw), lambda n, ch: (layer, 0, 0))
    p3 = p.reshape(n_seq, t, p.shape[1])
    y, wkv = pl.pallas_call(
        _rwkv_mix_body,
        grid=(n_seq // seq_per_step, t // c),
        in_specs=[tok(rw, c_slab), tok(rw, c_slab + 1), tok(rw, c_slab + 2), tok(LORA_W, c_lora),
                  vec(rw, 0), vec(rw, 1), vec(rw, 2), vec(LORA_W, (3 * rw) // LORA_W),
                  vec(rw, 0), mat, vec(rw, 0), mat, mat, vec(rw, 0), vec(rw, 0),
                  vec(rw, 0), vec(rw, 0), vec(rw, 0)],
        out_specs=[tok(rw, 0),
                   pl.BlockSpec((seq_per_step, heads, HEAD_DIM, HEAD_DIM), lambda n, ch: (n, 0, 0, 0))],
        out_shape=[jax.ShapeDtypeStruct((n_seq, t, rw), F32),
                   jax.ShapeDtypeStruct((n_seq, heads, HEAD_DIM, HEAD_DIM), F32)],
        scratch_shapes=[pltpu.VMEM((seq_per_step * (rw // LANES), LANES, LANES), F32)]
                       + [pltpu.VMEM((seq_per_step, c + 8, rw), F32)] * 3
                       + [pltpu.VMEM((seq_per_step, c + 8, LORA_W), F32)],
        compiler_params=_cparams(("parallel", "arbitrary")),
        name="rwkv_mix",
    )(p3, p3, p3, p3, lp["mu"], lp["mu"], lp["mu"], lp["mu"],
      lp["w0"], lp["w2p"], lp["a0"], lp["a2p"], lp["g2p"], lp["k_k"], lp["k_a"],
      lp["ln_w"], lp["ln_b"], lp["r_k"])
    return y.reshape(m, rw), wkv


def _sample_body(slope_ref, p_ref, kc_ref, vc_ref,
                 wkv_ref, shift_ref, cst_ref, qg_ref, kg_ref,
                 mu_ref, w0_ref, w2_ref, a0_ref, a2_ref, g2_ref, kk_ref, ka_ref,
                 rk_ref, lnw_ref, lnb_ref, cw_ref, cb_ref,
                 oatt_ref, yr_ref, yc_ref, kn_ref, vn_ref, wkvn_ref, cstn_ref,
                 *, col_att, col_rwkv, col_conv):
    heads_a = oatt_ref.shape[0]
    aw = heads_a * HEAD_DIM
    rw = yr_ref.shape[1]
    cw = yc_ref.shape[1]
    rp = shift_ref.shape[1]
    rows8 = lambda x: jnp.broadcast_to(x, (8, x.shape[1]))
    p_all = p_ref[...]

    by_head = lambda row: jnp.concatenate(
        [row[:, h * HEAD_DIM:(h + 1) * HEAD_DIM] for h in range(heads_a)], axis=0)
    ca = col_att
    q, k, v_new = (by_head(p_all[:, ca:ca + aw]), by_head(p_all[:, ca + aw:ca + 2 * aw]),
                   by_head(p_all[:, ca + 2 * aw:ca + 3 * aw]))
    qn = _rms(q, qg_ref[...]) * (HEAD_DIM ** -0.5)
    kn = _rms(k, kg_ref[...])
    kn_ref[...] = kn
    vn_ref[...] = v_new
    s_new = jnp.sum(kn * qn, axis=-1, keepdims=True)
    slope = slope_ref[...]
    past = kc_ref.shape[2]
    er = lax.broadcasted_iota(jnp.int32, (HEAD_DIM, HEAD_DIM), 0)
    ec = lax.broadcasted_iota(jnp.int32, (HEAD_DIM, HEAD_DIM), 1)
    eye = er == ec
    q_col = jnp.stack([jnp.sum(jnp.where(eye, qn[h:h + 1, :], 0.0), axis=1, keepdims=True)
                       for h in range(heads_a)], axis=0)
    dist = past - lax.broadcasted_iota(jnp.int32, (1, past), 1)
    biased = jnp.sum(kc_ref[...] * q_col, axis=1) - slope * dist.astype(F32)
    parts = []
    for window, dil in DILATION_GROUPS:
        assert dil & (dil - 1) == 0
        valid = ((dist & (dil - 1)) == 0) & (dist <= window)
        s = jnp.where(valid, biased, NEG_BIG)
        m = jnp.maximum(jnp.max(s, axis=1, keepdims=True), s_new)
        p = jnp.exp(s - m)
        p_new = jnp.exp(s_new - m)
        l = jnp.sum(p, axis=1, keepdims=True) + p_new
        parts.append((p, p_new, l, m + jnp.log(l)))
    mx = jnp.maximum(jnp.maximum(parts[0][3], parts[1][3]), parts[2][3])
    ws = [jnp.exp(lse - mx) for _, _, _, lse in parts]
    wsum = ws[0] + ws[1] + ws[2]
    scale_g = [w / (l * wsum) for w, (_, _, l, _) in zip(ws, parts)]
    coef = parts[0][0] * scale_g[0] + parts[1][0] * scale_g[1] + parts[2][0] * scale_g[2]
    coef_new = parts[0][1] * scale_g[0] + parts[1][1] * scale_g[1] + parts[2][1] * scale_g[2]
    o_col = jnp.sum(vc_ref[...] * coef[:, None, :], axis=2, keepdims=True)
    o_rows = jnp.concatenate([jnp.sum(jnp.where(eye, o_col[h], 0.0), axis=0, keepdims=True)
                              for h in range(heads_a)], axis=0)
    oatt_ref[...] = o_rows + coef_new * v_new

    pr = rows8(p_all[:, col_rwkv:col_rwkv + rp])
    sh = rows8(shift_ref[...])
    mu = mu_ref[...]
    c3 = 3 * rw
    r, lw, k, v, a, b, g = _rwkv_token_math(
        pr[:, 0:rw], pr[:, rw:2 * rw], pr[:, 2 * rw:c3], pr[:, c3:],
        sh[:, 0:rw], sh[:, rw:2 * rw], sh[:, 2 * rw:c3], sh[:, c3:],
        mu[:, 0:rw], mu[:, rw:2 * rw], mu[:, 2 * rw:c3], mu[:, c3:],
        w0_ref[...], w2_ref[...], a0_ref[...], a2_ref[...], g2_ref[...], kk_ref[...], ka_ref[...])
    decay = jnp.exp(lw)
    place_r = lax.broadcasted_iota(jnp.int32, (HEAD_DIM, rw), 0)
    place_c = lax.broadcasted_iota(jnp.int32, (HEAD_DIM, rw), 1)
    y_full = jnp.zeros((1, rw), F32)
    for h in range(rw // HEAD_DIM):
        hs = slice(h * HEAD_DIM, (h + 1) * HEAD_DIM)
        s_old = wkv_ref[h]
        sa = jnp.sum(s_old * a[0:1, hs], axis=1, keepdims=True)
        v_col = jnp.sum(jnp.where(eye, v[0:1, hs], 0.0), axis=1, keepdims=True)
        s_h = s_old * decay[0:1, hs] + sa * b[0:1, hs] + v_col * k[0:1, hs]
        wkvn_ref[h] = s_h
        y_col = jnp.sum(s_h * r[0:1, hs], axis=1, keepdims=True)
        y_full = y_full + jnp.sum(jnp.where(place_c == place_r + h * HEAD_DIM, y_col, 0.0),
                                  axis=0, keepdims=True)
    y8 = rows8(y_full)
    yr_ref[...] = _rwkv_post(y8, r, k, v, g, lnw_ref[...], lnb_ref[...], rk_ref[...])[0:1]

    c1 = col_conv
    z = p_all[:, c1 + 2 * cw:c1 + 3 * cw] * p_all[:, c1:c1 + cw]
    yc = cb_ref[...] + cw_ref[0:1, :] * cst_ref[0:1, :]
    yc = yc + cw_ref[1:2, :] * cst_ref[1:2, :]
    yc = yc + cw_ref[2:3, :] * z
    yc_ref[...] = p_all[:, c1 + cw:c1 + 2 * cw] * yc
    cstn_ref[0:1, :] = cst_ref[1:2, :]
    cstn_ref[1:2, :] = z


def _sample_mixers(p, cols, cache_k, cache_v, state_wkv, state_shift, state_conv, lp, slope_col, layer):
    nb = p.shape[0]
    depth, _, past, heads_a, _ = cache_k.shape
    rw = lp["rwkv_w"]
    rp = state_shift.shape[2]
    cw = state_conv.shape[3]
    heads = rw // HEAD_DIM
    span = ATT_SPAN

    row = lambda w: pl.BlockSpec((None, 1, w), lambda n: (n, 0, 0))
    lvec = lambda w: pl.BlockSpec((None, 1, w), lambda n: (layer, 0, 0))
    lmat = lambda r_, w: pl.BlockSpec((None, r_, w), lambda n: (layer, 0, 0))
    tile = pl.BlockSpec((None, heads_a, HEAD_DIM), lambda n: (n, 0, 0))

    assert past >= DILATION_GROUPS[-1][0]
    views = [jnp.transpose(c, (0, 1, 3, 4, 2)) for c in (cache_k, cache_v)]
    cache_specs = [pl.BlockSpec((None, None, heads_a, HEAD_DIM, past), lambda n: (layer, n, 0, 0, 0))] * 2

    outs = pl.pallas_call(
        functools.partial(_sample_body, col_att=cols[0], col_rwkv=cols[1], col_conv=cols[2]),
        grid=(nb,),
        in_specs=[pl.BlockSpec((heads_a, 1), lambda n: (0, 0)), row(p.shape[1])] + cache_specs + [
                  pl.BlockSpec((None, None, heads, HEAD_DIM, HEAD_DIM), lambda n: (layer, n, 0, 0, 0)),
                  pl.BlockSpec((None, None, 1, rp), lambda n: (layer, n, 0, 0)),
                  pl.BlockSpec((None, None, CONV_K - 1, cw), lambda n: (layer, n, 0, 0)),
                  lvec(HEAD_DIM), lvec(HEAD_DIM),
                  lvec(rp), lvec(rw), lmat(LORA_W, rw), lvec(rw), lmat(LORA_W, rw),
                  lmat(LORA_W, rw), lvec(rw), lvec(rw), lvec(rw), lvec(rw), lvec(rw),
                  lmat(CONV_K, cw), lvec(cw)],
        out_specs=[tile, row(rw), row(cw), tile, tile,
                   pl.BlockSpec((None, heads, HEAD_DIM, HEAD_DIM), lambda n: (n, 0, 0, 0)),
                   pl.BlockSpec((None, CONV_K - 1, cw), lambda n: (n, 0, 0))],
        out_shape=[jax.ShapeDtypeStruct((nb, heads_a, HEAD_DIM), F32), jax.ShapeDtypeStruct((nb, 1, rw), F32),
                   jax.ShapeDtypeStruct((nb, 1, cw), F32),
                   jax.ShapeDtypeStruct((nb, heads_a, HEAD_DIM), F32),
                   jax.ShapeDtypeStruct((nb, heads_a, HEAD_DIM), F32),
                   jax.ShapeDtypeStruct((nb, heads, HEAD_DIM, HEAD_DIM), F32),
                   jax.ShapeDtypeStruct((nb, CONV_K - 1, cw), F32)],
        compiler_params=_cparams(("parallel",)),
        name="sample_mixers",
    )(slope_col, p[:, None, :], *views,
      state_wkv, state_shift[:, :, None, :], state_conv,
      lp["q_gain"], lp["k_gain"],
      lp["mu"], lp["w0"], lp["w2p"], lp["a0"], lp["a2p"], lp["g2p"], lp["k_k"], lp["k_a"],
      lp["r_k"], lp["ln_w"], lp["ln_b"], lp["conv_w"], lp["conv_b"])
    o_att, y_rwkv, y_conv, k_new, v_new, wkv_new, conv_new = outs
    return o_att.reshape(nb, heads_a * HEAD_DIM), y_rwkv[:, 0], y_conv[:, 0], k_new, v_new, wkv_new, conv_new


def kernel(x_prompt, x_sample, cache_swa_k, cache_swa_v, state_wkv, state_shift, state_conv, ffn1_norm, ffn1_w_gate, ffn1_w_up, ffn1_w_down, mix_norm, w_in, q_norm, k_norm, rwkv_mu, rwkv_w0, rwkv_w2, rwkv_a0, rwkv_a2, rwkv_g2, rwkv_k_k, rwkv_k_a, rwkv_r_k, rwkv_ln_w, rwkv_ln_b, conv_w, conv_b, w_out, ffn2_norm, ffn2_w_gate, ffn2_w_up, ffn2_w_down):
    n_seq, t, d = x_prompt.shape
    nb = x_sample.shape[0]
    assert x_sample.shape[1] == 1
    depth = w_in.shape[0]
    att_heads, head_dim = cache_swa_k.shape[3], cache_swa_k.shape[4]
    assert head_dim == HEAD_DIM
    aw = att_heads * HEAD_DIM
    rw = rwkv_w0.shape[1]
    cw = conv_b.shape[1]
    rp = rwkv_mu.shape[1]
    assert rp == 3 * rw + LORA_W and w_in.shape[2] == 3 * aw + rp + 3 * cw

    row3 = lambda p: p.reshape(depth, 1, -1)
    col_att, col_rwkv, col_conv = 0, 3 * aw, 3 * aw + rp
    w_in16, w_out16 = w_in.astype(BF16), w_out.astype(BF16)
    zpad = lambda w, before: jnp.pad(w, ((0, 0), (before, LORA_W - before - w.shape[1]), (0, 0)))
    lp = {
        "rwkv_w": rw,
        "mu": row3(rwkv_mu), "w0": row3(rwkv_w0), "a0": row3(rwkv_a0),
        "w2p": zpad(rwkv_w2, 0), "a2p": zpad(rwkv_a2, W_LORA), "g2p": zpad(rwkv_g2, W_LORA + A_LORA),
        "k_k": row3(rwkv_k_k), "k_a": row3(rwkv_k_a), "r_k": row3(rwkv_r_k),
        "ln_w": row3(rwkv_ln_w), "ln_b": row3(rwkv_ln_b),
        "conv_w": conv_w, "conv_b": row3(conv_b),
        "q_gain": row3(q_norm), "k_gain": row3(k_norm),
        "q_gain_pair": row3(jnp.tile(q_norm, (1, LANES // HEAD_DIM))),
        "k_gain_pair": row3(jnp.tile(k_norm, (1, LANES // HEAD_DIM))),
    }
    g_ffn1, g_mix, g_ffn2 = row3(ffn1_norm), row3(mix_norm), row3(ffn2_norm)
    slopes = 2.0 ** (-8.0 * jnp.arange(1, att_heads + 1, dtype=F32) / att_heads)
    slope_col = slopes[:, None]

    m = n_seq * t
    xp = x_prompt.reshape(m, d)
    xs = x_sample.reshape(nb, d)
    tm_ffn, tf = 1024, 256
    tm_in, tn_in = 1024, 1280
    tm_out = 512
    p_states, s_states = [], []
    kv_stack = None
    for l in range(depth):
        xp, xs = _ffn(xp, xs, g_ffn1, ffn1_w_gate, ffn1_w_up, ffn1_w_down, l, tm_ffn, tf)
        pp, ps = _norm_matmul(xp, xs, g_mix, w_in16, l, tm_in, tn_in)

        o_att, *kv_stack = _attention_prompt(pp, col_att, aw, lp["q_gain_pair"], lp["k_gain_pair"], slopes,
                                             l, depth, n_seq, t, kv_stack)
        y_rwkv, wkv_p = _rwkv_mix_prompt(pp, col_rwkv, lp, l, n_seq, t, 4 if n_seq % 4 == 0 else 1)
        xp, z_last = _out_proj_conv(xp, o_att, y_rwkv, pp, col_conv, cw, lp["conv_w"], lp["conv_b"], w_out16,
                                    l, t, tm_out)
        p_states.append((
            wkv_p, pp.reshape(n_seq, t, -1)[:, -1, col_rwkv:col_rwkv + rp],
            z_last.reshape(n_seq, t // tm_out, 8, cw)[:, -1, 8 - (CONV_K - 1):]))

        so, sy, syc, sk, sv, swkv, sconv = _sample_mixers(
            ps, (col_att, col_rwkv, col_conv), cache_swa_k, cache_swa_v, state_wkv, state_shift, state_conv,
            lp, slope_col, l)
        xs = _out_proj(xs, so, sy, syc, w_out16, l, nb)
        s_states.append((sk[:, None], sv[:, None], swkv, ps[:, col_rwkv:col_rwkv + rp], sconv))

        xp, xs = _ffn(xp, xs, g_ffn2, ffn2_w_gate, ffn2_w_up, ffn2_w_down, l, tm_ffn, tf)

    stacked = lambda states, i: jnp.stack([st[i] for st in states], axis=0)
    rows = lambda x_t: jnp.transpose(x_t.reshape(depth, n_seq, att_heads, HEAD_DIM, t), (0, 1, 4, 2, 3))
    return (xp.reshape(n_seq, t, d), xs.reshape(nb, 1, d),
            rows(kv_stack[0]), rows(kv_stack[1]),
            stacked(p_states, 0), stacked(p_states, 1), stacked(p_states, 2),
            stacked(s_states, 0), stacked(s_states, 1), stacked(s_states, 2), stacked(s_states, 3),
            stacked(s_states, 4))
```

```python
import functools

import jax
import jax.numpy as jnp
from jax import lax
from jax.experimental import pallas as pl
from jax.experimental.pallas import tpu as pltpu

F32 = jnp.float32
BF16 = jnp.bfloat16

HEAD_DIM = 64
HEAD_SHIFT = 6
LANES = 128
NORM_EPS = 1e-6
RWKV_GN_EPS = 64e-5
DILATION_GROUPS = ((128, 1), (512, 4), (2048, 16))
ATT_SPAN = 128
W_LORA, A_LORA, G_LORA = 64, 64, 128
LORA_W = W_LORA + A_LORA + G_LORA
CONV_K = 3
RWKV_CHUNK = 64
VMEM_LIMIT = 56 * 1024 * 1024
NEG_BIG = -1e30

_NT = (((1,), (1,)), ((), ()))
_NN = (((1,), (0,)), ((), ()))
_BNN = (((2,), (1,)), ((0,), (0,)))
_BNT = (((2,), (2,)), ((0,), (0,)))
_BTN = (((1,), (1,)), ((0,), (0,)))


def _cparams(sem):
    return pltpu.CompilerParams(dimension_semantics=sem, vmem_limit_bytes=VMEM_LIMIT)


def _rms(x, g):
    return x * lax.rsqrt(jnp.mean(x * x, axis=-1, keepdims=True) + NORM_EPS) * g


def _split2(x):
    hi = x.astype(BF16)
    lo = (x - hi.astype(F32)).astype(BF16)
    return hi, lo


def _dot3(a, b, dims=_NN):
    ah, al = _split2(a)
    bh, bl = _split2(b)
    d = lambda x, y: lax.dot_general(x, y, dims, preferred_element_type=F32)
    return d(ah, bh) + (d(al, bh) + d(ah, bl))


def _dot1(a, b, dims=_NN):
    return lax.dot_general(a.astype(BF16), b.astype(BF16), dims, preferred_element_type=F32)


def _split3(x):
    h1 = x.astype(BF16)
    r1 = x - h1.astype(F32)
    h2 = r1.astype(BF16)
    h3 = (r1 - h2.astype(F32)).astype(BF16)
    return h1, h2, h3


def _dot_sel(x, sel):
    return sum(jnp.dot(h, sel, preferred_element_type=F32) for h in _split3(x))


def _cumsum_rows(x, tri_incl):
    return sum(jnp.dot(tri_incl, h, preferred_element_type=F32) for h in _split2(x))


def _head_blockdiag(n, dtype):
    r = lax.broadcasted_iota(jnp.int32, (n, n), 0) >> HEAD_SHIFT
    c = lax.broadcasted_iota(jnp.int32, (n, n), 1) >> HEAD_SHIFT
    return (r == c).astype(dtype)


def _segsum64(x):
    bd = _head_blockdiag(LANES, BF16)
    parts = [_dot_sel(x[:, i:i + LANES], bd) for i in range(0, x.shape[1], LANES)]
    return parts[0] if len(parts) == 1 else jnp.concatenate(parts, axis=1)


BF16_ROWS = 16


def _ffn_body(x_ref, xs_ref, g_ref, wg_ref, wu_ref, wd_ref, *refs, n_cast):
    cast_in = refs[:n_cast]
    o_ref, os_ref = refs[n_cast:n_cast + 2]
    cast_out = refs[n_cast + 2:2 * n_cast + 2]
    xn_ref = refs[2 * n_cast + 2]
    j = pl.program_id(1)
    tm, d = x_ref.shape

    @pl.when(j == 0)
    def _():
        xn_ref[0:tm, :] = _rms(x_ref[...], g_ref[...]).astype(BF16)
        xn_ref[tm:, :] = _rms(xs_ref[...], g_ref[...]).astype(BF16)
        o_ref[...] = jnp.zeros_like(o_ref)
        os_ref[...] = jnp.zeros_like(os_ref)

    for src, dst in zip(cast_in, cast_out):
        dst[...] = src[...].astype(BF16)

    xn = xn_ref[...]
    h = jnp.dot(xn, wg_ref[...].astype(BF16), preferred_element_type=F32)
    u = jnp.dot(xn, wu_ref[...].astype(BF16), preferred_element_type=F32)
    act = (h * jax.nn.sigmoid(h) * u).astype(BF16)
    half = d // 2
    for c in range(2):
        cols = slice(c * half, (c + 1) * half)
        part = jnp.dot(act, wd_ref[:, cols].astype(BF16), preferred_element_type=F32)
        o_ref[:, cols] += part[0:tm]
        os_ref[:, cols] += part[tm:]

    @pl.when(j == pl.num_programs(1) - 1)
    def _():
        o_ref[...] = x_ref[...] + 0.5 * o_ref[...]
        os_ref[...] = xs_ref[...] + 0.5 * os_ref[...]


def _ffn(x, xs, gain, wg, wu, wd, layer, tm, tf, cast=()):
    m, d = x.shape
    ms = xs.shape[0]
    f = wg.shape[2]
    ni, nj = m // tm, f // tf
    cast_in, cast_out, cast_shapes = [], [], []
    for w in cast:
        rows = w.shape[1]
        cr = next(r for r in range(BF16_ROWS, rows + 1, BF16_ROWS) if rows % r == 0 and rows // r <= ni * nj)
        chunks = rows // cr
        chunk = lambda i, j, chunks=chunks: jnp.minimum(i * nj + j, chunks - 1)
        cast_in.append(pl.BlockSpec((None, cr, w.shape[2]), lambda i, j, chunk=chunk: (layer, chunk(i, j), 0)))
        cast_out.append(pl.BlockSpec((cr, w.shape[2]), lambda i, j, chunk=chunk: (chunk(i, j), 0)))
        cast_shapes.append(jax.ShapeDtypeStruct(w.shape[1:], BF16))
    return pl.pallas_call(
        functools.partial(_ffn_body, n_cast=len(cast)),
        grid=(ni, nj),
        in_specs=[
            pl.BlockSpec((tm, d), lambda i, j: (i, 0)),
            pl.BlockSpec((ms, d), lambda i, j: (0, 0)),
            pl.BlockSpec((None, 1, d), lambda i, j: (layer, 0, 0)),
            pl.BlockSpec((None, d, tf), lambda i, j: (layer, 0, j)),
            pl.BlockSpec((None, d, tf), lambda i, j: (layer, 0, j)),
            pl.BlockSpec((None, tf, d), lambda i, j: (layer, j, 0)),
        ] + cast_in,
        out_specs=[pl.BlockSpec((tm, d), lambda i, j: (i, 0)),
                   pl.BlockSpec((ms, d), lambda i, j: (0, 0))] + cast_out,
        out_shape=[jax.ShapeDtypeStruct((m, d), F32), jax.ShapeDtypeStruct((ms, d), F32)] + cast_shapes,
        scratch_shapes=[pltpu.VMEM((tm + ms, d), BF16)],
        compiler_params=_cparams(("arbitrary", "arbitrary")),
        name="ffn",
    )(x, xs, gain, wg, wu, wd, *cast)


def _norm_matmul_body(x_ref, xs_ref, g_ref, w_ref, o_ref, os_ref, xn_ref):
    tm = x_ref.shape[0]

    @pl.when(pl.program_id(1) == 0)
    def _():
        xn_ref[0:tm, :] = _rms(x_ref[...], g_ref[...]).astype(BF16)
        xn_ref[tm:, :] = _rms(xs_ref[...], g_ref[...]).astype(BF16)

    out = jnp.dot(xn_ref[...], w_ref[...], preferred_element_type=F32)
    o_ref[...] = out[0:tm]
    os_ref[...] = out[tm:]


def _norm_matmul(x, xs, gain, w, layer, tm, tn):
    m, d = x.shape
    ms = xs.shape[0]
    n = w.shape[1]
    return pl.pallas_call(
        _norm_matmul_body,
        grid=(m // tm, n // tn),
        in_specs=[
            pl.BlockSpec((tm, d), lambda i, j: (i, 0)),
            pl.BlockSpec((ms, d), lambda i, j: (0, 0)),
            pl.BlockSpec((None, 1, d), lambda i, j: (layer, 0, 0)),
            pl.BlockSpec((d, tn), lambda i, j: (0, j)),
        ],
        out_specs=[pl.BlockSpec((tm, tn), lambda i, j: (i, j)),
                   pl.BlockSpec((ms, tn), lambda i, j: (0, j))],
        out_shape=[jax.ShapeDtypeStruct((m, n), F32), jax.ShapeDtypeStruct((ms, n), F32)],
        scratch_shapes=[pltpu.VMEM((tm + ms, d), BF16)],
        compiler_params=_cparams(("arbitrary", "arbitrary")),
        name="norm_matmul",
    )(x, xs, gain, w)


def _mix_out(x, o_att, y_rwkv, y_conv, w_ref):
    wa, wr = o_att.shape[1], y_rwkv.shape[1]
    acc = jnp.dot(o_att.astype(BF16), w_ref[0:wa, :], preferred_element_type=F32)
    acc += jnp.dot(y_rwkv.astype(BF16), w_ref[wa:wa + wr, :], preferred_element_type=F32)
    acc += jnp.dot(y_conv.astype(BF16), w_ref[wa + wr:, :], preferred_element_type=F32)
    return x + acc


def _out_proj_body(x_ref, a_ref, r_ref, c_ref, w_ref, o_ref):
    o_ref[...] = _mix_out(x_ref[...], a_ref[...], r_ref[...], c_ref[...], w_ref)


def _out_proj_conv_body(x_ref, a_ref, r_ref, u_ref, gb_ref, gc_ref, pu_ref, pgc_ref, cw_ref, cb_ref, w_ref,
                        o_ref, zl_ref, zc_ref, *, blocks_per_seq):
    tm = u_ref.shape[0]
    first = (pl.program_id(0) % blocks_per_seq) == 0
    z = gc_ref[...] * u_ref[...]
    zc_ref[0:8, :] = jnp.where(first, 0.0, pgc_ref[...] * pu_ref[...])
    zc_ref[8:, :] = z
    yc = cb_ref[...] + cw_ref[0:1, :] * zc_ref[pl.ds(6, tm), :]
    yc = yc + cw_ref[1:2, :] * zc_ref[pl.ds(7, tm), :]
    yc = yc + cw_ref[2:3, :] * z
    zl_ref[...] = zc_ref[pl.ds(tm, 8), :]
    o_ref[...] = _mix_out(x_ref[...], a_ref[...], r_ref[...], gb_ref[...] * yc, w_ref)


def _out_proj_conv(x, o_att, y_rwkv, p, col0, cw, conv_w, conv_b, w, layer, t, tm):
    m, d = x.shape
    assert col0 % LANES == 0 and t % tm == 0
    cur = lambda s: pl.BlockSpec((pl.Element(tm), pl.Element(cw)),
                                 lambda i: (pl.multiple_of(i * tm, tm), col0 + s * cw))
    prev = lambda s: pl.BlockSpec((pl.Element(8), pl.Element(cw)),
                                  lambda i: (pl.multiple_of(jnp.maximum(i * tm - 8, 0), 8), col0 + s * cw))
    return pl.pallas_call(
        functools.partial(_out_proj_conv_body, blocks_per_seq=t // tm),
        grid=(m // tm,),
        in_specs=[
            pl.BlockSpec((tm, d), lambda i: (i, 0)),
            pl.BlockSpec((tm, o_att.shape[1]), lambda i: (i, 0)),
            pl.BlockSpec((tm, y_rwkv.shape[1]), lambda i: (i, 0)),
            cur(0), cur(1), cur(2), prev(0), prev(2),
            pl.BlockSpec((None, CONV_K, cw), lambda i: (layer, 0, 0)),
            pl.BlockSpec((None, 1, cw), lambda i: (layer, 0, 0)),
            pl.BlockSpec((w.shape[0], d), lambda i: (0, 0), pipeline_mode=pl.Buffered(1)),
        ],
        out_specs=[pl.BlockSpec((tm, d), lambda i: (i, 0)), pl.BlockSpec((8, cw), lambda i: (i, 0))],
        out_shape=[jax.ShapeDtypeStruct((m, d), F32), jax.ShapeDtypeStruct((m // tm * 8, cw), F32)],
        scratch_shapes=[pltpu.VMEM((tm + 8, cw), F32)],
        compiler_params=_cparams(("parallel",)),
        name="out_proj_conv",
    )(x, o_att, y_rwkv, p, p, p, p, p, conv_w, conv_b, w)


def _out_proj(x, o_att, y_rwkv, y_conv, w, layer, tm):
    m, d = x.shape
    return pl.pallas_call(
        _out_proj_body,
        grid=(m // tm,),
        in_specs=[
            pl.BlockSpec((tm, d), lambda i: (i, 0)),
            pl.BlockSpec((tm, o_att.shape[1]), lambda i: (i, 0)),
            pl.BlockSpec((tm, y_rwkv.shape[1]), lambda i: (i, 0)),
            pl.BlockSpec((tm, y_conv.shape[1]), lambda i: (i, 0)),
            pl.BlockSpec((w.shape[0], d), lambda i: (0, 0), pipeline_mode=pl.Buffered(1)),
        ],
        out_specs=pl.BlockSpec((tm, d), lambda i: (i, 0)),
        out_shape=jax.ShapeDtypeStruct((m, d), F32),
        compiler_params=_cparams(("parallel",)),
        name="out_proj",
    )(x, o_att, y_rwkv, y_conv, w)


def _pair_headnorm(x, g, lo_half):
    x2 = x * x
    s0 = jnp.sum(jnp.where(lo_half, x2, 0.0), axis=-1, keepdims=True)
    s1 = jnp.sum(jnp.where(lo_half, 0.0, x2), axis=-1, keepdims=True)
    ms = jnp.where(lo_half, s0, s1) * (1.0 / HEAD_DIM)
    return x * lax.rsqrt(ms + NORM_EPS) * g


def _attn_body(slopes_ref, q_ref, k_ref, v_ref, qg_ref, kg_ref, *refs, n_carried):
    o_ref, kt_ref, vt_ref, qs_ref, kn_ref, bias_ref, og_ref, lse_ref = refs[n_carried:]
    hp = pl.program_id(1)
    t = q_ref.shape[0]
    span = ATT_SPAN
    lo_half = lax.broadcasted_iota(jnp.int32, (1, LANES), 1) < HEAD_DIM

    kn = _pair_headnorm(k_ref[...], kg_ref[...], lo_half)
    kn_ref[...] = kn
    kt_ref[...] = kn.T
    vt_ref[...] = v_ref[...].T
    qs_ref[...] = _pair_headnorm(q_ref[...], qg_ref[...], lo_half) * (HEAD_DIM ** -0.5)

    qi = lax.broadcasted_iota(jnp.int32, (span, 2 * span), 0)
    ki = lax.broadcasted_iota(jnp.int32, (span, 2 * span), 1)
    steps = qi + span - ki
    valid = (steps >= 0) & (steps <= span)
    for g, (_, dil) in enumerate(DILATION_GROUPS):
        dist = (steps * dil).astype(F32)
        for h in range(2):
            bias_ref[g, h * span:(h + 1) * span, :] = jnp.where(valid, -(slopes_ref[2 * hp + h] * dist), NEG_BIG)

    def unit(g, dil, start, has_prev):
        rows = pl.ds(start, span, stride=dil)
        qb = qs_ref[rows, :]
        q2 = jnp.concatenate([jnp.where(lo_half, qb, 0.0), jnp.where(lo_half, 0.0, qb)], axis=0).astype(BF16)
        if has_prev:
            prow = pl.ds(start - span * dil, span, stride=dil)
            kb = jnp.concatenate([kn_ref[prow, :], kn_ref[rows, :]], axis=0)
            vb = jnp.concatenate([v_ref[prow, :], v_ref[rows, :]], axis=0)
            bias = bias_ref[g]
        else:
            kb = kn_ref[rows, :]
            vb = v_ref[rows, :]
            bias = bias_ref[g, :, span:]
        s = lax.dot_general(q2, kb.astype(BF16), _NT, preferred_element_type=F32) + bias
        m = jnp.max(s, axis=-1, keepdims=True)
        p = jnp.exp(s - m)
        l = jnp.sum(p, axis=-1, keepdims=True)
        o = jnp.dot(p.astype(BF16), vb.astype(BF16), preferred_element_type=F32) / l
        lse = m + jnp.log(l)
        og_ref[g, rows, :] = jnp.where(lo_half, o[0:span], o[span:])
        lse_ref[g, rows, :] = jnp.where(lo_half, lse[0:span], lse[span:])

    for g, (window, dil) in enumerate(DILATION_GROUPS):
        assert window // dil == span
        nb = t // (span * dil)

        def residue(r, carry, g=g, dil=dil, nb=nb):
            unit(g, dil, r, False)
            if nb > 1:
                def blk(b, c):
                    unit(g, dil, b * (span * dil) + r, True)
                    return c
                lax.fori_loop(1, nb, blk, 0, unroll=3 if nb <= 4 else 5)
            return carry

        if dil == 1:
            residue(0, 0)
        else:
            lax.fori_loop(0, dil, residue, 0, unroll=4 if nb == 1 else 1)

    l0, l1, l2 = lse_ref[0], lse_ref[1], lse_ref[2]
    mx = jnp.maximum(jnp.maximum(l0, l1), l2)
    w0, w1, w2 = jnp.exp(l0 - mx), jnp.exp(l1 - mx), jnp.exp(l2 - mx)
    o_ref[...] = ((og_ref[0] * w0 + og_ref[1] * w1 + og_ref[2] * w2) / (w0 + w1 + w2)).astype(o_ref.dtype)


def _attention_prompt(p, col0, att_w, qg, kg, slopes, layer, depth, n_seq, t, kv_stack):
    m = p.shape[0]
    pairs = att_w // LANES
    assert col0 % LANES == 0
    blk = lambda off: pl.BlockSpec((t, LANES), lambda n, hp: (n, off + hp))
    c0 = col0 // LANES
    gain = pl.BlockSpec((None, 1, LANES), lambda n, hp: (layer, 0, 0))
    tr_spec = pl.BlockSpec((None, None, LANES, t), lambda n, hp: (layer, n, hp, 0))
    tr_shape = jax.ShapeDtypeStruct((depth, n_seq, att_w, t), F32)
    carried = () if kv_stack is None else tuple(kv_stack)
    n_in = 6
    return pl.pallas_call(
        functools.partial(_attn_body, n_carried=len(carried)),
        grid=(n_seq, pairs),
        in_specs=[pl.BlockSpec(memory_space=pltpu.SMEM),
                  blk(c0), blk(c0 + pairs), blk(c0 + 2 * pairs), gain, gain]
                 + [pl.BlockSpec(memory_space=pl.ANY)] * len(carried),
        out_specs=[blk(0), tr_spec, tr_spec],
        out_shape=[jax.ShapeDtypeStruct((m, att_w), BF16), tr_shape, tr_shape],
        input_output_aliases={n_in + i: 1 + i for i in range(len(carried))},
        scratch_shapes=[pltpu.VMEM((t, LANES), F32),
                        pltpu.VMEM((t, LANES), F32),
                        pltpu.VMEM((3, 2 * ATT_SPAN, 2 * ATT_SPAN), F32),
                        pltpu.VMEM((3, t, LANES), F32),
                        pltpu.VMEM((3, t, LANES), F32)],
        compiler_params=_cparams(("parallel", "arbitrary")),
        name="attn_prompt",
    )(slopes, p, p, p, qg, kg, *carried)


def _softplus(z):
    return jnp.maximum(z, 0.0) + jnp.log(1.0 + jnp.exp(-jnp.abs(z)))


def _rwkv_token_math(p_r, p_k, p_v, p_l, q_r, q_k, q_v, q_l, mu_r, mu_k, mu_v, mu_l,
                     w0, w2p, a0, a2p, g2p, k_k, k_a, gate_dot=_dot3):
    xr = p_r + (q_r - p_r) * mu_r
    xk = p_k + (q_k - p_k) * mu_k
    xv = p_v + (q_v - p_v) * mu_v
    xl = p_l + (q_l - p_l) * mu_l
    w_log = -_softplus(-(w0 + _dot3(jnp.tanh(xl), w2p))) - 0.5
    log_decay = -jnp.exp(w_log)
    sig_l = jax.nn.sigmoid(xl)
    a_gate = jax.nn.sigmoid(a0 + gate_dot(xl, a2p))
    g = gate_dot(sig_l, g2p)
    kk = xk * k_k
    kk = kk * lax.rsqrt(jnp.maximum(_segsum64_many([kk * kk])[0], 1e-24))
    kmod = xk * (1.0 + (a_gate - 1.0) * k_a)
    return xr, log_decay, kmod, xv, -kk, kk * a_gate, g


def _segsum64_many(xs):
    rows, width = xs[0].shape
    if rows % 16 != 0:
        return [_segsum64(x) for x in xs]
    nsl = width // LANES
    pieces = [part[:, i * LANES:(i + 1) * LANES]
              for x in xs for part in _split2(x) for i in range(nsl)]
    res = jnp.dot(jnp.concatenate(pieces, axis=0), _head_blockdiag(LANES, BF16), preferred_element_type=F32)
    outs = []
    for n in range(len(xs)):
        hi, lo = [jnp.concatenate([res[((2 * n + j) * nsl + i) * rows:((2 * n + j) * nsl + i + 1) * rows]
                                   for i in range(nsl)], axis=1) for j in range(2)]
        outs.append(hi + lo)
    return outs


def _rwkv_post(y, r, k, v, g, ln_w, ln_b, r_k):
    sum_y, sum_rk = _segsum64_many([y, r * k * r_k])
    yc = y - sum_y * (1.0 / HEAD_DIM)
    var_y = _segsum64_many([yc * yc])[0] * (1.0 / HEAD_DIM)
    yn = yc * lax.rsqrt(var_y + RWKV_GN_EPS) * ln_w + ln_b
    return (yn + sum_rk * v) * g


def _scan_chunk(r, lw, cum, k, v, a, b, s0):
    c = r.shape[1]
    c2 = 2 * c
    lane = lax.broadcasted_iota(jnp.int32, (1, 1, LANES), 2)
    head0 = lane < HEAD_DIM
    m0 = head0.astype(F32)
    m1 = 1.0 - m0
    stack = lambda x: jnp.concatenate([x * m0, x * m1], axis=1)
    twice = lambda x: jnp.concatenate([x, x], axis=1)
    pick = lambda x_st: jnp.where(head0, x_st[:, 0:c], x_st[:, c:])

    cum_prev = cum - lw
    cum_last = cum[:, c - 1:c, :]
    e_pos = jnp.exp(cum)
    e_neg = jnp.exp(-cum)
    e_rem = jnp.exp(cum_last - cum)
    at = a * jnp.exp(cum_prev)
    rt = r * e_pos
    bt = b * e_neg
    kt = k * e_neg
    bh = b * e_rem
    kh = k * e_rem

    big = _dot1(jnp.concatenate([stack(at), stack(rt)], axis=1),
                jnp.concatenate([twice(bt), twice(kt)], axis=1), _BNT)
    row = lax.broadcasted_iota(jnp.int32, (1, c2, c2), 1)
    col = lax.broadcasted_iota(jnp.int32, (1, c2, c2), 2)
    assert c == HEAD_DIM
    same_head = (row >> HEAD_SHIFT) == (col >> HEAD_SHIFT)
    tr, tc = row & (c - 1), col & (c - 1)
    strict = same_head & (tr > tc)
    incl = same_head & (tr >= tc)
    lmat = jnp.where(strict, big[:, 0:c2, 0:c2], 0.0)
    ak = jnp.where(strict, big[:, 0:c2, c2:], 0.0)
    rb = jnp.where(incl, big[:, c2:, 0:c2], 0.0)
    rk = jnp.where(incl, big[:, c2:, c2:], 0.0)

    eye = (row == col).astype(F32)
    x = eye + jnp.where((row >> 1) == (col >> 1), lmat, 0.0)
    s, log2s = 2, 1
    while s < c:
        joins = ((row >> (log2s + 1)) == (col >> (log2s + 1))) & ((row & s) != 0) & ((col & s) == 0)
        x = x + _dot1(_dot1(x, jnp.where(joins, lmat, 0.0), _BNN), x, _BNN)
        s, log2s = 2 * s, log2s + 1

    v_st = twice(v)
    w = _dot1(jnp.concatenate([at, rt], axis=1), s0, _BNT)
    u_st = _dot1(x, twice(w[:, 0:c]) + _dot1(ak, v_st, _BNN), _BNN)
    y_st = _dot1(jnp.concatenate([rb, rk], axis=2), jnp.concatenate([u_st, v_st], axis=1), _BNN)
    y = w[:, c:] + pick(y_st)
    u = pick(u_st)
    upd = _dot1(jnp.concatenate([u, v], axis=1), jnp.concatenate([bh, kh], axis=1), _BTN)
    s_new = s0 * jnp.exp(cum_last) + _head_blockdiag(LANES, F32)[None] * upd
    return y, s_new


def _rwkv_mix_body(pr_ref, pk_ref, pv_ref, plo_ref,
                   mur_ref, muk_ref, muv_ref, mul_ref, w0_ref, w2_ref, a0_ref, a2_ref, g2_ref,
                   kk_ref, ka_ref, lnw_ref, lnb_ref, rk_ref,
                   y_ref, s_out_ref, s_ref, sr_ref, sk_ref, sv_ref, sl_ref):
    ch = pl.program_id(1)
    nseq, c, rw = pr_ref.shape
    npair = rw // LANES
    shift_refs = (sr_ref, sk_ref, sv_ref, sl_ref)

    @pl.when(ch == 0)
    def _():
        s_ref[...] = jnp.zeros_like(s_ref)
        for ref in shift_refs:
            ref[:, 0:8, :] = jnp.zeros((nseq, 8, ref.shape[2]), F32)

    def cur_and_prev(cur_ref, sh_ref):
        cur = cur_ref[...]
        sh_ref[:, 8:, :] = cur
        prev = sh_ref[:, pl.ds(7, c), :]
        sh_ref[:, 7:8, :] = cur[:, c - 1:c, :]
        flat = lambda x: x.reshape(nseq * c, x.shape[2])
        return flat(cur), flat(prev)

    (p_r, q_r), (p_k, q_k), (p_v, q_v), (p_l, q_l) = [
        cur_and_prev(cur, sh) for cur, sh in zip((pr_ref, pk_ref, pv_ref, plo_ref), shift_refs)]
    r, lw, k, v, a, b, g = _rwkv_token_math(
        p_r, p_k, p_v, p_l, q_r, q_k, q_v, q_l,
        mur_ref[...], muk_ref[...], muv_ref[...], mul_ref[...],
        w0_ref[...], w2_ref[...], a0_ref[...], a2_ref[...], g2_ref[...], kk_ref[...], ka_ref[...],
        gate_dot=_dot1)

    pairs = lambda x: jnp.stack([x[s * c:(s + 1) * c, p * LANES:(p + 1) * LANES]
                                 for s in range(nseq) for p in range(npair)], axis=0)
    tri_incl = (lax.broadcasted_iota(jnp.int32, (c, c), 0)
                >= lax.broadcasted_iota(jnp.int32, (c, c), 1)).astype(BF16)
    cum = jnp.concatenate([_cumsum_rows(lw[s * c:(s + 1) * c], tri_incl) for s in range(nseq)], axis=0)
    y, s_new = _scan_chunk(pairs(r), pairs(lw), pairs(cum), pairs(k), pairs(v), pairs(a), pairs(b), s_ref[...])
    s_ref[...] = s_new
    y = jnp.concatenate([jnp.concatenate([y[s * npair + p] for p in range(npair)], axis=1)
                         for s in range(nseq)], axis=0)
    out = _rwkv_post(y, r, k, v, g, lnw_ref[...], lnb_ref[...], rk_ref[...])
    y_ref[...] = out.reshape(nseq, c, rw).astype(y_ref.dtype)

    @pl.when(ch == pl.num_programs(1) - 1)
    def _():
        for s in range(nseq):
            for p in range(npair):
                s_out_ref[s, 2 * p] = s_ref[s * npair + p, 0:HEAD_DIM, 0:HEAD_DIM]
                s_out_ref[s, 2 * p + 1] = s_ref[s * npair + p, HEAD_DIM:, HEAD_DIM:]


def _rwkv_mix_prompt(p, col0, lp, layer, n_seq, t, seq_per_step):
    m = p.shape[0]
    rw = lp["rwkv_w"]
    c = RWKV_CHUNK
    heads = rw // HEAD_DIM
    assert n_seq % seq_per_step == 0 and col0 % rw == 0 and (col0 + 3 * rw) % LORA_W == 0
    c_slab = col0 // rw
    c_lora = (col0 + 3 * rw) // LORA_W
    tok = lambda w, col: pl.BlockSpec((seq_per_step, c, w), lambda n, ch: (n, ch, col))
    vec = lambda w, col: pl.BlockSpec((None, 1, w), lambda n, ch: (layer, 0, col))
    mat = pl.BlockSpec((None, LORA_W, rw), lambda n, ch: (layer, 0, 0))
    p3 = p.reshape(n_seq, t, p.shape[1])
    y, wkv = pl.pallas_call(
        _rwkv_mix_body,
        grid=(n_seq // seq_per_step, t // c),
        in_specs=[tok(rw, c_slab), tok(rw, c_slab + 1), tok(rw, c_slab + 2), tok(LORA_W, c_lora),
                  vec(rw, 0), vec(rw, 1), vec(rw, 2), vec(LORA_W, (3 * rw) // LORA_W),
                  vec(rw, 0), mat, vec(rw, 0), mat, mat, vec(rw, 0), vec(rw, 0),
                  vec(rw, 0), vec(rw, 0), vec(rw, 0)],
        out_specs=[tok(rw, 0),
                   pl.BlockSpec((seq_per_step, heads, HEAD_DIM, HEAD_DIM), lambda n, ch: (n, 0, 0, 0))],
        out_shape=[jax.ShapeDtypeStruct((n_seq, t, rw), BF16),
                   jax.ShapeDtypeStruct((n_seq, heads, HEAD_DIM, HEAD_DIM), F32)],
        scratch_shapes=[pltpu.VMEM((seq_per_step * (rw // LANES), LANES, LANES), F32)]
                       + [pltpu.VMEM((seq_per_step, c + 8, rw), F32)] * 3
                       + [pltpu.VMEM((seq_per_step, c + 8, LORA_W), F32)],
        compiler_params=_cparams(("parallel", "arbitrary")),
        name="rwkv_mix",
    )(p3, p3, p3, p3, lp["mu"], lp["mu"], lp["mu"], lp["mu"],
      lp["w0"], lp["w2p"], lp["a0"], lp["a2p"], lp["g2p"], lp["k_k"], lp["k_a"],
      lp["ln_w"], lp["ln_b"], lp["r_k"])
    return y.reshape(m, rw), wkv


def _sample_body(slope_ref, p_ref, kc_ref, vc_ref,
                 wkv_ref, shift_ref, cst_ref, qg_ref, kg_ref,
                 mu_ref, w0_ref, w2_ref, a0_ref, a2_ref, g2_ref, kk_ref, ka_ref,
                 rk_ref, lnw_ref, lnb_ref, cw_ref, cb_ref,
                 oatt_ref, yr_ref, yc_ref, kn_ref, vn_ref, wkvn_ref, cstn_ref,
                 *, col_att, col_rwkv, col_conv):
    heads_a = oatt_ref.shape[0]
    aw = heads_a * HEAD_DIM
    rw = yr_ref.shape[1]
    cw = yc_ref.shape[1]
    rp = shift_ref.shape[1]
    rows8 = lambda x: jnp.broadcast_to(x, (8, x.shape[1]))
    p_all = p_ref[...]

    by_head = lambda row: jnp.concatenate(
        [row[:, h * HEAD_DIM:(h + 1) * HEAD_DIM] for h in range(heads_a)], axis=0)
    ca = col_att
    q, k, v_new = (by_head(p_all[:, ca:ca + aw]), by_head(p_all[:, ca + aw:ca + 2 * aw]),
                   by_head(p_all[:, ca + 2 * aw:ca + 3 * aw]))
    qn = _rms(q, qg_ref[...]) * (HEAD_DIM ** -0.5)
    kn = _rms(k, kg_ref[...])
    kn_ref[...] = kn
    vn_ref[...] = v_new
    s_new = jnp.sum(kn * qn, axis=-1, keepdims=True)
    slope = slope_ref[...]
    past = kc_ref.shape[2]
    er = lax.broadcasted_iota(jnp.int32, (HEAD_DIM, HEAD_DIM), 0)
    ec = lax.broadcasted_iota(jnp.int32, (HEAD_DIM, HEAD_DIM), 1)
    eye = er == ec
    q_col = jnp.stack([jnp.sum(jnp.where(eye, qn[h:h + 1, :], 0.0), axis=1, keepdims=True)
                       for h in range(heads_a)], axis=0)
    dist = past - lax.broadcasted_iota(jnp.int32, (1, past), 1)
    biased = jnp.sum(kc_ref[...] * q_col, axis=1) - slope * dist.astype(F32)
    parts = []
    for window, dil in DILATION_GROUPS:
        assert dil & (dil - 1) == 0
        valid = ((dist & (dil - 1)) == 0) & (dist <= window)
        s = jnp.where(valid, biased, NEG_BIG)
        m = jnp.maximum(jnp.max(s, axis=1, keepdims=True), s_new)
        p = jnp.exp(s - m)
        p_new = jnp.exp(s_new - m)
        l = jnp.sum(p, axis=1, keepdims=True) + p_new
        parts.append((p, p_new, l, m + jnp.log(l)))
    mx = jnp.maximum(jnp.maximum(parts[0][3], parts[1][3]), parts[2][3])
    ws = [jnp.exp(lse - mx) for _, _, _, lse in parts]
    wsum = ws[0] + ws[1] + ws[2]
    scale_g = [w / (l * wsum) for w, (_, _, l, _) in zip(ws, parts)]
    coef = parts[0][0] * scale_g[0] + parts[1][0] * scale_g[1] + parts[2][0] * scale_g[2]
    coef_new = parts[0][1] * scale_g[0] + parts[1][1] * scale_g[1] + parts[2][1] * scale_g[2]
    o_col = jnp.sum(vc_ref[...] * coef[:, None, :], axis=2, keepdims=True)
    o_rows = jnp.concatenate([jnp.sum(jnp.where(eye, o_col[h], 0.0), axis=0, keepdims=True)
                              for h in range(heads_a)], axis=0)
    oatt_ref[...] = o_rows + coef_new * v_new

    pr = rows8(p_all[:, col_rwkv:col_rwkv + rp])
    sh = rows8(shift_ref[...])
    mu = mu_ref[...]
    c3 = 3 * rw
    r, lw, k, v, a, b, g = _rwkv_token_math(
        pr[:, 0:rw], pr[:, rw:2 * rw], pr[:, 2 * rw:c3], pr[:, c3:],
        sh[:, 0:rw], sh[:, rw:2 * rw], sh[:, 2 * rw:c3], sh[:, c3:],
        mu[:, 0:rw], mu[:, rw:2 * rw], mu[:, 2 * rw:c3], mu[:, c3:],
        w0_ref[...], w2_ref[...], a0_ref[...], a2_ref[...], g2_ref[...], kk_ref[...], ka_ref[...])
    decay = jnp.exp(lw)
    place_r = lax.broadcasted_iota(jnp.int32, (HEAD_DIM, rw), 0)
    place_c = lax.broadcasted_iota(jnp.int32, (HEAD_DIM, rw), 1)
    y_full = jnp.zeros((1, rw), F32)
    for h in range(rw // HEAD_DIM):
        hs = slice(h * HEAD_DIM, (h + 1) * HEAD_DIM)
        s_old = wkv_ref[h]
        sa = jnp.sum(s_old * a[0:1, hs], axis=1, keepdims=True)
        v_col = jnp.sum(jnp.where(eye, v[0:1, hs], 0.0), axis=1, keepdims=True)
        s_h = s_old * decay[0:1, hs] + sa * b[0:1, hs] + v_col * k[0:1, hs]
        wkvn_ref[h] = s_h
        y_col = jnp.sum(s_h * r[0:1, hs], axis=1, keepdims=True)
        y_full = y_full + jnp.sum(jnp.where(place_c == place_r + h * HEAD_DIM, y_col, 0.0),
                                  axis=0, keepdims=True)
    y8 = rows8(y_full)
    yr_ref[...] = _rwkv_post(y8, r, k, v, g, lnw_ref[...], lnb_ref[...], rk_ref[...])[0:1]

    c1 = col_conv
    z = p_all[:, c1 + 2 * cw:c1 + 3 * cw] * p_all[:, c1:c1 + cw]
    yc = cb_ref[...] + cw_ref[0:1, :] * cst_ref[0:1, :]
    yc = yc + cw_ref[1:2, :] * cst_ref[1:2, :]
    yc = yc + cw_ref[2:3, :] * z
    yc_ref[...] = p_all[:, c1 + cw:c1 + 2 * cw] * yc
    cstn_ref[0:1, :] = cst_ref[1:2, :]
    cstn_ref[1:2, :] = z


def _sample_mixers(p, cols, cache_k, cache_v, state_wkv, state_shift, state_conv, lp, slope_col, layer):
    nb = p.shape[0]
    depth, _, past, heads_a, _ = cache_k.shape
    rw = lp["rwkv_w"]
    rp = state_shift.shape[2]
    cw = state_conv.shape[3]
    heads = rw // HEAD_DIM

    row = lambda w: pl.BlockSpec((None, 1, w), lambda n: (n, 0, 0))
    lvec = lambda w: pl.BlockSpec((None, 1, w), lambda n: (layer, 0, 0))
    lmat = lambda r_, w: pl.BlockSpec((None, r_, w), lambda n: (layer, 0, 0))
    tile = pl.BlockSpec((None, heads_a, HEAD_DIM), lambda n: (n, 0, 0))

    assert past >= DILATION_GROUPS[-1][0]
    views = [jnp.transpose(c, (0, 1, 3, 4, 2)) for c in (cache_k, cache_v)]
    cache_specs = [pl.BlockSpec((None, None, heads_a, HEAD_DIM, past), lambda n: (layer, n, 0, 0, 0))] * 2

    outs = pl.pallas_call(
        functools.partial(_sample_body, col_att=cols[0], col_rwkv=cols[1], col_conv=cols[2]),
        grid=(nb,),
        in_specs=[pl.BlockSpec((heads_a, 1), lambda n: (0, 0)), row(p.shape[1])] + cache_specs + [
                  pl.BlockSpec((None, None, heads, HEAD_DIM, HEAD_DIM), lambda n: (layer, n, 0, 0, 0)),
                  pl.BlockSpec((None, None, 1, rp), lambda n: (layer, n, 0, 0)),
                  pl.BlockSpec((None, None, CONV_K - 1, cw), lambda n: (layer, n, 0, 0)),
                  lvec(HEAD_DIM), lvec(HEAD_DIM),
                  lvec(rp), lvec(rw), lmat(LORA_W, rw), lvec(rw), lmat(LORA_W, rw),
                  lmat(LORA_W, rw), lvec(rw), lvec(rw), lvec(rw), lvec(rw), lvec(rw),
                  lmat(CONV_K, cw), lvec(cw)],
        out_specs=[tile, row(rw), row(cw), tile, tile,
                   pl.BlockSpec((None, heads, HEAD_DIM, HEAD_DIM), lambda n: (n, 0, 0, 0)),
                   pl.BlockSpec((None, CONV_K - 1, cw), lambda n: (n, 0, 0))],
        out_shape=[jax.ShapeDtypeStruct((nb, heads_a, HEAD_DIM), F32), jax.ShapeDtypeStruct((nb, 1, rw), F32),
                   jax.ShapeDtypeStruct((nb, 1, cw), F32),
                   jax.ShapeDtypeStruct((nb, heads_a, HEAD_DIM), F32),
                   jax.ShapeDtypeStruct((nb, heads_a, HEAD_DIM), F32),
                   jax.ShapeDtypeStruct((nb, heads, HEAD_DIM, HEAD_DIM), F32),
                   jax.ShapeDtypeStruct((nb, CONV_K - 1, cw), F32)],
        compiler_params=_cparams(("parallel",)),
        name="sample_mixers",
    )(slope_col, p[:, None, :], *views,
      state_wkv, state_shift[:, :, None, :], state_conv,
      lp["q_gain"], lp["k_gain"],
      lp["mu"], lp["w0"], lp["w2p"], lp["a0"], lp["a2p"], lp["g2p"], lp["k_k"], lp["k_a"],
      lp["r_k"], lp["ln_w"], lp["ln_b"], lp["conv_w"], lp["conv_b"])
    o_att, y_rwkv, y_conv, k_new, v_new, wkv_new, conv_new = outs
    return o_att.reshape(nb, heads_a * HEAD_DIM), y_rwkv[:, 0], y_conv[:, 0], k_new, v_new, wkv_new, conv_new


def kernel(x_prompt, x_sample, cache_swa_k, cache_swa_v, state_wkv, state_shift, state_conv, ffn1_norm, ffn1_w_gate, ffn1_w_up, ffn1_w_down, mix_norm, w_in, q_norm, k_norm, rwkv_mu, rwkv_w0, rwkv_w2, rwkv_a0, rwkv_a2, rwkv_g2, rwkv_k_k, rwkv_k_a, rwkv_r_k, rwkv_ln_w, rwkv_ln_b, conv_w, conv_b, w_out, ffn2_norm, ffn2_w_gate, ffn2_w_up, ffn2_w_down):
    n_seq, t, d = x_prompt.shape
    nb = x_sample.shape[0]
    assert x_sample.shape[1] == 1
    depth = w_in.shape[0]
    att_heads, head_dim = cache_swa_k.shape[3], cache_swa_k.shape[4]
    assert head_dim == HEAD_DIM
    aw = att_heads * HEAD_DIM
    rw = rwkv_w0.shape[1]
    cw = conv_b.shape[1]
    rp = rwkv_mu.shape[1]
    assert rp == 3 * rw + LORA_W and w_in.shape[2] == 3 * aw + rp + 3 * cw

    row3 = lambda p: p.reshape(depth, 1, -1)
    col_att, col_rwkv, col_conv = 0, 3 * aw, 3 * aw + rp
    zpad = lambda w, before: jnp.pad(w, ((0, 0), (before, LORA_W - before - w.shape[1]), (0, 0)))
    lp = {
        "rwkv_w": rw,
        "mu": row3(rwkv_mu), "w0": row3(rwkv_w0), "a0": row3(rwkv_a0),
        "w2p": zpad(rwkv_w2, 0), "a2p": zpad(rwkv_a2, W_LORA), "g2p": zpad(rwkv_g2, W_LORA + A_LORA),
        "k_k": row3(rwkv_k_k), "k_a": row3(rwkv_k_a), "r_k": row3(rwkv_r_k),
        "ln_w": row3(rwkv_ln_w), "ln_b": row3(rwkv_ln_b),
        "conv_w": conv_w, "conv_b": row3(conv_b),
        "q_gain": row3(q_norm), "k_gain": row3(k_norm),
        "q_gain_pair": row3(jnp.tile(q_norm, (1, LANES // HEAD_DIM))),
        "k_gain_pair": row3(jnp.tile(k_norm, (1, LANES // HEAD_DIM))),
    }
    g_ffn1, g_mix, g_ffn2 = row3(ffn1_norm), row3(mix_norm), row3(ffn2_norm)
    slopes = 2.0 ** (-8.0 * jnp.arange(1, att_heads + 1, dtype=F32) / att_heads)
    slope_col = slopes[:, None]

    m = n_seq * t
    xp = x_prompt.reshape(m, d)
    xs = x_sample.reshape(nb, d)
    tm_ffn, tf = 1024, 256
    tm_in, tn_in = 1024, 1280
    tm_out = 512
    p_states, s_states = [], []
    kv_stack = None
    for l in range(depth):
        xp, xs, w_in16, w_out16 = _ffn(xp, xs, g_ffn1, ffn1_w_gate, ffn1_w_up, ffn1_w_down, l, tm_ffn, tf,
                                       cast=(w_in, w_out))
        pp, ps = _norm_matmul(xp, xs, g_mix, w_in16, l, tm_in, tn_in)

        o_att, *kv_stack = _attention_prompt(pp, col_att, aw, lp["q_gain_pair"], lp["k_gain_pair"], slopes,
                                             l, depth, n_seq, t, kv_stack)
        y_rwkv, wkv_p = _rwkv_mix_prompt(pp, col_rwkv, lp, l, n_seq, t, 4 if n_seq % 4 == 0 else 1)
        xp, z_last = _out_proj_conv(xp, o_att, y_rwkv, pp, col_conv, cw, lp["conv_w"], lp["conv_b"], w_out16,
                                    l, t, tm_out)
        p_states.append((
            wkv_p, pp.reshape(n_seq, t, -1)[:, -1, col_rwkv:col_rwkv + rp],
            z_last.reshape(n_seq, t // tm_out, 8, cw)[:, -1, 8 - (CONV_K - 1):]))

        so, sy, syc, sk, sv, swkv, sconv = _sample_mixers(
            ps, (col_att, col_rwkv, col_conv), cache_swa_k, cache_swa_v, state_wkv, state_shift, state_conv,
            lp, slope_col, l)
        xs = _out_proj(xs, so, sy, syc, w_out16, l, nb)
        s_states.append((sk[:, None], sv[:, None], swkv, ps[:, col_rwkv:col_rwkv + rp], sconv))

        xp, xs = _ffn(xp, xs, g_ffn2, ffn2_w_gate, ffn2_w_up, ffn2_w_down, l, tm_ffn, tf)

    stacked = lambda states, i: jnp.stack([st[i] for st in states], axis=0)
    rows = lambda x_t: jnp.transpose(x_t.reshape(depth, n_seq, att_heads, HEAD_DIM, t), (0, 1, 4, 2, 3))
    return (xp.reshape(n_seq, t, d), xs.reshape(nb, 1, d),
            rows(kv_stack[0]), rows(kv_stack[1]),
            stacked(p_states, 0), stacked(p_states, 1), stacked(p_states, 2),
            stacked(s_states, 0), stacked(s_states, 1), stacked(s_states, 2), stacked(s_states, 3),
            stacked(s_states, 4))
```

```python
import functools

import jax
import jax.numpy as jnp
from jax import lax
from jax.experimental import pallas as pl
from jax.experimental.pallas import tpu as pltpu

F32 = jnp.float32
BF16 = jnp.bfloat16

HEAD_DIM = 64
HEAD_SHIFT = 6
LANES = 128
NORM_EPS = 1e-6
RWKV_GN_EPS = 64e-5
DILATION_GROUPS = ((128, 1), (512, 4), (2048, 16))
ATT_SPAN = 128
W_LORA, A_LORA, G_LORA = 64, 64, 128
LORA_W = W_LORA + A_LORA + G_LORA
CONV_K = 3
RWKV_CHUNK = 64
VMEM_LIMIT = 56 * 1024 * 1024
NEG_BIG = -1e30

_NT = (((1,), (1,)), ((), ()))
_NN = (((1,), (0,)), ((), ()))
_BNN = (((2,), (1,)), ((0,), (0,)))
_BNT = (((2,), (2,)), ((0,), (0,)))
_BTN = (((1,), (1,)), ((0,), (0,)))


def _cparams(sem):
    return pltpu.CompilerParams(dimension_semantics=sem, vmem_limit_bytes=VMEM_LIMIT)


def _rms(x, g):
    return x * lax.rsqrt(jnp.mean(x * x, axis=-1, keepdims=True) + NORM_EPS) * g


def _split2(x):
    hi = x.astype(BF16)
    lo = (x - hi.astype(F32)).astype(BF16)
    return hi, lo


def _dot3(a, b, dims=_NN):
    ah, al = _split2(a)
    bh, bl = _split2(b)
    d = lambda x, y: lax.dot_general(x, y, dims, preferred_element_type=F32)
    return d(ah, bh) + (d(al, bh) + d(ah, bl))


def _dot1(a, b, dims=_NN):
    return lax.dot_general(a.astype(BF16), b.astype(BF16), dims, preferred_element_type=F32)


def _split3(x):
    h1 = x.astype(BF16)
    r1 = x - h1.astype(F32)
    h2 = r1.astype(BF16)
    h3 = (r1 - h2.astype(F32)).astype(BF16)
    return h1, h2, h3


def _dot_sel(x, sel):
    return sum(jnp.dot(h, sel, preferred_element_type=F32) for h in _split3(x))


def _cumsum_rows(x, tri_incl):
    return sum(jnp.dot(tri_incl, h, preferred_element_type=F32) for h in _split2(x))


def _head_blockdiag(n, dtype):
    r = lax.broadcasted_iota(jnp.int32, (n, n), 0) >> HEAD_SHIFT
    c = lax.broadcasted_iota(jnp.int32, (n, n), 1) >> HEAD_SHIFT
    return (r == c).astype(dtype)


def _segsum64(x):
    bd = _head_blockdiag(LANES, BF16)
    parts = [_dot_sel(x[:, i:i + LANES], bd) for i in range(0, x.shape[1], LANES)]
    return parts[0] if len(parts) == 1 else jnp.concatenate(parts, axis=1)


BF16_ROWS = 16


def _ffn_body(x_ref, xs_ref, g_ref, wg_ref, wu_ref, wd_ref, *refs, n_cast):
    cast_in = refs[:n_cast]
    o_ref, os_ref = refs[n_cast:n_cast + 2]
    cast_out = refs[n_cast + 2:2 * n_cast + 2]
    xn_ref = refs[2 * n_cast + 2]
    j = pl.program_id(1)
    tm, d = x_ref.shape

    @pl.when(j == 0)
    def _():
        xn_ref[0:tm, :] = _rms(x_ref[...], g_ref[...]).astype(BF16)
        xn_ref[tm:, :] = _rms(xs_ref[...], g_ref[...]).astype(BF16)
        o_ref[...] = jnp.zeros_like(o_ref)
        os_ref[...] = jnp.zeros_like(os_ref)

    for src, dst in zip(cast_in, cast_out):
        dst[...] = src[...].astype(BF16)

    xn = xn_ref[...]
    h = jnp.dot(xn, wg_ref[...].astype(BF16), preferred_element_type=F32)
    u = jnp.dot(xn, wu_ref[...].astype(BF16), preferred_element_type=F32)
    act = (h * jax.nn.sigmoid(h) * u).astype(BF16)
    half = d // 2
    for c in range(2):
        cols = slice(c * half, (c + 1) * half)
        part = jnp.dot(act, wd_ref[:, cols].astype(BF16), preferred_element_type=F32)
        o_ref[:, cols] += part[0:tm]
        os_ref[:, cols] += part[tm:]

    @pl.when(j == pl.num_programs(1) - 1)
    def _():
        o_ref[...] = x_ref[...] + 0.5 * o_ref[...]
        os_ref[...] = xs_ref[...] + 0.5 * os_ref[...]


def _ffn(x, xs, gain, wg, wu, wd, layer, tm, tf, cast=()):
    m, d = x.shape
    ms = xs.shape[0]
    f = wg.shape[2]
    ni, nj = m // tm, f // tf
    cast_in, cast_out, cast_shapes = [], [], []
    for w in cast:
        rows = w.shape[1]
        cr = next(r for r in range(BF16_ROWS, rows + 1, BF16_ROWS) if rows % r == 0 and rows // r <= ni * nj)
        chunks = rows // cr
        chunk = lambda i, j, chunks=chunks: jnp.minimum(i * nj + j, chunks - 1)
        cast_in.append(pl.BlockSpec((None, cr, w.shape[2]), lambda i, j, chunk=chunk: (layer, chunk(i, j), 0)))
        cast_out.append(pl.BlockSpec((cr, w.shape[2]), lambda i, j, chunk=chunk: (chunk(i, j), 0)))
        cast_shapes.append(jax.ShapeDtypeStruct(w.shape[1:], BF16))
    return pl.pallas_call(
        functools.partial(_ffn_body, n_cast=len(cast)),
        grid=(ni, nj),
        in_specs=[
            pl.BlockSpec((tm, d), lambda i, j: (i, 0)),
            pl.BlockSpec((ms, d), lambda i, j: (0, 0)),
            pl.BlockSpec((None, 1, d), lambda i, j: (layer, 0, 0)),
            pl.BlockSpec((None, d, tf), lambda i, j: (layer, 0, j)),
            pl.BlockSpec((None, d, tf), lambda i, j: (layer, 0, j)),
            pl.BlockSpec((None, tf, d), lambda i, j: (layer, j, 0)),
        ] + cast_in,
        out_specs=[pl.BlockSpec((tm, d), lambda i, j: (i, 0)),
                   pl.BlockSpec((ms, d), lambda i, j: (0, 0))] + cast_out,
        out_shape=[jax.ShapeDtypeStruct((m, d), F32), jax.ShapeDtypeStruct((ms, d), F32)] + cast_shapes,
        scratch_shapes=[pltpu.VMEM((tm + ms, d), BF16)],
        compiler_params=_cparams(("arbitrary", "arbitrary")),
        name="ffn",
    )(x, xs, gain, wg, wu, wd, *cast)


def _norm_matmul_body(x_ref, xs_ref, g_ref, w_ref, o_ref, os_ref, xn_ref):
    tm = x_ref.shape[0]

    @pl.when(pl.program_id(1) == 0)
    def _():
        xn_ref[0:tm, :] = _rms(x_ref[...], g_ref[...]).astype(BF16)
        xn_ref[tm:, :] = _rms(xs_ref[...], g_ref[...]).astype(BF16)

    out = jnp.dot(xn_ref[...], w_ref[...], preferred_element_type=F32)
    o_ref[...] = out[0:tm]
    os_ref[...] = out[tm:]


def _norm_matmul(x, xs, gain, w, layer, tm, tn):
    m, d = x.shape
    ms = xs.shape[0]
    n = w.shape[1]
    return pl.pallas_call(
        _norm_matmul_body,
        grid=(m // tm, n // tn),
        in_specs=[
            pl.BlockSpec((tm, d), lambda i, j: (i, 0)),
            pl.BlockSpec((ms, d), lambda i, j: (0, 0)),
            pl.BlockSpec((None, 1, d), lambda i, j: (layer, 0, 0)),
            pl.BlockSpec((d, tn), lambda i, j: (0, j)),
        ],
        out_specs=[pl.BlockSpec((tm, tn), lambda i, j: (i, j)),
                   pl.BlockSpec((ms, tn), lambda i, j: (0, j))],
        out_shape=[jax.ShapeDtypeStruct((m, n), F32), jax.ShapeDtypeStruct((ms, n), F32)],
        scratch_shapes=[pltpu.VMEM((tm + ms, d), BF16)],
        compiler_params=_cparams(("arbitrary", "arbitrary")),
        name="norm_matmul",
    )(x, xs, gain, w)


def _mix_out(x, o_att, y_rwkv, y_conv, w_ref):
    wa, wr = o_att.shape[1], y_rwkv.shape[1]
    acc = jnp.dot(o_att.astype(BF16), w_ref[0:wa, :], preferred_element_type=F32)
    acc += jnp.dot(y_rwkv.astype(BF16), w_ref[wa:wa + wr, :], preferred_element_type=F32)
    acc += jnp.dot(y_conv.astype(BF16), w_ref[wa + wr:, :], preferred_element_type=F32)
    return x + acc


def _out_proj_body(x_ref, a_ref, r_ref, c_ref, w_ref, o_ref):
    o_ref[...] = _mix_out(x_ref[...], a_ref[...], r_ref[...], c_ref[...], w_ref)


def _out_proj_conv_body(x_ref, a_ref, r_ref, u_ref, gb_ref, gc_ref, pu_ref, pgc_ref, cw_ref, cb_ref, w_ref,
                        o_ref, zl_ref, zc_ref, *, blocks_per_seq):
    tm = u_ref.shape[0]
    first = (pl.program_id(0) % blocks_per_seq) == 0
    z = gc_ref[...] * u_ref[...]
    zc_ref[0:8, :] = jnp.where(first, 0.0, pgc_ref[...] * pu_ref[...])
    zc_ref[8:, :] = z
    yc = cb_ref[...] + cw_ref[0:1, :] * zc_ref[pl.ds(6, tm), :]
    yc = yc + cw_ref[1:2, :] * zc_ref[pl.ds(7, tm), :]
    yc = yc + cw_ref[2:3, :] * z
    zl_ref[...] = zc_ref[pl.ds(tm, 8), :]
    o_ref[...] = _mix_out(x_ref[...], a_ref[...], r_ref[...], gb_ref[...] * yc, w_ref)


def _out_proj_conv(x, o_att, y_rwkv, p, col0, cw, conv_w, conv_b, w, layer, t, tm):
    m, d = x.shape
    assert col0 % LANES == 0 and t % tm == 0
    cur = lambda s: pl.BlockSpec((pl.Element(tm), pl.Element(cw)),
                                 lambda i: (pl.multiple_of(i * tm, tm), col0 + s * cw))
    prev = lambda s: pl.BlockSpec((pl.Element(8), pl.Element(cw)),
                                  lambda i: (pl.multiple_of(jnp.maximum(i * tm - 8, 0), 8), col0 + s * cw))
    return pl.pallas_call(
        functools.partial(_out_proj_conv_body, blocks_per_seq=t // tm),
        grid=(m // tm,),
        in_specs=[
            pl.BlockSpec((tm, d), lambda i: (i, 0)),
            pl.BlockSpec((tm, o_att.shape[1]), lambda i: (i, 0)),
            pl.BlockSpec((tm, y_rwkv.shape[1]), lambda i: (i, 0)),
            cur(0), cur(1), cur(2), prev(0), prev(2),
            pl.BlockSpec((None, CONV_K, cw), lambda i: (layer, 0, 0)),
            pl.BlockSpec((None, 1, cw), lambda i: (layer, 0, 0)),
            pl.BlockSpec((w.shape[0], d), lambda i: (0, 0), pipeline_mode=pl.Buffered(1)),
        ],
        out_specs=[pl.BlockSpec((tm, d), lambda i: (i, 0)), pl.BlockSpec((8, cw), lambda i: (i, 0))],
        out_shape=[jax.ShapeDtypeStruct((m, d), F32), jax.ShapeDtypeStruct((m // tm * 8, cw), F32)],
        scratch_shapes=[pltpu.VMEM((tm + 8, cw), F32)],
        compiler_params=_cparams(("parallel",)),
        name="out_proj_conv",
    )(x, o_att, y_rwkv, p, p, p, p, p, conv_w, conv_b, w)


def _out_proj(x, o_att, y_rwkv, y_conv, w, layer, tm):
    m, d = x.shape
    return pl.pallas_call(
        _out_proj_body,
        grid=(m // tm,),
        in_specs=[
            pl.BlockSpec((tm, d), lambda i: (i, 0)),
            pl.BlockSpec((tm, o_att.shape[1]), lambda i: (i, 0)),
            pl.BlockSpec((tm, y_rwkv.shape[1]), lambda i: (i, 0)),
            pl.BlockSpec((tm, y_conv.shape[1]), lambda i: (i, 0)),
            pl.BlockSpec((w.shape[0], d), lambda i: (0, 0), pipeline_mode=pl.Buffered(1)),
        ],
        out_specs=pl.BlockSpec((tm, d), lambda i: (i, 0)),
        out_shape=jax.ShapeDtypeStruct((m, d), F32),
        compiler_params=_cparams(("parallel",)),
        name="out_proj",
    )(x, o_att, y_rwkv, y_conv, w)


def _pair_headnorm(x, g, lo_half):
    x2 = x * x
    s0 = jnp.sum(jnp.where(lo_half, x2, 0.0), axis=-1, keepdims=True)
    s1 = jnp.sum(jnp.where(lo_half, 0.0, x2), axis=-1, keepdims=True)
    ms = jnp.where(lo_half, s0, s1) * (1.0 / HEAD_DIM)
    return x * lax.rsqrt(ms + NORM_EPS) * g


def _attn_body(slopes_ref, q_ref, k_ref, v_ref, qg_ref, kg_ref, *refs, n_carried):
    o_ref, kt_ref, vt_ref, qs_ref, kn_ref, bias_ref, og_ref, lse_ref = refs[n_carried:]
    hp = pl.program_id(1)
    t = q_ref.shape[0]
    span = ATT_SPAN
    lo_half = lax.broadcasted_iota(jnp.int32, (1, LANES), 1) < HEAD_DIM

    kn = _pair_headnorm(k_ref[...], kg_ref[...], lo_half)
    kn_ref[...] = kn
    kt_ref[...] = kn.T
    vt_ref[...] = v_ref[...].T
    qs_ref[...] = _pair_headnorm(q_ref[...], qg_ref[...], lo_half) * (HEAD_DIM ** -0.5)

    qi = lax.broadcasted_iota(jnp.int32, (span, 2 * span), 0)
    ki = lax.broadcasted_iota(jnp.int32, (span, 2 * span), 1)
    steps = qi + span - ki
    valid = (steps >= 0) & (steps <= span)
    for g, (_, dil) in enumerate(DILATION_GROUPS):
        dist = (steps * dil).astype(F32)
        for h in range(2):
            bias_ref[g, h * span:(h + 1) * span, :] = jnp.where(valid, -(slopes_ref[2 * hp + h] * dist), NEG_BIG)

    def unit(g, dil, start, has_prev):
        rows = pl.ds(start, span, stride=dil)
        qb = qs_ref[rows, :]
        q2 = jnp.concatenate([jnp.where(lo_half, qb, 0.0), jnp.where(lo_half, 0.0, qb)], axis=0).astype(BF16)
        if has_prev:
            prow = pl.ds(start - span * dil, span, stride=dil)
            kb = jnp.concatenate([kn_ref[prow, :], kn_ref[rows, :]], axis=0)
            vb = jnp.concatenate([v_ref[prow, :], v_ref[rows, :]], axis=0)
            bias = bias_ref[g]
        else:
            kb = kn_ref[rows, :]
            vb = v_ref[rows, :]
            bias = bias_ref[g, :, span:]
        s = lax.dot_general(q2, kb.astype(BF16), _NT, preferred_element_type=F32) + bias
        m = jnp.max(s, axis=-1, keepdims=True)
        p = jnp.exp(s - m)
        if has_prev:
            v_ones = jnp.concatenate([vb.astype(BF16), jnp.ones((vb.shape[0], LANES), BF16)], axis=1)
            pv = jnp.dot(p.astype(BF16), v_ones, preferred_element_type=F32)
            l = pv[:, LANES:]
            o = pv[:, 0:LANES] / l
        else:
            l = jnp.sum(p, axis=-1, keepdims=True)
            o = jnp.dot(p.astype(BF16), vb.astype(BF16), preferred_element_type=F32) / l
        lse = m + jnp.log(l)
        og_ref[g, rows, :] = jnp.where(lo_half, o[0:span], o[span:])
        lse_ref[g, rows, :] = jnp.where(lo_half, lse[0:span], lse[span:])

    for g, (window, dil) in enumerate(DILATION_GROUPS):
        assert window // dil == span
        nb = t // (span * dil)

        def residue(r, carry, g=g, dil=dil, nb=nb):
            unit(g, dil, r, False)
            if nb > 1:
                def blk(b, c):
                    unit(g, dil, b * (span * dil) + r, True)
                    return c
                lax.fori_loop(1, nb, blk, 0, unroll=3 if nb <= 4 else 8)
            return carry

        if dil == 1:
            residue(0, 0)
        else:
            lax.fori_loop(0, dil, residue, 0, unroll=8 if nb == 1 else 2)

    l0, l1, l2 = lse_ref[0], lse_ref[1], lse_ref[2]
    mx = jnp.maximum(jnp.maximum(l0, l1), l2)
    w0, w1, w2 = jnp.exp(l0 - mx), jnp.exp(l1 - mx), jnp.exp(l2 - mx)
    o_ref[...] = ((og_ref[0] * w0 + og_ref[1] * w1 + og_ref[2] * w2) / (w0 + w1 + w2)).astype(o_ref.dtype)


def _attention_prompt(p, col0, att_w, qg, kg, slopes, layer, depth, n_seq, t, kv_stack):
    m = p.shape[0]
    pairs = att_w // LANES
    assert col0 % LANES == 0
    blk = lambda off: pl.BlockSpec((t, LANES), lambda n, hp: (n, off + hp))
    c0 = col0 // LANES
    gain = pl.BlockSpec((None, 1, LANES), lambda n, hp: (layer, 0, 0))
    tr_spec = pl.BlockSpec((None, None, LANES, t), lambda n, hp: (layer, n, hp, 0))
    tr_shape = jax.ShapeDtypeStruct((depth, n_seq, att_w, t), F32)
    carried = () if kv_stack is None else tuple(kv_stack)
    n_in = 6
    return pl.pallas_call(
        functools.partial(_attn_body, n_carried=len(carried)),
        grid=(n_seq, pairs),
        in_specs=[pl.BlockSpec(memory_space=pltpu.SMEM),
                  blk(c0), blk(c0 + pairs), blk(c0 + 2 * pairs), gain, gain]
                 + [pl.BlockSpec(memory_space=pl.ANY)] * len(carried),
        out_specs=[blk(0), tr_spec, tr_spec],
        out_shape=[jax.ShapeDtypeStruct((m, att_w), BF16), tr_shape, tr_shape],
        input_output_aliases={n_in + i: 1 + i for i in range(len(carried))},
        scratch_shapes=[pltpu.VMEM((t, LANES), F32),
                        pltpu.VMEM((t, LANES), F32),
                        pltpu.VMEM((3, 2 * ATT_SPAN, 2 * ATT_SPAN), F32),
                        pltpu.VMEM((3, t, LANES), F32),
                        pltpu.VMEM((3, t, LANES), F32)],
        compiler_params=_cparams(("parallel", "arbitrary")),
        name="attn_prompt",
    )(slopes, p, p, p, qg, kg, *carried)


def _softplus(z):
    return jnp.maximum(z, 0.0) + jnp.log(1.0 + jnp.exp(-jnp.abs(z)))


def _rwkv_token_math(p_r, p_k, p_v, p_l, q_r, q_k, q_v, q_l, mu_r, mu_k, mu_v, mu_l,
                     w0, w2p, a0, a2p, g2p, k_k, k_a, gate_dot=_dot3):
    xr = p_r + (q_r - p_r) * mu_r
    xk = p_k + (q_k - p_k) * mu_k
    xv = p_v + (q_v - p_v) * mu_v
    xl = p_l + (q_l - p_l) * mu_l
    w_log = -_softplus(-(w0 + _dot3(jnp.tanh(xl), w2p))) - 0.5
    log_decay = -jnp.exp(w_log)
    sig_l = jax.nn.sigmoid(xl)
    a_gate = jax.nn.sigmoid(a0 + gate_dot(xl, a2p))
    g = gate_dot(sig_l, g2p)
    kk = xk * k_k
    kk = kk * lax.rsqrt(jnp.maximum(_segsum64_many([kk * kk])[0], 1e-24))
    kmod = xk * (1.0 + (a_gate - 1.0) * k_a)
    return xr, log_decay, kmod, xv, -kk, kk * a_gate, g


def _segsum64_many(xs):
    rows, width = xs[0].shape
    if rows % 16 != 0:
        return [_segsum64(x) for x in xs]
    nsl = width // LANES
    pieces = [part[:, i * LANES:(i + 1) * LANES]
              for x in xs for part in _split2(x) for i in range(nsl)]
    res = jnp.dot(jnp.concatenate(pieces, axis=0), _head_blockdiag(LANES, BF16), preferred_element_type=F32)
    outs = []
    for n in range(len(xs)):
        hi, lo = [jnp.concatenate([res[((2 * n + j) * nsl + i) * rows:((2 * n + j) * nsl + i + 1) * rows]
                                   for i in range(nsl)], axis=1) for j in range(2)]
        outs.append(hi + lo)
    return outs


def _rwkv_post(y, r, k, v, g, ln_w, ln_b, r_k):
    sum_y, sum_rk = _segsum64_many([y, r * k * r_k])
    yc = y - sum_y * (1.0 / HEAD_DIM)
    var_y = _segsum64_many([yc * yc])[0] * (1.0 / HEAD_DIM)
    yn = yc * lax.rsqrt(var_y + RWKV_GN_EPS) * ln_w + ln_b
    return (yn + sum_rk * v) * g


def _scan_chunk(r, lw, cum, k, v, a, b, s0):
    c = r.shape[1]
    c2 = 2 * c
    lane = lax.broadcasted_iota(jnp.int32, (1, 1, LANES), 2)
    head0 = lane < HEAD_DIM
    m0 = head0.astype(F32)
    m1 = 1.0 - m0
    stack = lambda x: jnp.concatenate([x * m0, x * m1], axis=1)
    twice = lambda x: jnp.concatenate([x, x], axis=1)
    pick = lambda x_st: jnp.where(head0, x_st[:, 0:c], x_st[:, c:])

    cum_prev = cum - lw
    cum_last = cum[:, c - 1:c, :]
    e_pos = jnp.exp(cum)
    e_neg = jnp.exp(-cum)
    e_rem = jnp.exp(cum_last - cum)
    at = a * jnp.exp(cum_prev)
    rt = r * e_pos
    bt = b * e_neg
    kt = k * e_neg
    bh = b * e_rem
    kh = k * e_rem

    big = _dot1(jnp.concatenate([stack(at), stack(rt)], axis=1),
                jnp.concatenate([twice(bt), twice(kt)], axis=1), _BNT)
    row = lax.broadcasted_iota(jnp.int32, (1, c2, c2), 1)
    col = lax.broadcasted_iota(jnp.int32, (1, c2, c2), 2)
    assert c == HEAD_DIM
    same_head = (row >> HEAD_SHIFT) == (col >> HEAD_SHIFT)
    tr, tc = row & (c - 1), col & (c - 1)
    strict = same_head & (tr > tc)
    incl = same_head & (tr >= tc)
    lmat = jnp.where(strict, big[:, 0:c2, 0:c2], 0.0)
    ak = jnp.where(strict, big[:, 0:c2, c2:], 0.0)
    rb = jnp.where(incl, big[:, c2:, 0:c2], 0.0)
    rk = jnp.where(incl, big[:, c2:, c2:], 0.0)

    eye = (row == col).astype(F32)
    x = eye + jnp.where((row >> 1) == (col >> 1), lmat, 0.0)
    s, log2s = 2, 1
    while s < c:
        joins = ((row >> (log2s + 1)) == (col >> (log2s + 1))) & ((row & s) != 0) & ((col & s) == 0)
        x = x + _dot1(_dot1(x, jnp.where(joins, lmat, 0.0), _BNN), x, _BNN)
        s, log2s = 2 * s, log2s + 1

    v_st = twice(v)
    w = _dot1(jnp.concatenate([at, rt], axis=1), s0, _BNT)
    u_st = _dot1(x, twice(w[:, 0:c]) + _dot1(ak, v_st, _BNN), _BNN)
    y_st = _dot1(jnp.concatenate([rb, rk], axis=2), jnp.concatenate([u_st, v_st], axis=1), _BNN)
    y = w[:, c:] + pick(y_st)
    u = pick(u_st)
    upd = _dot1(jnp.concatenate([u, v], axis=1), jnp.concatenate([bh, kh], axis=1), _BTN)
    s_new = s0 * jnp.exp(cum_last) + _head_blockdiag(LANES, F32)[None] * upd
    return y, s_new


def _rwkv_mix_body(pr_ref, pk_ref, pv_ref, plo_ref,
                   mur_ref, muk_ref, muv_ref, mul_ref, w0_ref, w2_ref, a0_ref, a2_ref, g2_ref,
                   kk_ref, ka_ref, lnw_ref, lnb_ref, rk_ref,
                   y_ref, s_out_ref, s_ref, sr_ref, sk_ref, sv_ref, sl_ref):
    ch = pl.program_id(1)
    nseq, c, rw = pr_ref.shape
    npair = rw // LANES
    shift_refs = (sr_ref, sk_ref, sv_ref, sl_ref)

    @pl.when(ch == 0)
    def _():
        s_ref[...] = jnp.zeros_like(s_ref)
        for ref in shift_refs:
            ref[:, 0:8, :] = jnp.zeros((nseq, 8, ref.shape[2]), F32)

    def cur_and_prev(cur_ref, sh_ref):
        cur = cur_ref[...]
        sh_ref[:, 8:, :] = cur
        prev = sh_ref[:, pl.ds(7, c), :]
        sh_ref[:, 7:8, :] = cur[:, c - 1:c, :]
        flat = lambda x: x.reshape(nseq * c, x.shape[2])
        return flat(cur), flat(prev)

    (p_r, q_r), (p_k, q_k), (p_v, q_v), (p_l, q_l) = [
        cur_and_prev(cur, sh) for cur, sh in zip((pr_ref, pk_ref, pv_ref, plo_ref), shift_refs)]
    r, lw, k, v, a, b, g = _rwkv_token_math(
        p_r, p_k, p_v, p_l, q_r, q_k, q_v, q_l,
        mur_ref[...], muk_ref[...], muv_ref[...], mul_ref[...],
        w0_ref[...], w2_ref[...], a0_ref[...], a2_ref[...], g2_ref[...], kk_ref[...], ka_ref[...],
        gate_dot=_dot1)

    pairs = lambda x: jnp.stack([x[s * c:(s + 1) * c, p * LANES:(p + 1) * LANES]
                                 for s in range(nseq) for p in range(npair)], axis=0)
    tri_incl = (lax.broadcasted_iota(jnp.int32, (c, c), 0)
                >= lax.broadcasted_iota(jnp.int32, (c, c), 1)).astype(BF16)
    cum = jnp.concatenate([_cumsum_rows(lw[s * c:(s + 1) * c], tri_incl) for s in range(nseq)], axis=0)
    y, s_new = _scan_chunk(pairs(r), pairs(lw), pairs(cum), pairs(k), pairs(v), pairs(a), pairs(b), s_ref[...])
    s_ref[...] = s_new
    y = jnp.concatenate([jnp.concatenate([y[s * npair + p] for p in range(npair)], axis=1)
                         for s in range(nseq)], axis=0)
    out = _rwkv_post(y, r, k, v, g, lnw_ref[...], lnb_ref[...], rk_ref[...])
    y_ref[...] = out.reshape(nseq, c, rw).astype(y_ref.dtype)

    @pl.when(ch == pl.num_programs(1) - 1)
    def _():
        for s in range(nseq):
            for p in range(npair):
                s_out_ref[s, 2 * p] = s_ref[s * npair + p, 0:HEAD_DIM, 0:HEAD_DIM]
                s_out_ref[s, 2 * p + 1] = s_ref[s * npair + p, HEAD_DIM:, HEAD_DIM:]


def _rwkv_mix_prompt(p, col0, lp, layer, n_seq, t, seq_per_step):
    m = p.shape[0]
    rw = lp["rwkv_w"]
    c = RWKV_CHUNK
    heads = rw // HEAD_DIM
    assert n_seq % seq_per_step == 0 and col0 % rw == 0 and (col0 + 3 * rw) % LORA_W == 0
    c_slab = col0 // rw
    c_lora = (col0 + 3 * rw) // LORA_W
    tok = lambda w, col: pl.BlockSpec((seq_per_step, c, w), lambda n, ch: (n, ch, col))
    vec = lambda w, col: pl.BlockSpec((None, 1, w), lambda n, ch: (layer, 0, col))
    mat = pl.BlockSpec((None, LORA_W, rw), lambda n, ch: (layer, 0, 0))
    p3 = p.reshape(n_seq, t, p.shape[1])
    y, wkv = pl.pallas_call(
        _rwkv_mix_body,
        grid=(n_seq // seq_per_step, t // c),
        in_specs=[tok(rw, c_slab), tok(rw, c_slab + 1), tok(rw, c_slab + 2), tok(LORA_W, c_lora),
                  vec(rw, 0), vec(rw, 1), vec(rw, 2), vec(LORA_W, (3 * rw) // LORA_W),
                  vec(rw, 0), mat, vec(rw, 0), mat, mat, vec(rw, 0), vec(rw, 0),
                  vec(rw, 0), vec(rw, 0), vec(rw, 0)],
        out_specs=[tok(rw, 0),
                   pl.BlockSpec((seq_per_step, heads, HEAD_DIM, HEAD_DIM), lambda n, ch: (n, 0, 0, 0))],
        out_shape=[jax.ShapeDtypeStruct((n_seq, t, rw), BF16),
                   jax.ShapeDtypeStruct((n_seq, heads, HEAD_DIM, HEAD_DIM), F32)],
        scratch_shapes=[pltpu.VMEM((seq_per_step * (rw // LANES), LANES, LANES), F32)]
                       + [pltpu.VMEM((seq_per_step, c + 8, rw), F32)] * 3
                       + [pltpu.VMEM((seq_per_step, c + 8, LORA_W), F32)],
        compiler_params=_cparams(("parallel", "arbitrary")),
        name="rwkv_mix",
    )(p3, p3, p3, p3, lp["mu"], lp["mu"], lp["mu"], lp["mu"],
      lp["w0"], lp["w2p"], lp["a0"], lp["a2p"], lp["g2p"], lp["k_k"], lp["k_a"],
      lp["ln_w"], lp["ln_b"], lp["r_k"])
    return y.reshape(m, rw), wkv


def _sample_body(slope_ref, p_ref, kc_ref, vc_ref,
                 wkv_ref, shift_ref, cst_ref, qg_ref, kg_ref,
                 mu_ref, w0_ref, w2_ref, a0_ref, a2_ref, g2_ref, kk_ref, ka_ref,
                 rk_ref, lnw_ref, lnb_ref, cw_ref, cb_ref,
                 oatt_ref, yr_ref, yc_ref, kn_ref, vn_ref, wkvn_ref, cstn_ref,
                 *, col_att, col_rwkv, col_conv):
    heads_a = oatt_ref.shape[0]
    aw = heads_a * HEAD_DIM
    rw = yr_ref.shape[1]
    cw = yc_ref.shape[1]
    rp = shift_ref.shape[1]
    rows8 = lambda x: jnp.broadcast_to(x, (8, x.shape[1]))
    p_all = p_ref[...]

    by_head = lambda row: jnp.concatenate(
        [row[:, h * HEAD_DIM:(h + 1) * HEAD_DIM] for h in range(heads_a)], axis=0)
    ca = col_att
    q, k, v_new = (by_head(p_all[:, ca:ca + aw]), by_head(p_all[:, ca + aw:ca + 2 * aw]),
                   by_head(p_all[:, ca + 2 * aw:ca + 3 * aw]))
    qn = _rms(q, qg_ref[...]) * (HEAD_DIM ** -0.5)
    kn = _rms(k, kg_ref[...])
    kn_ref[...] = kn
    vn_ref[...] = v_new
    s_new = jnp.sum(kn * qn, axis=-1, keepdims=True)
    slope = slope_ref[...]
    past = kc_ref.shape[2]
    er = lax.broadcasted_iota(jnp.int32, (HEAD_DIM, HEAD_DIM), 0)
    ec = lax.broadcasted_iota(jnp.int32, (HEAD_DIM, HEAD_DIM), 1)
    eye = er == ec
    q_col = jnp.stack([jnp.sum(jnp.where(eye, qn[h:h + 1, :], 0.0), axis=1, keepdims=True)
                       for h in range(heads_a)], axis=0)
    dist = past - lax.broadcasted_iota(jnp.int32, (1, past), 1)
    biased = jnp.sum(kc_ref[...] * q_col, axis=1) - slope * dist.astype(F32)
    parts = []
    for window, dil in DILATION_GROUPS:
        assert dil & (dil - 1) == 0
        valid = ((dist & (dil - 1)) == 0) & (dist <= window)
        s = jnp.where(valid, biased, NEG_BIG)
        m = jnp.maximum(jnp.max(s, axis=1, keepdims=True), s_new)
        p = jnp.exp(s - m)
        p_new = jnp.exp(s_new - m)
        l = jnp.sum(p, axis=1, keepdims=True) + p_new
        parts.append((p, p_new, l, m + jnp.log(l)))
    mx = jnp.maximum(jnp.maximum(parts[0][3], parts[1][3]), parts[2][3])
    ws = [jnp.exp(lse - mx) for _, _, _, lse in parts]
    wsum = ws[0] + ws[1] + ws[2]
    scale_g = [w / (l * wsum) for w, (_, _, l, _) in zip(ws, parts)]
    coef = parts[0][0] * scale_g[0] + parts[1][0] * scale_g[1] + parts[2][0] * scale_g[2]
    coef_new = parts[0][1] * scale_g[0] + parts[1][1] * scale_g[1] + parts[2][1] * scale_g[2]
    o_col = jnp.sum(vc_ref[...] * coef[:, None, :], axis=2, keepdims=True)
    o_rows = jnp.concatenate([jnp.sum(jnp.where(eye, o_col[h], 0.0), axis=0, keepdims=True)
                              for h in range(heads_a)], axis=0)
    oatt_ref[...] = o_rows + coef_new * v_new

    pr = rows8(p_all[:, col_rwkv:col_rwkv + rp])
    sh = rows8(shift_ref[...])
    mu = mu_ref[...]
    c3 = 3 * rw
    r, lw, k, v, a, b, g = _rwkv_token_math(
        pr[:, 0:rw], pr[:, rw:2 * rw], pr[:, 2 * rw:c3], pr[:, c3:],
        sh[:, 0:rw], sh[:, rw:2 * rw], sh[:, 2 * rw:c3], sh[:, c3:],
        mu[:, 0:rw], mu[:, rw:2 * rw], mu[:, 2 * rw:c3], mu[:, c3:],
        w0_ref[...], w2_ref[...], a0_ref[...], a2_ref[...], g2_ref[...], kk_ref[...], ka_ref[...])
    decay = jnp.exp(lw)
    place_r = lax.broadcasted_iota(jnp.int32, (HEAD_DIM, rw), 0)
    place_c = lax.broadcasted_iota(jnp.int32, (HEAD_DIM, rw), 1)
    y_full = jnp.zeros((1, rw), F32)
    for h in range(rw // HEAD_DIM):
        hs = slice(h * HEAD_DIM, (h + 1) * HEAD_DIM)
        s_old = wkv_ref[h]
        sa = jnp.sum(s_old * a[0:1, hs], axis=1, keepdims=True)
        v_col = jnp.sum(jnp.where(eye, v[0:1, hs], 0.0), axis=1, keepdims=True)
        s_h = s_old * decay[0:1, hs] + sa * b[0:1, hs] + v_col * k[0:1, hs]
        wkvn_ref[h] = s_h
        y_col = jnp.sum(s_h * r[0:1, hs], axis=1, keepdims=True)
        y_full = y_full + jnp.sum(jnp.where(place_c == place_r + h * HEAD_DIM, y_col, 0.0),
                                  axis=0, keepdims=True)
    y8 = rows8(y_full)
    yr_ref[...] = _rwkv_post(y8, r, k, v, g, lnw_ref[...], lnb_ref[...], rk_ref[...])[0:1]

    c1 = col_conv
    z = p_all[:, c1 + 2 * cw:c1 + 3 * cw] * p_all[:, c1:c1 + cw]
    yc = cb_ref[...] + cw_ref[0:1, :] * cst_ref[0:1, :]
    yc = yc + cw_ref[1:2, :] * cst_ref[1:2, :]
    yc = yc + cw_ref[2:3, :] * z
    yc_ref[...] = p_all[:, c1 + cw:c1 + 2 * cw] * yc
    cstn_ref[0:1, :] = cst_ref[1:2, :]
    cstn_ref[1:2, :] = z


def _sample_mixers(p, cols, cache_k, cache_v, state_wkv, state_shift, state_conv, lp, slope_col, layer):
    nb = p.shape[0]
    depth, _, past, heads_a, _ = cache_k.shape
    rw = lp["rwkv_w"]
    rp = state_shift.shape[2]
    cw = state_conv.shape[3]
    heads = rw // HEAD_DIM

    row = lambda w: pl.BlockSpec((None, 1, w), lambda n: (n, 0, 0))
    lvec = lambda w: pl.BlockSpec((None, 1, w), lambda n: (layer, 0, 0))
    lmat = lambda r_, w: pl.BlockSpec((None, r_, w), lambda n: (layer, 0, 0))
    tile = pl.BlockSpec((None, heads_a, HEAD_DIM), lambda n: (n, 0, 0))

    assert past >= DILATION_GROUPS[-1][0]
    views = [jnp.transpose(c, (0, 1, 3, 4, 2)) for c in (cache_k, cache_v)]
    cache_specs = [pl.BlockSpec((None, None, heads_a, HEAD_DIM, past), lambda n: (layer, n, 0, 0, 0))] * 2

    outs = pl.pallas_call(
        functools.partial(_sample_body, col_att=cols[0], col_rwkv=cols[1], col_conv=cols[2]),
        grid=(nb,),
        in_specs=[pl.BlockSpec((heads_a, 1), lambda n: (0, 0)), row(p.shape[1])] + cache_specs + [
                  pl.BlockSpec((None, None, heads, HEAD_DIM, HEAD_DIM), lambda n: (layer, n, 0, 0, 0)),
                  pl.BlockSpec((None, None, 1, rp), lambda n: (layer, n, 0, 0)),
                  pl.BlockSpec((None, None, CONV_K - 1, cw), lambda n: (layer, n, 0, 0)),
                  lvec(HEAD_DIM), lvec(HEAD_DIM),
                  lvec(rp), lvec(rw), lmat(LORA_W, rw), lvec(rw), lmat(LORA_W, rw),
                  lmat(LORA_W, rw), lvec(rw), lvec(rw), lvec(rw), lvec(rw), lvec(rw),
                  lmat(CONV_K, cw), lvec(cw)],
        out_specs=[tile, row(rw), row(cw), tile, tile,
                   pl.BlockSpec((None, heads, HEAD_DIM, HEAD_DIM), lambda n: (n, 0, 0, 0)),
                   pl.BlockSpec((None, CONV_K - 1, cw), lambda n: (n, 0, 0))],
        out_shape=[jax.ShapeDtypeStruct((nb, heads_a, HEAD_DIM), F32), jax.ShapeDtypeStruct((nb, 1, rw), F32),
                   jax.ShapeDtypeStruct((nb, 1, cw), F32),
                   jax.ShapeDtypeStruct((nb, heads_a, HEAD_DIM), F32),
                   jax.ShapeDtypeStruct((nb, heads_a, HEAD_DIM), F32),
                   jax.ShapeDtypeStruct((nb, heads, HEAD_DIM, HEAD_DIM), F32),
                   jax.ShapeDtypeStruct((nb, CONV_K - 1, cw), F32)],
        compiler_params=_cparams(("parallel",)),
        name="sample_mixers",
    )(slope_col, p[:, None, :], *views,
      state_wkv, state_shift[:, :, None, :], state_conv,
      lp["q_gain"], lp["k_gain"],
      lp["mu"], lp["w0"], lp["w2p"], lp["a0"], lp["a2p"], lp["g2p"], lp["k_k"], lp["k_a"],
      lp["r_k"], lp["ln_w"], lp["ln_b"], lp["conv_w"], lp["conv_b"])
    o_att, y_rwkv, y_conv, k_new, v_new, wkv_new, conv_new = outs
    return o_att.reshape(nb, heads_a * HEAD_DIM), y_rwkv[:, 0], y_conv[:, 0], k_new, v_new, wkv_new, conv_new


def kernel(x_prompt, x_sample, cache_swa_k, cache_swa_v, state_wkv, state_shift, state_conv, ffn1_norm, ffn1_w_gate, ffn1_w_up, ffn1_w_down, mix_norm, w_in, q_norm, k_norm, rwkv_mu, rwkv_w0, rwkv_w2, rwkv_a0, rwkv_a2, rwkv_g2, rwkv_k_k, rwkv_k_a, rwkv_r_k, rwkv_ln_w, rwkv_ln_b, conv_w, conv_b, w_out, ffn2_norm, ffn2_w_gate, ffn2_w_up, ffn2_w_down):
    n_seq, t, d = x_prompt.shape
    nb = x_sample.shape[0]
    assert x_sample.shape[1] == 1
    depth = w_in.shape[0]
    att_heads, head_dim = cache_swa_k.shape[3], cache_swa_k.shape[4]
    assert head_dim == HEAD_DIM
    aw = att_heads * HEAD_DIM
    rw = rwkv_w0.shape[1]
    cw = conv_b.shape[1]
    rp = rwkv_mu.shape[1]
    assert rp == 3 * rw + LORA_W and w_in.shape[2] == 3 * aw + rp + 3 * cw

    row3 = lambda p: p.reshape(depth, 1, -1)
    col_att, col_rwkv, col_conv = 0, 3 * aw, 3 * aw + rp
    zpad = lambda w, before: jnp.pad(w, ((0, 0), (before, LORA_W - before - w.shape[1]), (0, 0)))
    lp = {
        "rwkv_w": rw,
        "mu": row3(rwkv_mu), "w0": row3(rwkv_w0), "a0": row3(rwkv_a0),
        "w2p": zpad(rwkv_w2, 0), "a2p": zpad(rwkv_a2, W_LORA), "g2p": zpad(rwkv_g2, W_LORA + A_LORA),
        "k_k": row3(rwkv_k_k), "k_a": row3(rwkv_k_a), "r_k": row3(rwkv_r_k),
        "ln_w": row3(rwkv_ln_w), "ln_b": row3(rwkv_ln_b),
        "conv_w": conv_w, "conv_b": row3(conv_b),
        "q_gain": row3(q_norm), "k_gain": row3(k_norm),
        "q_gain_pair": row3(jnp.tile(q_norm, (1, LANES // HEAD_DIM))),
        "k_gain_pair": row3(jnp.tile(k_norm, (1, LANES // HEAD_DIM))),
    }
    g_ffn1, g_mix, g_ffn2 = row3(ffn1_norm), row3(mix_norm), row3(ffn2_norm)
    slopes = 2.0 ** (-8.0 * jnp.arange(1, att_heads + 1, dtype=F32) / att_heads)
    slope_col = slopes[:, None]

    m = n_seq * t
    xp = x_prompt.reshape(m, d)
    xs = x_sample.reshape(nb, d)
    tm_ffn, tf = 1024, 256
    tm_in, tn_in = 1024, 1280
    tm_out = 512
    p_states, s_states = [], []
    kv_stack = None
    for l in range(depth):
        xp, xs, w_in16, w_out16 = _ffn(xp, xs, g_ffn1, ffn1_w_gate, ffn1_w_up, ffn1_w_down, l, tm_ffn, tf,
                                       cast=(w_in, w_out))
        pp, ps = _norm_matmul(xp, xs, g_mix, w_in16, l, tm_in, tn_in)

        o_att, *kv_stack = _attention_prompt(pp, col_att, aw, lp["q_gain_pair"], lp["k_gain_pair"], slopes,
                                             l, depth, n_seq, t, kv_stack)
        y_rwkv, wkv_p = _rwkv_mix_prompt(pp, col_rwkv, lp, l, n_seq, t, 4 if n_seq % 4 == 0 else 1)
        xp, z_last = _out_proj_conv(xp, o_att, y_rwkv, pp, col_conv, cw, lp["conv_w"], lp["conv_b"], w_out16,
                                    l, t, tm_out)
        p_states.append((
            wkv_p, pp.reshape(n_seq, t, -1)[:, -1, col_rwkv:col_rwkv + rp],
            z_last.reshape(n_seq, t // tm_out, 8, cw)[:, -1, 8 - (CONV_K - 1):]))

        so, sy, syc, sk, sv, swkv, sconv = _sample_mixers(
            ps, (col_att, col_rwkv, col_conv), cache_swa_k, cache_swa_v, state_wkv, state_shift, state_conv,
            lp, slope_col, l)
        xs = _out_proj(xs, so, sy, syc, w_out16, l, nb)
        s_states.append((sk[:, None], sv[:, None], swkv, ps[:, col_rwkv:col_rwkv + rp], sconv))

        xp, xs = _ffn(xp, xs, g_ffn2, ffn2_w_gate, ffn2_w_up, ffn2_w_down, l, tm_ffn, tf)

    stacked = lambda states, i: jnp.stack([st[i] for st in states], axis=0)
    rows = lambda x_t: jnp.transpose(x_t.reshape(depth, n_seq, att_heads, HEAD_DIM, t), (0, 1, 4, 2, 3))
    return (xp.reshape(n_seq, t, d), xs.reshape(nb, 1, d),
            rows(kv_stack[0]), rows(kv_stack[1]),
            stacked(p_states, 0), stacked(p_states, 1), stacked(p_states, 2),
            stacked(s_states, 0), stacked(s_states, 1), stacked(s_states, 2), stacked(s_states, 3),
            stacked(s_states, 4))
```

```python
import functools

import jax
import jax.numpy as jnp
from jax import lax
from jax.experimental import pallas as pl
from jax.experimental.pallas import tpu as pltpu

F32 = jnp.float32
BF16 = jnp.bfloat16

HEAD_DIM = 64
HEAD_SHIFT = 6
LANES = 128
NORM_EPS = 1e-6
RWKV_GN_EPS = 64e-5
DILATION_GROUPS = ((128, 1), (512, 4), (2048, 16))
ATT_SPAN = 128
ATT_DEINT = 4
W_LORA, A_LORA, G_LORA = 64, 64, 128
LORA_W = W_LORA + A_LORA + G_LORA
CONV_K = 3
RWKV_CHUNK = 64
VMEM_LIMIT = 56 * 1024 * 1024
NEG_BIG = -1e30

_NT = (((1,), (1,)), ((), ()))
_NN = (((1,), (0,)), ((), ()))
_BNN = (((2,), (1,)), ((0,), (0,)))
_BNT = (((2,), (2,)), ((0,), (0,)))
_BTN = (((1,), (1,)), ((0,), (0,)))


def _cparams(sem):
    return pltpu.CompilerParams(dimension_semantics=sem, vmem_limit_bytes=VMEM_LIMIT)


def _rms(x, g):
    return x * lax.rsqrt(jnp.mean(x * x, axis=-1, keepdims=True) + NORM_EPS) * g


def _split2(x):
    hi = x.astype(BF16)
    lo = (x - hi.astype(F32)).astype(BF16)
    return hi, lo


def _dot3(a, b, dims=_NN):
    ah, al = _split2(a)
    bh, bl = _split2(b)
    d = lambda x, y: lax.dot_general(x, y, dims, preferred_element_type=F32)
    return d(ah, bh) + (d(al, bh) + d(ah, bl))


def _dot1(a, b, dims=_NN):
    return lax.dot_general(a.astype(BF16), b.astype(BF16), dims, preferred_element_type=F32)


def _split3(x):
    h1 = x.astype(BF16)
    r1 = x - h1.astype(F32)
    h2 = r1.astype(BF16)
    h3 = (r1 - h2.astype(F32)).astype(BF16)
    return h1, h2, h3


def _dot_sel(x, sel):
    return sum(jnp.dot(h, sel, preferred_element_type=F32) for h in _split3(x))


def _cumsum_rows(x, tri_incl):
    return sum(jnp.dot(tri_incl, h, preferred_element_type=F32) for h in _split2(x))


def _head_blockdiag(n, dtype):
    r = lax.broadcasted_iota(jnp.int32, (n, n), 0) >> HEAD_SHIFT
    c = lax.broadcasted_iota(jnp.int32, (n, n), 1) >> HEAD_SHIFT
    return (r == c).astype(dtype)


def _segsum64(x):
    bd = _head_blockdiag(LANES, BF16)
    parts = [_dot_sel(x[:, i:i + LANES], bd) for i in range(0, x.shape[1], LANES)]
    return parts[0] if len(parts) == 1 else jnp.concatenate(parts, axis=1)


BF16_ROWS = 16


def _ffn_body(x_ref, xs_ref, g_ref, wg_ref, wu_ref, wd_ref, *refs, n_cast):
    cast_in = refs[:n_cast]
    o_ref, os_ref = refs[n_cast:n_cast + 2]
    cast_out = refs[n_cast + 2:2 * n_cast + 2]
    xn_ref = refs[2 * n_cast + 2]
    j = pl.program_id(1)
    tm, d = x_ref.shape

    @pl.when(j == 0)
    def _():
        xn_ref[0:tm, :] = _rms(x_ref[...], g_ref[...]).astype(BF16)
        xn_ref[tm:, :] = _rms(xs_ref[...], g_ref[...]).astype(BF16)
        o_ref[...] = jnp.zeros_like(o_ref)
        os_ref[...] = jnp.zeros_like(os_ref)

    for src, dst in zip(cast_in, cast_out):
        dst[...] = src[...].astype(BF16)

    xn = xn_ref[...]
    h = jnp.dot(xn, wg_ref[...].astype(BF16), preferred_element_type=F32)
    u = jnp.dot(xn, wu_ref[...].astype(BF16), preferred_element_type=F32)
    act = (h * jax.nn.sigmoid(h) * u).astype(BF16)
    half = d // 2
    for c in range(2):
        cols = slice(c * half, (c + 1) * half)
        part = jnp.dot(act, wd_ref[:, cols].astype(BF16), preferred_element_type=F32)
        o_ref[:, cols] += part[0:tm]
        os_ref[:, cols] += part[tm:]

    @pl.when(j == pl.num_programs(1) - 1)
    def _():
        o_ref[...] = x_ref[...] + 0.5 * o_ref[...]
        os_ref[...] = xs_ref[...] + 0.5 * os_ref[...]


def _ffn(x, xs, gain, wg, wu, wd, layer, tm, tf, cast=()):
    m, d = x.shape
    ms = xs.shape[0]
    f = wg.shape[2]
    ni, nj = m // tm, f // tf
    cast_in, cast_out, cast_shapes = [], [], []
    for w in cast:
        rows = w.shape[1]
        cr = next(r for r in range(BF16_ROWS, rows + 1, BF16_ROWS) if rows % r == 0 and rows // r <= ni * nj)
        chunks = rows // cr
        chunk = lambda i, j, chunks=chunks: jnp.minimum(i * nj + j, chunks - 1)
        cast_in.append(pl.BlockSpec((None, cr, w.shape[2]), lambda i, j, chunk=chunk: (layer, chunk(i, j), 0)))
        cast_out.append(pl.BlockSpec((cr, w.shape[2]), lambda i, j, chunk=chunk: (chunk(i, j), 0)))
        cast_shapes.append(jax.ShapeDtypeStruct(w.shape[1:], BF16))
    return pl.pallas_call(
        functools.partial(_ffn_body, n_cast=len(cast)),
        grid=(ni, nj),
        in_specs=[
            pl.BlockSpec((tm, d), lambda i, j: (i, 0)),
            pl.BlockSpec((ms, d), lambda i, j: (0, 0)),
            pl.BlockSpec((None, 1, d), lambda i, j: (layer, 0, 0)),
            pl.BlockSpec((None, d, tf), lambda i, j: (layer, 0, j)),
            pl.BlockSpec((None, d, tf), lambda i, j: (layer, 0, j)),
            pl.BlockSpec((None, tf, d), lambda i, j: (layer, j, 0)),
        ] + cast_in,
        out_specs=[pl.BlockSpec((tm, d), lambda i, j: (i, 0)),
                   pl.BlockSpec((ms, d), lambda i, j: (0, 0))] + cast_out,
        out_shape=[jax.ShapeDtypeStruct((m, d), F32), jax.ShapeDtypeStruct((ms, d), F32)] + cast_shapes,
        scratch_shapes=[pltpu.VMEM((tm + ms, d), BF16)],
        compiler_params=_cparams(("arbitrary", "arbitrary")),
        name="ffn",
    )(x, xs, gain, wg, wu, wd, *cast)


def _norm_matmul_body(x_ref, xs_ref, g_ref, w_ref, o_ref, os_ref, xn_ref):
    tm = x_ref.shape[0]

    @pl.when(pl.program_id(1) == 0)
    def _():
        xn_ref[0:tm, :] = _rms(x_ref[...], g_ref[...]).astype(BF16)
        xn_ref[tm:, :] = _rms(xs_ref[...], g_ref[...]).astype(BF16)

    out = jnp.dot(xn_ref[...], w_ref[...], preferred_element_type=F32)
    o_ref[...] = out[0:tm]
    os_ref[...] = out[tm:]


def _norm_matmul(x, xs, gain, w, layer, tm, tn):
    m, d = x.shape
    ms = xs.shape[0]
    n = w.shape[1]
    return pl.pallas_call(
        _norm_matmul_body,
        grid=(m // tm, n // tn),
        in_specs=[
            pl.BlockSpec((tm, d), lambda i, j: (i, 0)),
            pl.BlockSpec((ms, d), lambda i, j: (0, 0)),
            pl.BlockSpec((None, 1, d), lambda i, j: (layer, 0, 0)),
            pl.BlockSpec((d, tn), lambda i, j: (0, j)),
        ],
        out_specs=[pl.BlockSpec((tm, tn), lambda i, j: (i, j)),
                   pl.BlockSpec((ms, tn), lambda i, j: (0, j))],
        out_shape=[jax.ShapeDtypeStruct((m, n), F32), jax.ShapeDtypeStruct((ms, n), F32)],
        scratch_shapes=[pltpu.VMEM((tm + ms, d), BF16)],
        compiler_params=_cparams(("arbitrary", "arbitrary")),
        name="norm_matmul",
    )(x, xs, gain, w)


def _mix_out(x, o_att, y_rwkv, y_conv, w_ref):
    wa, wr = o_att.shape[1], y_rwkv.shape[1]
    acc = jnp.dot(o_att.astype(BF16), w_ref[0:wa, :], preferred_element_type=F32)
    acc += jnp.dot(y_rwkv.astype(BF16), w_ref[wa:wa + wr, :], preferred_element_type=F32)
    acc += jnp.dot(y_conv.astype(BF16), w_ref[wa + wr:, :], preferred_element_type=F32)
    return x + acc


def _out_proj_body(x_ref, a_ref, r_ref, c_ref, w_ref, o_ref):
    o_ref[...] = _mix_out(x_ref[...], a_ref[...], r_ref[...], c_ref[...], w_ref)


def _out_proj_conv_body(x_ref, a_ref, r_ref, u_ref, gb_ref, gc_ref, pu_ref, pgc_ref, cw_ref, cb_ref, w_ref,
                        o_ref, zl_ref, zc_ref, *, blocks_per_seq):
    tm = u_ref.shape[0]
    first = (pl.program_id(0) % blocks_per_seq) == 0
    z = gc_ref[...] * u_ref[...]
    zc_ref[0:8, :] = jnp.where(first, 0.0, pgc_ref[...] * pu_ref[...])
    zc_ref[8:, :] = z
    yc = cb_ref[...] + cw_ref[0:1, :] * zc_ref[pl.ds(6, tm), :]
    yc = yc + cw_ref[1:2, :] * zc_ref[pl.ds(7, tm), :]
    yc = yc + cw_ref[2:3, :] * z
    zl_ref[...] = zc_ref[pl.ds(tm, 8), :]
    o_ref[...] = _mix_out(x_ref[...], a_ref[...], r_ref[...], gb_ref[...] * yc, w_ref)


def _out_proj_conv(x, o_att, y_rwkv, p, col0, cw, conv_w, conv_b, w, layer, t, tm):
    m, d = x.shape
    assert col0 % LANES == 0 and t % tm == 0
    cur = lambda s: pl.BlockSpec((pl.Element(tm), pl.Element(cw)),
                                 lambda i: (pl.multiple_of(i * tm, tm), col0 + s * cw))
    prev = lambda s: pl.BlockSpec((pl.Element(8), pl.Element(cw)),
                                  lambda i: (pl.multiple_of(jnp.maximum(i * tm - 8, 0), 8), col0 + s * cw))
    return pl.pallas_call(
        functools.partial(_out_proj_conv_body, blocks_per_seq=t // tm),
        grid=(m // tm,),
        in_specs=[
            pl.BlockSpec((tm, d), lambda i: (i, 0)),
            pl.BlockSpec((tm, o_att.shape[1]), lambda i: (i, 0)),
            pl.BlockSpec((tm, y_rwkv.shape[1]), lambda i: (i, 0)),
            cur(0), cur(1), cur(2), prev(0), prev(2),
            pl.BlockSpec((None, CONV_K, cw), lambda i: (layer, 0, 0)),
            pl.BlockSpec((None, 1, cw), lambda i: (layer, 0, 0)),
            pl.BlockSpec((w.shape[0], d), lambda i: (0, 0), pipeline_mode=pl.Buffered(1)),
        ],
        out_specs=[pl.BlockSpec((tm, d), lambda i: (i, 0)), pl.BlockSpec((8, cw), lambda i: (i, 0))],
        out_shape=[jax.ShapeDtypeStruct((m, d), F32), jax.ShapeDtypeStruct((m // tm * 8, cw), F32)],
        scratch_shapes=[pltpu.VMEM((tm + 8, cw), F32)],
        compiler_params=_cparams(("parallel",)),
        name="out_proj_conv",
    )(x, o_att, y_rwkv, p, p, p, p, p, conv_w, conv_b, w)


def _out_proj(x, o_att, y_rwkv, y_conv, w, layer, tm):
    m, d = x.shape
    return pl.pallas_call(
        _out_proj_body,
        grid=(m // tm,),
        in_specs=[
            pl.BlockSpec((tm, d), lambda i: (i, 0)),
            pl.BlockSpec((tm, o_att.shape[1]), lambda i: (i, 0)),
            pl.BlockSpec((tm, y_rwkv.shape[1]), lambda i: (i, 0)),
            pl.BlockSpec((tm, y_conv.shape[1]), lambda i: (i, 0)),
            pl.BlockSpec((w.shape[0], d), lambda i: (0, 0), pipeline_mode=pl.Buffered(1)),
        ],
        out_specs=pl.BlockSpec((tm, d), lambda i: (i, 0)),
        out_shape=jax.ShapeDtypeStruct((m, d), F32),
        compiler_params=_cparams(("parallel",)),
        name="out_proj",
    )(x, o_att, y_rwkv, y_conv, w)


def _pair_headnorm(x, g):
    ms = _segsum64_many([x * x])[0] * (1.0 / HEAD_DIM)
    return x * lax.rsqrt(ms + NORM_EPS) * g


def _attn_body(slopes_ref, q_ref, k_ref, v_ref, qg_ref, kg_ref, *refs, n_carried):
    (o_ref, kt_ref, vt_ref, qs_ref, kn_ref, q4_ref, k4_ref, v4_ref, bias_ref,
     og_ref, lse_ref, og4_ref, lse4_ref) = refs[n_carried:]
    hp = pl.program_id(1)
    t = q_ref.shape[0]
    span = ATT_SPAN
    de = ATT_DEINT
    t4 = t // de
    lo_half = lax.broadcasted_iota(jnp.int32, (1, LANES), 1) < HEAD_DIM

    kn = _pair_headnorm(k_ref[...], kg_ref[...])
    kn_ref[...] = kn
    kt_ref[...] = kn.T
    vt_ref[...] = v_ref[...].T
    qs_ref[...] = _pair_headnorm(q_ref[...], qg_ref[...]) * (HEAD_DIM ** -0.5)
    for r in range(de):
        rows = pl.ds(r, t4, stride=de)
        q4_ref[r] = qs_ref[rows, :]
        k4_ref[r] = kn_ref[rows, :]
        v4_ref[r] = v_ref[rows, :]

    qi = lax.broadcasted_iota(jnp.int32, (span, 2 * span), 0)
    ki = lax.broadcasted_iota(jnp.int32, (span, 2 * span), 1)
    steps = qi + span - ki
    valid = (steps >= 0) & (steps <= span)
    for g, (_, dil) in enumerate(DILATION_GROUPS):
        dist = (steps * dil).astype(F32)
        for h in range(2):
            bias_ref[g, h * span:(h + 1) * span, :] = jnp.where(valid, -(slopes_ref[2 * hp + h] * dist), NEG_BIG)

    def attend(g, qb, kb, vb, has_prev, mxu_sum):
        q2 = jnp.concatenate([jnp.where(lo_half, qb, 0.0), jnp.where(lo_half, 0.0, qb)], axis=0).astype(BF16)
        bias = bias_ref[g] if has_prev else bias_ref[g, :, span:]
        s = lax.dot_general(q2, kb.astype(BF16), _NT, preferred_element_type=F32) + bias
        m = jnp.max(s, axis=-1, keepdims=True)
        p = jnp.exp(s - m)
        if mxu_sum:
            v_ones = jnp.concatenate([vb.astype(BF16), jnp.ones((vb.shape[0], LANES), BF16)], axis=1)
            pv = jnp.dot(p.astype(BF16), v_ones, preferred_element_type=F32)
            l = pv[:, LANES:]
            o = pv[:, 0:LANES] / l
        else:
            l = jnp.sum(p, axis=-1, keepdims=True)
            o = jnp.dot(p.astype(BF16), vb.astype(BF16), preferred_element_type=F32) / l
        lse = m + jnp.log(l)
        return jnp.where(lo_half, o[0:span], o[span:]), jnp.where(lo_half, lse[0:span], lse[span:])

    for g, (window, dil) in enumerate(DILATION_GROUPS):
        assert window // dil == span
        nb = t // (span * dil)

        if dil == 1:
            def unit(b, has_prev):
                rows = pl.ds(b * span, span)
                keys = pl.ds((b - 1) * span, 2 * span) if has_prev else rows
                o, lse = attend(0, qs_ref[rows, :], kn_ref[keys, :], v_ref[keys, :], has_prev, True)
                og_ref[rows, :] = o
                lse_ref[rows, :] = lse

            unit(0, False)
            lax.fori_loop(1, nb, lambda b, c: (unit(b, True), c)[1], 0, unroll=8)
            continue

        assert dil % de == 0
        st = dil // de

        def residue(r, carry, g=g, st=st, nb=nb):
            r4, off = r % de, r // de

            def unit(b, has_prev):
                rows = pl.ds(st * (b * span) + off, span, stride=st)
                keys = pl.ds(st * ((b - 1) * span) + off, 2 * span, stride=st) if has_prev else rows
                o, lse = attend(g, q4_ref[r4, rows, :], k4_ref[r4, keys, :], v4_ref[r4, keys, :], has_prev,
                                nb > 1)
                og4_ref[g - 1, r4, rows, :] = o
                lse4_ref[g - 1, r4, rows, :] = lse

            unit(0, False)
            if nb > 1:
                lax.fori_loop(1, nb, lambda b, c: (unit(b, True), c)[1], 0, unroll=3)
            return carry

        lax.fori_loop(0, dil, residue, 0, unroll=8 if nb == 1 else 2)

    for r in range(de):
        rows = pl.ds(r, t4, stride=de)
        l0, l1, l2 = lse_ref[rows, :], lse4_ref[0, r], lse4_ref[1, r]
        mx = jnp.maximum(jnp.maximum(l0, l1), l2)
        w0, w1, w2 = jnp.exp(l0 - mx), jnp.exp(l1 - mx), jnp.exp(l2 - mx)
        og_ref[rows, :] = (og_ref[rows, :] * w0 + og4_ref[0, r] * w1 + og4_ref[1, r] * w2) / (w0 + w1 + w2)
    o_ref[...] = og_ref[...].astype(o_ref.dtype)


def _attention_prompt(p, col0, att_w, qg, kg, slopes, layer, depth, n_seq, t, kv_stack):
    m = p.shape[0]
    pairs = att_w // LANES
    assert col0 % LANES == 0
    blk = lambda off: pl.BlockSpec((t, LANES), lambda n, hp: (n, off + hp))
    c0 = col0 // LANES
    gain = pl.BlockSpec((None, 1, LANES), lambda n, hp: (layer, 0, 0))
    tr_spec = pl.BlockSpec((None, None, LANES, t), lambda n, hp: (layer, n, hp, 0))
    tr_shape = jax.ShapeDtypeStruct((depth, n_seq, att_w, t), F32)
    carried = () if kv_stack is None else tuple(kv_stack)
    n_in = 6
    assert t % (ATT_DEINT * ATT_SPAN) == 0
    tok = pltpu.VMEM((t, LANES), F32)
    deint = pltpu.VMEM((ATT_DEINT, t // ATT_DEINT, LANES), F32)
    dilated = pltpu.VMEM((len(DILATION_GROUPS) - 1, ATT_DEINT, t // ATT_DEINT, LANES), F32)
    return pl.pallas_call(
        functools.partial(_attn_body, n_carried=len(carried)),
        grid=(n_seq, pairs),
        in_specs=[pl.BlockSpec(memory_space=pltpu.SMEM),
                  blk(c0), blk(c0 + pairs), blk(c0 + 2 * pairs), gain, gain]
                 + [pl.BlockSpec(memory_space=pl.ANY)] * len(carried),
        out_specs=[blk(0), tr_spec, tr_spec],
        out_shape=[jax.ShapeDtypeStruct((m, att_w), BF16), tr_shape, tr_shape],
        input_output_aliases={n_in + i: 1 + i for i in range(len(carried))},
        scratch_shapes=[tok, tok, deint, deint, deint,
                        pltpu.VMEM((len(DILATION_GROUPS), 2 * ATT_SPAN, 2 * ATT_SPAN), F32),
                        tok, tok, dilated, dilated],
        compiler_params=_cparams(("parallel", "arbitrary")),
        name="attn_prompt",
    )(slopes, p, p, p, qg, kg, *carried)


def _softplus(z):
    return jnp.maximum(z, 0.0) + jnp.log(1.0 + jnp.exp(-jnp.abs(z)))


def _rwkv_token_math(p_r, p_k, p_v, p_l, q_r, q_k, q_v, q_l, mu_r, mu_k, mu_v, mu_l,
                     w0, w2p, a0, a2p, g2p, k_k, k_a, gate_dot=_dot3):
    xr = p_r + (q_r - p_r) * mu_r
    xk = p_k + (q_k - p_k) * mu_k
    xv = p_v + (q_v - p_v) * mu_v
    xl = p_l + (q_l - p_l) * mu_l
    w_log = -_softplus(-(w0 + _dot3(jnp.tanh(xl), w2p))) - 0.5
    log_decay = -jnp.exp(w_log)
    sig_l = jax.nn.sigmoid(xl)
    a_gate = jax.nn.sigmoid(a0 + gate_dot(xl, a2p))
    g = gate_dot(sig_l, g2p)
    kk = xk * k_k
    kk = kk * lax.rsqrt(jnp.maximum(_segsum64_many([kk * kk])[0], 1e-24))
    kmod = xk * (1.0 + (a_gate - 1.0) * k_a)
    return xr, log_decay, kmod, xv, -kk, kk * a_gate, g


def _segsum64_many(xs):
    rows, width = xs[0].shape
    if rows % 16 != 0:
        return [_segsum64(x) for x in xs]
    nsl = width // LANES
    pieces = [part[:, i * LANES:(i + 1) * LANES]
              for x in xs for part in _split2(x) for i in range(nsl)]
    res = jnp.dot(jnp.concatenate(pieces, axis=0), _head_blockdiag(LANES, BF16), preferred_element_type=F32)
    outs = []
    for n in range(len(xs)):
        hi, lo = [jnp.concatenate([res[((2 * n + j) * nsl + i) * rows:((2 * n + j) * nsl + i + 1) * rows]
                                   for i in range(nsl)], axis=1) for j in range(2)]
        outs.append(hi + lo)
    return outs


def _rwkv_post(y, r, k, v, g, ln_w, ln_b, r_k):
    sum_y, sum_rk = _segsum64_many([y, r * k * r_k])
    yc = y - sum_y * (1.0 / HEAD_DIM)
    var_y = _segsum64_many([yc * yc])[0] * (1.0 / HEAD_DIM)
    yn = yc * lax.rsqrt(var_y + RWKV_GN_EPS) * ln_w + ln_b
    return (yn + sum_rk * v) * g


def _scan_chunk(r, lw, cum, k, v, a, b, s0):
    c = r.shape[1]
    c2 = 2 * c
    lane = lax.broadcasted_iota(jnp.int32, (1, 1, LANES), 2)
    head0 = lane < HEAD_DIM
    m0 = head0.astype(F32)
    m1 = 1.0 - m0
    stack = lambda x: jnp.concatenate([x * m0, x * m1], axis=1)
    twice = lambda x: jnp.concatenate([x, x], axis=1)
    pick = lambda x_st: jnp.where(head0, x_st[:, 0:c], x_st[:, c:])

    cum_prev = cum - lw
    cum_last = cum[:, c - 1:c, :]
    e_pos = jnp.exp(cum)
    e_neg = jnp.exp(-cum)
    e_rem = jnp.exp(cum_last - cum)
    at = a * jnp.exp(cum_prev)
    rt = r * e_pos
    bt = b * e_neg
    kt = k * e_neg
    bh = b * e_rem
    kh = k * e_rem

    big = _dot1(jnp.concatenate([stack(at), stack(rt)], axis=1),
                jnp.concatenate([twice(bt), twice(kt)], axis=1), _BNT)
    row = lax.broadcasted_iota(jnp.int32, (1, c2, c2), 1)
    col = lax.broadcasted_iota(jnp.int32, (1, c2, c2), 2)
    assert c == HEAD_DIM
    same_head = (row >> HEAD_SHIFT) == (col >> HEAD_SHIFT)
    tr, tc = row & (c - 1), col & (c - 1)
    strict = same_head & (tr > tc)
    incl = same_head & (tr >= tc)
    lmat = jnp.where(strict, big[:, 0:c2, 0:c2], 0.0)
    ak = jnp.where(strict, big[:, 0:c2, c2:], 0.0)
    rb = jnp.where(incl, big[:, c2:, 0:c2], 0.0)
    rk = jnp.where(incl, big[:, c2:, c2:], 0.0)

    eye = (row == col).astype(F32)
    x = eye + jnp.where((row >> 1) == (col >> 1), lmat, 0.0)
    s, log2s = 2, 1
    while s < c:
        joins = ((row >> (log2s + 1)) == (col >> (log2s + 1))) & ((row & s) != 0) & ((col & s) == 0)
        x = x + _dot1(_dot1(x, jnp.where(joins, lmat, 0.0), _BNN), x, _BNN)
        s, log2s = 2 * s, log2s + 1

    v_st = twice(v)
    w = _dot1(jnp.concatenate([at, rt], axis=1), s0, _BNT)
    u_st = _dot1(x, twice(w[:, 0:c]) + _dot1(ak, v_st, _BNN), _BNN)
    y_st = _dot1(jnp.concatenate([rb, rk], axis=2), jnp.concatenate([u_st, v_st], axis=1), _BNN)
    y = w[:, c:] + pick(y_st)
    u = pick(u_st)
    upd = _dot1(jnp.concatenate([u, v], axis=1), jnp.concatenate([bh, kh], axis=1), _BTN)
    s_new = s0 * jnp.exp(cum_last) + _head_blockdiag(LANES, F32)[None] * upd
    return y, s_new


def _rwkv_mix_body(pr_ref, pk_ref, pv_ref, plo_ref,
                   mur_ref, muk_ref, muv_ref, mul_ref, w0_ref, w2_ref, a0_ref, a2_ref, g2_ref,
                   kk_ref, ka_ref, lnw_ref, lnb_ref, rk_ref,
                   y_ref, s_out_ref, s_ref, sr_ref, sk_ref, sv_ref, sl_ref):
    ch = pl.program_id(1)
    nseq, c, rw = pr_ref.shape
    npair = rw // LANES
    shift_refs = (sr_ref, sk_ref, sv_ref, sl_ref)

    @pl.when(ch == 0)
    def _():
        s_ref[...] = jnp.zeros_like(s_ref)
        for ref in shift_refs:
            ref[:, 0:8, :] = jnp.zeros((nseq, 8, ref.shape[2]), F32)

    def cur_and_prev(cur_ref, sh_ref):
        cur = cur_ref[...]
        sh_ref[:, 8:, :] = cur
        prev = sh_ref[:, pl.ds(7, c), :]
        sh_ref[:, 7:8, :] = cur[:, c - 1:c, :]
        flat = lambda x: x.reshape(nseq * c, x.shape[2])
        return flat(cur), flat(prev)

    (p_r, q_r), (p_k, q_k), (p_v, q_v), (p_l, q_l) = [
        cur_and_prev(cur, sh) for cur, sh in zip((pr_ref, pk_ref, pv_ref, plo_ref), shift_refs)]
    r, lw, k, v, a, b, g = _rwkv_token_math(
        p_r, p_k, p_v, p_l, q_r, q_k, q_v, q_l,
        mur_ref[...], muk_ref[...], muv_ref[...], mul_ref[...],
        w0_ref[...], w2_ref[...], a0_ref[...], a2_ref[...], g2_ref[...], kk_ref[...], ka_ref[...],
        gate_dot=_dot1)

    pairs = lambda x: jnp.stack([x[s * c:(s + 1) * c, p * LANES:(p + 1) * LANES]
                                 for s in range(nseq) for p in range(npair)], axis=0)
    tri_incl = (lax.broadcasted_iota(jnp.int32, (c, c), 0)
                >= lax.broadcasted_iota(jnp.int32, (c, c), 1)).astype(BF16)
    cum = jnp.concatenate([_cumsum_rows(lw[s * c:(s + 1) * c], tri_incl) for s in range(nseq)], axis=0)
    y, s_new = _scan_chunk(pairs(r), pairs(lw), pairs(cum), pairs(k), pairs(v), pairs(a), pairs(b), s_ref[...])
    s_ref[...] = s_new
    y = jnp.concatenate([jnp.concatenate([y[s * npair + p] for p in range(npair)], axis=1)
                         for s in range(nseq)], axis=0)
    out = _rwkv_post(y, r, k, v, g, lnw_ref[...], lnb_ref[...], rk_ref[...])
    y_ref[...] = out.reshape(nseq, c, rw).astype(y_ref.dtype)

    @pl.when(ch == pl.num_programs(1) - 1)
    def _():
        for s in range(nseq):
            for p in range(npair):
                s_out_ref[s, 2 * p] = s_ref[s * npair + p, 0:HEAD_DIM, 0:HEAD_DIM]
                s_out_ref[s, 2 * p + 1] = s_ref[s * npair + p, HEAD_DIM:, HEAD_DIM:]


def _rwkv_mix_prompt(p, col0, lp, layer, n_seq, t, seq_per_step):
    m = p.shape[0]
    rw = lp["rwkv_w"]
    c = RWKV_CHUNK
    heads = rw // HEAD_DIM
    assert n_seq % seq_per_step == 0 and col0 % rw == 0 and (col0 + 3 * rw) % LORA_W == 0
    c_slab = col0 // rw
    c_lora = (col0 + 3 * rw) // LORA_W
    tok = lambda w, col: pl.BlockSpec((seq_per_step, c, w), lambda n, ch: (n, ch, col))
    vec = lambda w, col: pl.BlockSpec((None, 1, w), lambda n, ch: (layer, 0, col))
    mat = pl.BlockSpec((None, LORA_W, rw), lambda n, ch: (layer, 0, 0))
    p3 = p.reshape(n_seq, t, p.shape[1])
    y, wkv = pl.pallas_call(
        _rwkv_mix_body,
        grid=(n_seq // seq_per_step, t // c),
        in_specs=[tok(rw, c_slab), tok(rw, c_slab + 1), tok(rw, c_slab + 2), tok(LORA_W, c_lora),
                  vec(rw, 0), vec(rw, 1), vec(rw, 2), vec(LORA_W, (3 * rw) // LORA_W),
                  vec(rw, 0), mat, vec(rw, 0), mat, mat, vec(rw, 0), vec(rw, 0),
                  vec(rw, 0), vec(rw, 0), vec(rw, 0)],
        out_specs=[tok(rw, 0),
                   pl.BlockSpec((seq_per_step, heads, HEAD_DIM, HEAD_DIM), lambda n, ch: (n, 0, 0, 0))],
        out_shape=[jax.ShapeDtypeStruct((n_seq, t, rw), BF16),
                   jax.ShapeDtypeStruct((n_seq, heads, HEAD_DIM, HEAD_DIM), F32)],
        scratch_shapes=[pltpu.VMEM((seq_per_step * (rw // LANES), LANES, LANES), F32)]
                       + [pltpu.VMEM((seq_per_step, c + 8, rw), F32)] * 3
                       + [pltpu.VMEM((seq_per_step, c + 8, LORA_W), F32)],
        compiler_params=_cparams(("parallel", "arbitrary")),
        name="rwkv_mix",
    )(p3, p3, p3, p3, lp["mu"], lp["mu"], lp["mu"], lp["mu"],
      lp["w0"], lp["w2p"], lp["a0"], lp["a2p"], lp["g2p"], lp["k_k"], lp["k_a"],
      lp["ln_w"], lp["ln_b"], lp["r_k"])
    return y.reshape(m, rw), wkv


def _sample_body(slope_ref, p_ref, kc_ref, vc_ref,
                 wkv_ref, shift_ref, cst_ref, qg_ref, kg_ref,
                 mu_ref, w0_ref, w2_ref, a0_ref, a2_ref, g2_ref, kk_ref, ka_ref,
                 rk_ref, lnw_ref, lnb_ref, cw_ref, cb_ref,
                 oatt_ref, yr_ref, yc_ref, kn_ref, vn_ref, wkvn_ref, cstn_ref,
                 *, col_att, col_rwkv, col_conv):
    heads_a = oatt_ref.shape[0]
    aw = heads_a * HEAD_DIM
    rw = yr_ref.shape[1]
    cw = yc_ref.shape[1]
    rp = shift_ref.shape[1]
    rows8 = lambda x: jnp.broadcast_to(x, (8, x.shape[1]))
    p_all = p_ref[...]

    by_head = lambda row: jnp.concatenate(
        [row[:, h * HEAD_DIM:(h + 1) * HEAD_DIM] for h in range(heads_a)], axis=0)
    ca = col_att
    q, k, v_new = (by_head(p_all[:, ca:ca + aw]), by_head(p_all[:, ca + aw:ca + 2 * aw]),
                   by_head(p_all[:, ca + 2 * aw:ca + 3 * aw]))
    qn = _rms(q, qg_ref[...]) * (HEAD_DIM ** -0.5)
    kn = _rms(k, kg_ref[...])
    kn_ref[...] = kn
    vn_ref[...] = v_new
    s_new = jnp.sum(kn * qn, axis=-1, keepdims=True)
    slope = slope_ref[...]
    past = kc_ref.shape[2]
    er = lax.broadcasted_iota(jnp.int32, (HEAD_DIM, HEAD_DIM), 0)
    ec = lax.broadcasted_iota(jnp.int32, (HEAD_DIM, HEAD_DIM), 1)
    eye = er == ec
    q_col = jnp.stack([jnp.sum(jnp.where(eye, qn[h:h + 1, :], 0.0), axis=1, keepdims=True)
                       for h in range(heads_a)], axis=0)
    dist = past - lax.broadcasted_iota(jnp.int32, (1, past), 1)
    biased = jnp.sum(kc_ref[...] * q_col, axis=1) - slope * dist.astype(F32)
    parts = []
    for window, dil in DILATION_GROUPS:
        assert dil & (dil - 1) == 0
        valid = ((dist & (dil - 1)) == 0) & (dist <= window)
        s = jnp.where(valid, biased, NEG_BIG)
        m = jnp.maximum(jnp.max(s, axis=1, keepdims=True), s_new)
        p = jnp.exp(s - m)
        p_new = jnp.exp(s_new - m)
        l = jnp.sum(p, axis=1, keepdims=True) + p_new
        parts.append((p, p_new, l, m + jnp.log(l)))
    mx = jnp.maximum(jnp.maximum(parts[0][3], parts[1][3]), parts[2][3])
    ws = [jnp.exp(lse - mx) for _, _, _, lse in parts]
    wsum = ws[0] + ws[1] + ws[2]
    scale_g = [w / (l * wsum) for w, (_, _, l, _) in zip(ws, parts)]
    coef = parts[0][0] * scale_g[0] + parts[1][0] * scale_g[1] + parts[2][0] * scale_g[2]
    coef_new = parts[0][1] * scale_g[0] + parts[1][1] * scale_g[1] + parts[2][1] * scale_g[2]
    o_col = jnp.sum(vc_ref[...] * coef[:, None, :], axis=2, keepdims=True)
    o_rows = jnp.concatenate([jnp.sum(jnp.where(eye, o_col[h], 0.0), axis=0, keepdims=True)
                              for h in range(heads_a)], axis=0)
    oatt_ref[...] = o_rows + coef_new * v_new

    pr = rows8(p_all[:, col_rwkv:col_rwkv + rp])
    sh = rows8(shift_ref[...])
    mu = mu_ref[...]
    c3 = 3 * rw
    r, lw, k, v, a, b, g = _rwkv_token_math(
        pr[:, 0:rw], pr[:, rw:2 * rw], pr[:, 2 * rw:c3], pr[:, c3:],
        sh[:, 0:rw], sh[:, rw:2 * rw], sh[:, 2 * rw:c3], sh[:, c3:],
        mu[:, 0:rw], mu[:, rw:2 * rw], mu[:, 2 * rw:c3], mu[:, c3:],
        w0_ref[...], w2_ref[...], a0_ref[...], a2_ref[...], g2_ref[...], kk_ref[...], ka_ref[...])
    decay = jnp.exp(lw)
    place_r = lax.broadcasted_iota(jnp.int32, (HEAD_DIM, rw), 0)
    place_c = lax.broadcasted_iota(jnp.int32, (HEAD_DIM, rw), 1)
    y_full = jnp.zeros((1, rw), F32)
    for h in range(rw // HEAD_DIM):
        hs = slice(h * HEAD_DIM, (h + 1) * HEAD_DIM)
        s_old = wkv_ref[h]
        sa = jnp.sum(s_old * a[0:1, hs], axis=1, keepdims=True)
        v_col = jnp.sum(jnp.where(eye, v[0:1, hs], 0.0), axis=1, keepdims=True)
        s_h = s_old * decay[0:1, hs] + sa * b[0:1, hs] + v_col * k[0:1, hs]
        wkvn_ref[h] = s_h
        y_col = jnp.sum(s_h * r[0:1, hs], axis=1, keepdims=True)
        y_full = y_full + jnp.sum(jnp.where(place_c == place_r + h * HEAD_DIM, y_col, 0.0),
                                  axis=0, keepdims=True)
    y8 = rows8(y_full)
    yr_ref[...] = _rwkv_post(y8, r, k, v, g, lnw_ref[...], lnb_ref[...], rk_ref[...])[0:1]

    c1 = col_conv
    z = p_all[:, c1 + 2 * cw:c1 + 3 * cw] * p_all[:, c1:c1 + cw]
    yc = cb_ref[...] + cw_ref[0:1, :] * cst_ref[0:1, :]
    yc = yc + cw_ref[1:2, :] * cst_ref[1:2, :]
    yc = yc + cw_ref[2:3, :] * z
    yc_ref[...] = p_all[:, c1 + cw:c1 + 2 * cw] * yc
    cstn_ref[0:1, :] = cst_ref[1:2, :]
    cstn_ref[1:2, :] = z


def _sample_mixers(p, cols, cache_k, cache_v, state_wkv, state_shift, state_conv, lp, slope_col, layer):
    nb = p.shape[0]
    depth, _, past, heads_a, _ = cache_k.shape
    rw = lp["rwkv_w"]
    rp = state_shift.shape[2]
    cw = state_conv.shape[3]
    heads = rw // HEAD_DIM

    row = lambda w: pl.BlockSpec((None, 1, w), lambda n: (n, 0, 0))
    lvec = lambda w: pl.BlockSpec((None, 1, w), lambda n: (layer, 0, 0))
    lmat = lambda r_, w: pl.BlockSpec((None, r_, w), lambda n: (layer, 0, 0))
    tile = pl.BlockSpec((None, heads_a, HEAD_DIM), lambda n: (n, 0, 0))

    assert past >= DILATION_GROUPS[-1][0]
    views = [jnp.transpose(c, (0, 1, 3, 4, 2)) for c in (cache_k, cache_v)]
    cache_specs = [pl.BlockSpec((None, None, heads_a, HEAD_DIM, past), lambda n: (layer, n, 0, 0, 0))] * 2

    outs = pl.pallas_call(
        functools.partial(_sample_body, col_att=cols[0], col_rwkv=cols[1], col_conv=cols[2]),
        grid=(nb,),
        in_specs=[pl.BlockSpec((heads_a, 1), lambda n: (0, 0)), row(p.shape[1])] + cache_specs + [
                  pl.BlockSpec((None, None, heads, HEAD_DIM, HEAD_DIM), lambda n: (layer, n, 0, 0, 0)),
                  pl.BlockSpec((None, None, 1, rp), lambda n: (layer, n, 0, 0)),
                  pl.BlockSpec((None, None, CONV_K - 1, cw), lambda n: (layer, n, 0, 0)),
                  lvec(HEAD_DIM), lvec(HEAD_DIM),
                  lvec(rp), lvec(rw), lmat(LORA_W, rw), lvec(rw), lmat(LORA_W, rw),
                  lmat(LORA_W, rw), lvec(rw), lvec(rw), lvec(rw), lvec(rw), lvec(rw),
                  lmat(CONV_K, cw), lvec(cw)],
        out_specs=[tile, row(rw), row(cw), tile, tile,
                   pl.BlockSpec((None, heads, HEAD_DIM, HEAD_DIM), lambda n: (n, 0, 0, 0)),
                   pl.BlockSpec((None, CONV_K - 1, cw), lambda n: (n, 0, 0))],
        out_shape=[jax.ShapeDtypeStruct((nb, heads_a, HEAD_DIM), F32), jax.ShapeDtypeStruct((nb, 1, rw), F32),
                   jax.ShapeDtypeStruct((nb, 1, cw), F32),
                   jax.ShapeDtypeStruct((nb, heads_a, HEAD_DIM), F32),
                   jax.ShapeDtypeStruct((nb, heads_a, HEAD_DIM), F32),
                   jax.ShapeDtypeStruct((nb, heads, HEAD_DIM, HEAD_DIM), F32),
                   jax.ShapeDtypeStruct((nb, CONV_K - 1, cw), F32)],
        compiler_params=_cparams(("parallel",)),
        name="sample_mixers",
    )(slope_col, p[:, None, :], *views,
      state_wkv, state_shift[:, :, None, :], state_conv,
      lp["q_gain"], lp["k_gain"],
      lp["mu"], lp["w0"], lp["w2p"], lp["a0"], lp["a2p"], lp["g2p"], lp["k_k"], lp["k_a"],
      lp["r_k"], lp["ln_w"], lp["ln_b"], lp["conv_w"], lp["conv_b"])
    o_att, y_rwkv, y_conv, k_new, v_new, wkv_new, conv_new = outs
    return o_att.reshape(nb, heads_a * HEAD_DIM), y_rwkv[:, 0], y_conv[:, 0], k_new, v_new, wkv_new, conv_new


def kernel(x_prompt, x_sample, cache_swa_k, cache_swa_v, state_wkv, state_shift, state_conv, ffn1_norm, ffn1_w_gate, ffn1_w_up, ffn1_w_down, mix_norm, w_in, q_norm, k_norm, rwkv_mu, rwkv_w0, rwkv_w2, rwkv_a0, rwkv_a2, rwkv_g2, rwkv_k_k, rwkv_k_a, rwkv_r_k, rwkv_ln_w, rwkv_ln_b, conv_w, conv_b, w_out, ffn2_norm, ffn2_w_gate, ffn2_w_up, ffn2_w_down):
    n_seq, t, d = x_prompt.shape
    nb = x_sample.shape[0]
    assert x_sample.shape[1] == 1
    depth = w_in.shape[0]
    att_heads, head_dim = cache_swa_k.shape[3], cache_swa_k.shape[4]
    assert head_dim == HEAD_DIM
    aw = att_heads * HEAD_DIM
    rw = rwkv_w0.shape[1]
    cw = conv_b.shape[1]
    rp = rwkv_mu.shape[1]
    assert rp == 3 * rw + LORA_W and w_in.shape[2] == 3 * aw + rp + 3 * cw

    row3 = lambda p: p.reshape(depth, 1, -1)
    col_att, col_rwkv, col_conv = 0, 3 * aw, 3 * aw + rp
    zpad = lambda w, before: jnp.pad(w, ((0, 0), (before, LORA_W - before - w.shape[1]), (0, 0)))
    lp = {
        "rwkv_w": rw,
        "mu": row3(rwkv_mu), "w0": row3(rwkv_w0), "a0": row3(rwkv_a0),
        "w2p": zpad(rwkv_w2, 0), "a2p": zpad(rwkv_a2, W_LORA), "g2p": zpad(rwkv_g2, W_LORA + A_LORA),
        "k_k": row3(rwkv_k_k), "k_a": row3(rwkv_k_a), "r_k": row3(rwkv_r_k),
        "ln_w": row3(rwkv_ln_w), "ln_b": row3(rwkv_ln_b),
        "conv_w": conv_w, "conv_b": row3(conv_b),
        "q_gain": row3(q_norm), "k_gain": row3(k_norm),
        "q_gain_pair": row3(jnp.tile(q_norm, (1, LANES // HEAD_DIM))),
        "k_gain_pair": row3(jnp.tile(k_norm, (1, LANES // HEAD_DIM))),
    }
    g_ffn1, g_mix, g_ffn2 = row3(ffn1_norm), row3(mix_norm), row3(ffn2_norm)
    slopes = 2.0 ** (-8.0 * jnp.arange(1, att_heads + 1, dtype=F32) / att_heads)
    slope_col = slopes[:, None]

    m = n_seq * t
    xp = x_prompt.reshape(m, d)
    xs = x_sample.reshape(nb, d)
    tm_ffn, tf = 1024, 256
    tm_in, tn_in = 1024, 1280
    tm_out = 512
    p_states, s_states = [], []
    kv_stack = None
    for l in range(depth):
        xp, xs, w_in16, w_out16 = _ffn(xp, xs, g_ffn1, ffn1_w_gate, ffn1_w_up, ffn1_w_down, l, tm_ffn, tf,
                                       cast=(w_in, w_out))
        pp, ps = _norm_matmul(xp, xs, g_mix, w_in16, l, tm_in, tn_in)

        o_att, *kv_stack = _attention_prompt(pp, col_att, aw, lp["q_gain_pair"], lp["k_gain_pair"], slopes,
                                             l, depth, n_seq, t, kv_stack)
        y_rwkv, wkv_p = _rwkv_mix_prompt(pp, col_rwkv, lp, l, n_seq, t, 4 if n_seq % 4 == 0 else 1)
        xp, z_last = _out_proj_conv(xp, o_att, y_rwkv, pp, col_conv, cw, lp["conv_w"], lp["conv_b"], w_out16,
                                    l, t, tm_out)
        p_states.append((
            wkv_p, pp.reshape(n_seq, t, -1)[:, -1, col_rwkv:col_rwkv + rp],
            z_last.reshape(n_seq, t // tm_out, 8, cw)[:, -1, 8 - (CONV_K - 1):]))

        so, sy, syc, sk, sv, swkv, sconv = _sample_mixers(
            ps, (col_att, col_rwkv, col_conv), cache_swa_k, cache_swa_v, state_wkv, state_shift, state_conv,
            lp, slope_col, l)
        xs = _out_proj(xs, so, sy, syc, w_out16, l, nb)
        s_states.append((sk[:, None], sv[:, None], swkv, ps[:, col_rwkv:col_rwkv + rp], sconv))

        xp, xs = _ffn(xp, xs, g_ffn2, ffn2_w_gate, ffn2_w_up, ffn2_w_down, l, tm_ffn, tf)

    stacked = lambda states, i: jnp.stack([st[i] for st in states], axis=0)
    rows = lambda x_t: jnp.transpose(x_t.reshape(depth, n_seq, att_heads, HEAD_DIM, t), (0, 1, 4, 2, 3))
    return (xp.reshape(n_seq, t, d), xs.reshape(nb, 1, d),
            rows(kv_stack[0]), rows(kv_stack[1]),
            stacked(p_states, 0), stacked(p_states, 1), stacked(p_states, 2),
            stacked(s_states, 0), stacked(s_states, 1), stacked(s_states, 2), stacked(s_states, 3),
            stacked(s_states, 4))
```

```python
import functools

import jax
import jax.numpy as jnp
from jax import lax
from jax.experimental import pallas as pl
from jax.experimental.pallas import tpu as pltpu

F32 = jnp.float32
BF16 = jnp.bfloat16

HEAD_DIM = 64
HEAD_SHIFT = 6
LANES = 128
NORM_EPS = 1e-6
RWKV_GN_EPS = 64e-5
DILATION_GROUPS = ((128, 1), (512, 4), (2048, 16))
ATT_SPAN = 128
ATT_DEINT = 4
W_LORA, A_LORA, G_LORA = 64, 64, 128
LORA_W = W_LORA + A_LORA + G_LORA
CONV_K = 3
RWKV_CHUNK = 64
VMEM_LIMIT = 56 * 1024 * 1024
NEG_BIG = -1e30

_NT = (((1,), (1,)), ((), ()))
_NN = (((1,), (0,)), ((), ()))
_BNN = (((2,), (1,)), ((0,), (0,)))
_BNT = (((2,), (2,)), ((0,), (0,)))
_BTN = (((1,), (1,)), ((0,), (0,)))


def _cparams(sem):
    return pltpu.CompilerParams(dimension_semantics=sem, vmem_limit_bytes=VMEM_LIMIT)


def _rms(x, g):
    return x * lax.rsqrt(jnp.mean(x * x, axis=-1, keepdims=True) + NORM_EPS) * g


def _split2(x):
    hi = x.astype(BF16)
    lo = (x - hi.astype(F32)).astype(BF16)
    return hi, lo


def _dot3(a, b, dims=_NN):
    ah, al = _split2(a)
    bh, bl = _split2(b)
    d = lambda x, y: lax.dot_general(x, y, dims, preferred_element_type=F32)
    return d(ah, bh) + (d(al, bh) + d(ah, bl))


def _dot1(a, b, dims=_NN):
    return lax.dot_general(a.astype(BF16), b.astype(BF16), dims, preferred_element_type=F32)


def _split3(x):
    h1 = x.astype(BF16)
    r1 = x - h1.astype(F32)
    h2 = r1.astype(BF16)
    h3 = (r1 - h2.astype(F32)).astype(BF16)
    return h1, h2, h3


def _dot_sel(x, sel):
    return sum(jnp.dot(h, sel, preferred_element_type=F32) for h in _split3(x))


def _cumsum_rows(x, tri_incl):
    return sum(jnp.dot(tri_incl, h, preferred_element_type=F32) for h in _split2(x))


def _head_blockdiag(n, dtype):
    r = lax.broadcasted_iota(jnp.int32, (n, n), 0) >> HEAD_SHIFT
    c = lax.broadcasted_iota(jnp.int32, (n, n), 1) >> HEAD_SHIFT
    return (r == c).astype(dtype)


def _segsum64(x):
    bd = _head_blockdiag(LANES, BF16)
    parts = [_dot_sel(x[:, i:i + LANES], bd) for i in range(0, x.shape[1], LANES)]
    return parts[0] if len(parts) == 1 else jnp.concatenate(parts, axis=1)


BF16_ROWS = 16


def _ffn_body(x_ref, xs_ref, g_ref, wg_ref, wu_ref, wd_ref, *refs, n_cast):
    cast_in = refs[:n_cast]
    o_ref, os_ref = refs[n_cast:n_cast + 2]
    cast_out = refs[n_cast + 2:2 * n_cast + 2]
    xn_ref = refs[2 * n_cast + 2]
    j = pl.program_id(1)
    tm, d = x_ref.shape

    @pl.when(j == 0)
    def _():
        xn_ref[0:tm, :] = _rms(x_ref[...], g_ref[...]).astype(BF16)
        xn_ref[tm:, :] = _rms(xs_ref[...], g_ref[...]).astype(BF16)
        o_ref[...] = jnp.zeros_like(o_ref)
        os_ref[...] = jnp.zeros_like(os_ref)

    for src, dst in zip(cast_in, cast_out):
        dst[...] = src[...].astype(BF16)

    xn = xn_ref[...]
    h = jnp.dot(xn, wg_ref[...].astype(BF16), preferred_element_type=F32)
    u = jnp.dot(xn, wu_ref[...].astype(BF16), preferred_element_type=F32)
    act = (h * jax.nn.sigmoid(h) * u).astype(BF16)
    half = d // 2
    for c in range(2):
        cols = slice(c * half, (c + 1) * half)
        part = jnp.dot(act, wd_ref[:, cols].astype(BF16), preferred_element_type=F32)
        o_ref[:, cols] += part[0:tm]
        os_ref[:, cols] += part[tm:]

    @pl.when(j == pl.num_programs(1) - 1)
    def _():
        o_ref[...] = x_ref[...] + 0.5 * o_ref[...]
        os_ref[...] = xs_ref[...] + 0.5 * os_ref[...]


def _ffn(x, xs, gain, wg, wu, wd, layer, tm, tf, cast=()):
    m, d = x.shape
    ms = xs.shape[0]
    f = wg.shape[2]
    ni, nj = m // tm, f // tf
    cast_in, cast_out, cast_shapes = [], [], []
    for w in cast:
        rows = w.shape[1]
        cr = next(r for r in range(BF16_ROWS, rows + 1, BF16_ROWS) if rows % r == 0 and rows // r <= ni * nj)
        chunks = rows // cr
        chunk = lambda i, j, chunks=chunks: jnp.minimum(i * nj + j, chunks - 1)
        cast_in.append(pl.BlockSpec((None, cr, w.shape[2]), lambda i, j, chunk=chunk: (layer, chunk(i, j), 0)))
        cast_out.append(pl.BlockSpec((cr, w.shape[2]), lambda i, j, chunk=chunk: (chunk(i, j), 0)))
        cast_shapes.append(jax.ShapeDtypeStruct(w.shape[1:], BF16))
    return pl.pallas_call(
        functools.partial(_ffn_body, n_cast=len(cast)),
        grid=(ni, nj),
        in_specs=[
            pl.BlockSpec((tm, d), lambda i, j: (i, 0)),
            pl.BlockSpec((ms, d), lambda i, j: (0, 0)),
            pl.BlockSpec((None, 1, d), lambda i, j: (layer, 0, 0)),
            pl.BlockSpec((None, d, tf), lambda i, j: (layer, 0, j)),
            pl.BlockSpec((None, d, tf), lambda i, j: (layer, 0, j)),
            pl.BlockSpec((None, tf, d), lambda i, j: (layer, j, 0)),
        ] + cast_in,
        out_specs=[pl.BlockSpec((tm, d), lambda i, j: (i, 0)),
                   pl.BlockSpec((ms, d), lambda i, j: (0, 0))] + cast_out,
        out_shape=[jax.ShapeDtypeStruct((m, d), F32), jax.ShapeDtypeStruct((ms, d), F32)] + cast_shapes,
        scratch_shapes=[pltpu.VMEM((tm + ms, d), BF16)],
        compiler_params=_cparams(("arbitrary", "arbitrary")),
        name="ffn",
    )(x, xs, gain, wg, wu, wd, *cast)


def _norm_matmul_body(x_ref, xs_ref, g_ref, w_ref, o_ref, os_ref, xn_ref):
    tm = x_ref.shape[0]

    @pl.when(pl.program_id(1) == 0)
    def _():
        xn_ref[0:tm, :] = _rms(x_ref[...], g_ref[...]).astype(BF16)
        xn_ref[tm:, :] = _rms(xs_ref[...], g_ref[...]).astype(BF16)

    out = jnp.dot(xn_ref[...], w_ref[...], preferred_element_type=F32)
    o_ref[...] = out[0:tm]
    os_ref[...] = out[tm:]


def _norm_matmul(x, xs, gain, w, layer, tm, tn):
    m, d = x.shape
    ms = xs.shape[0]
    n = w.shape[1]
    return pl.pallas_call(
        _norm_matmul_body,
        grid=(m // tm, n // tn),
        in_specs=[
            pl.BlockSpec((tm, d), lambda i, j: (i, 0)),
            pl.BlockSpec((ms, d), lambda i, j: (0, 0)),
            pl.BlockSpec((None, 1, d), lambda i, j: (layer, 0, 0)),
            pl.BlockSpec((d, tn), lambda i, j: (0, j)),
        ],
        out_specs=[pl.BlockSpec((tm, tn), lambda i, j: (i, j)),
                   pl.BlockSpec((ms, tn), lambda i, j: (0, j))],
        out_shape=[jax.ShapeDtypeStruct((m, n), F32), jax.ShapeDtypeStruct((ms, n), F32)],
        scratch_shapes=[pltpu.VMEM((tm + ms, d), BF16)],
        compiler_params=_cparams(("arbitrary", "arbitrary")),
        name="norm_matmul",
    )(x, xs, gain, w)


def _mix_out(x, o_att, y_rwkv, y_conv, w_ref):
    wa, wr = o_att.shape[1], y_rwkv.shape[1]
    acc = jnp.dot(o_att.astype(BF16), w_ref[0:wa, :], preferred_element_type=F32)
    acc += jnp.dot(y_rwkv.astype(BF16), w_ref[wa:wa + wr, :], preferred_element_type=F32)
    acc += jnp.dot(y_conv.astype(BF16), w_ref[wa + wr:, :], preferred_element_type=F32)
    return x + acc


def _out_proj_body(x_ref, a_ref, r_ref, c_ref, w_ref, o_ref):
    o_ref[...] = _mix_out(x_ref[...], a_ref[...], r_ref[...], c_ref[...], w_ref)


def _out_proj_conv_body(x_ref, a_ref, r_ref, u_ref, gb_ref, gc_ref, pu_ref, pgc_ref, cw_ref, cb_ref, w_ref,
                        o_ref, zl_ref, zc_ref, *, blocks_per_seq):
    tm = u_ref.shape[0]
    first = (pl.program_id(0) % blocks_per_seq) == 0
    z = gc_ref[...] * u_ref[...]
    zc_ref[0:8, :] = jnp.where(first, 0.0, pgc_ref[...] * pu_ref[...])
    zc_ref[8:, :] = z
    yc = cb_ref[...] + cw_ref[0:1, :] * zc_ref[pl.ds(6, tm), :]
    yc = yc + cw_ref[1:2, :] * zc_ref[pl.ds(7, tm), :]
    yc = yc + cw_ref[2:3, :] * z
    zl_ref[...] = zc_ref[pl.ds(tm, 8), :]
    o_ref[...] = _mix_out(x_ref[...], a_ref[...], r_ref[...], gb_ref[...] * yc, w_ref)


def _out_proj_conv(x, o_att, y_rwkv, p, col0, cw, conv_w, conv_b, w, layer, t, tm):
    m, d = x.shape
    assert col0 % LANES == 0 and t % tm == 0
    cur = lambda s: pl.BlockSpec((pl.Element(tm), pl.Element(cw)),
                                 lambda i: (pl.multiple_of(i * tm, tm), col0 + s * cw))
    prev = lambda s: pl.BlockSpec((pl.Element(8), pl.Element(cw)),
                                  lambda i: (pl.multiple_of(jnp.maximum(i * tm - 8, 0), 8), col0 + s * cw))
    return pl.pallas_call(
        functools.partial(_out_proj_conv_body, blocks_per_seq=t // tm),
        grid=(m // tm,),
        in_specs=[
            pl.BlockSpec((tm, d), lambda i: (i, 0)),
            pl.BlockSpec((tm, o_att.shape[1]), lambda i: (i, 0)),
            pl.BlockSpec((tm, y_rwkv.shape[1]), lambda i: (i, 0)),
            cur(0), cur(1), cur(2), prev(0), prev(2),
            pl.BlockSpec((None, CONV_K, cw), lambda i: (layer, 0, 0)),
            pl.BlockSpec((None, 1, cw), lambda i: (layer, 0, 0)),
            pl.BlockSpec((w.shape[0], d), lambda i: (0, 0), pipeline_mode=pl.Buffered(1)),
        ],
        out_specs=[pl.BlockSpec((tm, d), lambda i: (i, 0)), pl.BlockSpec((8, cw), lambda i: (i, 0))],
        out_shape=[jax.ShapeDtypeStruct((m, d), F32), jax.ShapeDtypeStruct((m // tm * 8, cw), F32)],
        scratch_shapes=[pltpu.VMEM((tm + 8, cw), F32)],
        compiler_params=_cparams(("parallel",)),
        name="out_proj_conv",
    )(x, o_att, y_rwkv, p, p, p, p, p, conv_w, conv_b, w)


def _out_proj(x, o_att, y_rwkv, y_conv, w, layer, tm):
    m, d = x.shape
    return pl.pallas_call(
        _out_proj_body,
        grid=(m // tm,),
        in_specs=[
            pl.BlockSpec((tm, d), lambda i: (i, 0)),
            pl.BlockSpec((tm, o_att.shape[1]), lambda i: (i, 0)),
            pl.BlockSpec((tm, y_rwkv.shape[1]), lambda i: (i, 0)),
            pl.BlockSpec((tm, y_conv.shape[1]), lambda i: (i, 0)),
            pl.BlockSpec((w.shape[0], d), lambda i: (0, 0), pipeline_mode=pl.Buffered(1)),
        ],
        out_specs=pl.BlockSpec((tm, d), lambda i: (i, 0)),
        out_shape=jax.ShapeDtypeStruct((m, d), F32),
        compiler_params=_cparams(("parallel",)),
        name="out_proj",
    )(x, o_att, y_rwkv, y_conv, w)


def _pair_headnorm(x, g):
    ms = _segsum64_many([x * x])[0] * (1.0 / HEAD_DIM)
    return x * lax.rsqrt(ms + NORM_EPS) * g


def _attn_body(slopes_ref, q_ref, k_ref, v_ref, qg_ref, kg_ref, *refs, n_carried):
    (o_ref, kt_ref, vt_ref, qs_ref, kn_ref, q4_ref, k4_ref, v4_ref, bias_ref,
     og_ref, lse_ref, og4_ref, lse4_ref) = refs[n_carried:]
    hp = pl.program_id(1)
    t = q_ref.shape[0]
    span = ATT_SPAN
    de = ATT_DEINT
    t4 = t // de
    lo_half = lax.broadcasted_iota(jnp.int32, (1, LANES), 1) < HEAD_DIM

    kn = _pair_headnorm(k_ref[...], kg_ref[...])
    kn_ref[...] = kn
    kt_ref[...] = kn.T
    vt_ref[...] = v_ref[...].T
    qs_ref[...] = _pair_headnorm(q_ref[...], qg_ref[...]) * (HEAD_DIM ** -0.5)
    for r in range(de):
        rows = pl.ds(r, t4, stride=de)
        q4_ref[r] = qs_ref[rows, :]
        k4_ref[r] = kn_ref[rows, :]
        v4_ref[r] = v_ref[rows, :]

    qi = lax.broadcasted_iota(jnp.int32, (span, 2 * span), 0)
    ki = lax.broadcasted_iota(jnp.int32, (span, 2 * span), 1)
    steps = qi + span - ki
    valid = (steps >= 0) & (steps <= span)
    for g, (_, dil) in enumerate(DILATION_GROUPS):
        dist = (steps * dil).astype(F32)
        for h in range(2):
            bias_ref[g, h * span:(h + 1) * span, :] = jnp.where(valid, -(slopes_ref[2 * hp + h] * dist), NEG_BIG)

    def attend(g, qb, kb, vb, has_prev, mxu_sum):
        q2 = jnp.concatenate([jnp.where(lo_half, qb, 0.0), jnp.where(lo_half, 0.0, qb)], axis=0).astype(BF16)
        bias = bias_ref[g] if has_prev else bias_ref[g, :, span:]
        s = lax.dot_general(q2, kb.astype(BF16), _NT, preferred_element_type=F32) + bias
        m = jnp.max(s, axis=-1, keepdims=True)
        p = jnp.exp(s - m)
        if mxu_sum:
            v_ones = jnp.concatenate([vb.astype(BF16), jnp.ones((vb.shape[0], LANES), BF16)], axis=1)
            pv = jnp.dot(p.astype(BF16), v_ones, preferred_element_type=F32)
            l = pv[:, LANES:]
            o = pv[:, 0:LANES] / l
        else:
            l = jnp.sum(p, axis=-1, keepdims=True)
            o = jnp.dot(p.astype(BF16), vb.astype(BF16), preferred_element_type=F32) / l
        lse = m + jnp.log(l)
        return jnp.where(lo_half, o[0:span], o[span:]), jnp.where(lo_half, lse[0:span], lse[span:])

    for g, (window, dil) in enumerate(DILATION_GROUPS):
        assert window // dil == span
        nb = t // (span * dil)

        if dil == 1:
            def unit(b, has_prev):
                rows = pl.ds(b * span, span)
                keys = pl.ds((b - 1) * span, 2 * span) if has_prev else rows
                o, lse = attend(0, qs_ref[rows, :], kn_ref[keys, :], v_ref[keys, :], has_prev, True)
                og_ref[rows, :] = o
                lse_ref[rows, :] = lse

            unit(0, False)
            lax.fori_loop(1, nb, lambda b, c: (unit(b, True), c)[1], 0, unroll=8)
            continue

        assert dil % de == 0
        st = dil // de

        def residue(r, carry, g=g, st=st, nb=nb):
            r4, off = r % de, r // de

            def unit(b, has_prev):
                rows = pl.ds(st * (b * span) + off, span, stride=st)
                keys = pl.ds(st * ((b - 1) * span) + off, 2 * span, stride=st) if has_prev else rows
                o, lse = attend(g, q4_ref[r4, rows, :], k4_ref[r4, keys, :], v4_ref[r4, keys, :], has_prev,
                                nb > 1)
                og4_ref[g - 1, r4, rows, :] = o
                lse4_ref[g - 1, r4, rows, :] = lse

            unit(0, False)
            if nb > 1:
                lax.fori_loop(1, nb, lambda b, c: (unit(b, True), c)[1], 0, unroll=3)
            return carry

        lax.fori_loop(0, dil, residue, 0, unroll=16 if nb == 1 else 4)

    for r in range(de):
        rows = pl.ds(r, t4, stride=de)
        l0, l1, l2 = lse_ref[rows, :], lse4_ref[0, r], lse4_ref[1, r]
        mx = jnp.maximum(jnp.maximum(l0, l1), l2)
        w0, w1, w2 = jnp.exp(l0 - mx), jnp.exp(l1 - mx), jnp.exp(l2 - mx)
        og_ref[rows, :] = (og_ref[rows, :] * w0 + og4_ref[0, r] * w1 + og4_ref[1, r] * w2) / (w0 + w1 + w2)
    o_ref[...] = og_ref[...].astype(o_ref.dtype)


def _attention_prompt(p, col0, att_w, qg, kg, slopes, layer, depth, n_seq, t, kv_stack):
    m = p.shape[0]
    pairs = att_w // LANES
    assert col0 % LANES == 0
    blk = lambda off: pl.BlockSpec((t, LANES), lambda n, hp: (n, off + hp))
    c0 = col0 // LANES
    gain = pl.BlockSpec((None, 1, LANES), lambda n, hp: (layer, 0, 0))
    tr_spec = pl.BlockSpec((None, None, LANES, t), lambda n, hp: (layer, n, hp, 0))
    tr_shape = jax.ShapeDtypeStruct((depth, n_seq, att_w, t), F32)
    carried = () if kv_stack is None else tuple(kv_stack)
    n_in = 6
    assert t % (ATT_DEINT * ATT_SPAN) == 0
    tok = pltpu.VMEM((t, LANES), F32)
    deint = pltpu.VMEM((ATT_DEINT, t // ATT_DEINT, LANES), F32)
    dilated = pltpu.VMEM((len(DILATION_GROUPS) - 1, ATT_DEINT, t // ATT_DEINT, LANES), F32)
    return pl.pallas_call(
        functools.partial(_attn_body, n_carried=len(carried)),
        grid=(n_seq, pairs),
        in_specs=[pl.BlockSpec(memory_space=pltpu.SMEM),
                  blk(c0), blk(c0 + pairs), blk(c0 + 2 * pairs), gain, gain]
                 + [pl.BlockSpec(memory_space=pl.ANY)] * len(carried),
        out_specs=[blk(0), tr_spec, tr_spec],
        out_shape=[jax.ShapeDtypeStruct((m, att_w), BF16), tr_shape, tr_shape],
        input_output_aliases={n_in + i: 1 + i for i in range(len(carried))},
        scratch_shapes=[tok, tok, deint, deint, deint,
                        pltpu.VMEM((len(DILATION_GROUPS), 2 * ATT_SPAN, 2 * ATT_SPAN), F32),
                        tok, tok, dilated, dilated],
        compiler_params=_cparams(("parallel", "arbitrary")),
        name="attn_prompt",
    )(slopes, p, p, p, qg, kg, *carried)


def _softplus(z):
    return jnp.maximum(z, 0.0) + jnp.log(1.0 + jnp.exp(-jnp.abs(z)))


def _rwkv_token_math(p_r, p_k, p_v, p_l, q_r, q_k, q_v, q_l, mu_r, mu_k, mu_v, mu_l,
                     w0, w2p, a0, a2p, g2p, k_k, k_a, gate_dot=_dot3):
    xr = p_r + (q_r - p_r) * mu_r
    xk = p_k + (q_k - p_k) * mu_k
    xv = p_v + (q_v - p_v) * mu_v
    xl = p_l + (q_l - p_l) * mu_l
    w_log = -_softplus(-(w0 + _dot3(jnp.tanh(xl), w2p))) - 0.5
    log_decay = -jnp.exp(w_log)
    sig_l = jax.nn.sigmoid(xl)
    a_gate = jax.nn.sigmoid(a0 + gate_dot(xl, a2p))
    g = gate_dot(sig_l, g2p)
    kk = xk * k_k
    kk = kk * lax.rsqrt(jnp.maximum(_segsum64_many([kk * kk])[0], 1e-24))
    kmod = xk * (1.0 + (a_gate - 1.0) * k_a)
    return xr, log_decay, kmod, xv, -kk, kk * a_gate, g


def _segsum64_many(xs):
    rows, width = xs[0].shape
    if rows % 16 != 0:
        return [_segsum64(x) for x in xs]
    nsl = width // LANES
    pieces = [part[:, i * LANES:(i + 1) * LANES]
              for x in xs for part in _split2(x) for i in range(nsl)]
    res = jnp.dot(jnp.concatenate(pieces, axis=0), _head_blockdiag(LANES, BF16), preferred_element_type=F32)
    outs = []
    for n in range(len(xs)):
        hi, lo = [jnp.concatenate([res[((2 * n + j) * nsl + i) * rows:((2 * n + j) * nsl + i + 1) * rows]
                                   for i in range(nsl)], axis=1) for j in range(2)]
        outs.append(hi + lo)
    return outs


def _rwkv_post(y, r, k, v, g, ln_w, ln_b, r_k):
    sum_y, sum_rk = _segsum64_many([y, r * k * r_k])
    yc = y - sum_y * (1.0 / HEAD_DIM)
    var_y = _segsum64_many([yc * yc])[0] * (1.0 / HEAD_DIM)
    yn = yc * lax.rsqrt(var_y + RWKV_GN_EPS) * ln_w + ln_b
    return (yn + sum_rk * v) * g


def _scan_chunk(r, lw, cum, k, v, a, b, s0):
    c = r.shape[1]
    c2 = 2 * c
    lane = lax.broadcasted_iota(jnp.int32, (1, 1, LANES), 2)
    head0 = lane < HEAD_DIM
    m0 = head0.astype(F32)
    m1 = 1.0 - m0
    stack = lambda x: jnp.concatenate([x * m0, x * m1], axis=1)
    twice = lambda x: jnp.concatenate([x, x], axis=1)
    pick = lambda x_st: jnp.where(head0, x_st[:, 0:c], x_st[:, c:])

    cum_prev = cum - lw
    cum_last = cum[:, c - 1:c, :]
    e_pos = jnp.exp(cum)
    e_neg = jnp.exp(-cum)
    e_rem = jnp.exp(cum_last - cum)
    at = a * jnp.exp(cum_prev)
    rt = r * e_pos
    bt = b * e_neg
    kt = k * e_neg
    bh = b * e_rem
    kh = k * e_rem

    big = _dot1(jnp.concatenate([stack(at), stack(rt)], axis=1),
                jnp.concatenate([twice(bt), twice(kt)], axis=1), _BNT)
    row = lax.broadcasted_iota(jnp.int32, (1, c2, c2), 1)
    col = lax.broadcasted_iota(jnp.int32, (1, c2, c2), 2)
    assert c == HEAD_DIM
    same_head = (row >> HEAD_SHIFT) == (col >> HEAD_SHIFT)
    tr, tc = row & (c - 1), col & (c - 1)
    strict = same_head & (tr > tc)
    incl = same_head & (tr >= tc)
    lmat = jnp.where(strict, big[:, 0:c2, 0:c2], 0.0)
    ak = jnp.where(strict, big[:, 0:c2, c2:], 0.0)
    rb = jnp.where(incl, big[:, c2:, 0:c2], 0.0)
    rk = jnp.where(incl, big[:, c2:, c2:], 0.0)

    eye = (row == col).astype(F32)
    x = eye + jnp.where((row >> 1) == (col >> 1), lmat, 0.0)
    s, log2s = 2, 1
    while s < c:
        joins = ((row >> (log2s + 1)) == (col >> (log2s + 1))) & ((row & s) != 0) & ((col & s) == 0)
        x = x + _dot1(_dot1(x, jnp.where(joins, lmat, 0.0), _BNN), x, _BNN)
        s, log2s = 2 * s, log2s + 1

    v_st = twice(v)
    w = _dot1(jnp.concatenate([at, rt], axis=1), s0, _BNT)
    u_st = _dot1(x, twice(w[:, 0:c]) + _dot1(ak, v_st, _BNN), _BNN)
    y_st = _dot1(jnp.concatenate([rb, rk], axis=2), jnp.concatenate([u_st, v_st], axis=1), _BNN)
    y = w[:, c:] + pick(y_st)
    u = pick(u_st)
    upd = _dot1(jnp.concatenate([u, v], axis=1), jnp.concatenate([bh, kh], axis=1), _BTN)
    s_new = s0 * jnp.exp(cum_last) + _head_blockdiag(LANES, F32)[None] * upd
    return y, s_new


def _rwkv_mix_body(pr_ref, pk_ref, pv_ref, plo_ref,
                   mur_ref, muk_ref, muv_ref, mul_ref, w0_ref, w2_ref, a0_ref, a2_ref, g2_ref,
                   kk_ref, ka_ref, lnw_ref, lnb_ref, rk_ref,
                   y_ref, s_out_ref, s_ref, sr_ref, sk_ref, sv_ref, sl_ref):
    ch = pl.program_id(1)
    nseq, c, rw = pr_ref.shape
    npair = rw // LANES
    shift_refs = (sr_ref, sk_ref, sv_ref, sl_ref)

    @pl.when(ch == 0)
    def _():
        s_ref[...] = jnp.zeros_like(s_ref)
        for ref in shift_refs:
            ref[:, 0:8, :] = jnp.zeros((nseq, 8, ref.shape[2]), F32)

    def cur_and_prev(cur_ref, sh_ref):
        cur = cur_ref[...]
        sh_ref[:, 8:, :] = cur
        prev = sh_ref[:, pl.ds(7, c), :]
        sh_ref[:, 7:8, :] = cur[:, c - 1:c, :]
        flat = lambda x: x.reshape(nseq * c, x.shape[2])
        return flat(cur), flat(prev)

    (p_r, q_r), (p_k, q_k), (p_v, q_v), (p_l, q_l) = [
        cur_and_prev(cur, sh) for cur, sh in zip((pr_ref, pk_ref, pv_ref, plo_ref), shift_refs)]
    r, lw, k, v, a, b, g = _rwkv_token_math(
        p_r, p_k, p_v, p_l, q_r, q_k, q_v, q_l,
        mur_ref[...], muk_ref[...], muv_ref[...], mul_ref[...],
        w0_ref[...], w2_ref[...], a0_ref[...], a2_ref[...], g2_ref[...], kk_ref[...], ka_ref[...],
        gate_dot=_dot1)

    pairs = lambda x: jnp.stack([x[s * c:(s + 1) * c, p * LANES:(p + 1) * LANES]
                                 for s in range(nseq) for p in range(npair)], axis=0)
    tri_incl = (lax.broadcasted_iota(jnp.int32, (c, c), 0)
                >= lax.broadcasted_iota(jnp.int32, (c, c), 1)).astype(BF16)
    cum = jnp.concatenate([_cumsum_rows(lw[s * c:(s + 1) * c], tri_incl) for s in range(nseq)], axis=0)
    y, s_new = _scan_chunk(pairs(r), pairs(lw), pairs(cum), pairs(k), pairs(v), pairs(a), pairs(b), s_ref[...])
    s_ref[...] = s_new
    y = jnp.concatenate([jnp.concatenate([y[s * npair + p] for p in range(npair)], axis=1)
                         for s in range(nseq)], axis=0)
    out = _rwkv_post(y, r, k, v, g, lnw_ref[...], lnb_ref[...], rk_ref[...])
    y_ref[...] = out.reshape(nseq, c, rw).astype(y_ref.dtype)

    @pl.when(ch == pl.num_programs(1) - 1)
    def _():
        for s in range(nseq):
            for p in range(npair):
                s_out_ref[s, 2 * p] = s_ref[s * npair + p, 0:HEAD_DIM, 0:HEAD_DIM]
                s_out_ref[s, 2 * p + 1] = s_ref[s * npair + p, HEAD_DIM:, HEAD_DIM:]


def _rwkv_mix_prompt(p, col0, lp, layer, n_seq, t, seq_per_step):
    m = p.shape[0]
    rw = lp["rwkv_w"]
    c = RWKV_CHUNK
    heads = rw // HEAD_DIM
    assert n_seq % seq_per_step == 0 and col0 % rw == 0 and (col0 + 3 * rw) % LORA_W == 0
    c_slab = col0 // rw
    c_lora = (col0 + 3 * rw) // LORA_W
    tok = lambda w, col: pl.BlockSpec((seq_per_step, c, w), lambda n, ch: (n, ch, col))
    vec = lambda w, col: pl.BlockSpec((None, 1, w), lambda n, ch: (layer, 0, col))
    mat = pl.BlockSpec((None, LORA_W, rw), lambda n, ch: (layer, 0, 0))
    p3 = p.reshape(n_seq, t, p.shape[1])
    y, wkv = pl.pallas_call(
        _rwkv_mix_body,
        grid=(n_seq // seq_per_step, t // c),
        in_specs=[tok(rw, c_slab), tok(rw, c_slab + 1), tok(rw, c_slab + 2), tok(LORA_W, c_lora),
                  vec(rw, 0), vec(rw, 1), vec(rw, 2), vec(LORA_W, (3 * rw) // LORA_W),
                  vec(rw, 0), mat, vec(rw, 0), mat, mat, vec(rw, 0), vec(rw, 0),
                  vec(rw, 0), vec(rw, 0), vec(rw, 0)],
        out_specs=[tok(rw, 0),
                   pl.BlockSpec((seq_per_step, heads, HEAD_DIM, HEAD_DIM), lambda n, ch: (n, 0, 0, 0))],
        out_shape=[jax.ShapeDtypeStruct((n_seq, t, rw), BF16),
                   jax.ShapeDtypeStruct((n_seq, heads, HEAD_DIM, HEAD_DIM), F32)],
        scratch_shapes=[pltpu.VMEM((seq_per_step * (rw // LANES), LANES, LANES), F32)]
                       + [pltpu.VMEM((seq_per_step, c + 8, rw), F32)] * 3
                       + [pltpu.VMEM((seq_per_step, c + 8, LORA_W), F32)],
        compiler_params=_cparams(("parallel", "arbitrary")),
        name="rwkv_mix",
    )(p3, p3, p3, p3, lp["mu"], lp["mu"], lp["mu"], lp["mu"],
      lp["w0"], lp["w2p"], lp["a0"], lp["a2p"], lp["g2p"], lp["k_k"], lp["k_a"],
      lp["ln_w"], lp["ln_b"], lp["r_k"])
    return y.reshape(m, rw), wkv


def _sample_body(slope_ref, p_ref, kc_ref, vc_ref,
                 wkv_ref, shift_ref, cst_ref, qg_ref, kg_ref,
                 mu_ref, w0_ref, w2_ref, a0_ref, a2_ref, g2_ref, kk_ref, ka_ref,
                 rk_ref, lnw_ref, lnb_ref, cw_ref, cb_ref,
                 oatt_ref, yr_ref, yc_ref, kn_ref, vn_ref, wkvn_ref, cstn_ref,
                 *, col_att, col_rwkv, col_conv):
    heads_a = oatt_ref.shape[0]
    aw = heads_a * HEAD_DIM
    rw = yr_ref.shape[1]
    cw = yc_ref.shape[1]
    rp = shift_ref.shape[1]
    rows8 = lambda x: jnp.broadcast_to(x, (8, x.shape[1]))
    p_all = p_ref[...]

    by_head = lambda row: jnp.concatenate(
        [row[:, h * HEAD_DIM:(h + 1) * HEAD_DIM] for h in range(heads_a)], axis=0)
    ca = col_att
    q, k, v_new = (by_head(p_all[:, ca:ca + aw]), by_head(p_all[:, ca + aw:ca + 2 * aw]),
                   by_head(p_all[:, ca + 2 * aw:ca + 3 * aw]))
    qn = _rms(q, qg_ref[...]) * (HEAD_DIM ** -0.5)
    kn = _rms(k, kg_ref[...])
    kn_ref[...] = kn
    vn_ref[...] = v_new
    s_new = jnp.sum(kn * qn, axis=-1, keepdims=True)
    slope = slope_ref[...]
    past = kc_ref.shape[2]
    er = lax.broadcasted_iota(jnp.int32, (HEAD_DIM, HEAD_DIM), 0)
    ec = lax.broadcasted_iota(jnp.int32, (HEAD_DIM, HEAD_DIM), 1)
    eye = er == ec
    q_col = jnp.stack([jnp.sum(jnp.where(eye, qn[h:h + 1, :], 0.0), axis=1, keepdims=True)
                       for h in range(heads_a)], axis=0)
    dist = past - lax.broadcasted_iota(jnp.int32, (1, past), 1)
    biased = jnp.sum(kc_ref[...] * q_col, axis=1) - slope * dist.astype(F32)
    parts = []
    for window, dil in DILATION_GROUPS:
        assert dil & (dil - 1) == 0
        valid = ((dist & (dil - 1)) == 0) & (dist <= window)
        s = jnp.where(valid, biased, NEG_BIG)
        m = jnp.maximum(jnp.max(s, axis=1, keepdims=True), s_new)
        p = jnp.exp(s - m)
        p_new = jnp.exp(s_new - m)
        l = jnp.sum(p, axis=1, keepdims=True) + p_new
        parts.append((p, p_new, l, m + jnp.log(l)))
    mx = jnp.maximum(jnp.maximum(parts[0][3], parts[1][3]), parts[2][3])
    ws = [jnp.exp(lse - mx) for _, _, _, lse in parts]
    wsum = ws[0] + ws[1] + ws[2]
    scale_g = [w / (l * wsum) for w, (_, _, l, _) in zip(ws, parts)]
    coef = parts[0][0] * scale_g[0] + parts[1][0] * scale_g[1] + parts[2][0] * scale_g[2]
    coef_new = parts[0][1] * scale_g[0] + parts[1][1] * scale_g[1] + parts[2][1] * scale_g[2]
    o_col = jnp.sum(vc_ref[...] * coef[:, None, :], axis=2, keepdims=True)
    o_rows = jnp.concatenate([jnp.sum(jnp.where(eye, o_col[h], 0.0), axis=0, keepdims=True)
                              for h in range(heads_a)], axis=0)
    oatt_ref[...] = o_rows + coef_new * v_new

    pr = rows8(p_all[:, col_rwkv:col_rwkv + rp])
    sh = rows8(shift_ref[...])
    mu = mu_ref[...]
    c3 = 3 * rw
    r, lw, k, v, a, b, g = _rwkv_token_math(
        pr[:, 0:rw], pr[:, rw:2 * rw], pr[:, 2 * rw:c3], pr[:, c3:],
        sh[:, 0:rw], sh[:, rw:2 * rw], sh[:, 2 * rw:c3], sh[:, c3:],
        mu[:, 0:rw], mu[:, rw:2 * rw], mu[:, 2 * rw:c3], mu[:, c3:],
        w0_ref[...], w2_ref[...], a0_ref[...], a2_ref[...], g2_ref[...], kk_ref[...], ka_ref[...])
    decay = jnp.exp(lw)
    place_r = lax.broadcasted_iota(jnp.int32, (HEAD_DIM, rw), 0)
    place_c = lax.broadcasted_iota(jnp.int32, (HEAD_DIM, rw), 1)
    y_full = jnp.zeros((1, rw), F32)
    for h in range(rw // HEAD_DIM):
        hs = slice(h * HEAD_DIM, (h + 1) * HEAD_DIM)
        s_old = wkv_ref[h]
        sa = jnp.sum(s_old * a[0:1, hs], axis=1, keepdims=True)
        v_col = jnp.sum(jnp.where(eye, v[0:1, hs], 0.0), axis=1, keepdims=True)
        s_h = s_old * decay[0:1, hs] + sa * b[0:1, hs] + v_col * k[0:1, hs]
        wkvn_ref[h] = s_h
        y_col = jnp.sum(s_h * r[0:1, hs], axis=1, keepdims=True)
        y_full = y_full + jnp.sum(jnp.where(place_c == place_r + h * HEAD_DIM, y_col, 0.0),
                                  axis=0, keepdims=True)
    y8 = rows8(y_full)
    yr_ref[...] = _rwkv_post(y8, r, k, v, g, lnw_ref[...], lnb_ref[...], rk_ref[...])[0:1]

    c1 = col_conv
    z = p_all[:, c1 + 2 * cw:c1 + 3 * cw] * p_all[:, c1:c1 + cw]
    yc = cb_ref[...] + cw_ref[0:1, :] * cst_ref[0:1, :]
    yc = yc + cw_ref[1:2, :] * cst_ref[1:2, :]
    yc = yc + cw_ref[2:3, :] * z
    yc_ref[...] = p_all[:, c1 + cw:c1 + 2 * cw] * yc
    cstn_ref[0:1, :] = cst_ref[1:2, :]
    cstn_ref[1:2, :] = z


def _sample_mixers(p, cols, cache_k, cache_v, state_wkv, state_shift, state_conv, lp, slope_col, layer):
    nb = p.shape[0]
    depth, _, past, heads_a, _ = cache_k.shape
    rw = lp["rwkv_w"]
    rp = state_shift.shape[2]
    cw = state_conv.shape[3]
    heads = rw // HEAD_DIM

    row = lambda w: pl.BlockSpec((None, 1, w), lambda n: (n, 0, 0))
    lvec = lambda w: pl.BlockSpec((None, 1, w), lambda n: (layer, 0, 0))
    lmat = lambda r_, w: pl.BlockSpec((None, r_, w), lambda n: (layer, 0, 0))
    tile = pl.BlockSpec((None, heads_a, HEAD_DIM), lambda n: (n, 0, 0))

    assert past >= DILATION_GROUPS[-1][0]
    views = [jnp.transpose(c, (0, 1, 3, 4, 2)) for c in (cache_k, cache_v)]
    cache_specs = [pl.BlockSpec((None, None, heads_a, HEAD_DIM, past), lambda n: (layer, n, 0, 0, 0))] * 2

    outs = pl.pallas_call(
        functools.partial(_sample_body, col_att=cols[0], col_rwkv=cols[1], col_conv=cols[2]),
        grid=(nb,),
        in_specs=[pl.BlockSpec((heads_a, 1), lambda n: (0, 0)), row(p.shape[1])] + cache_specs + [
                  pl.BlockSpec((None, None, heads, HEAD_DIM, HEAD_DIM), lambda n: (layer, n, 0, 0, 0)),
                  pl.BlockSpec((None, None, 1, rp), lambda n: (layer, n, 0, 0)),
                  pl.BlockSpec((None, None, CONV_K - 1, cw), lambda n: (layer, n, 0, 0)),
                  lvec(HEAD_DIM), lvec(HEAD_DIM),
                  lvec(rp), lvec(rw), lmat(LORA_W, rw), lvec(rw), lmat(LORA_W, rw),
                  lmat(LORA_W, rw), lvec(rw), lvec(rw), lvec(rw), lvec(rw), lvec(rw),
                  lmat(CONV_K, cw), lvec(cw)],
        out_specs=[tile, row(rw), row(cw), tile, tile,
                   pl.BlockSpec((None, heads, HEAD_DIM, HEAD_DIM), lambda n: (n, 0, 0, 0)),
                   pl.BlockSpec((None, CONV_K - 1, cw), lambda n: (n, 0, 0))],
        out_shape=[jax.ShapeDtypeStruct((nb, heads_a, HEAD_DIM), F32), jax.ShapeDtypeStruct((nb, 1, rw), F32),
                   jax.ShapeDtypeStruct((nb, 1, cw), F32),
                   jax.ShapeDtypeStruct((nb, heads_a, HEAD_DIM), F32),
                   jax.ShapeDtypeStruct((nb, heads_a, HEAD_DIM), F32),
                   jax.ShapeDtypeStruct((nb, heads, HEAD_DIM, HEAD_DIM), F32),
                   jax.ShapeDtypeStruct((nb, CONV_K - 1, cw), F32)],
        compiler_params=_cparams(("parallel",)),
        name="sample_mixers",
    )(slope_col, p[:, None, :], *views,
      state_wkv, state_shift[:, :, None, :], state_conv,
      lp["q_gain"], lp["k_gain"],
      lp["mu"], lp["w0"], lp["w2p"], lp["a0"], lp["a2p"], lp["g2p"], lp["k_k"], lp["k_a"],
      lp["r_k"], lp["ln_w"], lp["ln_b"], lp["conv_w"], lp["conv_b"])
    o_att, y_rwkv, y_conv, k_new, v_new, wkv_new, conv_new = outs
    return o_att.reshape(nb, heads_a * HEAD_DIM), y_rwkv[:, 0], y_conv[:, 0], k_new, v_new, wkv_new, conv_new


def kernel(x_prompt, x_sample, cache_swa_k, cache_swa_v, state_wkv, state_shift, state_conv, ffn1_norm, ffn1_w_gate, ffn1_w_up, ffn1_w_down, mix_norm, w_in, q_norm, k_norm, rwkv_mu, rwkv_w0, rwkv_w2, rwkv_a0, rwkv_a2, rwkv_g2, rwkv_k_k, rwkv_k_a, rwkv_r_k, rwkv_ln_w, rwkv_ln_b, conv_w, conv_b, w_out, ffn2_norm, ffn2_w_gate, ffn2_w_up, ffn2_w_down):
    n_seq, t, d = x_prompt.shape
    nb = x_sample.shape[0]
    assert x_sample.shape[1] == 1
    depth = w_in.shape[0]
    att_heads, head_dim = cache_swa_k.shape[3], cache_swa_k.shape[4]
    assert head_dim == HEAD_DIM
    aw = att_heads * HEAD_DIM
    rw = rwkv_w0.shape[1]
    cw = conv_b.shape[1]
    rp = rwkv_mu.shape[1]
    assert rp == 3 * rw + LORA_W and w_in.shape[2] == 3 * aw + rp + 3 * cw

    row3 = lambda p: p.reshape(depth, 1, -1)
    col_att, col_rwkv, col_conv = 0, 3 * aw, 3 * aw + rp
    zpad = lambda w, before: jnp.pad(w, ((0, 0), (before, LORA_W - before - w.shape[1]), (0, 0)))
    lp = {
        "rwkv_w": rw,
        "mu": row3(rwkv_mu), "w0": row3(rwkv_w0), "a0": row3(rwkv_a0),
        "w2p": zpad(rwkv_w2, 0), "a2p": zpad(rwkv_a2, W_LORA), "g2p": zpad(rwkv_g2, W_LORA + A_LORA),
        "k_k": row3(rwkv_k_k), "k_a": row3(rwkv_k_a), "r_k": row3(rwkv_r_k),
        "ln_w": row3(rwkv_ln_w), "ln_b": row3(rwkv_ln_b),
        "conv_w": conv_w, "conv_b": row3(conv_b),
        "q_gain": row3(q_norm), "k_gain": row3(k_norm),
        "q_gain_pair": row3(jnp.tile(q_norm, (1, LANES // HEAD_DIM))),
        "k_gain_pair": row3(jnp.tile(k_norm, (1, LANES // HEAD_DIM))),
    }
    g_ffn1, g_mix, g_ffn2 = row3(ffn1_norm), row3(mix_norm), row3(ffn2_norm)
    slopes = 2.0 ** (-8.0 * jnp.arange(1, att_heads + 1, dtype=F32) / att_heads)
    slope_col = slopes[:, None]

    m = n_seq * t
    xp = x_prompt.reshape(m, d)
    xs = x_sample.reshape(nb, d)
    tm_ffn, tf = 1024, 256
    tm_in, tn_in = 1024, 1280
    tm_out = 512
    p_states, s_states = [], []
    kv_stack = None
    for l in range(depth):
        xp, xs, w_in16, w_out16 = _ffn(xp, xs, g_ffn1, ffn1_w_gate, ffn1_w_up, ffn1_w_down, l, tm_ffn, tf,
                                       cast=(w_in, w_out))
        pp, ps = _norm_matmul(xp, xs, g_mix, w_in16, l, tm_in, tn_in)

        o_att, *kv_stack = _attention_prompt(pp, col_att, aw, lp["q_gain_pair"], lp["k_gain_pair"], slopes,
                                             l, depth, n_seq, t, kv_stack)
        y_rwkv, wkv_p = _rwkv_mix_prompt(pp, col_rwkv, lp, l, n_seq, t, 4 if n_seq % 4 == 0 else 1)
        xp, z_last = _out_proj_conv(xp, o_att, y_rwkv, pp, col_conv, cw, lp["conv_w"], lp["conv_b"], w_out16,
                                    l, t, tm_out)
        p_states.append((
            wkv_p, pp.reshape(n_seq, t, -1)[:, -1, col_rwkv:col_rwkv + rp],
            z_last.reshape(n_seq, t // tm_out, 8, cw)[:, -1, 8 - (CONV_K - 1):]))

        so, sy, syc, sk, sv, swkv, sconv = _sample_mixers(
            ps, (col_att, col_rwkv, col_conv), cache_swa_k, cache_swa_v, state_wkv, state_shift, state_conv,
            lp, slope_col, l)
        xs = _out_proj(xs, so, sy, syc, w_out16, l, nb)
        s_states.append((sk[:, None], sv[:, None], swkv, ps[:, col_rwkv:col_rwkv + rp], sconv))

        xp, xs = _ffn(xp, xs, g_ffn2, ffn2_w_gate, ffn2_w_up, ffn2_w_down, l, tm_ffn, tf)

    stacked = lambda states, i: jnp.stack([st[i] for st in states], axis=0)
    rows = lambda x_t: jnp.transpose(x_t.reshape(depth, n_seq, att_heads, HEAD_DIM, t), (0, 1, 4, 2, 3))
    return (xp.reshape(n_seq, t, d), xs.reshape(nb, 1, d),
            rows(kv_stack[0]), rows(kv_stack[1]),
            stacked(p_states, 0), stacked(p_states, 1), stacked(p_states, 2),
            stacked(s_states, 0), stacked(s_states, 1), stacked(s_states, 2), stacked(s_states, 3),
            stacked(s_states, 4))
```

```python
import functools

import jax
import jax.numpy as jnp
from jax import lax
from jax.experimental import pallas as pl
from jax.experimental.pallas import tpu as pltpu

F32 = jnp.float32
BF16 = jnp.bfloat16

HEAD_DIM = 64
HEAD_SHIFT = 6
LANES = 128
NORM_EPS = 1e-6
RWKV_GN_EPS = 64e-5
DILATION_GROUPS = ((128, 1), (512, 4), (2048, 16))
ATT_SPAN = 128
ATT_DEINT = 4
W_LORA, A_LORA, G_LORA = 64, 64, 128
LORA_W = W_LORA + A_LORA + G_LORA
CONV_K = 3
RWKV_CHUNK = 64
VMEM_LIMIT = 56 * 1024 * 1024
NEG_BIG = -1e30

_NT = (((1,), (1,)), ((), ()))
_NN = (((1,), (0,)), ((), ()))
_BNN = (((2,), (1,)), ((0,), (0,)))
_BNT = (((2,), (2,)), ((0,), (0,)))
_BTN = (((1,), (1,)), ((0,), (0,)))


def _cparams(sem):
    return pltpu.CompilerParams(dimension_semantics=sem, vmem_limit_bytes=VMEM_LIMIT)


def _rms(x, g):
    return x * lax.rsqrt(jnp.mean(x * x, axis=-1, keepdims=True) + NORM_EPS) * g


def _split2(x):
    hi = x.astype(BF16)
    lo = (x - hi.astype(F32)).astype(BF16)
    return hi, lo


def _dot3(a, b, dims=_NN):
    ah, al = _split2(a)
    bh, bl = _split2(b)
    d = lambda x, y: lax.dot_general(x, y, dims, preferred_element_type=F32)
    return d(ah, bh) + (d(al, bh) + d(ah, bl))


def _dot1(a, b, dims=_NN):
    return lax.dot_general(a.astype(BF16), b.astype(BF16), dims, preferred_element_type=F32)


def _split3(x):
    h1 = x.astype(BF16)
    r1 = x - h1.astype(F32)
    h2 = r1.astype(BF16)
    h3 = (r1 - h2.astype(F32)).astype(BF16)
    return h1, h2, h3


def _dot_sel(x, sel):
    return sum(jnp.dot(h, sel, preferred_element_type=F32) for h in _split3(x))


def _cumsum_rows(x, tri_incl):
    return sum(jnp.dot(tri_incl, h, preferred_element_type=F32) for h in _split2(x))


def _head_blockdiag(n, dtype):
    r = lax.broadcasted_iota(jnp.int32, (n, n), 0) >> HEAD_SHIFT
    c = lax.broadcasted_iota(jnp.int32, (n, n), 1) >> HEAD_SHIFT
    return (r == c).astype(dtype)


def _segsum64(x):
    bd = _head_blockdiag(LANES, BF16)
    parts = [_dot_sel(x[:, i:i + LANES], bd) for i in range(0, x.shape[1], LANES)]
    return parts[0] if len(parts) == 1 else jnp.concatenate(parts, axis=1)


BF16_ROWS = 16


def _ffn_body(x_ref, xs_ref, g_ref, wg_ref, wu_ref, wd_ref, *refs, n_cast, n_zero):
    cast_in = refs[:n_cast]
    o_ref, os_ref = refs[n_cast:n_cast + 2]
    cast_out = refs[n_cast + 2:2 * n_cast + 2]
    zero_out = refs[2 * n_cast + 2:2 * n_cast + 2 + n_zero]
    xn_ref = refs[2 * n_cast + 2 + n_zero]
    j = pl.program_id(1)
    tm, d = x_ref.shape

    @pl.when(j == 0)
    def _():
        xn_ref[0:tm, :] = _rms(x_ref[...], g_ref[...]).astype(BF16)
        xn_ref[tm:, :] = _rms(xs_ref[...], g_ref[...]).astype(BF16)
        o_ref[...] = jnp.zeros_like(o_ref)
        os_ref[...] = jnp.zeros_like(os_ref)

    for src, dst in zip(cast_in, cast_out):
        dst[...] = src[...].astype(BF16)
    for dst in zero_out:
        dst[...] = jnp.zeros_like(dst)

    xn = xn_ref[...]
    h = jnp.dot(xn, wg_ref[...].astype(BF16), preferred_element_type=F32)
    u = jnp.dot(xn, wu_ref[...].astype(BF16), preferred_element_type=F32)
    act = (h * jax.nn.sigmoid(h) * u).astype(BF16)
    half = d // 2
    for c in range(2):
        cols = slice(c * half, (c + 1) * half)
        part = jnp.dot(act, wd_ref[:, cols].astype(BF16), preferred_element_type=F32)
        o_ref[:, cols] += part[0:tm]
        os_ref[:, cols] += part[tm:]

    @pl.when(j == pl.num_programs(1) - 1)
    def _():
        o_ref[...] = x_ref[...] + 0.5 * o_ref[...]
        os_ref[...] = xs_ref[...] + 0.5 * os_ref[...]


def _ffn(x, xs, gain, wg, wu, wd, layer, tm, tf, cast=(), zeros=()):
    m, d = x.shape
    ms = xs.shape[0]
    f = wg.shape[2]
    ni, nj = m // tm, f // tf

    def row_chunks(rows):
        cr = next(r for r in range(BF16_ROWS, rows + 1, BF16_ROWS) if rows % r == 0 and rows // r <= ni * nj)
        return cr, lambda i, j: jnp.minimum(i * nj + j, rows // cr - 1)

    cast_in, cast_out, cast_shapes = [], [], []
    for w in cast:
        cr, chunk = row_chunks(w.shape[1])
        cast_in.append(pl.BlockSpec((None, cr, w.shape[2]), lambda i, j, chunk=chunk: (layer, chunk(i, j), 0)))
        cast_out.append(pl.BlockSpec((cr, w.shape[2]), lambda i, j, chunk=chunk: (chunk(i, j), 0)))
        cast_shapes.append(jax.ShapeDtypeStruct(w.shape[1:], BF16))
    for rows, cols in zeros:
        cr, chunk = row_chunks(rows)
        cast_out.append(pl.BlockSpec((cr, cols), lambda i, j, chunk=chunk: (chunk(i, j), 0)))
        cast_shapes.append(jax.ShapeDtypeStruct((rows, cols), F32))
    return pl.pallas_call(
        functools.partial(_ffn_body, n_cast=len(cast), n_zero=len(zeros)),
        grid=(ni, nj),
        in_specs=[
            pl.BlockSpec((tm, d), lambda i, j: (i, 0)),
            pl.BlockSpec((ms, d), lambda i, j: (0, 0)),
            pl.BlockSpec((None, 1, d), lambda i, j: (layer, 0, 0)),
            pl.BlockSpec((None, d, tf), lambda i, j: (layer, 0, j)),
            pl.BlockSpec((None, d, tf), lambda i, j: (layer, 0, j)),
            pl.BlockSpec((None, tf, d), lambda i, j: (layer, j, 0)),
        ] + cast_in,
        out_specs=[pl.BlockSpec((tm, d), lambda i, j: (i, 0)),
                   pl.BlockSpec((ms, d), lambda i, j: (0, 0))] + cast_out,
        out_shape=[jax.ShapeDtypeStruct((m, d), F32), jax.ShapeDtypeStruct((ms, d), F32)] + cast_shapes,
        scratch_shapes=[pltpu.VMEM((tm + ms, d), BF16)],
        compiler_params=_cparams(("arbitrary", "arbitrary")),
        name="ffn",
    )(x, xs, gain, wg, wu, wd, *cast)


def _norm_matmul_body(x_ref, xs_ref, g_ref, w_ref, o_ref, os_ref, xn_ref):
    tm = x_ref.shape[0]

    @pl.when(pl.program_id(1) == 0)
    def _():
        xn_ref[0:tm, :] = _rms(x_ref[...], g_ref[...]).astype(BF16)
        xn_ref[tm:, :] = _rms(xs_ref[...], g_ref[...]).astype(BF16)

    out = jnp.dot(xn_ref[...], w_ref[...], preferred_element_type=F32)
    o_ref[...] = out[0:tm]
    os_ref[...] = out[tm:]


def _norm_matmul(x, xs, gain, w, layer, tm, tn):
    m, d = x.shape
    ms = xs.shape[0]
    n = w.shape[1]
    ni = m // tm
    out, out_s = pl.pallas_call(
        _norm_matmul_body,
        grid=(ni, n // tn),
        in_specs=[
            pl.BlockSpec((tm, d), lambda i, j: (i, 0)),
            pl.BlockSpec((ms, d), lambda i, j: (0, 0)),
            pl.BlockSpec((None, 1, d), lambda i, j: (layer, 0, 0)),
            pl.BlockSpec((d, tn), lambda i, j: (0, j)),
        ],
        out_specs=[pl.BlockSpec((tm, tn), lambda i, j: (i, j)),
                   pl.BlockSpec((None, ms, tn), lambda i, j: (i, 0, j))],
        out_shape=[jax.ShapeDtypeStruct((m, n), F32), jax.ShapeDtypeStruct((ni, ms, n), F32)],
        scratch_shapes=[pltpu.VMEM((tm + ms, d), BF16)],
        compiler_params=_cparams(("parallel", "arbitrary")),
        name="norm_matmul",
    )(x, xs, gain, w)
    return out, out_s[0]


def _mix_out(x, o_att, y_rwkv, y_conv, w_ref):
    wa, wr = o_att.shape[1], y_rwkv.shape[1]
    acc = jnp.dot(o_att.astype(BF16), w_ref[0:wa, :], preferred_element_type=F32)
    acc += jnp.dot(y_rwkv.astype(BF16), w_ref[wa:wa + wr, :], preferred_element_type=F32)
    acc += jnp.dot(y_conv.astype(BF16), w_ref[wa + wr:, :], preferred_element_type=F32)
    return x + acc


def _out_proj_body(x_ref, a_ref, r_ref, c_ref, w_ref, o_ref):
    o_ref[...] = _mix_out(x_ref[...], a_ref[...], r_ref[...], c_ref[...], w_ref)


def _out_proj_conv_body(x_ref, a_ref, r_ref, u_ref, gb_ref, gc_ref, pu_ref, pgc_ref, cw_ref, cb_ref, w_ref,
                        o_ref, zl_ref, zc_ref, *, blocks_per_seq):
    tm = u_ref.shape[0]
    first = (pl.program_id(0) % blocks_per_seq) == 0
    z = gc_ref[...] * u_ref[...]
    zc_ref[0:8, :] = jnp.where(first, 0.0, pgc_ref[...] * pu_ref[...])
    zc_ref[8:, :] = z
    yc = cb_ref[...] + cw_ref[0:1, :] * zc_ref[pl.ds(6, tm), :]
    yc = yc + cw_ref[1:2, :] * zc_ref[pl.ds(7, tm), :]
    yc = yc + cw_ref[2:3, :] * z
    zl_ref[...] = zc_ref[pl.ds(tm, 8), :]
    o_ref[...] = _mix_out(x_ref[...], a_ref[...], r_ref[...], gb_ref[...] * yc, w_ref)


def _out_proj_conv(x, o_att, y_rwkv, p, col0, cw, conv_w, conv_b, w, layer, t, tm):
    m, d = x.shape
    assert col0 % LANES == 0 and t % tm == 0
    cur = lambda s: pl.BlockSpec((pl.Element(tm), pl.Element(cw)),
                                 lambda i: (pl.multiple_of(i * tm, tm), col0 + s * cw))
    prev = lambda s: pl.BlockSpec((pl.Element(8), pl.Element(cw)),
                                  lambda i: (pl.multiple_of(jnp.maximum(i * tm - 8, 0), 8), col0 + s * cw))
    return pl.pallas_call(
        functools.partial(_out_proj_conv_body, blocks_per_seq=t // tm),
        grid=(m // tm,),
        in_specs=[
            pl.BlockSpec((tm, d), lambda i: (i, 0)),
            pl.BlockSpec((tm, o_att.shape[1]), lambda i: (i, 0)),
            pl.BlockSpec((tm, y_rwkv.shape[1]), lambda i: (i, 0)),
            cur(0), cur(1), cur(2), prev(0), prev(2),
            pl.BlockSpec((None, CONV_K, cw), lambda i: (layer, 0, 0)),
            pl.BlockSpec((None, 1, cw), lambda i: (layer, 0, 0)),
            pl.BlockSpec((w.shape[0], d), lambda i: (0, 0), pipeline_mode=pl.Buffered(1)),
        ],
        out_specs=[pl.BlockSpec((tm, d), lambda i: (i, 0)), pl.BlockSpec((8, cw), lambda i: (i, 0))],
        out_shape=[jax.ShapeDtypeStruct((m, d), F32), jax.ShapeDtypeStruct((m // tm * 8, cw), F32)],
        scratch_shapes=[pltpu.VMEM((tm + 8, cw), F32)],
        compiler_params=_cparams(("parallel",)),
        name="out_proj_conv",
    )(x, o_att, y_rwkv, p, p, p, p, p, conv_w, conv_b, w)


def _out_proj(x, o_att, y_rwkv, y_conv, w, layer, tm):
    m, d = x.shape
    return pl.pallas_call(
        _out_proj_body,
        grid=(m // tm,),
        in_specs=[
            pl.BlockSpec((tm, d), lambda i: (i, 0)),
            pl.BlockSpec((tm, o_att.shape[1]), lambda i: (i, 0)),
            pl.BlockSpec((tm, y_rwkv.shape[1]), lambda i: (i, 0)),
            pl.BlockSpec((tm, y_conv.shape[1]), lambda i: (i, 0)),
            pl.BlockSpec((w.shape[0], d), lambda i: (0, 0), pipeline_mode=pl.Buffered(1)),
        ],
        out_specs=pl.BlockSpec((tm, d), lambda i: (i, 0)),
        out_shape=jax.ShapeDtypeStruct((m, d), F32),
        compiler_params=_cparams(("parallel",)),
        name="out_proj",
    )(x, o_att, y_rwkv, y_conv, w)


def _pair_headnorm(x, g):
    ms = _segsum64_many([x * x])[0] * (1.0 / HEAD_DIM)
    return x * lax.rsqrt(ms + NORM_EPS) * g


def _attn_body(slopes_ref, q_ref, k_ref, v_ref, qg_ref, kg_ref, *refs, n_carried):
    (o_ref, kt_ref, vt_ref, qs_ref, kn_ref, q4_ref, k4_ref, v4_ref, bias_ref,
     og_ref, lse_ref, og4_ref, lse4_ref) = refs[n_carried:]
    hp = pl.program_id(1)
    t = q_ref.shape[0]
    span = ATT_SPAN
    de = ATT_DEINT
    t4 = t // de
    lo_half = lax.broadcasted_iota(jnp.int32, (1, LANES), 1) < HEAD_DIM

    kn = _pair_headnorm(k_ref[...], kg_ref[...])
    kn_ref[...] = kn
    kt_ref[...] = kn.T
    vt_ref[...] = v_ref[...].T
    qs_ref[...] = _pair_headnorm(q_ref[...], qg_ref[...]) * (HEAD_DIM ** -0.5)
    for r in range(de):
        rows = pl.ds(r, t4, stride=de)
        q4_ref[r] = qs_ref[rows, :]
        k4_ref[r] = kn_ref[rows, :]
        v4_ref[r] = v_ref[rows, :]

    qi = lax.broadcasted_iota(jnp.int32, (span, 2 * span), 0)
    ki = lax.broadcasted_iota(jnp.int32, (span, 2 * span), 1)
    steps = qi + span - ki
    valid = (steps >= 0) & (steps <= span)
    for g, (_, dil) in enumerate(DILATION_GROUPS):
        dist = (steps * dil).astype(F32)
        for h in range(2):
            bias_ref[g, h * span:(h + 1) * span, :] = jnp.where(valid, -(slopes_ref[2 * hp + h] * dist), NEG_BIG)

    def attend(g, qb, kb, vb, has_prev, mxu_sum):
        q2 = jnp.concatenate([jnp.where(lo_half, qb, 0.0), jnp.where(lo_half, 0.0, qb)], axis=0).astype(BF16)
        bias = bias_ref[g] if has_prev else bias_ref[g, :, span:]
        s = lax.dot_general(q2, kb.astype(BF16), _NT, preferred_element_type=F32) + bias
        m = jnp.max(s, axis=-1, keepdims=True)
        p = jnp.exp(s - m)
        if mxu_sum:
            v_ones = jnp.concatenate([vb.astype(BF16), jnp.ones((vb.shape[0], LANES), BF16)], axis=1)
            pv = jnp.dot(p.astype(BF16), v_ones, preferred_element_type=F32)
            l = pv[:, LANES:]
            o = pv[:, 0:LANES] / l
        else:
            l = jnp.sum(p, axis=-1, keepdims=True)
            o = jnp.dot(p.astype(BF16), vb.astype(BF16), preferred_element_type=F32) / l
        lse = m + jnp.log(l)
        return jnp.where(lo_half, o[0:span], o[span:]), jnp.where(lo_half, lse[0:span], lse[span:])

    for g, (window, dil) in enumerate(DILATION_GROUPS):
        assert window // dil == span
        nb = t // (span * dil)

        if dil == 1:
            def unit(b, has_prev):
                rows = pl.ds(b * span, span)
                keys = pl.ds((b - 1) * span, 2 * span) if has_prev else rows
                o, lse = attend(0, qs_ref[rows, :], kn_ref[keys, :], v_ref[keys, :], has_prev, True)
                og_ref[rows, :] = o
                lse_ref[rows, :] = lse

            unit(0, False)
            lax.fori_loop(1, nb, lambda b, c: (unit(b, True), c)[1], 0, unroll=8)
            continue

        assert dil % de == 0
        st = dil // de

        def residue(r, carry, g=g, st=st, nb=nb):
            r4, off = r % de, r // de

            def unit(b, has_prev):
                rows = pl.ds(st * (b * span) + off, span, stride=st)
                keys = pl.ds(st * ((b - 1) * span) + off, 2 * span, stride=st) if has_prev else rows
                o, lse = attend(g, q4_ref[r4, rows, :], k4_ref[r4, keys, :], v4_ref[r4, keys, :], has_prev,
                                nb > 1)
                og4_ref[g - 1, r4, rows, :] = o
                lse4_ref[g - 1, r4, rows, :] = lse

            unit(0, False)
            if nb > 1:
                lax.fori_loop(1, nb, lambda b, c: (unit(b, True), c)[1], 0, unroll=3)
            return carry

        lax.fori_loop(0, dil, residue, 0, unroll=16 if nb == 1 else 4)

    for r in range(de):
        rows = pl.ds(r, t4, stride=de)
        l0, l1, l2 = lse_ref[rows, :], lse4_ref[0, r], lse4_ref[1, r]
        mx = jnp.maximum(jnp.maximum(l0, l1), l2)
        w0, w1, w2 = jnp.exp(l0 - mx), jnp.exp(l1 - mx), jnp.exp(l2 - mx)
        og_ref[rows, :] = (og_ref[rows, :] * w0 + og4_ref[0, r] * w1 + og4_ref[1, r] * w2) / (w0 + w1 + w2)
    o_ref[...] = og_ref[...].astype(o_ref.dtype)


def _attention_prompt(p, col0, att_w, qg, kg, slopes, layer, depth, n_seq, t, kv_stack):
    m = p.shape[0]
    pairs = att_w // LANES
    assert col0 % LANES == 0
    blk = lambda off: pl.BlockSpec((t, LANES), lambda n, hp: (n, off + hp))
    c0 = col0 // LANES
    gain = pl.BlockSpec((None, 1, LANES), lambda n, hp: (layer, 0, 0))
    tr_spec = pl.BlockSpec((None, None, LANES, t), lambda n, hp: (layer, n, hp, 0))
    tr_shape = jax.ShapeDtypeStruct((depth, n_seq, att_w, t), F32)
    carried = tuple(kv_stack)
    assert all(c.shape == tr_shape.shape for c in carried)
    n_in = 6
    assert t % (ATT_DEINT * ATT_SPAN) == 0
    tok = pltpu.VMEM((t, LANES), F32)
    deint = pltpu.VMEM((ATT_DEINT, t // ATT_DEINT, LANES), F32)
    dilated = pltpu.VMEM((len(DILATION_GROUPS) - 1, ATT_DEINT, t // ATT_DEINT, LANES), F32)
    return pl.pallas_call(
        functools.partial(_attn_body, n_carried=len(carried)),
        grid=(n_seq, pairs),
        in_specs=[pl.BlockSpec(memory_space=pltpu.SMEM),
                  blk(c0), blk(c0 + pairs), blk(c0 + 2 * pairs), gain, gain]
                 + [pl.BlockSpec(memory_space=pl.ANY)] * len(carried),
        out_specs=[blk(0), tr_spec, tr_spec],
        out_shape=[jax.ShapeDtypeStruct((m, att_w), BF16), tr_shape, tr_shape],
        input_output_aliases={n_in + i: 1 + i for i in range(len(carried))},
        scratch_shapes=[tok, tok, deint, deint, deint,
                        pltpu.VMEM((len(DILATION_GROUPS), 2 * ATT_SPAN, 2 * ATT_SPAN), F32),
                        tok, tok, dilated, dilated],
        compiler_params=_cparams(("parallel", "arbitrary")),
        name="attn_prompt",
    )(slopes, p, p, p, qg, kg, *carried)


def _softplus(z):
    return jnp.maximum(z, 0.0) + jnp.log(1.0 + jnp.exp(-jnp.abs(z)))


def _rwkv_token_math(p_r, p_k, p_v, p_l, q_r, q_k, q_v, q_l, mu_r, mu_k, mu_v, mu_l,
                     w0, w2p, a0, a2p, g2p, k_k, k_a, gate_dot=_dot3):
    xr = p_r + (q_r - p_r) * mu_r
    xk = p_k + (q_k - p_k) * mu_k
    xv = p_v + (q_v - p_v) * mu_v
    xl = p_l + (q_l - p_l) * mu_l
    w_log = -_softplus(-(w0 + _dot3(jnp.tanh(xl), w2p))) - 0.5
    log_decay = -jnp.exp(w_log)
    sig_l = jax.nn.sigmoid(xl)
    a_gate = jax.nn.sigmoid(a0 + gate_dot(xl, a2p))
    g = gate_dot(sig_l, g2p)
    kk = xk * k_k
    kk = kk * lax.rsqrt(jnp.maximum(_segsum64_many([kk * kk])[0], 1e-24))
    kmod = xk * (1.0 + (a_gate - 1.0) * k_a)
    return xr, log_decay, kmod, xv, -kk, kk * a_gate, g


def _segsum64_many(xs):
    rows, width = xs[0].shape
    if rows % 16 != 0:
        return [_segsum64(x) for x in xs]
    nsl = width // LANES
    pieces = [part[:, i * LANES:(i + 1) * LANES]
              for x in xs for part in _split2(x) for i in range(nsl)]
    res = jnp.dot(jnp.concatenate(pieces, axis=0), _head_blockdiag(LANES, BF16), preferred_element_type=F32)
    outs = []
    for n in range(len(xs)):
        hi, lo = [jnp.concatenate([res[((2 * n + j) * nsl + i) * rows:((2 * n + j) * nsl + i + 1) * rows]
                                   for i in range(nsl)], axis=1) for j in range(2)]
        outs.append(hi + lo)
    return outs


def _rwkv_post(y, r, k, v, g, ln_w, ln_b, r_k):
    sum_y, sum_rk = _segsum64_many([y, r * k * r_k])
    yc = y - sum_y * (1.0 / HEAD_DIM)
    var_y = _segsum64_many([yc * yc])[0] * (1.0 / HEAD_DIM)
    yn = yc * lax.rsqrt(var_y + RWKV_GN_EPS) * ln_w + ln_b
    return (yn + sum_rk * v) * g


def _scan_chunk(r, lw, cum, k, v, a, b, s0):
    c = r.shape[1]
    c2 = 2 * c
    lane = lax.broadcasted_iota(jnp.int32, (1, 1, LANES), 2)
    head0 = lane < HEAD_DIM
    m0 = head0.astype(F32)
    m1 = 1.0 - m0
    stack = lambda x: jnp.concatenate([x * m0, x * m1], axis=1)
    twice = lambda x: jnp.concatenate([x, x], axis=1)
    pick = lambda x_st: jnp.where(head0, x_st[:, 0:c], x_st[:, c:])

    cum_prev = cum - lw
    cum_last = cum[:, c - 1:c, :]
    e_pos = jnp.exp(cum)
    e_neg = jnp.exp(-cum)
    e_rem = jnp.exp(cum_last - cum)
    at = a * jnp.exp(cum_prev)
    rt = r * e_pos
    bt = b * e_neg
    kt = k * e_neg
    bh = b * e_rem
    kh = k * e_rem

    big = _dot1(jnp.concatenate([stack(at), stack(rt)], axis=1),
                jnp.concatenate([twice(bt), twice(kt)], axis=1), _BNT)
    row = lax.broadcasted_iota(jnp.int32, (1, c2, c2), 1)
    col = lax.broadcasted_iota(jnp.int32, (1, c2, c2), 2)
    assert c == HEAD_DIM
    same_head = (row >> HEAD_SHIFT) == (col >> HEAD_SHIFT)
    tr, tc = row & (c - 1), col & (c - 1)
    strict = same_head & (tr > tc)
    incl = same_head & (tr >= tc)
    lmat = jnp.where(strict, big[:, 0:c2, 0:c2], 0.0)
    ak = jnp.where(strict, big[:, 0:c2, c2:], 0.0)
    rb = jnp.where(incl, big[:, c2:, 0:c2], 0.0)
    rk = jnp.where(incl, big[:, c2:, c2:], 0.0)

    eye = (row == col).astype(F32)
    x = eye + jnp.where((row >> 1) == (col >> 1), lmat, 0.0)
    s, log2s = 2, 1
    while s < c:
        joins = ((row >> (log2s + 1)) == (col >> (log2s + 1))) & ((row & s) != 0) & ((col & s) == 0)
        x = x + _dot1(_dot1(x, jnp.where(joins, lmat, 0.0), _BNN), x, _BNN)
        s, log2s = 2 * s, log2s + 1

    v_st = twice(v)
    w = _dot1(jnp.concatenate([at, rt], axis=1), s0, _BNT)
    u_st = _dot1(x, twice(w[:, 0:c]) + _dot1(ak, v_st, _BNN), _BNN)
    y_st = _dot1(jnp.concatenate([rb, rk], axis=2), jnp.concatenate([u_st, v_st], axis=1), _BNN)
    y = w[:, c:] + pick(y_st)
    u = pick(u_st)
    upd = _dot1(jnp.concatenate([u, v], axis=1), jnp.concatenate([bh, kh], axis=1), _BTN)
    s_new = s0 * jnp.exp(cum_last) + _head_blockdiag(LANES, F32)[None] * upd
    return y, s_new


def _rwkv_mix_body(pr_ref, pk_ref, pv_ref, plo_ref,
                   mur_ref, muk_ref, muv_ref, mul_ref, w0_ref, w2_ref, a0_ref, a2_ref, g2_ref,
                   kk_ref, ka_ref, lnw_ref, lnb_ref, rk_ref,
                   y_ref, s_out_ref, s_ref, sr_ref, sk_ref, sv_ref, sl_ref):
    ch = pl.program_id(1)
    nseq, c, rw = pr_ref.shape
    npair = rw // LANES
    shift_refs = (sr_ref, sk_ref, sv_ref, sl_ref)

    @pl.when(ch == 0)
    def _():
        s_ref[...] = jnp.zeros_like(s_ref)
        for ref in shift_refs:
            ref[:, 0:8, :] = jnp.zeros((nseq, 8, ref.shape[2]), F32)

    def cur_and_prev(cur_ref, sh_ref):
        cur = cur_ref[...]
        sh_ref[:, 8:, :] = cur
        prev = sh_ref[:, pl.ds(7, c), :]
        sh_ref[:, 7:8, :] = cur[:, c - 1:c, :]
        flat = lambda x: x.reshape(nseq * c, x.shape[2])
        return flat(cur), flat(prev)

    (p_r, q_r), (p_k, q_k), (p_v, q_v), (p_l, q_l) = [
        cur_and_prev(cur, sh) for cur, sh in zip((pr_ref, pk_ref, pv_ref, plo_ref), shift_refs)]
    r, lw, k, v, a, b, g = _rwkv_token_math(
        p_r, p_k, p_v, p_l, q_r, q_k, q_v, q_l,
        mur_ref[...], muk_ref[...], muv_ref[...], mul_ref[...],
        w0_ref[...], w2_ref[...], a0_ref[...], a2_ref[...], g2_ref[...], kk_ref[...], ka_ref[...],
        gate_dot=_dot1)

    pairs = lambda x: jnp.stack([x[s * c:(s + 1) * c, p * LANES:(p + 1) * LANES]
                                 for s in range(nseq) for p in range(npair)], axis=0)
    tri_incl = (lax.broadcasted_iota(jnp.int32, (c, c), 0)
                >= lax.broadcasted_iota(jnp.int32, (c, c), 1)).astype(BF16)
    cum = jnp.concatenate([_cumsum_rows(lw[s * c:(s + 1) * c], tri_incl) for s in range(nseq)], axis=0)
    y, s_new = _scan_chunk(pairs(r), pairs(lw), pairs(cum), pairs(k), pairs(v), pairs(a), pairs(b), s_ref[...])
    s_ref[...] = s_new
    y = jnp.concatenate([jnp.concatenate([y[s * npair + p] for p in range(npair)], axis=1)
                         for s in range(nseq)], axis=0)
    out = _rwkv_post(y, r, k, v, g, lnw_ref[...], lnb_ref[...], rk_ref[...])
    y_ref[...] = out.reshape(nseq, c, rw).astype(y_ref.dtype)

    @pl.when(ch == pl.num_programs(1) - 1)
    def _():
        for s in range(nseq):
            for p in range(npair):
                s_out_ref[s, 2 * p] = s_ref[s * npair + p, 0:HEAD_DIM, 0:HEAD_DIM]
                s_out_ref[s, 2 * p + 1] = s_ref[s * npair + p, HEAD_DIM:, HEAD_DIM:]


def _rwkv_mix_prompt(p, col0, lp, layer, n_seq, t, seq_per_step):
    m = p.shape[0]
    rw = lp["rwkv_w"]
    c = RWKV_CHUNK
    heads = rw // HEAD_DIM
    assert n_seq % seq_per_step == 0 and col0 % rw == 0 and (col0 + 3 * rw) % LORA_W == 0
    c_slab = col0 // rw
    c_lora = (col0 + 3 * rw) // LORA_W
    tok = lambda w, col: pl.BlockSpec((seq_per_step, c, w), lambda n, ch: (n, ch, col))
    vec = lambda w, col: pl.BlockSpec((None, 1, w), lambda n, ch: (layer, 0, col))
    mat = pl.BlockSpec((None, LORA_W, rw), lambda n, ch: (layer, 0, 0))
    p3 = p.reshape(n_seq, t, p.shape[1])
    y, wkv = pl.pallas_call(
        _rwkv_mix_body,
        grid=(n_seq // seq_per_step, t // c),
        in_specs=[tok(rw, c_slab), tok(rw, c_slab + 1), tok(rw, c_slab + 2), tok(LORA_W, c_lora),
                  vec(rw, 0), vec(rw, 1), vec(rw, 2), vec(LORA_W, (3 * rw) // LORA_W),
                  vec(rw, 0), mat, vec(rw, 0), mat, mat, vec(rw, 0), vec(rw, 0),
                  vec(rw, 0), vec(rw, 0), vec(rw, 0)],
        out_specs=[tok(rw, 0),
                   pl.BlockSpec((seq_per_step, heads, HEAD_DIM, HEAD_DIM), lambda n, ch: (n, 0, 0, 0))],
        out_shape=[jax.ShapeDtypeStruct((n_seq, t, rw), BF16),
                   jax.ShapeDtypeStruct((n_seq, heads, HEAD_DIM, HEAD_DIM), F32)],
        scratch_shapes=[pltpu.VMEM((seq_per_step * (rw // LANES), LANES, LANES), F32)]
                       + [pltpu.VMEM((seq_per_step, c + 8, rw), F32)] * 3
                       + [pltpu.VMEM((seq_per_step, c + 8, LORA_W), F32)],
        compiler_params=_cparams(("parallel", "arbitrary")),
        name="rwkv_mix",
    )(p3, p3, p3, p3, lp["mu"], lp["mu"], lp["mu"], lp["mu"],
      lp["w0"], lp["w2p"], lp["a0"], lp["a2p"], lp["g2p"], lp["k_k"], lp["k_a"],
      lp["ln_w"], lp["ln_b"], lp["r_k"])
    return y.reshape(m, rw), wkv


def _sample_body(slope_ref, p_ref, kc_ref, vc_ref,
                 wkv_ref, shift_ref, cst_ref, qg_ref, kg_ref,
                 mu_ref, w0_ref, w2_ref, a0_ref, a2_ref, g2_ref, kk_ref, ka_ref,
                 rk_ref, lnw_ref, lnb_ref, cw_ref, cb_ref,
                 oatt_ref, yr_ref, yc_ref, kn_ref, vn_ref, wkvn_ref, cstn_ref,
                 *, col_att, col_rwkv, col_conv):
    heads_a = oatt_ref.shape[0]
    aw = heads_a * HEAD_DIM
    rw = yr_ref.shape[1]
    cw = yc_ref.shape[1]
    rp = shift_ref.shape[1]
    rows8 = lambda x: jnp.broadcast_to(x, (8, x.shape[1]))
    p_all = p_ref[...]

    by_head = lambda row: jnp.concatenate(
        [row[:, h * HEAD_DIM:(h + 1) * HEAD_DIM] for h in range(heads_a)], axis=0)
    ca = col_att
    q, k, v_new = (by_head(p_all[:, ca:ca + aw]), by_head(p_all[:, ca + aw:ca + 2 * aw]),
                   by_head(p_all[:, ca + 2 * aw:ca + 3 * aw]))
    qn = _rms(q, qg_ref[...]) * (HEAD_DIM ** -0.5)
    kn = _rms(k, kg_ref[...])
    kn_ref[...] = kn
    vn_ref[...] = v_new
    s_new = jnp.sum(kn * qn, axis=-1, keepdims=True)
    slope = slope_ref[...]
    past = kc_ref.shape[2]
    er = lax.broadcasted_iota(jnp.int32, (HEAD_DIM, HEAD_DIM), 0)
    ec = lax.broadcasted_iota(jnp.int32, (HEAD_DIM, HEAD_DIM), 1)
    eye = er == ec
    q_col = jnp.stack([jnp.sum(jnp.where(eye, qn[h:h + 1, :], 0.0), axis=1, keepdims=True)
                       for h in range(heads_a)], axis=0)
    dist = past - lax.broadcasted_iota(jnp.int32, (1, past), 1)
    biased = jnp.sum(kc_ref[...] * q_col, axis=1) - slope * dist.astype(F32)
    parts = []
    for window, dil in DILATION_GROUPS:
        assert dil & (dil - 1) == 0
        valid = ((dist & (dil - 1)) == 0) & (dist <= window)
        s = jnp.where(valid, biased, NEG_BIG)
        m = jnp.maximum(jnp.max(s, axis=1, keepdims=True), s_new)
        p = jnp.exp(s - m)
        p_new = jnp.exp(s_new - m)
        l = jnp.sum(p, axis=1, keepdims=True) + p_new
        parts.append((p, p_new, l, m + jnp.log(l)))
    mx = jnp.maximum(jnp.maximum(parts[0][3], parts[1][3]), parts[2][3])
    ws = [jnp.exp(lse - mx) for _, _, _, lse in parts]
    wsum = ws[0] + ws[1] + ws[2]
    scale_g = [w / (l * wsum) for w, (_, _, l, _) in zip(ws, parts)]
    coef = parts[0][0] * scale_g[0] + parts[1][0] * scale_g[1] + parts[2][0] * scale_g[2]
    coef_new = parts[0][1] * scale_g[0] + parts[1][1] * scale_g[1] + parts[2][1] * scale_g[2]
    o_col = jnp.sum(vc_ref[...] * coef[:, None, :], axis=2, keepdims=True)
    o_rows = jnp.concatenate([jnp.sum(jnp.where(eye, o_col[h], 0.0), axis=0, keepdims=True)
                              for h in range(heads_a)], axis=0)
    oatt_ref[...] = o_rows + coef_new * v_new

    pr = rows8(p_all[:, col_rwkv:col_rwkv + rp])
    sh = rows8(shift_ref[...])
    mu = mu_ref[...]
    c3 = 3 * rw
    r, lw, k, v, a, b, g = _rwkv_token_math(
        pr[:, 0:rw], pr[:, rw:2 * rw], pr[:, 2 * rw:c3], pr[:, c3:],
        sh[:, 0:rw], sh[:, rw:2 * rw], sh[:, 2 * rw:c3], sh[:, c3:],
        mu[:, 0:rw], mu[:, rw:2 * rw], mu[:, 2 * rw:c3], mu[:, c3:],
        w0_ref[...], w2_ref[...], a0_ref[...], a2_ref[...], g2_ref[...], kk_ref[...], ka_ref[...])
    decay = jnp.exp(lw)
    place_r = lax.broadcasted_iota(jnp.int32, (HEAD_DIM, rw), 0)
    place_c = lax.broadcasted_iota(jnp.int32, (HEAD_DIM, rw), 1)
    y_full = jnp.zeros((1, rw), F32)
    for h in range(rw // HEAD_DIM):
        hs = slice(h * HEAD_DIM, (h + 1) * HEAD_DIM)
        s_old = wkv_ref[h]
        sa = jnp.sum(s_old * a[0:1, hs], axis=1, keepdims=True)
        v_col = jnp.sum(jnp.where(eye, v[0:1, hs], 0.0), axis=1, keepdims=True)
        s_h = s_old * decay[0:1, hs] + sa * b[0:1, hs] + v_col * k[0:1, hs]
        wkvn_ref[h] = s_h
        y_col = jnp.sum(s_h * r[0:1, hs], axis=1, keepdims=True)
        y_full = y_full + jnp.sum(jnp.where(place_c == place_r + h * HEAD_DIM, y_col, 0.0),
                                  axis=0, keepdims=True)
    y8 = rows8(y_full)
    yr_ref[...] = _rwkv_post(y8, r, k, v, g, lnw_ref[...], lnb_ref[...], rk_ref[...])[0:1]

    c1 = col_conv
    z = p_all[:, c1 + 2 * cw:c1 + 3 * cw] * p_all[:, c1:c1 + cw]
    yc = cb_ref[...] + cw_ref[0:1, :] * cst_ref[0:1, :]
    yc = yc + cw_ref[1:2, :] * cst_ref[1:2, :]
    yc = yc + cw_ref[2:3, :] * z
    yc_ref[...] = p_all[:, c1 + cw:c1 + 2 * cw] * yc
    cstn_ref[0:1, :] = cst_ref[1:2, :]
    cstn_ref[1:2, :] = z


def _sample_mixers(p, cols, cache_k, cache_v, state_wkv, state_shift, state_conv, lp, slope_col, layer):
    nb = p.shape[0]
    depth, _, past, heads_a, _ = cache_k.shape
    rw = lp["rwkv_w"]
    rp = state_shift.shape[2]
    cw = state_conv.shape[3]
    heads = rw // HEAD_DIM

    row = lambda w: pl.BlockSpec((None, 1, w), lambda n: (n, 0, 0))
    lvec = lambda w: pl.BlockSpec((None, 1, w), lambda n: (layer, 0, 0))
    lmat = lambda r_, w: pl.BlockSpec((None, r_, w), lambda n: (layer, 0, 0))
    tile = pl.BlockSpec((None, heads_a, HEAD_DIM), lambda n: (n, 0, 0))

    assert past >= DILATION_GROUPS[-1][0]
    views = [jnp.transpose(c, (0, 1, 3, 4, 2)) for c in (cache_k, cache_v)]
    cache_specs = [pl.BlockSpec((None, None, heads_a, HEAD_DIM, past), lambda n: (layer, n, 0, 0, 0))] * 2

    outs = pl.pallas_call(
        functools.partial(_sample_body, col_att=cols[0], col_rwkv=cols[1], col_conv=cols[2]),
        grid=(nb,),
        in_specs=[pl.BlockSpec((heads_a, 1), lambda n: (0, 0)), row(p.shape[1])] + cache_specs + [
                  pl.BlockSpec((None, None, heads, HEAD_DIM, HEAD_DIM), lambda n: (layer, n, 0, 0, 0)),
                  pl.BlockSpec((None, None, 1, rp), lambda n: (layer, n, 0, 0)),
                  pl.BlockSpec((None, None, CONV_K - 1, cw), lambda n: (layer, n, 0, 0)),
                  lvec(HEAD_DIM), lvec(HEAD_DIM),
                  lvec(rp), lvec(rw), lmat(LORA_W, rw), lvec(rw), lmat(LORA_W, rw),
                  lmat(LORA_W, rw), lvec(rw), lvec(rw), lvec(rw), lvec(rw), lvec(rw),
                  lmat(CONV_K, cw), lvec(cw)],
        out_specs=[tile, row(rw), row(cw), tile, tile,
                   pl.BlockSpec((None, heads, HEAD_DIM, HEAD_DIM), lambda n: (n, 0, 0, 0)),
                   pl.BlockSpec((None, CONV_K - 1, cw), lambda n: (n, 0, 0))],
        out_shape=[jax.ShapeDtypeStruct((nb, heads_a, HEAD_DIM), F32), jax.ShapeDtypeStruct((nb, 1, rw), F32),
                   jax.ShapeDtypeStruct((nb, 1, cw), F32),
                   jax.ShapeDtypeStruct((nb, heads_a, HEAD_DIM), F32),
                   jax.ShapeDtypeStruct((nb, heads_a, HEAD_DIM), F32),
                   jax.ShapeDtypeStruct((nb, heads, HEAD_DIM, HEAD_DIM), F32),
                   jax.ShapeDtypeStruct((nb, CONV_K - 1, cw), F32)],
        compiler_params=_cparams(("parallel",)),
        name="sample_mixers",
    )(slope_col, p[:, None, :], *views,
      state_wkv, state_shift[:, :, None, :], state_conv,
      lp["q_gain"], lp["k_gain"],
      lp["mu"], lp["w0"], lp["w2p"], lp["a0"], lp["a2p"], lp["g2p"], lp["k_k"], lp["k_a"],
      lp["r_k"], lp["ln_w"], lp["ln_b"], lp["conv_w"], lp["conv_b"])
    o_att, y_rwkv, y_conv, k_new, v_new, wkv_new, conv_new = outs
    return o_att.reshape(nb, heads_a * HEAD_DIM), y_rwkv[:, 0], y_conv[:, 0], k_new, v_new, wkv_new, conv_new


def kernel(x_prompt, x_sample, cache_swa_k, cache_swa_v, state_wkv, state_shift, state_conv, ffn1_norm, ffn1_w_gate, ffn1_w_up, ffn1_w_down, mix_norm, w_in, q_norm, k_norm, rwkv_mu, rwkv_w0, rwkv_w2, rwkv_a0, rwkv_a2, rwkv_g2, rwkv_k_k, rwkv_k_a, rwkv_r_k, rwkv_ln_w, rwkv_ln_b, conv_w, conv_b, w_out, ffn2_norm, ffn2_w_gate, ffn2_w_up, ffn2_w_down):
    n_seq, t, d = x_prompt.shape
    nb = x_sample.shape[0]
    assert x_sample.shape[1] == 1
    depth = w_in.shape[0]
    att_heads, head_dim = cache_swa_k.shape[3], cache_swa_k.shape[4]
    assert head_dim == HEAD_DIM
    aw = att_heads * HEAD_DIM
    rw = rwkv_w0.shape[1]
    cw = conv_b.shape[1]
    rp = rwkv_mu.shape[1]
    assert rp == 3 * rw + LORA_W and w_in.shape[2] == 3 * aw + rp + 3 * cw

    row3 = lambda p: p.reshape(depth, 1, -1)
    col_att, col_rwkv, col_conv = 0, 3 * aw, 3 * aw + rp
    zpad = lambda w, before: jnp.pad(w, ((0, 0), (before, LORA_W - before - w.shape[1]), (0, 0)))
    lp = {
        "rwkv_w": rw,
        "mu": row3(rwkv_mu), "w0": row3(rwkv_w0), "a0": row3(rwkv_a0),
        "w2p": zpad(rwkv_w2, 0), "a2p": zpad(rwkv_a2, W_LORA), "g2p": zpad(rwkv_g2, W_LORA + A_LORA),
        "k_k": row3(rwkv_k_k), "k_a": row3(rwkv_k_a), "r_k": row3(rwkv_r_k),
        "ln_w": row3(rwkv_ln_w), "ln_b": row3(rwkv_ln_b),
        "conv_w": conv_w, "conv_b": row3(conv_b),
        "q_gain": row3(q_norm), "k_gain": row3(k_norm),
        "q_gain_pair": row3(jnp.tile(q_norm, (1, LANES // HEAD_DIM))),
        "k_gain_pair": row3(jnp.tile(k_norm, (1, LANES // HEAD_DIM))),
    }
    g_ffn1, g_mix, g_ffn2 = row3(ffn1_norm), row3(mix_norm), row3(ffn2_norm)
    slopes = 2.0 ** (-8.0 * jnp.arange(1, att_heads + 1, dtype=F32) / att_heads)
    slope_col = slopes[:, None]

    m = n_seq * t
    xp = x_prompt.reshape(m, d)
    xs = x_sample.reshape(nb, d)
    tm_ffn, tf = 1024, 256
    tm_in, tn_in = 1024, 1280
    tm_out = 512
    p_states, s_states = [], []
    kv_shape = (depth, n_seq, aw, t)
    for l in range(depth):
        xp, xs, w_in16, w_out16, *fresh = _ffn(
            xp, xs, g_ffn1, ffn1_w_gate, ffn1_w_up, ffn1_w_down, l, tm_ffn, tf, cast=(w_in, w_out),
            zeros=((depth * n_seq * aw, t),) * 2 if l == 0 else ())
        if l == 0:
            kv_stack = [z.reshape(kv_shape) for z in fresh]
        pp, ps = _norm_matmul(xp, xs, g_mix, w_in16, l, tm_in, tn_in)

        o_att, *kv_stack = _attention_prompt(pp, col_att, aw, lp["q_gain_pair"], lp["k_gain_pair"], slopes,
                                             l, depth, n_seq, t, kv_stack)
        y_rwkv, wkv_p = _rwkv_mix_prompt(pp, col_rwkv, lp, l, n_seq, t, 4 if n_seq % 4 == 0 else 1)
        xp, z_last = _out_proj_conv(xp, o_att, y_rwkv, pp, col_conv, cw, lp["conv_w"], lp["conv_b"], w_out16,
                                    l, t, tm_out)
        p_states.append((
            wkv_p, pp.reshape(n_seq, t, -1)[:, -1, col_rwkv:col_rwkv + rp],
            z_last.reshape(n_seq, t // tm_out, 8, cw)[:, -1, 8 - (CONV_K - 1):]))

        so, sy, syc, sk, sv, swkv, sconv = _sample_mixers(
            ps, (col_att, col_rwkv, col_conv), cache_swa_k, cache_swa_v, state_wkv, state_shift, state_conv,
            lp, slope_col, l)
        xs = _out_proj(xs, so, sy, syc, w_out16, l, nb)
        s_states.append((sk[:, None], sv[:, None], swkv, ps[:, col_rwkv:col_rwkv + rp], sconv))

        xp, xs = _ffn(xp, xs, g_ffn2, ffn2_w_gate, ffn2_w_up, ffn2_w_down, l, tm_ffn, tf)

    stacked = lambda states, i: jnp.stack([st[i] for st in states], axis=0)
    rows = lambda x_t: jnp.transpose(x_t.reshape(depth, n_seq, att_heads, HEAD_DIM, t), (0, 1, 4, 2, 3))
    return (xp.reshape(n_seq, t, d), xs.reshape(nb, 1, d),
            rows(kv_stack[0]), rows(kv_stack[1]),
            stacked(p_states, 0), stacked(p_states, 1), stacked(p_states, 2),
            stacked(s_states, 0), stacked(s_states, 1), stacked(s_states, 2), stacked(s_states, 3),
            stacked(s_states, 4))
```

```python
import functools

import jax
import jax.numpy as jnp
from jax import lax
from jax.experimental import pallas as pl
from jax.experimental.pallas import tpu as pltpu

F32 = jnp.float32
BF16 = jnp.bfloat16

HEAD_DIM = 64
HEAD_SHIFT = 6
LANES = 128
NORM_EPS = 1e-6
RWKV_GN_EPS = 64e-5
DILATION_GROUPS = ((128, 1), (512, 4), (2048, 16))
ATT_SPAN = 128
ATT_DEINT = 4
W_LORA, A_LORA, G_LORA = 64, 64, 128
LORA_W = W_LORA + A_LORA + G_LORA
CONV_K = 3
RWKV_CHUNK = 64
VMEM_LIMIT = 56 * 1024 * 1024
NEG_BIG = -1e30

_NT = (((1,), (1,)), ((), ()))
_NN = (((1,), (0,)), ((), ()))
_BNN = (((2,), (1,)), ((0,), (0,)))
_BNT = (((2,), (2,)), ((0,), (0,)))
_BTN = (((1,), (1,)), ((0,), (0,)))


def _cparams(sem):
    return pltpu.CompilerParams(dimension_semantics=sem, vmem_limit_bytes=VMEM_LIMIT)


def _rms(x, g):
    return x * lax.rsqrt(jnp.mean(x * x, axis=-1, keepdims=True) + NORM_EPS) * g


def _split2(x):
    hi = x.astype(BF16)
    lo = (x - hi.astype(F32)).astype(BF16)
    return hi, lo


def _dot3(a, b, dims=_NN):
    ah, al = _split2(a)
    bh, bl = _split2(b)
    d = lambda x, y: lax.dot_general(x, y, dims, preferred_element_type=F32)
    return d(ah, bh) + (d(al, bh) + d(ah, bl))


def _dot1(a, b, dims=_NN):
    return lax.dot_general(a.astype(BF16), b.astype(BF16), dims, preferred_element_type=F32)


def _split3(x):
    h1 = x.astype(BF16)
    r1 = x - h1.astype(F32)
    h2 = r1.astype(BF16)
    h3 = (r1 - h2.astype(F32)).astype(BF16)
    return h1, h2, h3


def _dot_sel(x, sel):
    return sum(jnp.dot(h, sel, preferred_element_type=F32) for h in _split3(x))


def _cumsum_rows(x, tri_incl):
    return sum(jnp.dot(tri_incl, h, preferred_element_type=F32) for h in _split2(x))


def _head_blockdiag(n, dtype):
    r = lax.broadcasted_iota(jnp.int32, (n, n), 0) >> HEAD_SHIFT
    c = lax.broadcasted_iota(jnp.int32, (n, n), 1) >> HEAD_SHIFT
    return (r == c).astype(dtype)


def _segsum64(x):
    bd = _head_blockdiag(LANES, BF16)
    parts = [_dot_sel(x[:, i:i + LANES], bd) for i in range(0, x.shape[1], LANES)]
    return parts[0] if len(parts) == 1 else jnp.concatenate(parts, axis=1)


BF16_ROWS = 16


def _ffn_body(x_ref, xs_ref, g_ref, wg_ref, wu_ref, wd_ref, *refs, n_cast, n_zero):
    cast_in = refs[:n_cast]
    o_ref, os_ref = refs[n_cast:n_cast + 2]
    cast_out = refs[n_cast + 2:2 * n_cast + 2]
    zero_out = refs[2 * n_cast + 2:2 * n_cast + 2 + n_zero]
    xn_ref = refs[2 * n_cast + 2 + n_zero]
    j = pl.program_id(1)
    tm, d = x_ref.shape

    @pl.when(j == 0)
    def _():
        xn_ref[0:tm, :] = _rms(x_ref[...], g_ref[...]).astype(BF16)
        xn_ref[tm:, :] = _rms(xs_ref[...], g_ref[...]).astype(BF16)
        o_ref[...] = jnp.zeros_like(o_ref)
        os_ref[...] = jnp.zeros_like(os_ref)

    for src, dst in zip(cast_in, cast_out):
        dst[...] = src[...].astype(BF16)
    for dst in zero_out:
        dst[...] = jnp.zeros_like(dst)

    xn = xn_ref[...]
    h = jnp.dot(xn, wg_ref[...].astype(BF16), preferred_element_type=F32)
    u = jnp.dot(xn, wu_ref[...].astype(BF16), preferred_element_type=F32)
    act = (h * jax.nn.sigmoid(h) * u).astype(BF16)
    half = d // 2
    for c in range(2):
        cols = slice(c * half, (c + 1) * half)
        part = jnp.dot(act, wd_ref[:, cols].astype(BF16), preferred_element_type=F32)
        o_ref[:, cols] += part[0:tm]
        os_ref[:, cols] += part[tm:]

    @pl.when(j == pl.num_programs(1) - 1)
    def _():
        o_ref[...] = x_ref[...] + 0.5 * o_ref[...]
        os_ref[...] = xs_ref[...] + 0.5 * os_ref[...]


def _ffn(x, xs, gain, wg, wu, wd, layer, tm, tf, cast=(), zeros=()):
    m, d = x.shape
    ms = xs.shape[0]
    f = wg.shape[2]
    ni, nj = m // tm, f // tf

    def row_chunks(rows):
        cr = next(r for r in range(BF16_ROWS, rows + 1, BF16_ROWS) if rows % r == 0 and rows // r <= ni * nj)
        return cr, lambda i, j: jnp.minimum(i * nj + j, rows // cr - 1)

    cast_in, cast_out, cast_shapes = [], [], []
    for w in cast:
        cr, chunk = row_chunks(w.shape[1])
        cast_in.append(pl.BlockSpec((None, cr, w.shape[2]), lambda i, j, chunk=chunk: (layer, chunk(i, j), 0)))
        cast_out.append(pl.BlockSpec((cr, w.shape[2]), lambda i, j, chunk=chunk: (chunk(i, j), 0)))
        cast_shapes.append(jax.ShapeDtypeStruct(w.shape[1:], BF16))
    for rows, cols in zeros:
        cr, chunk = row_chunks(rows)
        cast_out.append(pl.BlockSpec((cr, cols), lambda i, j, chunk=chunk: (chunk(i, j), 0)))
        cast_shapes.append(jax.ShapeDtypeStruct((rows, cols), F32))
    return pl.pallas_call(
        functools.partial(_ffn_body, n_cast=len(cast), n_zero=len(zeros)),
        grid=(ni, nj),
        in_specs=[
            pl.BlockSpec((tm, d), lambda i, j: (i, 0)),
            pl.BlockSpec((ms, d), lambda i, j: (0, 0)),
            pl.BlockSpec((None, 1, d), lambda i, j: (layer, 0, 0)),
            pl.BlockSpec((None, d, tf), lambda i, j: (layer, 0, j)),
            pl.BlockSpec((None, d, tf), lambda i, j: (layer, 0, j)),
            pl.BlockSpec((None, tf, d), lambda i, j: (layer, j, 0)),
        ] + cast_in,
        out_specs=[pl.BlockSpec((tm, d), lambda i, j: (i, 0)),
                   pl.BlockSpec((ms, d), lambda i, j: (0, 0))] + cast_out,
        out_shape=[jax.ShapeDtypeStruct((m, d), F32), jax.ShapeDtypeStruct((ms, d), F32)] + cast_shapes,
        scratch_shapes=[pltpu.VMEM((tm + ms, d), BF16)],
        compiler_params=_cparams(("arbitrary", "arbitrary")),
        name="ffn",
    )(x, xs, gain, wg, wu, wd, *cast)


def _norm_matmul_body(x_ref, xs_ref, g_ref, w_ref, o_ref, os_ref, xn_ref):
    tm = x_ref.shape[0]

    @pl.when(pl.program_id(1) == 0)
    def _():
        xn_ref[0:tm, :] = _rms(x_ref[...], g_ref[...]).astype(BF16)
        xn_ref[tm:, :] = _rms(xs_ref[...], g_ref[...]).astype(BF16)

    out = jnp.dot(xn_ref[...], w_ref[...], preferred_element_type=F32)
    o_ref[...] = out[0:tm]
    os_ref[...] = out[tm:]


def _norm_matmul(x, xs, gain, w, layer, tm, tn):
    m, d = x.shape
    ms = xs.shape[0]
    n = w.shape[1]
    ni = m // tm
    out, out_s = pl.pallas_call(
        _norm_matmul_body,
        grid=(ni, n // tn),
        in_specs=[
            pl.BlockSpec((tm, d), lambda i, j: (i, 0)),
            pl.BlockSpec((ms, d), lambda i, j: (0, 0)),
            pl.BlockSpec((None, 1, d), lambda i, j: (layer, 0, 0)),
            pl.BlockSpec((d, tn), lambda i, j: (0, j)),
        ],
        out_specs=[pl.BlockSpec((tm, tn), lambda i, j: (i, j)),
                   pl.BlockSpec((None, ms, tn), lambda i, j: (i, 0, j))],
        out_shape=[jax.ShapeDtypeStruct((m, n), F32), jax.ShapeDtypeStruct((ni, ms, n), F32)],
        scratch_shapes=[pltpu.VMEM((tm + ms, d), BF16)],
        compiler_params=_cparams(("parallel", "arbitrary")),
        name="norm_matmul",
    )(x, xs, gain, w)
    return out, out_s[0]


def _mix_out(x, o_att, y_rwkv, y_conv, w_ref):
    wa, wr = o_att.shape[1], y_rwkv.shape[1]
    acc = jnp.dot(o_att.astype(BF16), w_ref[0:wa, :], preferred_element_type=F32)
    acc += jnp.dot(y_rwkv.astype(BF16), w_ref[wa:wa + wr, :], preferred_element_type=F32)
    acc += jnp.dot(y_conv.astype(BF16), w_ref[wa + wr:, :], preferred_element_type=F32)
    return x + acc


def _out_proj_body(x_ref, a_ref, r_ref, c_ref, w_ref, o_ref):
    o_ref[...] = _mix_out(x_ref[...], a_ref[...], r_ref[...], c_ref[...], w_ref)


def _out_proj_conv_body(x_ref, a_ref, r_ref, u_ref, gb_ref, gc_ref, pu_ref, pgc_ref, cw_ref, cb_ref, w_ref,
                        o_ref, zl_ref, zc_ref, *, blocks_per_seq):
    tm = u_ref.shape[0]
    first = (pl.program_id(0) % blocks_per_seq) == 0
    z = gc_ref[...] * u_ref[...]
    zc_ref[0:8, :] = jnp.where(first, 0.0, pgc_ref[...] * pu_ref[...])
    zc_ref[8:, :] = z
    yc = cb_ref[...] + cw_ref[0:1, :] * zc_ref[pl.ds(6, tm), :]
    yc = yc + cw_ref[1:2, :] * zc_ref[pl.ds(7, tm), :]
    yc = yc + cw_ref[2:3, :] * z
    zl_ref[...] = zc_ref[pl.ds(tm, 8), :]
    o_ref[...] = _mix_out(x_ref[...], a_ref[...], r_ref[...], gb_ref[...] * yc, w_ref)


def _out_proj_conv(x, o_att, y_rwkv, p, col0, cw, conv_w, conv_b, w, layer, t, tm):
    m, d = x.shape
    assert col0 % LANES == 0 and t % tm == 0
    cur = lambda s: pl.BlockSpec((pl.Element(tm), pl.Element(cw)),
                                 lambda i: (pl.multiple_of(i * tm, tm), col0 + s * cw))
    prev = lambda s: pl.BlockSpec((pl.Element(8), pl.Element(cw)),
                                  lambda i: (pl.multiple_of(jnp.maximum(i * tm - 8, 0), 8), col0 + s * cw))
    return pl.pallas_call(
        functools.partial(_out_proj_conv_body, blocks_per_seq=t // tm),
        grid=(m // tm,),
        in_specs=[
            pl.BlockSpec((tm, d), lambda i: (i, 0)),
            pl.BlockSpec((tm, o_att.shape[1]), lambda i: (i, 0)),
            pl.BlockSpec((tm, y_rwkv.shape[1]), lambda i: (i, 0)),
            cur(0), cur(1), cur(2), prev(0), prev(2),
            pl.BlockSpec((None, CONV_K, cw), lambda i: (layer, 0, 0)),
            pl.BlockSpec((None, 1, cw), lambda i: (layer, 0, 0)),
            pl.BlockSpec((w.shape[0], d), lambda i: (0, 0), pipeline_mode=pl.Buffered(1)),
        ],
        out_specs=[pl.BlockSpec((tm, d), lambda i: (i, 0)), pl.BlockSpec((8, cw), lambda i: (i, 0))],
        out_shape=[jax.ShapeDtypeStruct((m, d), F32), jax.ShapeDtypeStruct((m // tm * 8, cw), F32)],
        scratch_shapes=[pltpu.VMEM((tm + 8, cw), F32)],
        compiler_params=_cparams(("parallel",)),
        name="out_proj_conv",
    )(x, o_att, y_rwkv, p, p, p, p, p, conv_w, conv_b, w)


def _out_proj(x, o_att, y_rwkv, y_conv, w, layer, tm):
    m, d = x.shape
    return pl.pallas_call(
        _out_proj_body,
        grid=(m // tm,),
        in_specs=[
            pl.BlockSpec((tm, d), lambda i: (i, 0)),
            pl.BlockSpec((tm, o_att.shape[1]), lambda i: (i, 0)),
            pl.BlockSpec((tm, y_rwkv.shape[1]), lambda i: (i, 0)),
            pl.BlockSpec((tm, y_conv.shape[1]), lambda i: (i, 0)),
            pl.BlockSpec((w.shape[0], d), lambda i: (0, 0), pipeline_mode=pl.Buffered(1)),
        ],
        out_specs=pl.BlockSpec((tm, d), lambda i: (i, 0)),
        out_shape=jax.ShapeDtypeStruct((m, d), F32),
        compiler_params=_cparams(("parallel",)),
        name="out_proj",
    )(x, o_att, y_rwkv, y_conv, w)


def _pair_headnorm(x, g):
    ms = _segsum64_many([x * x])[0] * (1.0 / HEAD_DIM)
    return x * lax.rsqrt(ms + NORM_EPS) * g


def _attn_body(slopes_ref, q_ref, k_ref, v_ref, qg_ref, kg_ref, *refs, n_carried):
    (o_ref, kt_ref, vt_ref, qs_ref, kn_ref, q4_ref, k4_ref, v4_ref, bias_ref,
     og_ref, lse_ref, og4_ref, lse4_ref) = refs[n_carried:]
    hp = pl.program_id(0)
    t = q_ref.shape[0]
    span = ATT_SPAN
    de = ATT_DEINT
    t4 = t // de
    lo_half = lax.broadcasted_iota(jnp.int32, (1, LANES), 1) < HEAD_DIM

    kn = _pair_headnorm(k_ref[...], kg_ref[...])
    kn_ref[...] = kn
    kt_ref[...] = kn.T
    vt_ref[...] = v_ref[...].T
    qs_ref[...] = _pair_headnorm(q_ref[...], qg_ref[...]) * (HEAD_DIM ** -0.5)
    for r in range(de):
        rows = pl.ds(r, t4, stride=de)
        q4_ref[r] = qs_ref[rows, :]
        k4_ref[r] = kn_ref[rows, :]
        v4_ref[r] = v_ref[rows, :]

    @pl.when(pl.program_id(1) == 0)
    def _():
        qi = lax.broadcasted_iota(jnp.int32, (span, 2 * span), 0)
        ki = lax.broadcasted_iota(jnp.int32, (span, 2 * span), 1)
        steps = qi + span - ki
        valid = (steps >= 0) & (steps <= span)
        for g, (_, dil) in enumerate(DILATION_GROUPS):
            dist = (steps * dil).astype(F32)
            for h in range(2):
                bias_ref[g, h * span:(h + 1) * span, :] = jnp.where(
                    valid, -(slopes_ref[2 * hp + h] * dist), NEG_BIG)

    def attend(g, qb, kb, vb, has_prev, mxu_sum):
        q2 = jnp.concatenate([jnp.where(lo_half, qb, 0.0), jnp.where(lo_half, 0.0, qb)], axis=0).astype(BF16)
        bias = bias_ref[g] if has_prev else bias_ref[g, :, span:]
        s = lax.dot_general(q2, kb.astype(BF16), _NT, preferred_element_type=F32) + bias
        m = jnp.max(s, axis=-1, keepdims=True)
        p = jnp.exp(s - m)
        if mxu_sum:
            v_ones = jnp.concatenate([vb.astype(BF16), jnp.ones((vb.shape[0], LANES), BF16)], axis=1)
            pv = jnp.dot(p.astype(BF16), v_ones, preferred_element_type=F32)
            l = pv[:, LANES:]
            o = pv[:, 0:LANES] / l
        else:
            l = jnp.sum(p, axis=-1, keepdims=True)
            o = jnp.dot(p.astype(BF16), vb.astype(BF16), preferred_element_type=F32) / l
        lse = m + jnp.log(l)
        return jnp.where(lo_half, o[0:span], o[span:]), jnp.where(lo_half, lse[0:span], lse[span:])

    for g, (window, dil) in enumerate(DILATION_GROUPS):
        assert window // dil == span
        nb = t // (span * dil)

        if dil == 1:
            def unit(b, has_prev):
                rows = pl.ds(b * span, span)
                keys = pl.ds((b - 1) * span, 2 * span) if has_prev else rows
                o, lse = attend(0, qs_ref[rows, :], kn_ref[keys, :], v_ref[keys, :], has_prev, True)
                og_ref[rows, :] = o
                lse_ref[rows, :] = lse

            unit(0, False)
            lax.fori_loop(1, nb, lambda b, c: (unit(b, True), c)[1], 0, unroll=8)
            continue

        assert dil % de == 0
        st = dil // de

        def residue(r, carry, g=g, st=st, nb=nb):
            r4, off = r % de, r // de

            def unit(b, has_prev):
                rows = pl.ds(st * (b * span) + off, span, stride=st)
                keys = pl.ds(st * ((b - 1) * span) + off, 2 * span, stride=st) if has_prev else rows
                o, lse = attend(g, q4_ref[r4, rows, :], k4_ref[r4, keys, :], v4_ref[r4, keys, :], has_prev,
                                nb > 1)
                og4_ref[g - 1, r4, rows, :] = o
                lse4_ref[g - 1, r4, rows, :] = lse

            unit(0, False)
            if nb > 1:
                lax.fori_loop(1, nb, lambda b, c: (unit(b, True), c)[1], 0, unroll=3)
            return carry

        lax.fori_loop(0, dil, residue, 0, unroll=16 if nb == 1 else 4)

    for r in range(de):
        rows = pl.ds(r, t4, stride=de)
        l0, l1, l2 = lse_ref[rows, :], lse4_ref[0, r], lse4_ref[1, r]
        mx = jnp.maximum(jnp.maximum(l0, l1), l2)
        w0, w1, w2 = jnp.exp(l0 - mx), jnp.exp(l1 - mx), jnp.exp(l2 - mx)
        og_ref[rows, :] = (og_ref[rows, :] * w0 + og4_ref[0, r] * w1 + og4_ref[1, r] * w2) / (w0 + w1 + w2)
    o_ref[...] = og_ref[...].astype(o_ref.dtype)


def _attention_prompt(p, col0, att_w, qg, kg, slopes, layer, depth, n_seq, t, kv_stack):
    m = p.shape[0]
    pairs = att_w // LANES
    assert col0 % LANES == 0
    blk = lambda off: pl.BlockSpec((t, LANES), lambda hp, n: (n, off + hp))
    c0 = col0 // LANES
    gain = pl.BlockSpec((None, 1, LANES), lambda hp, n: (layer, 0, 0))
    tr_spec = pl.BlockSpec((None, None, LANES, t), lambda hp, n: (layer, n, hp, 0))
    tr_shape = jax.ShapeDtypeStruct((depth, n_seq, att_w, t), F32)
    carried = tuple(kv_stack)
    assert all(c.shape == tr_shape.shape for c in carried)
    n_in = 6
    assert t % (ATT_DEINT * ATT_SPAN) == 0
    tok = pltpu.VMEM((t, LANES), F32)
    deint = pltpu.VMEM((ATT_DEINT, t // ATT_DEINT, LANES), F32)
    dilated = pltpu.VMEM((len(DILATION_GROUPS) - 1, ATT_DEINT, t // ATT_DEINT, LANES), F32)
    return pl.pallas_call(
        functools.partial(_attn_body, n_carried=len(carried)),
        grid=(pairs, n_seq),
        in_specs=[pl.BlockSpec(memory_space=pltpu.SMEM),
                  blk(c0), blk(c0 + pairs), blk(c0 + 2 * pairs), gain, gain]
                 + [pl.BlockSpec(memory_space=pl.ANY)] * len(carried),
        out_specs=[blk(0), tr_spec, tr_spec],
        out_shape=[jax.ShapeDtypeStruct((m, att_w), BF16), tr_shape, tr_shape],
        input_output_aliases={n_in + i: 1 + i for i in range(len(carried))},
        scratch_shapes=[tok, tok, deint, deint, deint,
                        pltpu.VMEM((len(DILATION_GROUPS), 2 * ATT_SPAN, 2 * ATT_SPAN), F32),
                        tok, tok, dilated, dilated],
        compiler_params=_cparams(("parallel", "arbitrary")),
        name="attn_prompt",
    )(slopes, p, p, p, qg, kg, *carried)


def _softplus(z):
    return jnp.maximum(z, 0.0) + jnp.log(1.0 + jnp.exp(-jnp.abs(z)))


def _rwkv_token_math(p_r, p_k, p_v, p_l, q_r, q_k, q_v, q_l, mu_r, mu_k, mu_v, mu_l,
                     w0, w2p, a0, a2p, g2p, k_k, k_a, gate_dot=_dot3):
    xr = p_r + (q_r - p_r) * mu_r
    xk = p_k + (q_k - p_k) * mu_k
    xv = p_v + (q_v - p_v) * mu_v
    xl = p_l + (q_l - p_l) * mu_l
    w_log = -_softplus(-(w0 + _dot3(jnp.tanh(xl), w2p))) - 0.5
    log_decay = -jnp.exp(w_log)
    sig_l = jax.nn.sigmoid(xl)
    a_gate = jax.nn.sigmoid(a0 + gate_dot(xl, a2p))
    g = gate_dot(sig_l, g2p)
    kk = xk * k_k
    kk = kk * lax.rsqrt(jnp.maximum(_segsum64_many([kk * kk])[0], 1e-24))
    kmod = xk * (1.0 + (a_gate - 1.0) * k_a)
    return xr, log_decay, kmod, xv, -kk, kk * a_gate, g


def _segsum64_many(xs):
    rows, width = xs[0].shape
    if rows % 16 != 0:
        return [_segsum64(x) for x in xs]
    nsl = width // LANES
    pieces = [part[:, i * LANES:(i + 1) * LANES]
              for x in xs for part in _split2(x) for i in range(nsl)]
    res = jnp.dot(jnp.concatenate(pieces, axis=0), _head_blockdiag(LANES, BF16), preferred_element_type=F32)
    outs = []
    for n in range(len(xs)):
        hi, lo = [jnp.concatenate([res[((2 * n + j) * nsl + i) * rows:((2 * n + j) * nsl + i + 1) * rows]
                                   for i in range(nsl)], axis=1) for j in range(2)]
        outs.append(hi + lo)
    return outs


def _rwkv_post(y, r, k, v, g, ln_w, ln_b, r_k):
    sum_y, sum_rk = _segsum64_many([y, r * k * r_k])
    yc = y - sum_y * (1.0 / HEAD_DIM)
    var_y = _segsum64_many([yc * yc])[0] * (1.0 / HEAD_DIM)
    yn = yc * lax.rsqrt(var_y + RWKV_GN_EPS) * ln_w + ln_b
    return (yn + sum_rk * v) * g


def _scan_chunk(r, lw, cum, k, v, a, b, s0):
    c = r.shape[1]
    c2 = 2 * c
    lane = lax.broadcasted_iota(jnp.int32, (1, 1, LANES), 2)
    head0 = lane < HEAD_DIM
    m0 = head0.astype(F32)
    m1 = 1.0 - m0
    stack = lambda x: jnp.concatenate([x * m0, x * m1], axis=1)
    twice = lambda x: jnp.concatenate([x, x], axis=1)
    pick = lambda x_st: jnp.where(head0, x_st[:, 0:c], x_st[:, c:])

    cum_prev = cum - lw
    cum_last = cum[:, c - 1:c, :]
    e_pos = jnp.exp(cum)
    e_neg = jnp.exp(-cum)
    e_rem = jnp.exp(cum_last - cum)
    at = a * jnp.exp(cum_prev)
    rt = r * e_pos
    bt = b * e_neg
    kt = k * e_neg
    bh = b * e_rem
    kh = k * e_rem

    big = _dot1(jnp.concatenate([stack(at), stack(rt)], axis=1),
                jnp.concatenate([twice(bt), twice(kt)], axis=1), _BNT)
    row = lax.broadcasted_iota(jnp.int32, (1, c2, c2), 1)
    col = lax.broadcasted_iota(jnp.int32, (1, c2, c2), 2)
    assert c == HEAD_DIM
    same_head = (row >> HEAD_SHIFT) == (col >> HEAD_SHIFT)
    tr, tc = row & (c - 1), col & (c - 1)
    strict = same_head & (tr > tc)
    incl = same_head & (tr >= tc)
    lmat = jnp.where(strict, big[:, 0:c2, 0:c2], 0.0)
    ak = jnp.where(strict, big[:, 0:c2, c2:], 0.0)
    rb = jnp.where(incl, big[:, c2:, 0:c2], 0.0)
    rk = jnp.where(incl, big[:, c2:, c2:], 0.0)

    eye = (row == col).astype(F32)
    x = eye + jnp.where((row >> 1) == (col >> 1), lmat, 0.0)
    s, log2s = 2, 1
    while s < c:
        joins = ((row >> (log2s + 1)) == (col >> (log2s + 1))) & ((row & s) != 0) & ((col & s) == 0)
        x = x + _dot1(_dot1(x, jnp.where(joins, lmat, 0.0), _BNN), x, _BNN)
        s, log2s = 2 * s, log2s + 1

    v_st = twice(v)
    w = _dot1(jnp.concatenate([at, rt], axis=1), s0, _BNT)
    u_st = _dot1(x, twice(w[:, 0:c]) + _dot1(ak, v_st, _BNN), _BNN)
    y_st = _dot1(jnp.concatenate([rb, rk], axis=2), jnp.concatenate([u_st, v_st], axis=1), _BNN)
    y = w[:, c:] + pick(y_st)
    u = pick(u_st)
    upd = _dot1(jnp.concatenate([u, v], axis=1), jnp.concatenate([bh, kh], axis=1), _BTN)
    s_new = s0 * jnp.exp(cum_last) + _head_blockdiag(LANES, F32)[None] * upd
    return y, s_new


def _rwkv_mix_body(pr_ref, pk_ref, pv_ref, plo_ref,
                   mur_ref, muk_ref, muv_ref, mul_ref, w0_ref, w2_ref, a0_ref, a2_ref, g2_ref,
                   kk_ref, ka_ref, lnw_ref, lnb_ref, rk_ref,
                   y_ref, s_out_ref, s_ref, sr_ref, sk_ref, sv_ref, sl_ref):
    ch = pl.program_id(1)
    nseq, c, rw = pr_ref.shape
    npair = rw // LANES
    shift_refs = (sr_ref, sk_ref, sv_ref, sl_ref)

    @pl.when(ch == 0)
    def _():
        s_ref[...] = jnp.zeros_like(s_ref)
        for ref in shift_refs:
            ref[:, 0:8, :] = jnp.zeros((nseq, 8, ref.shape[2]), F32)

    def cur_and_prev(cur_ref, sh_ref):
        cur = cur_ref[...]
        sh_ref[:, 8:, :] = cur
        prev = sh_ref[:, pl.ds(7, c), :]
        sh_ref[:, 7:8, :] = cur[:, c - 1:c, :]
        flat = lambda x: x.reshape(nseq * c, x.shape[2])
        return flat(cur), flat(prev)

    (p_r, q_r), (p_k, q_k), (p_v, q_v), (p_l, q_l) = [
        cur_and_prev(cur, sh) for cur, sh in zip((pr_ref, pk_ref, pv_ref, plo_ref), shift_refs)]
    r, lw, k, v, a, b, g = _rwkv_token_math(
        p_r, p_k, p_v, p_l, q_r, q_k, q_v, q_l,
        mur_ref[...], muk_ref[...], muv_ref[...], mul_ref[...],
        w0_ref[...], w2_ref[...], a0_ref[...], a2_ref[...], g2_ref[...], kk_ref[...], ka_ref[...],
        gate_dot=_dot1)

    pairs = lambda x: jnp.stack([x[s * c:(s + 1) * c, p * LANES:(p + 1) * LANES]
                                 for s in range(nseq) for p in range(npair)], axis=0)
    tri_incl = (lax.broadcasted_iota(jnp.int32, (c, c), 0)
                >= lax.broadcasted_iota(jnp.int32, (c, c), 1)).astype(BF16)
    cum = jnp.concatenate([_cumsum_rows(lw[s * c:(s + 1) * c], tri_incl) for s in range(nseq)], axis=0)
    y, s_new = _scan_chunk(pairs(r), pairs(lw), pairs(cum), pairs(k), pairs(v), pairs(a), pairs(b), s_ref[...])
    s_ref[...] = s_new
    y = jnp.concatenate([jnp.concatenate([y[s * npair + p] for p in range(npair)], axis=1)
                         for s in range(nseq)], axis=0)
    out = _rwkv_post(y, r, k, v, g, lnw_ref[...], lnb_ref[...], rk_ref[...])
    y_ref[...] = out.reshape(nseq, c, rw).astype(y_ref.dtype)

    @pl.when(ch == pl.num_programs(1) - 1)
    def _():
        for s in range(nseq):
            for p in range(npair):
                s_out_ref[s, 2 * p] = s_ref[s * npair + p, 0:HEAD_DIM, 0:HEAD_DIM]
                s_out_ref[s, 2 * p + 1] = s_ref[s * npair + p, HEAD_DIM:, HEAD_DIM:]


def _rwkv_mix_prompt(p, col0, lp, layer, n_seq, t, seq_per_step):
    m = p.shape[0]
    rw = lp["rwkv_w"]
    c = RWKV_CHUNK
    heads = rw // HEAD_DIM
    assert n_seq % seq_per_step == 0 and col0 % rw == 0 and (col0 + 3 * rw) % LORA_W == 0
    c_slab = col0 // rw
    c_lora = (col0 + 3 * rw) // LORA_W
    tok = lambda w, col: pl.BlockSpec((seq_per_step, c, w), lambda n, ch: (n, ch, col))
    vec = lambda w, col: pl.BlockSpec((None, 1, w), lambda n, ch: (layer, 0, col))
    mat = pl.BlockSpec((None, LORA_W, rw), lambda n, ch: (layer, 0, 0))
    p3 = p.reshape(n_seq, t, p.shape[1])
    y, wkv = pl.pallas_call(
        _rwkv_mix_body,
        grid=(n_seq // seq_per_step, t // c),
        in_specs=[tok(rw, c_slab), tok(rw, c_slab + 1), tok(rw, c_slab + 2), tok(LORA_W, c_lora),
                  vec(rw, 0), vec(rw, 1), vec(rw, 2), vec(LORA_W, (3 * rw) // LORA_W),
                  vec(rw, 0), mat, vec(rw, 0), mat, mat, vec(rw, 0), vec(rw, 0),
                  vec(rw, 0), vec(rw, 0), vec(rw, 0)],
        out_specs=[tok(rw, 0),
                   pl.BlockSpec((seq_per_step, heads, HEAD_DIM, HEAD_DIM), lambda n, ch: (n, 0, 0, 0))],
        out_shape=[jax.ShapeDtypeStruct((n_seq, t, rw), BF16),
                   jax.ShapeDtypeStruct((n_seq, heads, HEAD_DIM, HEAD_DIM), F32)],
        scratch_shapes=[pltpu.VMEM((seq_per_step * (rw // LANES), LANES, LANES), F32)]
                       + [pltpu.VMEM((seq_per_step, c + 8, rw), F32)] * 3
                       + [pltpu.VMEM((seq_per_step, c + 8, LORA_W), F32)],
        compiler_params=_cparams(("parallel", "arbitrary")),
        name="rwkv_mix",
    )(p3, p3, p3, p3, lp["mu"], lp["mu"], lp["mu"], lp["mu"],
      lp["w0"], lp["w2p"], lp["a0"], lp["a2p"], lp["g2p"], lp["k_k"], lp["k_a"],
      lp["ln_w"], lp["ln_b"], lp["r_k"])
    return y.reshape(m, rw), wkv


def _sample_body(slope_ref, p_ref, kc_ref, vc_ref,
                 wkv_ref, shift_ref, cst_ref, qg_ref, kg_ref,
                 mu_ref, w0_ref, w2_ref, a0_ref, a2_ref, g2_ref, kk_ref, ka_ref,
                 rk_ref, lnw_ref, lnb_ref, cw_ref, cb_ref,
                 oatt_ref, yr_ref, yc_ref, kn_ref, vn_ref, wkvn_ref, cstn_ref,
                 *, col_att, col_rwkv, col_conv):
    heads_a = oatt_ref.shape[0]
    aw = heads_a * HEAD_DIM
    rw = yr_ref.shape[1]
    cw = yc_ref.shape[1]
    rp = shift_ref.shape[1]
    rows8 = lambda x: jnp.broadcast_to(x, (8, x.shape[1]))
    p_all = p_ref[...]

    by_head = lambda row: jnp.concatenate(
        [row[:, h * HEAD_DIM:(h + 1) * HEAD_DIM] for h in range(heads_a)], axis=0)
    ca = col_att
    q, k, v_new = (by_head(p_all[:, ca:ca + aw]), by_head(p_all[:, ca + aw:ca + 2 * aw]),
                   by_head(p_all[:, ca + 2 * aw:ca + 3 * aw]))
    qn = _rms(q, qg_ref[...]) * (HEAD_DIM ** -0.5)
    kn = _rms(k, kg_ref[...])
    kn_ref[...] = kn
    vn_ref[...] = v_new
    s_new = jnp.sum(kn * qn, axis=-1, keepdims=True)
    slope = slope_ref[...]
    past = kc_ref.shape[2]
    er = lax.broadcasted_iota(jnp.int32, (HEAD_DIM, HEAD_DIM), 0)
    ec = lax.broadcasted_iota(jnp.int32, (HEAD_DIM, HEAD_DIM), 1)
    eye = er == ec
    q_col = jnp.stack([jnp.sum(jnp.where(eye, qn[h:h + 1, :], 0.0), axis=1, keepdims=True)
                       for h in range(heads_a)], axis=0)
    dist = past - lax.broadcasted_iota(jnp.int32, (1, past), 1)
    biased = jnp.sum(kc_ref[...] * q_col, axis=1) - slope * dist.astype(F32)
    parts = []
    for window, dil in DILATION_GROUPS:
        assert dil & (dil - 1) == 0
        valid = ((dist & (dil - 1)) == 0) & (dist <= window)
        s = jnp.where(valid, biased, NEG_BIG)
        m = jnp.maximum(jnp.max(s, axis=1, keepdims=True), s_new)
        p = jnp.exp(s - m)
        p_new = jnp.exp(s_new - m)
        l = jnp.sum(p, axis=1, keepdims=True) + p_new
        parts.append((p, p_new, l, m + jnp.log(l)))
    mx = jnp.maximum(jnp.maximum(parts[0][3], parts[1][3]), parts[2][3])
    ws = [jnp.exp(lse - mx) for _, _, _, lse in parts]
    wsum = ws[0] + ws[1] + ws[2]
    scale_g = [w / (l * wsum) for w, (_, _, l, _) in zip(ws, parts)]
    coef = parts[0][0] * scale_g[0] + parts[1][0] * scale_g[1] + parts[2][0] * scale_g[2]
    coef_new = parts[0][1] * scale_g[0] + parts[1][1] * scale_g[1] + parts[2][1] * scale_g[2]
    o_col = jnp.sum(vc_ref[...] * coef[:, None, :], axis=2, keepdims=True)
    o_rows = jnp.concatenate([jnp.sum(jnp.where(eye, o_col[h], 0.0), axis=0, keepdims=True)
                              for h in range(heads_a)], axis=0)
    oatt_ref[...] = o_rows + coef_new * v_new

    pr = rows8(p_all[:, col_rwkv:col_rwkv + rp])
    sh = rows8(shift_ref[...])
    mu = mu_ref[...]
    c3 = 3 * rw
    r, lw, k, v, a, b, g = _rwkv_token_math(
        pr[:, 0:rw], pr[:, rw:2 * rw], pr[:, 2 * rw:c3], pr[:, c3:],
        sh[:, 0:rw], sh[:, rw:2 * rw], sh[:, 2 * rw:c3], sh[:, c3:],
        mu[:, 0:rw], mu[:, rw:2 * rw], mu[:, 2 * rw:c3], mu[:, c3:],
        w0_ref[...], w2_ref[...], a0_ref[...], a2_ref[...], g2_ref[...], kk_ref[...], ka_ref[...])
    decay = jnp.exp(lw)
    place_r = lax.broadcasted_iota(jnp.int32, (HEAD_DIM, rw), 0)
    place_c = lax.broadcasted_iota(jnp.int32, (HEAD_DIM, rw), 1)
    y_full = jnp.zeros((1, rw), F32)
    for h in range(rw // HEAD_DIM):
        hs = slice(h * HEAD_DIM, (h + 1) * HEAD_DIM)
        s_old = wkv_ref[h]
        sa = jnp.sum(s_old * a[0:1, hs], axis=1, keepdims=True)
        v_col = jnp.sum(jnp.where(eye, v[0:1, hs], 0.0), axis=1, keepdims=True)
        s_h = s_old * decay[0:1, hs] + sa * b[0:1, hs] + v_col * k[0:1, hs]
        wkvn_ref[h] = s_h
        y_col = jnp.sum(s_h * r[0:1, hs], axis=1, keepdims=True)
        y_full = y_full + jnp.sum(jnp.where(place_c == place_r + h * HEAD_DIM, y_col, 0.0),
                                  axis=0, keepdims=True)
    y8 = rows8(y_full)
    yr_ref[...] = _rwkv_post(y8, r, k, v, g, lnw_ref[...], lnb_ref[...], rk_ref[...])[0:1]

    c1 = col_conv
    z = p_all[:, c1 + 2 * cw:c1 + 3 * cw] * p_all[:, c1:c1 + cw]
    yc = cb_ref[...] + cw_ref[0:1, :] * cst_ref[0:1, :]
    yc = yc + cw_ref[1:2, :] * cst_ref[1:2, :]
    yc = yc + cw_ref[2:3, :] * z
    yc_ref[...] = p_all[:, c1 + cw:c1 + 2 * cw] * yc
    cstn_ref[0:1, :] = cst_ref[1:2, :]
    cstn_ref[1:2, :] = z


def _sample_mixers(p, cols, cache_k, cache_v, state_wkv, state_shift, state_conv, lp, slope_col, layer):
    nb = p.shape[0]
    depth, _, past, heads_a, _ = cache_k.shape
    rw = lp["rwkv_w"]
    rp = state_shift.shape[2]
    cw = state_conv.shape[3]
    heads = rw // HEAD_DIM

    row = lambda w: pl.BlockSpec((None, 1, w), lambda n: (n, 0, 0))
    lvec = lambda w: pl.BlockSpec((None, 1, w), lambda n: (layer, 0, 0))
    lmat = lambda r_, w: pl.BlockSpec((None, r_, w), lambda n: (layer, 0, 0))
    tile = pl.BlockSpec((None, heads_a, HEAD_DIM), lambda n: (n, 0, 0))

    assert past >= DILATION_GROUPS[-1][0]
    views = [jnp.transpose(c, (0, 1, 3, 4, 2)) for c in (cache_k, cache_v)]
    cache_specs = [pl.BlockSpec((None, None, heads_a, HEAD_DIM, past), lambda n: (layer, n, 0, 0, 0))] * 2

    outs = pl.pallas_call(
        functools.partial(_sample_body, col_att=cols[0], col_rwkv=cols[1], col_conv=cols[2]),
        grid=(nb,),
        in_specs=[pl.BlockSpec((heads_a, 1), lambda n: (0, 0)), row(p.shape[1])] + cache_specs + [
                  pl.BlockSpec((None, None, heads, HEAD_DIM, HEAD_DIM), lambda n: (layer, n, 0, 0, 0)),
                  pl.BlockSpec((None, None, 1, rp), lambda n: (layer, n, 0, 0)),
                  pl.BlockSpec((None, None, CONV_K - 1, cw), lambda n: (layer, n, 0, 0)),
                  lvec(HEAD_DIM), lvec(HEAD_DIM),
                  lvec(rp), lvec(rw), lmat(LORA_W, rw), lvec(rw), lmat(LORA_W, rw),
                  lmat(LORA_W, rw), lvec(rw), lvec(rw), lvec(rw), lvec(rw), lvec(rw),
                  lmat(CONV_K, cw), lvec(cw)],
        out_specs=[tile, row(rw), row(cw), tile, tile,
                   pl.BlockSpec((None, heads, HEAD_DIM, HEAD_DIM), lambda n: (n, 0, 0, 0)),
                   pl.BlockSpec((None, CONV_K - 1, cw), lambda n: (n, 0, 0))],
        out_shape=[jax.ShapeDtypeStruct((nb, heads_a, HEAD_DIM), F32), jax.ShapeDtypeStruct((nb, 1, rw), F32),
                   jax.ShapeDtypeStruct((nb, 1, cw), F32),
                   jax.ShapeDtypeStruct((nb, heads_a, HEAD_DIM), F32),
                   jax.ShapeDtypeStruct((nb, heads_a, HEAD_DIM), F32),
                   jax.ShapeDtypeStruct((nb, heads, HEAD_DIM, HEAD_DIM), F32),
                   jax.ShapeDtypeStruct((nb, CONV_K - 1, cw), F32)],
        compiler_params=_cparams(("parallel",)),
        name="sample_mixers",
    )(slope_col, p[:, None, :], *views,
      state_wkv, state_shift[:, :, None, :], state_conv,
      lp["q_gain"], lp["k_gain"],
      lp["mu"], lp["w0"], lp["w2p"], lp["a0"], lp["a2p"], lp["g2p"], lp["k_k"], lp["k_a"],
      lp["r_k"], lp["ln_w"], lp["ln_b"], lp["conv_w"], lp["conv_b"])
    o_att, y_rwkv, y_conv, k_new, v_new, wkv_new, conv_new = outs
    return o_att.reshape(nb, heads_a * HEAD_DIM), y_rwkv[:, 0], y_conv[:, 0], k_new, v_new, wkv_new, conv_new


def kernel(x_prompt, x_sample, cache_swa_k, cache_swa_v, state_wkv, state_shift, state_conv, ffn1_norm, ffn1_w_gate, ffn1_w_up, ffn1_w_down, mix_norm, w_in, q_norm, k_norm, rwkv_mu, rwkv_w0, rwkv_w2, rwkv_a0, rwkv_a2, rwkv_g2, rwkv_k_k, rwkv_k_a, rwkv_r_k, rwkv_ln_w, rwkv_ln_b, conv_w, conv_b, w_out, ffn2_norm, ffn2_w_gate, ffn2_w_up, ffn2_w_down):
    n_seq, t, d = x_prompt.shape
    nb = x_sample.shape[0]
    assert x_sample.shape[1] == 1
    depth = w_in.shape[0]
    att_heads, head_dim = cache_swa_k.shape[3], cache_swa_k.shape[4]
    assert head_dim == HEAD_DIM
    aw = att_heads * HEAD_DIM
    rw = rwkv_w0.shape[1]
    cw = conv_b.shape[1]
    rp = rwkv_mu.shape[1]
    assert rp == 3 * rw + LORA_W and w_in.shape[2] == 3 * aw + rp + 3 * cw

    row3 = lambda p: p.reshape(depth, 1, -1)
    col_att, col_rwkv, col_conv = 0, 3 * aw, 3 * aw + rp
    zpad = lambda w, before: jnp.pad(w, ((0, 0), (before, LORA_W - before - w.shape[1]), (0, 0)))
    lp = {
        "rwkv_w": rw,
        "mu": row3(rwkv_mu), "w0": row3(rwkv_w0), "a0": row3(rwkv_a0),
        "w2p": zpad(rwkv_w2, 0), "a2p": zpad(rwkv_a2, W_LORA), "g2p": zpad(rwkv_g2, W_LORA + A_LORA),
        "k_k": row3(rwkv_k_k), "k_a": row3(rwkv_k_a), "r_k": row3(rwkv_r_k),
        "ln_w": row3(rwkv_ln_w), "ln_b": row3(rwkv_ln_b),
        "conv_w": conv_w, "conv_b": row3(conv_b),
        "q_gain": row3(q_norm), "k_gain": row3(k_norm),
        "q_gain_pair": row3(jnp.tile(q_norm, (1, LANES // HEAD_DIM))),
        "k_gain_pair": row3(jnp.tile(k_norm, (1, LANES // HEAD_DIM))),
    }
    g_ffn1, g_mix, g_ffn2 = row3(ffn1_norm), row3(mix_norm), row3(ffn2_norm)
    slopes = 2.0 ** (-8.0 * jnp.arange(1, att_heads + 1, dtype=F32) / att_heads)
    slope_col = slopes[:, None]

    m = n_seq * t
    xp = x_prompt.reshape(m, d)
    xs = x_sample.reshape(nb, d)
    tm_ffn, tf = 1024, 256
    tm_in, tn_in = 1024, 1280
    tm_out = 512
    p_states, s_states = [], []
    kv_shape = (depth, n_seq, aw, t)
    for l in range(depth):
        xp, xs, w_in16, w_out16, *fresh = _ffn(
            xp, xs, g_ffn1, ffn1_w_gate, ffn1_w_up, ffn1_w_down, l, tm_ffn, tf, cast=(w_in, w_out),
            zeros=((depth * n_seq * aw, t),) * 2 if l == 0 else ())
        if l == 0:
            kv_stack = [z.reshape(kv_shape) for z in fresh]
        pp, ps = _norm_matmul(xp, xs, g_mix, w_in16, l, tm_in, tn_in)

        o_att, *kv_stack = _attention_prompt(pp, col_att, aw, lp["q_gain_pair"], lp["k_gain_pair"], slopes,
                                             l, depth, n_seq, t, kv_stack)
        y_rwkv, wkv_p = _rwkv_mix_prompt(pp, col_rwkv, lp, l, n_seq, t, 4 if n_seq % 4 == 0 else 1)
        xp, z_last = _out_proj_conv(xp, o_att, y_rwkv, pp, col_conv, cw, lp["conv_w"], lp["conv_b"], w_out16,
                                    l, t, tm_out)
        p_states.append((
            wkv_p, pp.reshape(n_seq, t, -1)[:, -1, col_rwkv:col_rwkv + rp],
            z_last.reshape(n_seq, t // tm_out, 8, cw)[:, -1, 8 - (CONV_K - 1):]))

        so, sy, syc, sk, sv, swkv, sconv = _sample_mixers(
            ps, (col_att, col_rwkv, col_conv), cache_swa_k, cache_swa_v, state_wkv, state_shift, state_conv,
            lp, slope_col, l)
        xs = _out_proj(xs, so, sy, syc, w_out16, l, nb)
        s_states.append((sk[:, None], sv[:, None], swkv, ps[:, col_rwkv:col_rwkv + rp], sconv))

        xp, xs = _ffn(xp, xs, g_ffn2, ffn2_w_gate, ffn2_w_up, ffn2_w_down, l, tm_ffn, tf)

    stacked = lambda states, i: jnp.stack([st[i] for st in states], axis=0)
    rows = lambda x_t: jnp.transpose(x_t.reshape(depth, n_seq, att_heads, HEAD_DIM, t), (0, 1, 4, 2, 3))
    return (xp.reshape(n_seq, t, d), xs.reshape(nb, 1, d),
            rows(kv_stack[0]), rows(kv_stack[1]),
            stacked(p_states, 0), stacked(p_states, 1), stacked(p_states, 2),
            stacked(s_states, 0), stacked(s_states, 1), stacked(s_states, 2), stacked(s_states, 3),
            stacked(s_states, 4))
```
